```python
import math
import jax
import jax.numpy as jnp
from jax import lax
import numpy as np

D_MODEL = 1024
BATCH = 2
SEQ = 8192
DEPTH = 2

RWKV_HEADS = 8
RWKV_HEAD_DIM = 64
RWKV_DIM = RWKV_HEADS * RWKV_HEAD_DIM
DECAY_LORA = 64
AAA_LORA = 64
GATE_LORA = 128
RWKV_GN_EPS = 64e-5
MLA_HEADS = 8
QK_NOPE_DIM = 64
QK_ROPE_DIM = 32
V_HEAD_DIM = 64
Q_LORA_RANK = 256
KV_LORA_RANK = 128
MLA_DIM = MLA_HEADS * V_HEAD_DIM
ROPE_THETA = 10000.0
Q_BLOCK = 128
ATTN_SCALE = 1.0 / math.sqrt(QK_NOPE_DIM + QK_ROPE_DIM)
S5_DIM = 512
S5_GROUP = 16
S5_GROUPS = S5_DIM // S5_GROUP
S5_STATE = 64
S5_DT_MIN = 1e-3
S5_DT_MAX = 1e-1
N_BRANCHES = 3
D_FF = ((8 * D_MODEL + 3 * 256 - 1) // (3 * 256)) * 256
LN_EPS = 1e-5
RMS_EPS = 1e-6
DEEPNORM_ALPHA = (2.0 * DEPTH) ** 0.25
DEEPNORM_BETA = (8.0 * DEPTH) ** -0.25
RWKV_COLS = 3 * RWKV_DIM + DECAY_LORA + AAA_LORA + GATE_LORA
MLA_COLS = Q_LORA_RANK + KV_LORA_RANK + QK_ROPE_DIM
S5_COLS = S5_DIM
GATE_COLS = N_BRANCHES * D_MODEL
IN_COLS = RWKV_COLS + MLA_COLS + S5_COLS + GATE_COLS

kernel_name = 'hybrid_rwkv7_mla_s5_gated_block'


def layer_norm(x, g, b):
    xf = x.astype(jnp.float32)
    mu = jnp.mean(xf, -1, keepdims=True)
    var = jnp.mean(jnp.square(xf - mu), -1, keepdims=True)
    return ((xf - mu) * lax.rsqrt(var + LN_EPS) * g + b).astype(x.dtype)


def rms_norm(x, g):
    xf = x.astype(jnp.float32)
    return (xf * lax.rsqrt(jnp.mean(xf * xf, -1, keepdims=True) + RMS_EPS) * g).astype(x.dtype)


def token_shift(p):
    return jnp.pad(p, ((0, 0), (1, 0), (0, 0)))[:, :-1]


def apply_rope(x, cos, sin):
    half = x.shape[-1] // 2
    xf = x.astype(jnp.float32)
    x1, x2 = xf[..., :half], xf[..., half:]
    return jnp.concatenate([x1 * cos - x2 * sin, x2 * cos + x1 * sin], -1).astype(x.dtype)


def rwkv7_scan(r, w, k, v, a, b):
    def step(state, inp):
        r_t, w_t, k_t, v_t, a_t, b_t = inp
        sa = jnp.einsum('bhvk,bhk->bhv', state, a_t)
        state = state * w_t[:, :, None, :] + sa[..., None] * b_t[:, :, None, :] + v_t[..., None] * k_t[:, :, None, :]
        return state, jnp.einsum('bhvk,bhk->bhv', state, r_t)
    bsz, _, h, n = r.shape
    xs = tuple(jnp.swapaxes(z, 0, 1) for z in (r, w, k, v, a, b))
    s0 = jnp.zeros((bsz, h, n, n), jnp.float32)
    _, ys = lax.scan(step, s0, xs)
    return jnp.swapaxes(ys, 0, 1)


def rwkv7_time_mix(p, mu, w0, w2, a0, a2, g2, k_k, k_a, r_k, gn_g, gn_b, w_o):
    bsz, t, _ = p.shape
    f32 = jnp.float32
    p = p + (token_shift(p) - p) * mu
    c0 = RWKV_DIM
    c1 = 2 * RWKV_DIM
    c2 = 3 * RWKV_DIM
    c3 = c2 + DECAY_LORA
    c4 = c3 + AAA_LORA
    r, k, v, dw, da, dg = p[..., :c0], p[..., c0:c1], p[..., c1:c2], p[..., c2:c3], p[..., c3:c4], p[..., c4:]
    w_log = -jax.nn.softplus(-(w0 + jnp.tanh(dw) @ w2).astype(f32)) - 0.5
    decay = jnp.exp(-jnp.exp(w_log))
    a = jax.nn.sigmoid((a0 + da @ a2).astype(f32))
    g = jax.nn.sigmoid(dg) @ g2

    def heads(z):
        return z.astype(f32).reshape(bsz, t, RWKV_HEADS, RWKV_HEAD_DIM)

    r, k, v, a, decay = heads(r), heads(k), heads(v), heads(a), heads(decay)
    kk = k * k_k.astype(f32).reshape(RWKV_HEADS, RWKV_HEAD_DIM)
    kk = kk * lax.rsqrt(jnp.maximum(jnp.sum(kk * kk, -1, keepdims=True), 1e-12))
    k = k * (1.0 + (a - 1.0) * k_a.astype(f32).reshape(RWKV_HEADS, RWKV_HEAD_DIM))
    y = rwkv7_scan(r, decay, k, v, -kk, kk * a)
    m = jnp.mean(y, -1, keepdims=True)
    var = jnp.mean(jnp.square(y - m), -1, keepdims=True)
    y = ((y - m) * lax.rsqrt(var + RWKV_GN_EPS)).reshape(bsz, t, RWKV_DIM) * gn_g + gn_b
    bonus = jnp.sum(r * k * r_k.astype(f32), -1, keepdims=True) * v
    y = y + bonus.reshape(bsz, t, RWKV_DIM)
    return (y * g).astype(p.dtype) @ w_o


def causal_block_attention(q_nope, q_rope, k_nope, k_rope, v):
    bsz, t, h, _ = q_nope.shape
    nb = t // Q_BLOCK
    kpos = jnp.arange(t)

    def to_blocks(z):
        return jnp.moveaxis(z.reshape(bsz, nb, Q_BLOCK, *z.shape[2:]), 1, 0)

    def one_block(args):
        qn, qr, start = args
        s = jnp.einsum('bqhd,bkhd->bhqk', qn, k_nope) + jnp.einsum('bqhr,bkr->bhqk', qr, k_rope)
        s = s.astype(jnp.float32) * ATTN_SCALE
        qpos = start + jnp.arange(Q_BLOCK)
        s = jnp.where(kpos[None, :] <= qpos[:, None], s, jnp.finfo(jnp.float32).min)
        pr = jax.nn.softmax(s, axis=-1).astype(v.dtype)
        return jnp.einsum('bhqk,bkhd->bqhd', pr, v)

    starts = jnp.arange(nb, dtype=jnp.int32) * Q_BLOCK
    out = lax.map(one_block, (to_blocks(q_nope), to_blocks(q_rope), starts))
    return jnp.moveaxis(out, 0, 1).reshape(bsz, t, h, v.shape[-1])


def mla_branch(p, cos, sin, q_norm, q_up, kv_norm, kv_up, w_o):
    bsz, t, _ = p.shape
    c_q = p[..., :Q_LORA_RANK]
    c_kv = p[..., Q_LORA_RANK:Q_LORA_RANK + KV_LORA_RANK]
    k_r = p[..., Q_LORA_RANK + KV_LORA_RANK:]
    q = (rms_norm(c_q, q_norm) @ q_up).reshape(bsz, t, MLA_HEADS, QK_NOPE_DIM + QK_ROPE_DIM)
    kv = (rms_norm(c_kv, kv_norm) @ kv_up).reshape(bsz, t, MLA_HEADS, QK_NOPE_DIM + V_HEAD_DIM)
    q_nope = q[..., :QK_NOPE_DIM]
    q_rope = apply_rope(q[..., QK_NOPE_DIM:], cos[:, None, :], sin[:, None, :])
    k_nope, v = kv[..., :QK_NOPE_DIM], kv[..., QK_NOPE_DIM:]
    k_rope = apply_rope(k_r, cos, sin)
    o = causal_block_attention(q_nope, q_rope, k_nope, k_rope, v)
    return o.reshape(bsz, t, MLA_DIM) @ w_o


def _complex_linear_combine(e1, e2):
    a1r, a1i, b1r, b1i = e1
    a2r, a2i, b2r, b2i = e2
    return (a2r * a1r - a2i * a1i,
            a2r * a1i + a2i * a1r,
            a2r * b1r - a2i * b1i + b2r,
            a2r * b1i + a2i * b1r + b2i)


def s5_branch(u, lambda_re, lambda_im, log_step, b_re, b_im, c_re, c_im, d_skip, w_glu):
    bsz, t, _ = u.shape
    f32 = jnp.float32
    lam_re = jnp.minimum(lambda_re.astype(f32), -1e-4)
    lam_im = lambda_im.astype(f32)
    step = jnp.exp(log_step.astype(f32))[:, None]
    mag = jnp.exp(lam_re * step)
    ang = lam_im * step
    lb_re, lb_im = mag * jnp.cos(ang), mag * jnp.sin(ang)
    den = lam_re * lam_re + lam_im * lam_im
    n_re = lb_re - 1.0
    f_re = (n_re * lam_re + lb_im * lam_im) / den
    f_im = (lb_im * lam_re - n_re * lam_im) / den
    br, bi = b_re.astype(f32), b_im.astype(f32)
    bb_re = f_re[..., None] * br - f_im[..., None] * bi
    bb_im = f_re[..., None] * bi + f_im[..., None] * br
    uf = u.astype(f32)
    ug = uf.reshape(bsz, t, S5_GROUPS, S5_GROUP)
    bu_re = jnp.einsum('btgc,gpc->btgp', ug, bb_re)
    bu_im = jnp.einsum('btgc,gpc->btgp', ug, bb_im)
    a_re = jnp.broadcast_to(lb_re, bu_re.shape)
    a_im = jnp.broadcast_to(lb_im, bu_im.shape)
    _, _, s_re, s_im = lax.associative_scan(_complex_linear_combine, (a_re, a_im, bu_re, bu_im), axis=1)
    y = jnp.einsum('btgp,gcp->btgc', s_re, c_re.astype(f32)) - jnp.einsum('btgp,gcp->btgc', s_im, c_im.astype(f32))
    y = y.reshape(bsz, t, S5_DIM) + d_skip.astype(f32) * uf
    y = jax.nn.gelu(y).astype(u.dtype)
    h = y @ w_glu
    return h[..., :D_MODEL] * jax.nn.sigmoid(h[..., D_MODEL:])


def setup_inputs(seed: int = 0) -> dict:
    key = jax.random.key(seed)
    ks = iter(jax.random.split(key, 48))
    L = DEPTH
    f32 = jnp.float32

    def nrm(shape, scale):
        return jax.random.normal(next(ks), shape, f32) * scale

    n = jnp.arange(RWKV_DIM, dtype=f32) / (RWKV_DIM - 1)
    ratio = jnp.arange(L, dtype=f32) / max(L - 1, 1)
    decay_speed = -7.0 + 5.0 * n[None, :] ** (0.85 + jnp.sqrt(ratio)[:, None])
    inp = {}
    inp['x'] = nrm((BATCH, SEQ, D_MODEL), 1.0)
    inp['w_in'] = nrm((L, D_MODEL, IN_COLS), D_MODEL ** -0.5)
    inp['rwkv_mu'] = jax.random.uniform(next(ks), (L, RWKV_COLS), f32)
    inp['rwkv_w0'] = decay_speed + 0.5 + nrm((L, RWKV_DIM), 0.01)
    inp['rwkv_w2'] = nrm((L, DECAY_LORA, RWKV_DIM), 0.1 * DECAY_LORA ** -0.5)
    inp['rwkv_a0'] = nrm((L, RWKV_DIM), 0.01)
    inp['rwkv_a2'] = nrm((L, AAA_LORA, RWKV_DIM), 0.1 * AAA_LORA ** -0.5)
    inp['rwkv_g2'] = nrm((L, GATE_LORA, RWKV_DIM), GATE_LORA ** -0.5)
    inp['rwkv_k_k'] = 0.85 + nrm((L, RWKV_DIM), 0.02)
    inp['rwkv_k_a'] = 1.0 + nrm((L, RWKV_DIM), 0.02)
    inp['rwkv_r_k'] = -0.04 + nrm((L, RWKV_HEADS, RWKV_HEAD_DIM), 0.02)
    inp['rwkv_gn_g'] = 1.0 + nrm((L, RWKV_DIM), 0.02)
    inp['rwkv_gn_b'] = nrm((L, RWKV_DIM), 0.02)
    inp['rwkv_out'] = nrm((L, RWKV_DIM, D_MODEL), DEEPNORM_BETA * RWKV_DIM ** -0.5)
    inp['mla_q_norm'] = 1.0 + nrm((L, Q_LORA_RANK), 0.02)
    inp['mla_q_up'] = nrm((L, Q_LORA_RANK, MLA_HEADS * (QK_NOPE_DIM + QK_ROPE_DIM)), Q_LORA_RANK ** -0.5)
    inp['mla_kv_norm'] = 1.0 + nrm((L, KV_LORA_RANK), 0.02)
    inp['mla_kv_up'] = nrm((L, KV_LORA_RANK, MLA_HEADS * (QK_NOPE_DIM + V_HEAD_DIM)), KV_LORA_RANK ** -0.5)
    inp['mla_out'] = nrm((L, MLA_DIM, D_MODEL), DEEPNORM_BETA * MLA_DIM ** -0.5)
    inp['s5_lambda_re'] = -0.5 + nrm((L, S5_GROUPS, S5_STATE), 0.01)
    inp['s5_lambda_im'] = jnp.broadcast_to(math.pi * jnp.arange(S5_STATE, dtype=f32), (L, S5_GROUPS, S5_STATE))
    inp['s5_log_step'] = jax.random.uniform(next(ks), (L, S5_GROUPS), f32, math.log(S5_DT_MIN), math.log(S5_DT_MAX))
    inp['s5_b_re'] = nrm((L, S5_GROUPS, S5_STATE, S5_GROUP), (2 * S5_GROUP) ** -0.5)
    inp['s5_b_im'] = nrm((L, S5_GROUPS, S5_STATE, S5_GROUP), (2 * S5_GROUP) ** -0.5)
    inp['s5_c_re'] = nrm((L, S5_GROUPS, S5_GROUP, S5_STATE), S5_STATE ** -0.5)
    inp['s5_c_im'] = nrm((L, S5_GROUPS, S5_GROUP, S5_STATE), S5_STATE ** -0.5)
    inp['s5_d'] = nrm((L, S5_DIM), 1.0)
    inp['s5_glu'] = nrm((L, S5_DIM, 2 * D_MODEL), DEEPNORM_BETA * S5_DIM ** -0.5)
    inp['gate_b'] = nrm((L, N_BRANCHES, D_MODEL), 0.01)
    inp['w_out'] = nrm((L, D_MODEL, D_MODEL), DEEPNORM_BETA * D_MODEL ** -0.5)
    inp['ln1_g'] = 1.0 + nrm((L, D_MODEL), 0.02)
    inp['ln1_b'] = nrm((L, D_MODEL), 0.02)
    inp['ffn_w1'] = nrm((L, D_MODEL, D_FF), D_MODEL ** -0.5)
    inp['ffn_w3'] = nrm((L, D_MODEL, D_FF), DEEPNORM_BETA * D_MODEL ** -0.5)
    inp['ffn_w2'] = nrm((L, D_FF, D_MODEL), DEEPNORM_BETA * D_FF ** -0.5)
    inp['ln2_g'] = 1.0 + nrm((L, D_MODEL), 0.02)
    inp['ln2_b'] = nrm((L, D_MODEL), 0.02)
    return inp


def reference(x, w_in, rwkv_mu, rwkv_w0, rwkv_w2, rwkv_a0, rwkv_a2, rwkv_g2, rwkv_k_k, rwkv_k_a, rwkv_r_k,
              rwkv_gn_g, rwkv_gn_b, rwkv_out, mla_q_norm, mla_q_up, mla_kv_norm, mla_kv_up, mla_out,
              s5_lambda_re, s5_lambda_im, s5_log_step, s5_b_re, s5_b_im, s5_c_re, s5_c_im, s5_d, s5_glu,
              gate_b, w_out, ln1_g, ln1_b, ffn_w1, ffn_w3, ffn_w2, ln2_g, ln2_b):
    bsz, t, _ = x.shape
    pos = jnp.arange(t, dtype=jnp.float32)
    inv_freq = ROPE_THETA ** (-jnp.arange(0, QK_ROPE_DIM, 2, dtype=jnp.float32) / QK_ROPE_DIM)
    ang = pos[:, None] * inv_freq[None, :]
    cos, sin = jnp.cos(ang), jnp.sin(ang)
    o_mla = RWKV_COLS
    o_s5 = o_mla + MLA_COLS
    o_gate = o_s5 + S5_COLS
    for l in range(DEPTH):
        p = x @ w_in[l]
        y_a = rwkv7_time_mix(p[..., :o_mla], rwkv_mu[l], rwkv_w0[l], rwkv_w2[l], rwkv_a0[l], rwkv_a2[l],
                             rwkv_g2[l], rwkv_k_k[l], rwkv_k_a[l], rwkv_r_k[l], rwkv_gn_g[l], rwkv_gn_b[l],
                             rwkv_out[l])
        y_b = mla_branch(p[..., o_mla:o_s5], cos, sin, mla_q_norm[l], mla_q_up[l], mla_kv_norm[l],
                         mla_kv_up[l], mla_out[l])
        y_c = s5_branch(p[..., o_s5:o_gate], s5_lambda_re[l], s5_lambda_im[l], s5_log_step[l], s5_b_re[l],
                        s5_b_im[l], s5_c_re[l], s5_c_im[l], s5_d[l], s5_glu[l])
        gates = jax.nn.sigmoid(p[..., o_gate:].reshape(bsz, t, N_BRANCHES, D_MODEL) + gate_b[l])
        merged = gates[..., 0, :] * y_a + gates[..., 1, :] * y_b + gates[..., 2, :] * y_c
        x = layer_norm(DEEPNORM_ALPHA * x + merged @ w_out[l], ln1_g[l], ln1_b[l])
        h = jax.nn.silu(x @ ffn_w1[l]) * (x @ ffn_w3[l])
        x = layer_norm(DEEPNORM_ALPHA * x + h @ ffn_w2[l], ln2_g[l], ln2_b[l])
    return x
```

```python
import functools
import math

import jax
import jax.numpy as jnp
from jax import lax
from jax.experimental import pallas as pl
from jax.experimental.pallas import tpu as pltpu

F32 = jnp.float32
BF16 = jnp.bfloat16

D_MODEL = 1024
RWKV_HEADS = 8
RWKV_HEAD_DIM = 64
RWKV_DIM = RWKV_HEADS * RWKV_HEAD_DIM
DECAY_LORA = 64
AAA_LORA = 64
GATE_LORA = 128
RWKV_GN_EPS = 64e-5
MLA_HEADS = 8
QK_NOPE_DIM = 64
QK_ROPE_DIM = 32
V_HEAD_DIM = 64
Q_LORA_RANK = 256
KV_LORA_RANK = 128
MLA_DIM = MLA_HEADS * V_HEAD_DIM
ROPE_THETA = 10000.0
ATTN_SCALE = 1.0 / math.sqrt(QK_NOPE_DIM + QK_ROPE_DIM)
S5_DIM = 512
S5_GROUP = 16
S5_GROUPS = S5_DIM // S5_GROUP
S5_STATE = 64
N_BRANCHES = 3
LN_EPS = 1e-5
RMS_EPS = 1e-6
RWKV_COLS = 3 * RWKV_DIM + DECAY_LORA + AAA_LORA + GATE_LORA
MLA_COLS = Q_LORA_RANK + KV_LORA_RANK + QK_ROPE_DIM
S5_COLS = S5_DIM

LANES = 128
HEAD_PAIR = LANES // RWKV_HEAD_DIM
N_PAIRS = RWKV_HEADS // HEAD_PAIR
S5_SLAB_GROUPS = LANES // S5_GROUP
S5_SLABS = S5_GROUPS // S5_SLAB_GROUPS
S5_SLAB_STATE = S5_SLAB_GROUPS * S5_STATE
VMEM_LIMIT = 48 * 1024 * 1024

RWKV_CHUNK = 64
RWKV_BLOCK = 256
NEG_BIG = -1e30


def _bf(x):
    return x.astype(BF16)


def _dot(a, b):
    return jnp.dot(a, b, preferred_element_type=F32)


def _dot_nt(a, b):
    return lax.dot_general(a, b, (((1,), (1,)), ((), ())), preferred_element_type=F32)


def _dot_tn(a, b):
    return lax.dot_general(a, b, (((0,), (0,)), ((), ())), preferred_element_type=F32)


def _sigmoid(x):
    return 1.0 / (1.0 + jnp.exp(-x))


def _params(*sem):
    return pltpu.CompilerParams(dimension_semantics=sem, vmem_limit_bytes=VMEM_LIMIT)


def _layer_norm(y, g, b):
    mu = jnp.mean(y, -1, keepdims=True)
    d = y - mu
    var = jnp.mean(d * d, -1, keepdims=True)
    return d * lax.rsqrt(var + LN_EPS) * g + b


def _pair_sum(x, m0):
    s0 = jnp.sum(jnp.where(m0, x, 0.0), -1, keepdims=True)
    s1 = jnp.sum(jnp.where(m0, 0.0, x), -1, keepdims=True)
    return jnp.where(m0, s0, s1)


def _rwkv_prep_kernel(x_ref, w_ref, mu_ref, w0_ref, a0_ref, wl_ref, g2_ref,
                      r_ref, k_ref, v_ref, ld_ref, a_ref, g_ref, prev_ref):
    t = pl.program_id(1)

    @pl.when(t == 0)
    def _():
        prev_ref[...] = jnp.zeros_like(prev_ref)

    p = _dot(x_ref[...], w_ref[...])
    rolled = pltpu.roll(p, 1, axis=0)
    row = lax.broadcasted_iota(jnp.int32, (p.shape[0], 1), 0)
    shifted = jnp.where(row == 0, prev_ref[0:1, :], rolled)
    prev_ref[0:1, :] = rolled[0:1, :]
    p = p + (shifted - p) * mu_ref[...]

    c0, c1, c2 = RWKV_DIM, 2 * RWKV_DIM, 3 * RWKV_DIM
    r_ref[...] = p[:, :c0]
    k_ref[...] = p[:, c0:c1]
    v_ref[...] = p[:, c1:c2]
    lora = p[:, c2:c2 + DECAY_LORA + AAA_LORA]
    lane = lax.broadcasted_iota(jnp.int32, (1, DECAY_LORA + AAA_LORA), 1)
    lora = jnp.where(lane < DECAY_LORA, jnp.tanh(lora), lora)
    wa = _dot(_bf(lora), wl_ref[...])
    ld_ref[...] = (-math.exp(-0.5)) * _sigmoid(w0_ref[...] + wa[:, :RWKV_DIM])
    a_ref[...] = _sigmoid(a0_ref[...] + wa[:, RWKV_DIM:])
    dg = p[:, c2 + DECAY_LORA + AAA_LORA:]
    g_ref[...] = _dot(_bf(_sigmoid(dg)), g2_ref[...])


def _rwkv_prep(x_bf, w_rwkv, mu, w0, a0, w_lora, g2, tm):
    bsz, t, _ = x_bf.shape
    out = jax.ShapeDtypeStruct((bsz, t, RWKV_DIM), F32)
    tok = lambda b, i: (b, i, 0)
    fixed = lambda b, i: (0, 0)
    o_spec = pl.BlockSpec((None, tm, RWKV_DIM), tok)
    return pl.pallas_call(
        _rwkv_prep_kernel,
        grid=(bsz, t // tm),
        in_specs=[
            pl.BlockSpec((None, tm, D_MODEL), tok),
            pl.BlockSpec((D_MODEL, RWKV_COLS), fixed),
            pl.BlockSpec((1, RWKV_COLS), fixed),
            pl.BlockSpec((1, RWKV_DIM), fixed),
            pl.BlockSpec((1, RWKV_DIM), fixed),
            pl.BlockSpec((DECAY_LORA + AAA_LORA, 2 * RWKV_DIM), fixed),
            pl.BlockSpec((GATE_LORA, RWKV_DIM), fixed),
        ],
        out_specs=[o_spec] * 6,
        out_shape=[out] * 6,
        scratch_shapes=[pltpu.VMEM((8, RWKV_COLS), F32)],
        compiler_params=_params("parallel", "arbitrary"),
    )(x_bf, w_rwkv, mu, w0, a0, w_lora, g2)


def _rwkv_scan_kernel(r_ref, k_ref, v_ref, ld_ref, a_ref, g_ref,
                      kk_ref, ka_ref, rk_ref, gng_ref, gnb_ref, o_ref, s_ref):
    rows, chunk = RWKV_BLOCK, RWKV_CHUNK
    n_chunks = rows // chunk

    @pl.when(pl.program_id(2) == 0)
    def _():
        s_ref[...] = jnp.zeros_like(s_ref)

    lane = lax.broadcasted_iota(jnp.int32, (1, LANES), 1)
    m0 = lane < RWKV_HEAD_DIM
    r, k, v, ld, a = r_ref[...], k_ref[...], v_ref[...], ld_ref[...], a_ref[...]

    kk = k * kk_ref[...]
    kk = kk * lax.rsqrt(jnp.maximum(_pair_sum(kk * kk, m0), 1e-12))
    k2 = k * (1.0 + (a - 1.0) * ka_ref[...])
    bv = kk * a

    rin = lax.broadcasted_iota(jnp.int32, (rows, 1), 0) & (chunk - 1)
    cl = ld
    sh = 1
    while sh < chunk:
        cl = cl + jnp.where(rin >= sh, pltpu.roll(cl, sh, axis=0), 0.0)
        sh *= 2
    w_inc = jnp.exp(cl)
    w_inv = jnp.exp(-cl)
    at = -kk * jnp.exp(cl - ld)
    rt = r * w_inc
    bt = _bf(bv * w_inv)
    kt = _bf(k2 * w_inv)
    v_bf = _bf(v)

    at_h = (jnp.where(m0, at, 0.0), jnp.where(m0, 0.0, at))
    rt_h = (jnp.where(m0, rt, 0.0), jnp.where(m0, 0.0, rt))
    lhs = _bf(jnp.concatenate([at_h[0], at_h[1], rt_h[0], rt_h[1]], axis=0))
    rhs = jnp.concatenate([bt, kt], axis=0)
    gram = _dot_nt(lhs, rhs)

    ri = lax.broadcasted_iota(jnp.int32, (rows, rows), 0)
    ci = lax.broadcasted_iota(jnp.int32, (rows, rows), 1)
    same = (ri // chunk) == (ci // chunk)
    strict = same & (ri > ci)
    incl = same & (ri >= ci)
    eye = jnp.where(ri == ci, 1.0, 0.0)

    p_h, u0_h, y0_h, arb_h = [], [], [], []
    for h in range(HEAD_PAIR):
        a_ab = jnp.where(strict, gram[h * rows:(h + 1) * rows, :rows], 0.0)
        a_ak = jnp.where(strict, gram[h * rows:(h + 1) * rows, rows:], 0.0)
        a_rb = jnp.where(incl, gram[(2 + h) * rows:(3 + h) * rows, :rows], 0.0)
        a_rk = jnp.where(incl, gram[(2 + h) * rows:(3 + h) * rows, rows:], 0.0)
        x = a_ab
        t_inv = eye + x
        n = 2
        while n < chunk:
            xb = _bf(x)
            x = _dot(xb, xb)
            t_inv = t_inv + _dot(_bf(t_inv), _bf(x))
            n *= 2
        t_bf = _bf(t_inv)
        u0 = _dot(t_bf, _bf(_dot(_bf(a_ak), v_bf)))
        p = _dot(t_bf, _bf(at_h[h]))
        a_rb = _bf(a_rb)
        y0 = _dot(a_rb, _bf(u0)) + _dot(_bf(a_rk), v_bf)
        p_h.append(p)
        u0_h.append(u0)
        y0_h.append(y0)
        arb_h.append(a_rb)

    p_all = p_h[0] + p_h[1]
    q_all = rt + _dot(arb_h[0], _bf(p_h[0])) + _dot(arb_h[1], _bf(p_h[1]))
    u0_all = jnp.where(m0, u0_h[0], u0_h[1])
    y0_all = jnp.where(m0, y0_h[0], y0_h[1])

    bi = lax.broadcasted_iota(jnp.int32, (LANES, LANES), 0) // RWKV_HEAD_DIM
    bj = lax.broadcasted_iota(jnp.int32, (LANES, LANES), 1) // RWKV_HEAD_DIM
    blockdiag = bi == bj

    s = s_ref[...]
    ys = []
    for c in range(n_chunks):
        lo, hi = c * chunk, (c + 1) * chunk
        bt_c, kt_c = bt[lo:hi], kt[lo:hi]
        m_c = jnp.where(blockdiag, _dot_tn(_bf(p_all[lo:hi]), bt_c), 0.0)
        c_c = jnp.where(blockdiag, _dot_tn(_bf(u0_all[lo:hi]), bt_c) + _dot_tn(v_bf[lo:hi], kt_c), 0.0)
        s_bf = _bf(s)
        ys.append(_dot_nt(_bf(q_all[lo:hi]), s_bf) + y0_all[lo:hi])
        s = (s + _dot(s_bf, _bf(m_c)) + c_c) * w_inc[hi - 1:hi, :]
    s_ref[...] = s
    y = jnp.concatenate(ys, axis=0)

    inv_n = 1.0 / RWKV_HEAD_DIM
    mean = _pair_sum(y, m0) * inv_n
    d = y - mean
    var = _pair_sum(d * d, m0) * inv_n
    yn = d * lax.rsqrt(var + RWKV_GN_EPS) * gng_ref[...] + gnb_ref[...]
    bonus = _pair_sum(r * k2 * rk_ref[...], m0) * v
    o_ref[...] = ((yn + bonus) * g_ref[...]).astype(o_ref.dtype)


def _rwkv_scan(r, k, v, ld, a, g, k_k, k_a, r_k, gn_g, gn_b):
    bsz, t, _ = r.shape
    tok = pl.BlockSpec((None, RWKV_BLOCK, LANES), lambda b, h, c: (b, c, h))
    par = pl.BlockSpec((1, LANES), lambda b, h, c: (0, h))
    return pl.pallas_call(
        _rwkv_scan_kernel,
        grid=(bsz, N_PAIRS, t // RWKV_BLOCK),
        in_specs=[tok] * 6 + [par] * 5,
        out_specs=tok,
        out_shape=jax.ShapeDtypeStruct((bsz, t, RWKV_DIM), BF16),
        scratch_shapes=[pltpu.VMEM((LANES, LANES), F32)],
        compiler_params=_params("parallel", "parallel", "arbitrary"),
    )(r, k, v, ld, a, g, k_k, k_a, r_k, gn_g, gn_b)


def _rope(z, cos_t, sin_lo, sin_hi):
    n = z.shape[-1]
    half = QK_ROPE_DIM // 2
    return z * cos_t + pltpu.roll(z, n - half, axis=1) * sin_lo + pltpu.roll(z, half, axis=1) * sin_hi


def _mla_prep_kernel(x_ref, w_ref, qn_ref, qup_ref, kvn_ref, kvup_ref, cos_ref, slo_ref, shi_ref,
                     q_ref, k_ref, v_ref):
    p = _dot(x_ref[...], w_ref[...])
    c_q = p[:, :Q_LORA_RANK]
    c_kv = p[:, Q_LORA_RANK:Q_LORA_RANK + KV_LORA_RANK]
    kr = p[:, Q_LORA_RANK + KV_LORA_RANK:]
    c_q = c_q * lax.rsqrt(jnp.mean(c_q * c_q, -1, keepdims=True) + RMS_EPS) * qn_ref[...]
    c_kv = c_kv * lax.rsqrt(jnp.mean(c_kv * c_kv, -1, keepdims=True) + RMS_EPS) * kvn_ref[...]
    q = _dot(_bf(c_q), qup_ref[...])
    kv = _dot(_bf(c_kv), kvup_ref[...])
    cos_t, slo, shi = cos_ref[...], slo_ref[...], shi_ref[...]
    reps = MLA_HEADS
    q = _rope(q, jnp.tile(cos_t, (1, reps)), jnp.tile(slo, (1, reps)), jnp.tile(shi, (1, reps)))
    q_ref[...] = (q * ATTN_SCALE).astype(q_ref.dtype)
    kr = _rope(kr, cos_t, slo, shi)
    k_ref[...] = (kv[:, :MLA_HEADS * LANES] + jnp.tile(kr, (1, reps))).astype(k_ref.dtype)
    v_ref[...] = kv[:, MLA_HEADS * LANES:].astype(v_ref.dtype)


def _mla_prep(x_bf, w_mla, q_norm, q_up, kv_norm, kv_up, cos_t, sin_lo, sin_hi, tm):
    bsz, t, _ = x_bf.shape
    tok = lambda b, i: (b, i, 0)
    fixed = lambda b, i: (0, 0)
    tab = pl.BlockSpec((tm, LANES), lambda b, i: (i, 0))
    wq = MLA_HEADS * LANES
    return pl.pallas_call(
        _mla_prep_kernel,
        grid=(bsz, t // tm),
        in_specs=[
            pl.BlockSpec((None, tm, D_MODEL), tok),
            pl.BlockSpec(w_mla.shape, fixed),
            pl.BlockSpec((1, Q_LORA_RANK), fixed),
            pl.BlockSpec(q_up.shape, fixed),
            pl.BlockSpec((1, KV_LORA_RANK), fixed),
            pl.BlockSpec(kv_up.shape, fixed),
            tab, tab, tab,
        ],
        out_specs=[pl.BlockSpec((None, tm, wq), tok), pl.BlockSpec((None, tm, wq), tok),
                   pl.BlockSpec((None, tm, MLA_DIM), tok)],
        out_shape=[jax.ShapeDtypeStruct((bsz, t, wq), BF16), jax.ShapeDtypeStruct((bsz, t, wq), BF16),
                   jax.ShapeDtypeStruct((bsz, t, MLA_DIM), BF16)],
        compiler_params=_params("parallel", "parallel"),
    )(x_bf, w_mla, q_norm, q_up, kv_norm, kv_up, cos_t, sin_lo, sin_hi)


def _attn_kernel(q_ref, k_ref, v_ref, o_ref, m_ref, l_ref, acc_ref, *, tq):
    i = pl.program_id(2)
    lane = lax.broadcasted_iota(jnp.int32, (1, LANES), 1)
    m0 = lane < V_HEAD_DIM
    outs = []
    for h in range(HEAD_PAIR):
        hs = slice(h * LANES, (h + 1) * LANES)
        q = q_ref[:, hs]
        m_ref[...] = jnp.full_like(m_ref, NEG_BIG)
        l_ref[...] = jnp.zeros_like(l_ref)
        acc_ref[...] = jnp.zeros_like(acc_ref)

        def step(j, masked):
            start = pl.multiple_of(j * tq, tq)
            s = _dot_nt(q, k_ref[pl.ds(start, tq), hs])
            if masked:
                qi = lax.broadcasted_iota(jnp.int32, (tq, tq), 0)
                ki = lax.broadcasted_iota(jnp.int32, (tq, tq), 1)
                s = jnp.where(ki <= qi, s, NEG_BIG)
            m_old = m_ref[...]
            m_new = jnp.maximum(m_old, jnp.max(s, -1, keepdims=True))
            alpha = jnp.exp(m_old - m_new)
            p = jnp.exp(s - m_new)
            l_ref[...] = alpha * l_ref[...] + jnp.sum(p, -1, keepdims=True)
            acc_ref[...] = alpha * acc_ref[...] + _dot(_bf(p), v_ref[pl.ds(start, tq), :])
            m_ref[...] = m_new

        def body(j, carry):
            step(j, False)
            return carry

        lax.fori_loop(0, i, body, 0)
        step(i, True)
        outs.append(acc_ref[...] / l_ref[...])
    o_ref[...] = jnp.where(m0, outs[0], outs[1]).astype(o_ref.dtype)


def _attention(q, k, v, tq):
    bsz, t, _ = q.shape
    return pl.pallas_call(
        functools.partial(_attn_kernel, tq=tq),
        grid=(bsz, N_PAIRS, t // tq),
        in_specs=[
            pl.BlockSpec((None, tq, HEAD_PAIR * LANES), lambda b, h, i: (b, i, h)),
            pl.BlockSpec((None, t, HEAD_PAIR * LANES), lambda b, h, i: (b, 0, h)),
            pl.BlockSpec((None, t, LANES), lambda b, h, i: (b, 0, h)),
        ],
        out_specs=pl.BlockSpec((None, tq, LANES), lambda b, h, i: (b, i, h)),
        out_shape=jax.ShapeDtypeStruct((bsz, t, MLA_DIM), BF16),
        scratch_shapes=[pltpu.VMEM((tq, 1), F32), pltpu.VMEM((tq, 1), F32), pltpu.VMEM((tq, LANES), F32)],
        compiler_params=_params("parallel", "parallel", "arbitrary"),
    )(q, k, v)


def _s5_disc_kernel(lre_ref, lim_ref, ls_ref, br_ref, bi_ref, lbr_ref, lbi_ref, bbr_ref, bbi_ref):
    lam_re = jnp.minimum(lre_ref[...], -1e-4)
    lam_im = lim_ref[...]
    step = jnp.exp(ls_ref[...])
    mag = jnp.exp(lam_re * step)
    ang = lam_im * step
    lb_re, lb_im = mag * jnp.cos(ang), mag * jnp.sin(ang)
    den = lam_re * lam_re + lam_im * lam_im
    n_re = lb_re - 1.0
    f_re = (n_re * lam_re + lb_im * lam_im) / den
    f_im = (lb_im * lam_re - n_re * lam_im) / den
    br, bi = br_ref[...], bi_ref[...]
    lbr_ref[...] = lb_re
    lbi_ref[...] = lb_im
    bbr_ref[...] = f_re * br - f_im * bi
    bbi_ref[...] = f_re * bi + f_im * br


def _s5_discretize(lambda_re, lambda_im, log_step, b_re, b_im):
    rep = lambda z: jnp.repeat(z, S5_GROUP, axis=0)
    ls = rep(jnp.broadcast_to(log_step[:, None], (S5_GROUPS, S5_STATE)))
    tr = lambda z: jnp.transpose(z, (0, 2, 1)).reshape(S5_DIM, S5_STATE)
    out = jax.ShapeDtypeStruct((S5_DIM, S5_STATE), F32)
    return pl.pallas_call(_s5_disc_kernel, out_shape=[out] * 4)(
        rep(lambda_re), rep(lambda_im), ls, tr(b_re), tr(b_im))


def _s5_kernel(x_ref, w_ref, bmat_ref, cmat_ref, lam_ref, d_ref, glu_ref, o_ref, carry_ref, *, ts):
    @pl.when(pl.program_id(1) == 0)
    def _():
        carry_ref[...] = jnp.zeros_like(carry_ref)

    n = S5_SLAB_STATE
    u = _dot(x_ref[...], w_ref[...])
    rowi = lax.broadcasted_iota(jnp.int32, (ts, 1), 0)
    ys = []
    for sl in range(S5_SLABS):
        bu = _dot(_bf(u[:, sl * LANES:(sl + 1) * LANES]), bmat_ref[sl])
        lam = lam_ref[sl]
        ar, ai = lam[:, :n], lam[:, n:]
        cr, cim = carry_ref[sl, 0:1, :n], carry_ref[sl, 0:1, n:]
        first = rowi == 0
        hr = bu[:, :n] + jnp.where(first, ar * cr - ai * cim, 0.0)
        hi = bu[:, n:] + jnp.where(first, ar * cim + ai * cr, 0.0)
        sh = 1
        while sh < ts:
            keep = rowi >= sh
            sr = jnp.where(keep, pltpu.roll(hr, sh, axis=0), 0.0)
            si = jnp.where(keep, pltpu.roll(hi, sh, axis=0), 0.0)
            hr, hi = hr + ar * sr - ai * si, hi + ar * si + ai * sr
            ar, ai = ar * ar - ai * ai, 2.0 * ar * ai
            sh *= 2
        carry_ref[sl, 0:1, :n] = hr[ts - 1:ts, :]
        carry_ref[sl, 0:1, n:] = hi[ts - 1:ts, :]
        ys.append(_dot(_bf(jnp.concatenate([hr, hi], axis=1)), cmat_ref[sl]))
    y = jnp.concatenate(ys, axis=1) + d_ref[...] * u
    y = 0.5 * y * (1.0 + jnp.tanh(math.sqrt(2.0 / math.pi) * (y + 0.044715 * (y * y * y))))
    h = _dot(_bf(y), glu_ref[...])
    o_ref[...] = (h[:, :D_MODEL] * _sigmoid(h[:, D_MODEL:])).astype(o_ref.dtype)


def _s5(x_bf, w_s5, bmat, cmat, lam, d_skip, w_glu, ts):
    bsz, t, _ = x_bf.shape
    tok = lambda b, i: (b, i, 0)
    f2 = lambda b, i: (0, 0)
    f3 = lambda b, i: (0, 0, 0)
    return pl.pallas_call(
        functools.partial(_s5_kernel, ts=ts),
        grid=(bsz, t // ts),
        in_specs=[
            pl.BlockSpec((None, ts, D_MODEL), tok),
            pl.BlockSpec(w_s5.shape, f2),
            pl.BlockSpec(bmat.shape, f3),
            pl.BlockSpec(cmat.shape, f3),
            pl.BlockSpec(lam.shape, f3),
            pl.BlockSpec((1, S5_DIM), f2),
            pl.BlockSpec(w_glu.shape, f2),
        ],
        out_specs=pl.BlockSpec((None, ts, D_MODEL), tok),
        out_shape=jax.ShapeDtypeStruct((bsz, t, D_MODEL), BF16),
        scratch_shapes=[pltpu.VMEM((S5_SLABS, 8, 2 * S5_SLAB_STATE), F32)],
        compiler_params=_params("parallel", "arbitrary"),
    )(x_bf, w_s5, bmat, cmat, lam, d_skip, w_glu)


def _s5_matrices(lb_re, lb_im, bb_re, bb_im, c_re, c_im):
    eye = jnp.eye(S5_SLAB_GROUPS, dtype=F32)
    s, g, c, p = S5_SLABS, S5_SLAB_GROUPS, S5_GROUP, S5_STATE

    def b_blocks(bb):
        return jnp.einsum("sgcp,gh->sgchp", bb.reshape(s, g, c, p), eye).reshape(s, g * c, g * p)

    def c_blocks(cc):
        return jnp.einsum("sgcp,gh->sgphc", cc.reshape(s, g, c, p), eye).reshape(s, g * p, g * c)

    bmat = jnp.concatenate([b_blocks(bb_re), b_blocks(bb_im)], axis=2)
    cmat = jnp.concatenate([c_blocks(c_re), -c_blocks(c_im)], axis=1)
    row = lambda z: z[::S5_GROUP].reshape(s, 1, g * p)
    lam = jnp.concatenate([row(lb_re), row(lb_im)], axis=2)
    return _bf(bmat), _bf(cmat), lam


def _merge_kernel(x_ref, xb_ref, ya_ref, ob_ref, yc_ref, wg_ref, gb_ref, wra_ref, wmo_ref, wo_ref,
                  lg_ref, lb_ref, o_ref, ob16_ref, *, alpha):
    gates = _sigmoid(_dot(xb_ref[...], wg_ref[...]) + gb_ref[...])
    y_a = _dot(ya_ref[...], wra_ref[...])
    y_b = _dot(ob_ref[...], wmo_ref[...])
    y_c = yc_ref[...].astype(F32)
    merged = (gates[:, :D_MODEL] * y_a + gates[:, D_MODEL:2 * D_MODEL] * y_b
              + gates[:, 2 * D_MODEL:] * y_c)
    y = alpha * x_ref[...] + _dot(_bf(merged), wo_ref[...])
    out = _layer_norm(y, lg_ref[...], lb_ref[...])
    o_ref[...] = out
    ob16_ref[...] = _bf(out)


def _merge(x, x_bf, ya, ob, yc, w_gate, gate_b, w_ra, w_mo, w_out, ln_g, ln_b, alpha, tm):
    n = x.shape[0]
    tok = lambda i: (i, 0)
    fixed = lambda i: (0, 0)
    full = lambda a: pl.BlockSpec(a.shape, fixed)
    return pl.pallas_call(
        functools.partial(_merge_kernel, alpha=alpha),
        grid=(n // tm,),
        in_specs=[
            pl.BlockSpec((tm, D_MODEL), tok), pl.BlockSpec((tm, D_MODEL), tok),
            pl.BlockSpec((tm, RWKV_DIM), tok), pl.BlockSpec((tm, MLA_DIM), tok),
            pl.BlockSpec((tm, D_MODEL), tok),
            full(w_gate), full(gate_b), full(w_ra), full(w_mo), full(w_out), full(ln_g), full(ln_b),
        ],
        out_specs=[pl.BlockSpec((tm, D_MODEL), tok)] * 2,
        out_shape=[jax.ShapeDtypeStruct((n, D_MODEL), F32), jax.ShapeDtypeStruct((n, D_MODEL), BF16)],
        compiler_params=_params("parallel"),
    )(x, x_bf, ya, ob, yc, w_gate, gate_b, w_ra, w_mo, w_out, ln_g, ln_b)


def _ffn_kernel(x_ref, xb_ref, w1_ref, w3_ref, w2_ref, lg_ref, lb_ref, o_ref, ob16_ref, acc_ref, *, alpha):
    j = pl.program_id(1)

    @pl.when(j == 0)
    def _():
        acc_ref[...] = jnp.zeros_like(acc_ref)

    xb = xb_ref[...]
    h1 = _dot(xb, w1_ref[...])
    h3 = _dot(xb, w3_ref[...])
    h = h1 * _sigmoid(h1) * h3
    acc_ref[...] += _dot(_bf(h), w2_ref[...])

    @pl.when(j == pl.num_programs(1) - 1)
    def _():
        out = _layer_norm(alpha * x_ref[...] + acc_ref[...], lg_ref[...], lb_ref[...])
        o_ref[...] = out
        ob16_ref[...] = _bf(out)


def _ffn(x, x_bf, w1, w3, w2, ln_g, ln_b, alpha, tm, tf):
    n = x.shape[0]
    d_ff = w1.shape[1]
    tok = lambda i, j: (i, 0)
    fixed = lambda i, j: (0, 0)
    return pl.pallas_call(
        functools.partial(_ffn_kernel, alpha=alpha),
        grid=(n // tm, d_ff // tf),
        in_specs=[
            pl.BlockSpec((tm, D_MODEL), tok), pl.BlockSpec((tm, D_MODEL), tok),
            pl.BlockSpec((D_MODEL, tf), lambda i, j: (0, j)),
            pl.BlockSpec((D_MODEL, tf), lambda i, j: (0, j)),
            pl.BlockSpec((tf, D_MODEL), lambda i, j: (j, 0)),
            pl.BlockSpec((1, D_MODEL), fixed), pl.BlockSpec((1, D_MODEL), fixed),
        ],
        out_specs=[pl.BlockSpec((tm, D_MODEL), tok)] * 2,
        out_shape=[jax.ShapeDtypeStruct((n, D_MODEL), F32), jax.ShapeDtypeStruct((n, D_MODEL), BF16)],
        scratch_shapes=[pltpu.VMEM((tm, D_MODEL), F32)],
        compiler_params=_params("parallel", "arbitrary"),
    )(x, x_bf, w1, w3, w2, ln_g, ln_b)


def _rope_tables(t):
    pos = jnp.arange(t, dtype=F32)
    inv_freq = ROPE_THETA ** (-jnp.arange(0, QK_ROPE_DIM, 2, dtype=F32) / QK_ROPE_DIM)
    ang = pos[:, None] * inv_freq[None, :]
    cos, sin = jnp.cos(ang), jnp.sin(ang)
    half = QK_ROPE_DIM // 2
    ones = jnp.ones((t, QK_NOPE_DIM), F32)
    zeros = jnp.zeros((t, QK_NOPE_DIM), F32)
    tail1 = jnp.ones((t, LANES - QK_NOPE_DIM - QK_ROPE_DIM), F32)
    tail0 = jnp.zeros((t, LANES - QK_NOPE_DIM - QK_ROPE_DIM), F32)
    zh = jnp.zeros((t, half), F32)
    cos_t = jnp.concatenate([ones, cos, cos, tail1], axis=1)
    sin_lo = jnp.concatenate([zeros, -sin, zh, tail0], axis=1)
    sin_hi = jnp.concatenate([zeros, zh, sin, tail0], axis=1)
    return cos_t, sin_lo, sin_hi


def _row(z):
    return z.reshape(1, -1).astype(F32)


def _tile(t, want):
    return min(t, want)


def kernel(x, w_in, rwkv_mu, rwkv_w0, rwkv_w2, rwkv_a0, rwkv_a2, rwkv_g2, rwkv_k_k, rwkv_k_a, rwkv_r_k,
           rwkv_gn_g, rwkv_gn_b, rwkv_out, mla_q_norm, mla_q_up, mla_kv_norm, mla_kv_up, mla_out,
           s5_lambda_re, s5_lambda_im, s5_log_step, s5_b_re, s5_b_im, s5_c_re, s5_c_im, s5_d, s5_glu,
           gate_b, w_out, ln1_g, ln1_b, ffn_w1, ffn_w3, ffn_w2, ln2_g, ln2_b):
    bsz, t, _ = x.shape
    depth = w_in.shape[0]
    alpha = (2.0 * depth) ** 0.25
    n = bsz * t
    o_mla = RWKV_COLS
    o_s5 = o_mla + MLA_COLS
    o_gate = o_s5 + S5_COLS
    cos_t, sin_lo, sin_hi = _rope_tables(t)
    d_ff = ffn_w1.shape[2]
    tf = d_ff // 2 if (d_ff // 2) % LANES == 0 else d_ff

    x_bf = _bf(x)
    for l in range(depth):
        wl = w_in[l]
        zl = jnp.zeros((DECAY_LORA, RWKV_DIM), F32)
        w_lora = jnp.concatenate([jnp.concatenate([rwkv_w2[l], zl], axis=1),
                                  jnp.concatenate([zl, rwkv_a2[l]], axis=1)], axis=0)
        r, k, v, ld, a, g = _rwkv_prep(
            x_bf, _bf(wl[:, :o_mla]), _row(rwkv_mu[l]), _row(rwkv_w0[l]), _row(rwkv_a0[l]),
            _bf(w_lora), _bf(rwkv_g2[l]), _tile(t, 512))
        ya = _rwkv_scan(r, k, v, ld, a, g, _row(rwkv_k_k[l]), _row(rwkv_k_a[l]), _row(rwkv_r_k[l]),
                        _row(rwkv_gn_g[l]), _row(rwkv_gn_b[l]))
        nq = Q_LORA_RANK + KV_LORA_RANK
        w_kr = jnp.zeros((D_MODEL, LANES), F32).at[:, QK_NOPE_DIM:QK_NOPE_DIM + QK_ROPE_DIM].set(
            wl[:, o_mla + nq:o_s5])
        w_mla = _bf(jnp.concatenate([wl[:, o_mla:o_mla + nq], w_kr], axis=1))
        q_up = mla_q_up[l].reshape(Q_LORA_RANK, MLA_HEADS, QK_NOPE_DIM + QK_ROPE_DIM)
        q_up = jnp.pad(q_up, ((0, 0), (0, 0), (0, LANES - QK_NOPE_DIM - QK_ROPE_DIM)))
        q_up = _bf(q_up.reshape(Q_LORA_RANK, MLA_HEADS * LANES))
        kv_up = mla_kv_up[l].reshape(KV_LORA_RANK, MLA_HEADS, QK_NOPE_DIM + V_HEAD_DIM)
        k_up = jnp.pad(kv_up[:, :, :QK_NOPE_DIM], ((0, 0), (0, 0), (0, LANES - QK_NOPE_DIM)))
        kv_up = _bf(jnp.concatenate([k_up.reshape(KV_LORA_RANK, MLA_HEADS * LANES),
                                     kv_up[:, :, QK_NOPE_DIM:].reshape(KV_LORA_RANK, MLA_DIM)], axis=1))
        q, kx, vx = _mla_prep(x_bf, w_mla, _row(mla_q_norm[l]), q_up, _row(mla_kv_norm[l]), kv_up,
                              cos_t, sin_lo, sin_hi, _tile(t, 512))
        ob = _attention(q, kx, vx, _tile(t, 256))
        lb_re, lb_im, bb_re, bb_im = _s5_discretize(s5_lambda_re[l], s5_lambda_im[l], s5_log_step[l],
                                                    s5_b_re[l], s5_b_im[l])
        tr = lambda z: z.reshape(S5_DIM, S5_STATE)
        bmat, cmat, lam = _s5_matrices(lb_re, lb_im, bb_re, bb_im, tr(s5_c_re[l]), tr(s5_c_im[l]))
        yc = _s5(x_bf, _bf(wl[:, o_s5:o_gate]), bmat, cmat, lam, _row(s5_d[l]), _bf(s5_glu[l]),
                 _tile(t, 256))
        xf = x.reshape(n, D_MODEL)
        x1, x1_bf = _merge(xf, x_bf.reshape(n, D_MODEL), ya.reshape(n, RWKV_DIM), ob.reshape(n, MLA_DIM),
                           yc.reshape(n, D_MODEL), _bf(wl[:, o_gate:]), _row(gate_b[l]),
                           _bf(rwkv_out[l]), _bf(mla_out[l]), _bf(w_out[l]), _row(ln1_g[l]), _row(ln1_b[l]),
                           alpha, _tile(n, 256))
        x2, x2_bf = _ffn(x1, x1_bf, _bf(ffn_w1[l]), _bf(ffn_w3[l]), _bf(ffn_w2[l]), _row(ln2_g[l]),
                         _row(ln2_b[l]), alpha, _tile(n, 512), tf)
        x = x2.reshape(bsz, t, D_MODEL)
        x_bf = x2_bf.reshape(bsz, t, D_MODEL)
    return x
```

```python
import functools
import math

import jax
import jax.numpy as jnp
from jax import lax
from jax.experimental import pallas as pl
from jax.experimental.pallas import tpu as pltpu

F32 = jnp.float32
BF16 = jnp.bfloat16

D_MODEL = 1024
RWKV_HEADS = 8
RWKV_HEAD_DIM = 64
RWKV_DIM = RWKV_HEADS * RWKV_HEAD_DIM
DECAY_LORA = 64
AAA_LORA = 64
GATE_LORA = 128
RWKV_GN_EPS = 64e-5
MLA_HEADS = 8
QK_NOPE_DIM = 64
QK_ROPE_DIM = 32
V_HEAD_DIM = 64
Q_LORA_RANK = 256
KV_LORA_RANK = 128
MLA_DIM = MLA_HEADS * V_HEAD_DIM
ROPE_THETA = 10000.0
ATTN_SCALE = 1.0 / math.sqrt(QK_NOPE_DIM + QK_ROPE_DIM)
LOG2_E = math.log2(math.e)
S5_DIM = 512
S5_GROUP = 16
S5_GROUPS = S5_DIM // S5_GROUP
S5_STATE = 64
N_BRANCHES = 3
LN_EPS = 1e-5
RMS_EPS = 1e-6
RWKV_COLS = 3 * RWKV_DIM + DECAY_LORA + AAA_LORA + GATE_LORA
MLA_COLS = Q_LORA_RANK + KV_LORA_RANK + QK_ROPE_DIM
S5_COLS = S5_DIM

LANES = 128
HEAD_PAIR = LANES // RWKV_HEAD_DIM
N_PAIRS = RWKV_HEADS // HEAD_PAIR
S5_SLAB_GROUPS = LANES // S5_GROUP
S5_SLABS = S5_GROUPS // S5_SLAB_GROUPS
S5_SLAB_STATE = S5_SLAB_GROUPS * S5_STATE
VMEM_LIMIT = 48 * 1024 * 1024

RWKV_CHUNK = 64
RWKV_BLOCK = 256
NEG_BIG = -1e30


def _bf(x):
    return x.astype(BF16)


def _dot(a, b):
    return jnp.dot(a, b, preferred_element_type=F32)


def _dot_nt(a, b):
    return lax.dot_general(a, b, (((1,), (1,)), ((), ())), preferred_element_type=F32)


def _dot_tn(a, b):
    return lax.dot_general(a, b, (((0,), (0,)), ((), ())), preferred_element_type=F32)


def _sigmoid(x):
    return 1.0 / (1.0 + jnp.exp(-x))


def _params(*sem):
    return pltpu.CompilerParams(dimension_semantics=sem, vmem_limit_bytes=VMEM_LIMIT)


def _layer_norm(y, g, b):
    mu = jnp.mean(y, -1, keepdims=True)
    d = y - mu
    var = jnp.mean(d * d, -1, keepdims=True)
    return d * lax.rsqrt(var + LN_EPS) * g + b


def _pair_sum(x, m0):
    s0 = jnp.sum(jnp.where(m0, x, 0.0), -1, keepdims=True)
    s1 = jnp.sum(jnp.where(m0, 0.0, x), -1, keepdims=True)
    return jnp.where(m0, s0, s1)


def _rwkv_prep_kernel(x_ref, w_ref, mu_ref, w0_ref, a0_ref, wl_ref, g2_ref,
                      r_ref, k_ref, v_ref, ld_ref, a_ref, g_ref, prev_ref):
    t = pl.program_id(1)

    @pl.when(t == 0)
    def _():
        prev_ref[...] = jnp.zeros_like(prev_ref)

    p = _dot(x_ref[...], w_ref[...])
    rolled = pltpu.roll(p, 1, axis=0)
    row = lax.broadcasted_iota(jnp.int32, (p.shape[0], 1), 0)
    shifted = jnp.where(row == 0, prev_ref[0:1, :], rolled)
    prev_ref[0:1, :] = rolled[0:1, :]
    p = p + (shifted - p) * mu_ref[...]

    c0, c1, c2 = RWKV_DIM, 2 * RWKV_DIM, 3 * RWKV_DIM
    r_ref[...] = p[:, :c0]
    k_ref[...] = p[:, c0:c1]
    v_ref[...] = p[:, c1:c2]
    lora = p[:, c2:c2 + DECAY_LORA + AAA_LORA]
    lane = lax.broadcasted_iota(jnp.int32, (1, DECAY_LORA + AAA_LORA), 1)
    lora = jnp.where(lane < DECAY_LORA, jnp.tanh(lora), lora)
    wa = _dot(_bf(lora), wl_ref[...])
    ld_ref[...] = (-math.exp(-0.5)) * _sigmoid(w0_ref[...] + wa[:, :RWKV_DIM])
    a_ref[...] = _sigmoid(a0_ref[...] + wa[:, RWKV_DIM:])
    dg = p[:, c2 + DECAY_LORA + AAA_LORA:]
    g_ref[...] = _dot(_bf(_sigmoid(dg)), g2_ref[...])


def _rwkv_prep(x_bf, w_rwkv, mu, w0, a0, w_lora, g2, tm):
    bsz, t, _ = x_bf.shape
    out = jax.ShapeDtypeStruct((bsz, t, RWKV_DIM), F32)
    tok = lambda b, i: (b, i, 0)
    fixed = lambda b, i: (0, 0)
    o_spec = pl.BlockSpec((None, tm, RWKV_DIM), tok)
    return pl.pallas_call(
        _rwkv_prep_kernel,
        grid=(bsz, t // tm),
        in_specs=[
            pl.BlockSpec((None, tm, D_MODEL), tok),
            pl.BlockSpec((D_MODEL, RWKV_COLS), fixed),
            pl.BlockSpec((1, RWKV_COLS), fixed),
            pl.BlockSpec((1, RWKV_DIM), fixed),
            pl.BlockSpec((1, RWKV_DIM), fixed),
            pl.BlockSpec((DECAY_LORA + AAA_LORA, 2 * RWKV_DIM), fixed),
            pl.BlockSpec((GATE_LORA, RWKV_DIM), fixed),
        ],
        out_specs=[o_spec] * 6,
        out_shape=[out] * 6,
        scratch_shapes=[pltpu.VMEM((8, RWKV_COLS), F32)],
        compiler_params=_params("parallel", "arbitrary"),
        name="rwkv_prep",
    )(x_bf, w_rwkv, mu, w0, a0, w_lora, g2)


def _rwkv_scan_kernel(r_ref, k_ref, v_ref, ld_ref, a_ref, g_ref,
                      kk_ref, ka_ref, rk_ref, gng_ref, gnb_ref, o_ref, s_ref):
    @pl.when(pl.program_id(1) == 0)
    def _():
        s_ref[...] = jnp.zeros_like(s_ref)

    par = (kk_ref[...], ka_ref[...], rk_ref[...], gng_ref[...], gnb_ref[...])
    nb = r_ref.shape[0]
    ins = [(r_ref[b], k_ref[b], v_ref[b], ld_ref[b], a_ref[b], g_ref[b]) for b in range(nb)]
    outs, states = _rwkv_blocks(ins, par, [s_ref[b] for b in range(nb)])
    for b in range(nb):
        o_ref[b] = outs[b].astype(o_ref.dtype)
        s_ref[b] = states[b]


def _rwkv_blocks(ins, par, states):
    rows, chunk = RWKV_BLOCK, RWKV_CHUNK
    n_chunks = rows // chunk
    k_k, k_a, r_k, gn_g, gn_b = par
    nb = len(ins)
    inst = [(b, h) for b in range(nb) for h in range(HEAD_PAIR)]

    lane = lax.broadcasted_iota(jnp.int32, (1, LANES), 1)
    m0 = lane < RWKV_HEAD_DIM
    rin = lax.broadcasted_iota(jnp.int32, (rows, 1), 0) & (chunk - 1)

    pre = []
    for r, k, v, ld, a, g in ins:
        kk = k * k_k
        kk = kk * lax.rsqrt(jnp.maximum(_pair_sum(kk * kk, m0), 1e-12))
        k2 = k * (1.0 + (a - 1.0) * k_a)
        bv = kk * a
        cl = ld
        sh = 1
        while sh < chunk:
            cl = cl + jnp.where(rin >= sh, pltpu.roll(cl, sh, axis=0), 0.0)
            sh *= 2
        w_inc = jnp.exp(cl)
        w_inv = jnp.exp(-cl)
        at = -kk * jnp.exp(cl - ld)
        rt = r * w_inc
        pre.append(dict(
            k2=k2, w_inc=w_inc, rt=rt, bt=_bf(bv * w_inv), kt=_bf(k2 * w_inv), v_bf=_bf(v),
            at_h=(_bf(jnp.where(m0, at, 0.0)), _bf(jnp.where(m0, 0.0, at))),
            rt_h=(_bf(jnp.where(m0, rt, 0.0)), _bf(jnp.where(m0, 0.0, rt)))))

    grams = []
    for d in pre:
        lhs = jnp.concatenate([d["at_h"][0], d["at_h"][1], d["rt_h"][0], d["rt_h"][1]], axis=0)
        rhs = jnp.concatenate([d["bt"], d["kt"]], axis=0)
        grams.append(_dot_nt(lhs, rhs))

    ri = lax.broadcasted_iota(jnp.int32, (rows, rows), 0)
    ci = lax.broadcasted_iota(jnp.int32, (rows, rows), 1)
    same = (ri // chunk) == (ci // chunk)
    strict = same & (ri > ci)
    incl = same & (ri >= ci)
    eye = jnp.where(ri == ci, 1.0, 0.0)

    xs = [jnp.where(strict, grams[b][h * rows:(h + 1) * rows, :rows], 0.0) for b, h in inst]
    t_inv = [eye + x for x in xs]
    n = 2
    while n < chunk:
        xb = [_bf(x) for x in xs]
        xs = [_dot(x, x) for x in xb]
        t_inv = [t + _dot(_bf(t), _bf(x)) for t, x in zip(t_inv, xs)]
        n *= 2
    t_bf = [_bf(t) for t in t_inv]

    akv = [_dot(_bf(jnp.where(strict, grams[b][h * rows:(h + 1) * rows, rows:], 0.0)), pre[b]["v_bf"])
           for b, h in inst]
    u0_h = [_dot(t, _bf(z)) for t, z in zip(t_bf, akv)]
    p_h = [_dot(t, pre[b]["at_h"][h]) for t, (b, h) in zip(t_bf, inst)]
    a_rb = [_bf(jnp.where(incl, grams[b][(2 + h) * rows:(3 + h) * rows, :rows], 0.0)) for b, h in inst]
    a_rk = [_bf(jnp.where(incl, grams[b][(2 + h) * rows:(3 + h) * rows, rows:], 0.0)) for b, h in inst]
    y0_h = [_dot(a_rb[i], _bf(u0_h[i])) + _dot(a_rk[i], pre[b]["v_bf"]) for i, (b, h) in enumerate(inst)]
    qp_h = [_dot(a_rb[i], _bf(p_h[i])) for i in range(len(inst))]

    bi = lax.broadcasted_iota(jnp.int32, (LANES, LANES), 0) // RWKV_HEAD_DIM
    bj = lax.broadcasted_iota(jnp.int32, (LANES, LANES), 1) // RWKV_HEAD_DIM
    blockdiag = bi == bj

    p_all, q_all, u0_all, y0_all = [], [], [], []
    for b in range(nb):
        i0, i1 = HEAD_PAIR * b, HEAD_PAIR * b + 1
        p_all.append(_bf(p_h[i0] + p_h[i1]))
        q_all.append(_bf(pre[b]["rt"] + qp_h[i0] + qp_h[i1]))
        u0_all.append(_bf(jnp.where(m0, u0_h[i0], u0_h[i1])))
        y0_all.append(jnp.where(m0, y0_h[i0], y0_h[i1]))

    m_c, c_c = {}, {}
    for c in range(n_chunks):
        lo, hi = c * chunk, (c + 1) * chunk
        for b in range(nb):
            bt_c, kt_c = pre[b]["bt"][lo:hi], pre[b]["kt"][lo:hi]
            m_c[b, c] = _bf(jnp.where(blockdiag, _dot_tn(p_all[b][lo:hi], bt_c), 0.0))
            c_c[b, c] = jnp.where(
                blockdiag, _dot_tn(u0_all[b][lo:hi], bt_c) + _dot_tn(pre[b]["v_bf"][lo:hi], kt_c), 0.0)

    ys = [[] for _ in range(nb)]
    states = list(states)
    for c in range(n_chunks):
        lo, hi = c * chunk, (c + 1) * chunk
        for b in range(nb):
            s_bf = _bf(states[b])
            ys[b].append(_dot_nt(q_all[b][lo:hi], s_bf) + y0_all[b][lo:hi])
            states[b] = (states[b] + _dot(s_bf, m_c[b, c]) + c_c[b, c]) * pre[b]["w_inc"][hi - 1:hi, :]

    outs = []
    inv_n = 1.0 / RWKV_HEAD_DIM
    for b, (r, k, v, ld, a, g) in enumerate(ins):
        y = jnp.concatenate(ys[b], axis=0)
        mean = _pair_sum(y, m0) * inv_n
        d = y - mean
        var = _pair_sum(d * d, m0) * inv_n
        yn = d * lax.rsqrt(var + RWKV_GN_EPS) * gn_g + gn_b
        bonus = _pair_sum(r * pre[b]["k2"] * r_k, m0) * v
        outs.append((yn + bonus) * g)
    return outs, states


def _rwkv_scan(r, k, v, ld, a, g, k_k, k_a, r_k, gn_g, gn_b):
    bsz, t, _ = r.shape
    tok = pl.BlockSpec((bsz, RWKV_BLOCK, LANES), lambda h, c: (0, c, h))
    par = pl.BlockSpec((1, LANES), lambda h, c: (0, h))
    return pl.pallas_call(
        _rwkv_scan_kernel,
        grid=(N_PAIRS, t // RWKV_BLOCK),
        in_specs=[tok] * 6 + [par] * 5,
        out_specs=tok,
        out_shape=jax.ShapeDtypeStruct((bsz, t, RWKV_DIM), BF16),
        scratch_shapes=[pltpu.VMEM((bsz, LANES, LANES), F32)],
        compiler_params=_params("parallel", "arbitrary"),
        name="rwkv_scan",
    )(r, k, v, ld, a, g, k_k, k_a, r_k, gn_g, gn_b)


def _rope(z, cos_t, sin_lo, sin_hi):
    n = z.shape[-1]
    half = QK_ROPE_DIM // 2
    return z * cos_t + pltpu.roll(z, n - half, axis=1) * sin_lo + pltpu.roll(z, half, axis=1) * sin_hi


def _mla_prep_kernel(x_ref, w_ref, qn_ref, qup_ref, kvn_ref, kvup_ref, cos_ref, slo_ref, shi_ref,
                     q_ref, k_ref, v_ref):
    p = _dot(x_ref[...], w_ref[...])
    c_q = p[:, :Q_LORA_RANK]
    c_kv = p[:, Q_LORA_RANK:Q_LORA_RANK + KV_LORA_RANK]
    kr = p[:, Q_LORA_RANK + KV_LORA_RANK:]
    c_q = c_q * lax.rsqrt(jnp.mean(c_q * c_q, -1, keepdims=True) + RMS_EPS) * qn_ref[...]
    c_kv = c_kv * lax.rsqrt(jnp.mean(c_kv * c_kv, -1, keepdims=True) + RMS_EPS) * kvn_ref[...]
    q = _dot(_bf(c_q), qup_ref[...])
    kv = _dot(_bf(c_kv), kvup_ref[...])
    cos_t, slo, shi = cos_ref[...], slo_ref[...], shi_ref[...]
    reps = MLA_HEADS
    q = _rope(q, jnp.tile(cos_t, (1, reps)), jnp.tile(slo, (1, reps)), jnp.tile(shi, (1, reps)))
    q_ref[...] = (q * (ATTN_SCALE * LOG2_E)).astype(q_ref.dtype)
    kr = _rope(kr, cos_t, slo, shi)
    k_ref[...] = (kv[:, :MLA_HEADS * LANES] + jnp.tile(kr, (1, reps))).astype(k_ref.dtype)
    v_ref[...] = kv[:, MLA_HEADS * LANES:].astype(v_ref.dtype)


def _mla_prep(x_bf, w_mla, q_norm, q_up, kv_norm, kv_up, cos_t, sin_lo, sin_hi, tm):
    bsz, t, _ = x_bf.shape
    tok = lambda b, i: (b, i, 0)
    fixed = lambda b, i: (0, 0)
    tab = pl.BlockSpec((tm, LANES), lambda b, i: (i, 0))
    wq = MLA_HEADS * LANES
    return pl.pallas_call(
        _mla_prep_kernel,
        grid=(bsz, t // tm),
        in_specs=[
            pl.BlockSpec((None, tm, D_MODEL), tok),
            pl.BlockSpec(w_mla.shape, fixed),
            pl.BlockSpec((1, Q_LORA_RANK), fixed),
            pl.BlockSpec(q_up.shape, fixed),
            pl.BlockSpec((1, KV_LORA_RANK), fixed),
            pl.BlockSpec(kv_up.shape, fixed),
            tab, tab, tab,
        ],
        out_specs=[pl.BlockSpec((None, tm, wq), tok), pl.BlockSpec((None, tm, wq), tok),
                   pl.BlockSpec((None, tm, MLA_DIM), tok)],
        out_shape=[jax.ShapeDtypeStruct((bsz, t, wq), BF16), jax.ShapeDtypeStruct((bsz, t, wq), BF16),
                   jax.ShapeDtypeStruct((bsz, t, MLA_DIM), BF16)],
        compiler_params=_params("parallel", "parallel"),
        name="mla_prep",
    )(x_bf, w_mla, q_norm, q_up, kv_norm, kv_up, cos_t, sin_lo, sin_hi)


def _attn_kernel(q_ref, k_ref, v_ref, o_ref, m_ref, l_ref, acc_ref, *, tq):
    i = pl.program_id(2)
    lane = lax.broadcasted_iota(jnp.int32, (1, LANES), 1)
    m0 = lane < V_HEAD_DIM
    reps = tq // LANES
    m_ref[...] = jnp.full_like(m_ref, NEG_BIG)
    l_ref[...] = jnp.zeros_like(l_ref)
    acc_ref[...] = jnp.zeros_like(acc_ref)

    def step(j, masked):
        start = pl.multiple_of(j * tq, tq)
        v_blk = v_ref[pl.ds(start, tq), :]
        for h in range(HEAD_PAIR):
            hs = slice(h * LANES, (h + 1) * LANES)
            s = _dot_nt(q_ref[:, hs], k_ref[pl.ds(start, tq), hs])
            if masked:
                qi = lax.broadcasted_iota(jnp.int32, (tq, tq), 0)
                ki = lax.broadcasted_iota(jnp.int32, (tq, tq), 1)
                s = jnp.where(ki <= qi, s, NEG_BIG)
            m_old = m_ref[h]
            m_new = jnp.maximum(m_old, jnp.max(s, -1, keepdims=True))
            alpha = jnp.exp2(m_old - m_new)
            p = jnp.exp2(s - jnp.tile(m_new, (1, reps)))
            p_sum = p[:, :LANES]
            for c in range(1, reps):
                p_sum = p_sum + p[:, c * LANES:(c + 1) * LANES]
            l_ref[h] = alpha * l_ref[h] + p_sum
            acc_ref[h] = alpha * acc_ref[h] + _dot(_bf(p), v_blk)
            m_ref[h] = m_new

    def body(j, carry):
        step(j, False)
        return carry

    lax.fori_loop(0, i, body, 0)
    step(i, True)
    l0 = jnp.sum(l_ref[0], -1, keepdims=True)
    l1 = jnp.sum(l_ref[1], -1, keepdims=True)
    o_ref[...] = jnp.where(m0, acc_ref[0] / l0, acc_ref[1] / l1).astype(o_ref.dtype)


def _attention(q, k, v, tq):
    bsz, t, _ = q.shape
    return pl.pallas_call(
        functools.partial(_attn_kernel, tq=tq),
        grid=(bsz, N_PAIRS, t // tq),
        in_specs=[
            pl.BlockSpec((None, tq, HEAD_PAIR * LANES), lambda b, h, i: (b, i, h)),
            pl.BlockSpec((None, t, HEAD_PAIR * LANES), lambda b, h, i: (b, 0, h)),
            pl.BlockSpec((None, t, LANES), lambda b, h, i: (b, 0, h)),
        ],
        out_specs=pl.BlockSpec((None, tq, LANES), lambda b, h, i: (b, i, h)),
        out_shape=jax.ShapeDtypeStruct((bsz, t, MLA_DIM), BF16),
        scratch_shapes=[pltpu.VMEM((HEAD_PAIR, tq, LANES), F32)] * 3,
        compiler_params=_params("parallel", "parallel", "arbitrary"),
        name="mla_attention",
    )(q, k, v)


def _s5_disc_kernel(lre_ref, lim_ref, ls_ref, br_ref, bi_ref, lbr_ref, lbi_ref, bbr_ref, bbi_ref):
    lam_re = jnp.minimum(lre_ref[...], -1e-4)
    lam_im = lim_ref[...]
    step = jnp.exp(ls_ref[...])
    mag = jnp.exp(lam_re * step)
    ang = lam_im * step
    lb_re, lb_im = mag * jnp.cos(ang), mag * jnp.sin(ang)
    den = lam_re * lam_re + lam_im * lam_im
    n_re = lb_re - 1.0
    f_re = (n_re * lam_re + lb_im * lam_im) / den
    f_im = (lb_im * lam_re - n_re * lam_im) / den
    br, bi = br_ref[...], bi_ref[...]
    lbr_ref[...] = lb_re
    lbi_ref[...] = lb_im
    bbr_ref[...] = f_re * br - f_im * bi
    bbi_ref[...] = f_re * bi + f_im * br


def _s5_discretize(lambda_re, lambda_im, log_step, b_re, b_im):
    rep = lambda z: jnp.repeat(z, S5_GROUP, axis=0)
    ls = rep(jnp.broadcast_to(log_step[:, None], (S5_GROUPS, S5_STATE)))
    tr = lambda z: jnp.transpose(z, (0, 2, 1)).reshape(S5_DIM, S5_STATE)
    out = jax.ShapeDtypeStruct((S5_DIM, S5_STATE), F32)
    return pl.pallas_call(_s5_disc_kernel, out_shape=[out] * 4, name="s5_discretize")(
        rep(lambda_re), rep(lambda_im), ls, tr(b_re), tr(b_im))


def _s5_kernel(x_ref, w_ref, bmat_ref, cmat_ref, lam_ref, d_ref, glu_ref, o_ref, carry_ref, *, ts):
    @pl.when(pl.program_id(1) == 0)
    def _():
        carry_ref[...] = jnp.zeros_like(carry_ref)

    n = S5_SLAB_STATE
    u = _dot(x_ref[...], w_ref[...])
    rowi = lax.broadcasted_iota(jnp.int32, (ts, 1), 0)
    ys = []
    for sl in range(S5_SLABS):
        bu = _dot(_bf(u[:, sl * LANES:(sl + 1) * LANES]), bmat_ref[sl])
        lam = lam_ref[sl]
        ar, ai = lam[:, :n], lam[:, n:]
        cr, cim = carry_ref[sl, 0:1, :n], carry_ref[sl, 0:1, n:]
        first = rowi == 0
        hr = bu[:, :n] + jnp.where(first, ar * cr - ai * cim, 0.0)
        hi = bu[:, n:] + jnp.where(first, ar * cim + ai * cr, 0.0)
        sh = 1
        while sh < ts:
            keep = rowi >= sh
            sr = jnp.where(keep, pltpu.roll(hr, sh, axis=0), 0.0)
            si = jnp.where(keep, pltpu.roll(hi, sh, axis=0), 0.0)
            hr, hi = hr + ar * sr - ai * si, hi + ar * si + ai * sr
            ar, ai = ar * ar - ai * ai, 2.0 * ar * ai
            sh *= 2
        carry_ref[sl, 0:1, :n] = hr[ts - 1:ts, :]
        carry_ref[sl, 0:1, n:] = hi[ts - 1:ts, :]
        ys.append(_dot(_bf(jnp.concatenate([hr, hi], axis=1)), cmat_ref[sl]))
    y = jnp.concatenate(ys, axis=1) + d_ref[...] * u
    y = 0.5 * y * (1.0 + jnp.tanh(math.sqrt(2.0 / math.pi) * (y + 0.044715 * (y * y * y))))
    h = _dot(_bf(y), glu_ref[...])
    o_ref[...] = (h[:, :D_MODEL] * _sigmoid(h[:, D_MODEL:])).astype(o_ref.dtype)


def _s5(x_bf, w_s5, bmat, cmat, lam, d_skip, w_glu, ts):
    bsz, t, _ = x_bf.shape
    tok = lambda b, i: (b, i, 0)
    f2 = lambda b, i: (0, 0)
    f3 = lambda b, i: (0, 0, 0)
    return pl.pallas_call(
        functools.partial(_s5_kernel, ts=ts),
        grid=(bsz, t // ts),
        in_specs=[
            pl.BlockSpec((None, ts, D_MODEL), tok),
            pl.BlockSpec(w_s5.shape, f2),
            pl.BlockSpec(bmat.shape, f3),
            pl.BlockSpec(cmat.shape, f3),
            pl.BlockSpec(lam.shape, f3),
            pl.BlockSpec((1, S5_DIM), f2),
            pl.BlockSpec(w_glu.shape, f2),
        ],
        out_specs=pl.BlockSpec((None, ts, D_MODEL), tok),
        out_shape=jax.ShapeDtypeStruct((bsz, t, D_MODEL), BF16),
        scratch_shapes=[pltpu.VMEM((S5_SLABS, 8, 2 * S5_SLAB_STATE), F32)],
        compiler_params=_params("parallel", "arbitrary"),
        name="s5_scan",
    )(x_bf, w_s5, bmat, cmat, lam, d_skip, w_glu)


def _s5_matrices(lb_re, lb_im, bb_re, bb_im, c_re, c_im):
    eye = jnp.eye(S5_SLAB_GROUPS, dtype=F32)
    s, g, c, p = S5_SLABS, S5_SLAB_GROUPS, S5_GROUP, S5_STATE

    def b_blocks(bb):
        return jnp.einsum("sgcp,gh->sgchp", bb.reshape(s, g, c, p), eye).reshape(s, g * c, g * p)

    def c_blocks(cc):
        return jnp.einsum("sgcp,gh->sgphc", cc.reshape(s, g, c, p), eye).reshape(s, g * p, g * c)

    bmat = jnp.concatenate([b_blocks(bb_re), b_blocks(bb_im)], axis=2)
    cmat = jnp.concatenate([c_blocks(c_re), -c_blocks(c_im)], axis=1)
    row = lambda z: z[::S5_GROUP].reshape(s, 1, g * p)
    lam = jnp.concatenate([row(lb_re), row(lb_im)], axis=2)
    return _bf(bmat), _bf(cmat), lam


def _merge_kernel(x_ref, xb_ref, ya_ref, ob_ref, yc_ref, wg_ref, gb_ref, wra_ref, wmo_ref, wo_ref,
                  lg_ref, lb_ref, o_ref, ob16_ref, *, alpha):
    gates = _sigmoid(_dot(xb_ref[...], wg_ref[...]) + gb_ref[...])
    y_a = _dot(ya_ref[...], wra_ref[...])
    y_b = _dot(ob_ref[...], wmo_ref[...])
    y_c = yc_ref[...].astype(F32)
    merged = (gates[:, :D_MODEL] * y_a + gates[:, D_MODEL:2 * D_MODEL] * y_b
              + gates[:, 2 * D_MODEL:] * y_c)
    y = alpha * x_ref[...] + _dot(_bf(merged), wo_ref[...])
    out = _layer_norm(y, lg_ref[...], lb_ref[...])
    o_ref[...] = out
    ob16_ref[...] = _bf(out)


def _merge(x, x_bf, ya, ob, yc, w_gate, gate_b, w_ra, w_mo, w_out, ln_g, ln_b, alpha, tm):
    n = x.shape[0]
    tok = lambda i: (i, 0)
    fixed = lambda i: (0, 0)
    full = lambda a: pl.BlockSpec(a.shape, fixed)
    return pl.pallas_call(
        functools.partial(_merge_kernel, alpha=alpha),
        grid=(n // tm,),
        in_specs=[
            pl.BlockSpec((tm, D_MODEL), tok), pl.BlockSpec((tm, D_MODEL), tok),
            pl.BlockSpec((tm, RWKV_DIM), tok), pl.BlockSpec((tm, MLA_DIM), tok),
            pl.BlockSpec((tm, D_MODEL), tok),
            full(w_gate), full(gate_b), full(w_ra), full(w_mo), full(w_out), full(ln_g), full(ln_b),
        ],
        out_specs=[pl.BlockSpec((tm, D_MODEL), tok)] * 2,
        out_shape=[jax.ShapeDtypeStruct((n, D_MODEL), F32), jax.ShapeDtypeStruct((n, D_MODEL), BF16)],
        compiler_params=_params("parallel"),
        name="merge_ln",
    )(x, x_bf, ya, ob, yc, w_gate, gate_b, w_ra, w_mo, w_out, ln_g, ln_b)


def _ffn_kernel(x_ref, xb_ref, w1_ref, w3_ref, w2_ref, lg_ref, lb_ref, o_ref, ob16_ref, acc_ref, *, alpha):
    j = pl.program_id(1)

    @pl.when(j == 0)
    def _():
        acc_ref[...] = jnp.zeros_like(acc_ref)

    xb = xb_ref[...]
    h1 = _dot(xb, w1_ref[...])
    h3 = _dot(xb, w3_ref[...])
    h = h1 * _sigmoid(h1) * h3
    acc_ref[...] += _dot(_bf(h), w2_ref[...])

    @pl.when(j == pl.num_programs(1) - 1)
    def _():
        out = _layer_norm(alpha * x_ref[...] + acc_ref[...], lg_ref[...], lb_ref[...])
        o_ref[...] = out
        ob16_ref[...] = _bf(out)


def _ffn(x, x_bf, w1, w3, w2, ln_g, ln_b, alpha, tm, tf):
    n = x.shape[0]
    d_ff = w1.shape[1]
    tok = lambda i, j: (i, 0)
    fixed = lambda i, j: (0, 0)
    return pl.pallas_call(
        functools.partial(_ffn_kernel, alpha=alpha),
        grid=(n // tm, d_ff // tf),
        in_specs=[
            pl.BlockSpec((tm, D_MODEL), tok), pl.BlockSpec((tm, D_MODEL), tok),
            pl.BlockSpec((D_MODEL, tf), lambda i, j: (0, j)),
            pl.BlockSpec((D_MODEL, tf), lambda i, j: (0, j)),
            pl.BlockSpec((tf, D_MODEL), lambda i, j: (j, 0)),
            pl.BlockSpec((1, D_MODEL), fixed), pl.BlockSpec((1, D_MODEL), fixed),
        ],
        out_specs=[pl.BlockSpec((tm, D_MODEL), tok)] * 2,
        out_shape=[jax.ShapeDtypeStruct((n, D_MODEL), F32), jax.ShapeDtypeStruct((n, D_MODEL), BF16)],
        scratch_shapes=[pltpu.VMEM((tm, D_MODEL), F32)],
        compiler_params=_params("parallel", "arbitrary"),
        name="ffn_ln",
    )(x, x_bf, w1, w3, w2, ln_g, ln_b)


def _rope_tables(t):
    pos = jnp.arange(t, dtype=F32)
    inv_freq = ROPE_THETA ** (-jnp.arange(0, QK_ROPE_DIM, 2, dtype=F32) / QK_ROPE_DIM)
    ang = pos[:, None] * inv_freq[None, :]
    cos, sin = jnp.cos(ang), jnp.sin(ang)
    half = QK_ROPE_DIM // 2
    ones = jnp.ones((t, QK_NOPE_DIM), F32)
    zeros = jnp.zeros((t, QK_NOPE_DIM), F32)
    tail1 = jnp.ones((t, LANES - QK_NOPE_DIM - QK_ROPE_DIM), F32)
    tail0 = jnp.zeros((t, LANES - QK_NOPE_DIM - QK_ROPE_DIM), F32)
    zh = jnp.zeros((t, half), F32)
    cos_t = jnp.concatenate([ones, cos, cos, tail1], axis=1)
    sin_lo = jnp.concatenate([zeros, -sin, zh, tail0], axis=1)
    sin_hi = jnp.concatenate([zeros, zh, sin, tail0], axis=1)
    return cos_t, sin_lo, sin_hi


def _row(z):
    return z.reshape(1, -1).astype(F32)


def _tile(t, want):
    return min(t, want)


def kernel(x, w_in, rwkv_mu, rwkv_w0, rwkv_w2, rwkv_a0, rwkv_a2, rwkv_g2, rwkv_k_k, rwkv_k_a, rwkv_r_k,
           rwkv_gn_g, rwkv_gn_b, rwkv_out, mla_q_norm, mla_q_up, mla_kv_norm, mla_kv_up, mla_out,
           s5_lambda_re, s5_lambda_im, s5_log_step, s5_b_re, s5_b_im, s5_c_re, s5_c_im, s5_d, s5_glu,
           gate_b, w_out, ln1_g, ln1_b, ffn_w1, ffn_w3, ffn_w2, ln2_g, ln2_b):
    bsz, t, _ = x.shape
    depth = w_in.shape[0]
    alpha = (2.0 * depth) ** 0.25
    n = bsz * t
    o_mla = RWKV_COLS
    o_s5 = o_mla + MLA_COLS
    o_gate = o_s5 + S5_COLS
    cos_t, sin_lo, sin_hi = _rope_tables(t)
    d_ff = ffn_w1.shape[2]
    tf = d_ff // 2 if (d_ff // 2) % LANES == 0 else d_ff

    x_bf = _bf(x)
    for l in range(depth):
        wl = w_in[l]
        zl = jnp.zeros((DECAY_LORA, RWKV_DIM), F32)
        w_lora = jnp.concatenate([jnp.concatenate([rwkv_w2[l], zl], axis=1),
                                  jnp.concatenate([zl, rwkv_a2[l]], axis=1)], axis=0)
        r, k, v, ld, a, g = _rwkv_prep(
            x_bf, _bf(wl[:, :o_mla]), _row(rwkv_mu[l]), _row(rwkv_w0[l]), _row(rwkv_a0[l]),
            _bf(w_lora), _bf(rwkv_g2[l]), _tile(t, 512))
        ya = _rwkv_scan(r, k, v, ld, a, g, _row(rwkv_k_k[l]), _row(rwkv_k_a[l]), _row(rwkv_r_k[l]),
                        _row(rwkv_gn_g[l]), _row(rwkv_gn_b[l]))
        nq = Q_LORA_RANK + KV_LORA_RANK
        w_kr = jnp.zeros((D_MODEL, LANES), F32).at[:, QK_NOPE_DIM:QK_NOPE_DIM + QK_ROPE_DIM].set(
            wl[:, o_mla + nq:o_s5])
        w_mla = _bf(jnp.concatenate([wl[:, o_mla:o_mla + nq], w_kr], axis=1))
        q_up = mla_q_up[l].reshape(Q_LORA_RANK, MLA_HEADS, QK_NOPE_DIM + QK_ROPE_DIM)
        q_up = jnp.pad(q_up, ((0, 0), (0, 0), (0, LANES - QK_NOPE_DIM - QK_ROPE_DIM)))
        q_up = _bf(q_up.reshape(Q_LORA_RANK, MLA_HEADS * LANES))
        kv_up = mla_kv_up[l].reshape(KV_LORA_RANK, MLA_HEADS, QK_NOPE_DIM + V_HEAD_DIM)
        k_up = jnp.pad(kv_up[:, :, :QK_NOPE_DIM], ((0, 0), (0, 0), (0, LANES - QK_NOPE_DIM)))
        kv_up = _bf(jnp.concatenate([k_up.reshape(KV_LORA_RANK, MLA_HEADS * LANES),
                                     kv_up[:, :, QK_NOPE_DIM:].reshape(KV_LORA_RANK, MLA_DIM)], axis=1))
        q, kx, vx = _mla_prep(x_bf, w_mla, _row(mla_q_norm[l]), q_up, _row(mla_kv_norm[l]), kv_up,
                              cos_t, sin_lo, sin_hi, _tile(t, 512))
        ob = _attention(q, kx, vx, _tile(t, 512))
        lb_re, lb_im, bb_re, bb_im = _s5_discretize(s5_lambda_re[l], s5_lambda_im[l], s5_log_step[l],
                                                    s5_b_re[l], s5_b_im[l])
        tr = lambda z: z.reshape(S5_DIM, S5_STATE)
        bmat, cmat, lam = _s5_matrices(lb_re, lb_im, bb_re, bb_im, tr(s5_c_re[l]), tr(s5_c_im[l]))
        yc = _s5(x_bf, _bf(wl[:, o_s5:o_gate]), bmat, cmat, lam, _row(s5_d[l]), _bf(s5_glu[l]),
                 _tile(t, 256))
        xf = x.reshape(n, D_MODEL)
        x1, x1_bf = _merge(xf, x_bf.reshape(n, D_MODEL), ya.reshape(n, RWKV_DIM), ob.reshape(n, MLA_DIM),
                           yc.reshape(n, D_MODEL), _bf(wl[:, o_gate:]), _row(gate_b[l]),
                           _bf(rwkv_out[l]), _bf(mla_out[l]), _bf(w_out[l]), _row(ln1_g[l]), _row(ln1_b[l]),
                           alpha, _tile(n, 256))
        x2, x2_bf = _ffn(x1, x1_bf, _bf(ffn_w1[l]), _bf(ffn_w3[l]), _bf(ffn_w2[l]), _row(ln2_g[l]),
                         _row(ln2_b[l]), alpha, _tile(n, 512), tf)
        x = x2.reshape(bsz, t, D_MODEL)
        x_bf = x2_bf.reshape(bsz, t, D_MODEL)
    return x
```

```python
import functools
import math

import jax
import jax.numpy as jnp
from jax import lax
from jax.experimental import pallas as pl
from jax.experimental.pallas import tpu as pltpu

F32 = jnp.float32
BF16 = jnp.bfloat16

D_MODEL = 1024
RWKV_HEADS = 8
RWKV_HEAD_DIM = 64
RWKV_DIM = RWKV_HEADS * RWKV_HEAD_DIM
DECAY_LORA = 64
AAA_LORA = 64
GATE_LORA = 128
RWKV_GN_EPS = 64e-5
MLA_HEADS = 8
QK_NOPE_DIM = 64
QK_ROPE_DIM = 32
V_HEAD_DIM = 64
Q_LORA_RANK = 256
KV_LORA_RANK = 128
MLA_DIM = MLA_HEADS * V_HEAD_DIM
ROPE_THETA = 10000.0
ATTN_SCALE = 1.0 / math.sqrt(QK_NOPE_DIM + QK_ROPE_DIM)
LOG2_E = math.log2(math.e)
S5_DIM = 512
S5_GROUP = 16
S5_GROUPS = S5_DIM // S5_GROUP
S5_STATE = 64
N_BRANCHES = 3
LN_EPS = 1e-5
RMS_EPS = 1e-6
RWKV_COLS = 3 * RWKV_DIM + DECAY_LORA + AAA_LORA + GATE_LORA
MLA_COLS = Q_LORA_RANK + KV_LORA_RANK + QK_ROPE_DIM
S5_COLS = S5_DIM

LANES = 128
HEAD_PAIR = LANES // RWKV_HEAD_DIM
N_PAIRS = RWKV_HEADS // HEAD_PAIR
S5_SLAB_GROUPS = LANES // S5_GROUP
S5_SLABS = S5_GROUPS // S5_SLAB_GROUPS
S5_SLAB_STATE = S5_SLAB_GROUPS * S5_STATE
S5_CHUNK = 8
VMEM_LIMIT = 48 * 1024 * 1024

RWKV_CHUNK = 64
RWKV_BLOCK = 256
NEG_BIG = -1e30


def _bf(x):
    return x.astype(BF16)


def _dot(a, b):
    return jnp.dot(a, b, preferred_element_type=F32)


def _dot_nt(a, b):
    return lax.dot_general(a, b, (((1,), (1,)), ((), ())), preferred_element_type=F32)


def _dot_tn(a, b):
    return lax.dot_general(a, b, (((0,), (0,)), ((), ())), preferred_element_type=F32)


def _sigmoid(x):
    return 1.0 / (1.0 + jnp.exp(-x))


def _params(*sem):
    return pltpu.CompilerParams(dimension_semantics=sem, vmem_limit_bytes=VMEM_LIMIT)


def _layer_norm(y, g, b):
    mu = jnp.mean(y, -1, keepdims=True)
    d = y - mu
    var = jnp.mean(d * d, -1, keepdims=True)
    return d * lax.rsqrt(var + LN_EPS) * g + b


def _pair_sum(x, m0):
    s0 = jnp.sum(jnp.where(m0, x, 0.0), -1, keepdims=True)
    s1 = jnp.sum(jnp.where(m0, 0.0, x), -1, keepdims=True)
    return jnp.where(m0, s0, s1)


def _rwkv_prep_kernel(x_ref, w_ref, mu_ref, w0_ref, a0_ref, wl_ref, g2_ref,
                      r_ref, k_ref, v_ref, ld_ref, a_ref, g_ref, prev_ref):
    t = pl.program_id(1)

    @pl.when(t == 0)
    def _():
        prev_ref[...] = jnp.zeros_like(prev_ref)

    p = _dot(x_ref[...], w_ref[...])
    rolled = pltpu.roll(p, 1, axis=0)
    row = lax.broadcasted_iota(jnp.int32, (p.shape[0], 1), 0)
    shifted = jnp.where(row == 0, prev_ref[0:1, :], rolled)
    prev_ref[0:1, :] = rolled[0:1, :]
    p = p + (shifted - p) * mu_ref[...]

    c0, c1, c2 = RWKV_DIM, 2 * RWKV_DIM, 3 * RWKV_DIM
    r_ref[...] = p[:, :c0]
    k_ref[...] = p[:, c0:c1]
    v_ref[...] = p[:, c1:c2]
    lora = p[:, c2:c2 + DECAY_LORA + AAA_LORA]
    lane = lax.broadcasted_iota(jnp.int32, (1, DECAY_LORA + AAA_LORA), 1)
    lora = jnp.where(lane < DECAY_LORA, jnp.tanh(lora), lora)
    wa = _dot(_bf(lora), wl_ref[...])
    ld_ref[...] = (-math.exp(-0.5)) * _sigmoid(w0_ref[...] + wa[:, :RWKV_DIM])
    a_ref[...] = _sigmoid(a0_ref[...] + wa[:, RWKV_DIM:])
    dg = p[:, c2 + DECAY_LORA + AAA_LORA:]
    g_ref[...] = _dot(_bf(_sigmoid(dg)), g2_ref[...])


def _rwkv_prep(x_bf, w_rwkv, mu, w0, a0, w_lora, g2, tm):
    bsz, t, _ = x_bf.shape
    out = jax.ShapeDtypeStruct((bsz, t, RWKV_DIM), F32)
    tok = lambda b, i: (b, i, 0)
    fixed = lambda b, i: (0, 0)
    o_spec = pl.BlockSpec((None, tm, RWKV_DIM), tok)
    return pl.pallas_call(
        _rwkv_prep_kernel,
        grid=(bsz, t // tm),
        in_specs=[
            pl.BlockSpec((None, tm, D_MODEL), tok),
            pl.BlockSpec((D_MODEL, RWKV_COLS), fixed),
            pl.BlockSpec((1, RWKV_COLS), fixed),
            pl.BlockSpec((1, RWKV_DIM), fixed),
            pl.BlockSpec((1, RWKV_DIM), fixed),
            pl.BlockSpec((DECAY_LORA + AAA_LORA, 2 * RWKV_DIM), fixed),
            pl.BlockSpec((GATE_LORA, RWKV_DIM), fixed),
        ],
        out_specs=[o_spec] * 6,
        out_shape=[out] * 6,
        scratch_shapes=[pltpu.VMEM((8, RWKV_COLS), F32)],
        compiler_params=_params("parallel", "arbitrary"),
        name="rwkv_prep",
    )(x_bf, w_rwkv, mu, w0, a0, w_lora, g2)


def _rwkv_scan_kernel(r_ref, k_ref, v_ref, ld_ref, a_ref, g_ref,
                      kk_ref, ka_ref, rk_ref, gng_ref, gnb_ref, o_ref, s_ref):
    @pl.when(pl.program_id(1) == 0)
    def _():
        s_ref[...] = jnp.zeros_like(s_ref)

    par = (kk_ref[...], ka_ref[...], rk_ref[...], gng_ref[...], gnb_ref[...])
    nb = r_ref.shape[0]
    ins = [(r_ref[b], k_ref[b], v_ref[b], ld_ref[b], a_ref[b], g_ref[b]) for b in range(nb)]
    outs, states = _rwkv_blocks(ins, par, [s_ref[b] for b in range(nb)])
    for b in range(nb):
        o_ref[b] = outs[b].astype(o_ref.dtype)
        s_ref[b] = states[b]


def _rwkv_blocks(ins, par, states):
    rows, chunk = RWKV_BLOCK, RWKV_CHUNK
    n_chunks = rows // chunk
    k_k, k_a, r_k, gn_g, gn_b = par
    nb = len(ins)
    inst = [(b, h) for b in range(nb) for h in range(HEAD_PAIR)]

    lane = lax.broadcasted_iota(jnp.int32, (1, LANES), 1)
    m0 = lane < RWKV_HEAD_DIM
    rin = lax.broadcasted_iota(jnp.int32, (rows, 1), 0) & (chunk - 1)

    pre = []
    for r, k, v, ld, a, g in ins:
        kk = k * k_k
        kk = kk * lax.rsqrt(jnp.maximum(_pair_sum(kk * kk, m0), 1e-12))
        k2 = k * (1.0 + (a - 1.0) * k_a)
        bv = kk * a
        cl = ld
        sh = 1
        while sh < chunk:
            cl = cl + jnp.where(rin >= sh, pltpu.roll(cl, sh, axis=0), 0.0)
            sh *= 2
        w_inc = jnp.exp(cl)
        w_inv = jnp.exp(-cl)
        at = -kk * jnp.exp(cl - ld)
        rt = r * w_inc
        pre.append(dict(
            k2=k2, w_inc=w_inc, rt=rt, bt=_bf(bv * w_inv), kt=_bf(k2 * w_inv), v_bf=_bf(v),
            at_h=(_bf(jnp.where(m0, at, 0.0)), _bf(jnp.where(m0, 0.0, at))),
            rt_h=(_bf(jnp.where(m0, rt, 0.0)), _bf(jnp.where(m0, 0.0, rt)))))

    grams = []
    for d in pre:
        lhs = jnp.concatenate([d["at_h"][0], d["at_h"][1], d["rt_h"][0], d["rt_h"][1]], axis=0)
        rhs = jnp.concatenate([d["bt"], d["kt"]], axis=0)
        grams.append(_dot_nt(lhs, rhs))

    ri = lax.broadcasted_iota(jnp.int32, (rows, rows), 0)
    ci = lax.broadcasted_iota(jnp.int32, (rows, rows), 1)
    same = (ri // chunk) == (ci // chunk)
    strict = same & (ri > ci)
    incl = same & (ri >= ci)
    eye = jnp.where(ri == ci, 1.0, 0.0)

    xs = [jnp.where(strict, grams[b][h * rows:(h + 1) * rows, :rows], 0.0) for b, h in inst]
    t_inv = [eye + x for x in xs]
    n = 2
    while n < chunk:
        xb = [_bf(x) for x in xs]
        xs = [_dot(x, x) for x in xb]
        t_inv = [t + _dot(_bf(t), _bf(x)) for t, x in zip(t_inv, xs)]
        n *= 2
    t_bf = [_bf(t) for t in t_inv]

    akv = [_dot(_bf(jnp.where(strict, grams[b][h * rows:(h + 1) * rows, rows:], 0.0)), pre[b]["v_bf"])
           for b, h in inst]
    u0_h = [_dot(t, _bf(z)) for t, z in zip(t_bf, akv)]
    p_h = [_dot(t, pre[b]["at_h"][h]) for t, (b, h) in zip(t_bf, inst)]
    a_rb = [_bf(jnp.where(incl, grams[b][(2 + h) * rows:(3 + h) * rows, :rows], 0.0)) for b, h in inst]
    a_rk = [_bf(jnp.where(incl, grams[b][(2 + h) * rows:(3 + h) * rows, rows:], 0.0)) for b, h in inst]
    y0_h = [_dot(a_rb[i], _bf(u0_h[i])) + _dot(a_rk[i], pre[b]["v_bf"]) for i, (b, h) in enumerate(inst)]
    qp_h = [_dot(a_rb[i], _bf(p_h[i])) for i in range(len(inst))]

    bi = lax.broadcasted_iota(jnp.int32, (LANES, LANES), 0) // RWKV_HEAD_DIM
    bj = lax.broadcasted_iota(jnp.int32, (LANES, LANES), 1) // RWKV_HEAD_DIM
    blockdiag = bi == bj

    p_all, q_all, u0_all, y0_all = [], [], [], []
    for b in range(nb):
        i0, i1 = HEAD_PAIR * b, HEAD_PAIR * b + 1
        p_all.append(_bf(p_h[i0] + p_h[i1]))
        q_all.append(_bf(pre[b]["rt"] + qp_h[i0] + qp_h[i1]))
        u0_all.append(_bf(jnp.where(m0, u0_h[i0], u0_h[i1])))
        y0_all.append(jnp.where(m0, y0_h[i0], y0_h[i1]))

    m_c, c_c = {}, {}
    for c in range(n_chunks):
        lo, hi = c * chunk, (c + 1) * chunk
        for b in range(nb):
            bt_c, kt_c = pre[b]["bt"][lo:hi], pre[b]["kt"][lo:hi]
            m_c[b, c] = _bf(jnp.where(blockdiag, _dot_tn(p_all[b][lo:hi], bt_c), 0.0))
            c_c[b, c] = jnp.where(
                blockdiag, _dot_tn(u0_all[b][lo:hi], bt_c) + _dot_tn(pre[b]["v_bf"][lo:hi], kt_c), 0.0)

    ys = [[] for _ in range(nb)]
    states = list(states)
    for c in range(n_chunks):
        lo, hi = c * chunk, (c + 1) * chunk
        for b in range(nb):
            s_bf = _bf(states[b])
            ys[b].append(_dot_nt(q_all[b][lo:hi], s_bf) + y0_all[b][lo:hi])
            states[b] = (states[b] + _dot(s_bf, m_c[b, c]) + c_c[b, c]) * pre[b]["w_inc"][hi - 1:hi, :]

    outs = []
    inv_n = 1.0 / RWKV_HEAD_DIM
    for b, (r, k, v, ld, a, g) in enumerate(ins):
        y = jnp.concatenate(ys[b], axis=0)
        mean = _pair_sum(y, m0) * inv_n
        d = y - mean
        var = _pair_sum(d * d, m0) * inv_n
        yn = d * lax.rsqrt(var + RWKV_GN_EPS) * gn_g + gn_b
        bonus = _pair_sum(r * pre[b]["k2"] * r_k, m0) * v
        outs.append((yn + bonus) * g)
    return outs, states


def _rwkv_scan(r, k, v, ld, a, g, k_k, k_a, r_k, gn_g, gn_b):
    bsz, t, _ = r.shape
    tok = pl.BlockSpec((bsz, RWKV_BLOCK, LANES), lambda h, c: (0, c, h))
    par = pl.BlockSpec((1, LANES), lambda h, c: (0, h))
    return pl.pallas_call(
        _rwkv_scan_kernel,
        grid=(N_PAIRS, t // RWKV_BLOCK),
        in_specs=[tok] * 6 + [par] * 5,
        out_specs=tok,
        out_shape=jax.ShapeDtypeStruct((bsz, t, RWKV_DIM), BF16),
        scratch_shapes=[pltpu.VMEM((bsz, LANES, LANES), F32)],
        compiler_params=_params("parallel", "arbitrary"),
        name="rwkv_scan",
    )(r, k, v, ld, a, g, k_k, k_a, r_k, gn_g, gn_b)


def _rope(z, cos_t, sin_lo, sin_hi):
    n = z.shape[-1]
    half = QK_ROPE_DIM // 2
    return z * cos_t + pltpu.roll(z, n - half, axis=1) * sin_lo + pltpu.roll(z, half, axis=1) * sin_hi


def _mla_prep_kernel(x_ref, w_ref, qn_ref, qup_ref, kvn_ref, kvup_ref, cos_ref, slo_ref, shi_ref,
                     q_ref, k_ref, v_ref):
    p = _dot(x_ref[...], w_ref[...])
    c_q = p[:, :Q_LORA_RANK]
    c_kv = p[:, Q_LORA_RANK:Q_LORA_RANK + KV_LORA_RANK]
    kr = p[:, Q_LORA_RANK + KV_LORA_RANK:]
    c_q = c_q * lax.rsqrt(jnp.mean(c_q * c_q, -1, keepdims=True) + RMS_EPS) * qn_ref[...]
    c_kv = c_kv * lax.rsqrt(jnp.mean(c_kv * c_kv, -1, keepdims=True) + RMS_EPS) * kvn_ref[...]
    q = _dot(_bf(c_q), qup_ref[...])
    kv = _dot(_bf(c_kv), kvup_ref[...])
    cos_t, slo, shi = cos_ref[...], slo_ref[...], shi_ref[...]
    reps = MLA_HEADS
    q = _rope(q, jnp.tile(cos_t, (1, reps)), jnp.tile(slo, (1, reps)), jnp.tile(shi, (1, reps)))
    q_ref[...] = (q * (ATTN_SCALE * LOG2_E)).astype(q_ref.dtype)
    kr = _rope(kr, cos_t, slo, shi)
    k_ref[...] = (kv[:, :MLA_HEADS * LANES] + jnp.tile(kr, (1, reps))).astype(k_ref.dtype)
    v_ref[...] = kv[:, MLA_HEADS * LANES:].astype(v_ref.dtype)


def _mla_prep(x_bf, w_mla, q_norm, q_up, kv_norm, kv_up, cos_t, sin_lo, sin_hi, tm):
    bsz, t, _ = x_bf.shape
    tok = lambda b, i: (b, i, 0)
    fixed = lambda b, i: (0, 0)
    tab = pl.BlockSpec((tm, LANES), lambda b, i: (i, 0))
    wq = MLA_HEADS * LANES
    return pl.pallas_call(
        _mla_prep_kernel,
        grid=(bsz, t // tm),
        in_specs=[
            pl.BlockSpec((None, tm, D_MODEL), tok),
            pl.BlockSpec(w_mla.shape, fixed),
            pl.BlockSpec((1, Q_LORA_RANK), fixed),
            pl.BlockSpec(q_up.shape, fixed),
            pl.BlockSpec((1, KV_LORA_RANK), fixed),
            pl.BlockSpec(kv_up.shape, fixed),
            tab, tab, tab,
        ],
        out_specs=[pl.BlockSpec((None, tm, wq), tok), pl.BlockSpec((None, tm, wq), tok),
                   pl.BlockSpec((None, tm, MLA_DIM), tok)],
        out_shape=[jax.ShapeDtypeStruct((bsz, t, wq), BF16), jax.ShapeDtypeStruct((bsz, t, wq), BF16),
                   jax.ShapeDtypeStruct((bsz, t, MLA_DIM), BF16)],
        compiler_params=_params("parallel", "parallel"),
        name="mla_prep",
    )(x_bf, w_mla, q_norm, q_up, kv_norm, kv_up, cos_t, sin_lo, sin_hi)


def _attn_kernel(q_ref, k_ref, v_ref, o_ref, m_ref, l_ref, acc_ref, *, tq):
    i = pl.program_id(2)
    lane = lax.broadcasted_iota(jnp.int32, (1, LANES), 1)
    m0 = lane < V_HEAD_DIM
    reps = tq // LANES
    m_ref[...] = jnp.full_like(m_ref, NEG_BIG)
    l_ref[...] = jnp.zeros_like(l_ref)
    acc_ref[...] = jnp.zeros_like(acc_ref)

    def step(j, masked):
        start = pl.multiple_of(j * tq, tq)
        v_blk = v_ref[pl.ds(start, tq), :]
        heads = range(HEAD_PAIR)
        s_h = [_dot_nt(q_ref[:, h * LANES:(h + 1) * LANES], k_ref[pl.ds(start, tq), h * LANES:(h + 1) * LANES])
               for h in heads]
        alphas, ps = [], []
        for h in heads:
            s = s_h[h]
            if masked:
                qi = lax.broadcasted_iota(jnp.int32, (tq, tq), 0)
                ki = lax.broadcasted_iota(jnp.int32, (tq, tq), 1)
                s = jnp.where(ki <= qi, s, NEG_BIG)
            m_old = m_ref[h]
            m_new = jnp.maximum(m_old, jnp.max(s, -1, keepdims=True))
            alpha = jnp.exp2(m_old - m_new)
            p = jnp.exp2(s - jnp.tile(m_new, (1, reps)))
            p_sum = p[:, :LANES]
            for c in range(1, reps):
                p_sum = p_sum + p[:, c * LANES:(c + 1) * LANES]
            l_ref[h] = alpha * l_ref[h] + p_sum
            m_ref[h] = m_new
            alphas.append(alpha)
            ps.append(_bf(p))
        for h in heads:
            acc_ref[h] = alphas[h] * acc_ref[h] + _dot(ps[h], v_blk)

    def body(j, carry):
        step(j, False)
        return carry

    lax.fori_loop(0, i, body, 0)
    step(i, True)
    l0 = jnp.sum(l_ref[0], -1, keepdims=True)
    l1 = jnp.sum(l_ref[1], -1, keepdims=True)
    o_ref[...] = jnp.where(m0, acc_ref[0] / l0, acc_ref[1] / l1).astype(o_ref.dtype)


def _attention(q, k, v, tq):
    bsz, t, _ = q.shape
    return pl.pallas_call(
        functools.partial(_attn_kernel, tq=tq),
        grid=(bsz, N_PAIRS, t // tq),
        in_specs=[
            pl.BlockSpec((None, tq, HEAD_PAIR * LANES), lambda b, h, i: (b, i, h)),
            pl.BlockSpec((None, t, HEAD_PAIR * LANES), lambda b, h, i: (b, 0, h)),
            pl.BlockSpec((None, t, LANES), lambda b, h, i: (b, 0, h)),
        ],
        out_specs=pl.BlockSpec((None, tq, LANES), lambda b, h, i: (b, i, h)),
        out_shape=jax.ShapeDtypeStruct((bsz, t, MLA_DIM), BF16),
        scratch_shapes=[pltpu.VMEM((HEAD_PAIR, tq, LANES), F32)] * 3,
        compiler_params=_params("parallel", "parallel", "arbitrary"),
        name="mla_attention",
    )(q, k, v)


def _s5_disc_kernel(lre_ref, lim_ref, ls_ref, br_ref, bi_ref, lbr_ref, lbi_ref, bbr_ref, bbi_ref):
    lam_re = jnp.minimum(lre_ref[...], -1e-4)
    lam_im = lim_ref[...]
    step = jnp.exp(ls_ref[...])
    mag = jnp.exp(lam_re * step)
    ang = lam_im * step
    lb_re, lb_im = mag * jnp.cos(ang), mag * jnp.sin(ang)
    den = lam_re * lam_re + lam_im * lam_im
    n_re = lb_re - 1.0
    f_re = (n_re * lam_re + lb_im * lam_im) / den
    f_im = (lb_im * lam_re - n_re * lam_im) / den
    br, bi = br_ref[...], bi_ref[...]
    lbr_ref[...] = lb_re
    lbi_ref[...] = lb_im
    bbr_ref[...] = f_re * br - f_im * bi
    bbi_ref[...] = f_re * bi + f_im * br


def _s5_discretize(lambda_re, lambda_im, log_step, b_re, b_im):
    rep = lambda z: jnp.repeat(z, S5_GROUP, axis=0)
    ls = rep(jnp.broadcast_to(log_step[:, None], (S5_GROUPS, S5_STATE)))
    tr = lambda z: jnp.transpose(z, (0, 2, 1)).reshape(S5_DIM, S5_STATE)
    out = jax.ShapeDtypeStruct((S5_DIM, S5_STATE), F32)
    return pl.pallas_call(_s5_disc_kernel, out_shape=[out] * 4, name="s5_discretize")(
        rep(lambda_re), rep(lambda_im), ls, tr(b_re), tr(b_im))


def _split_bf(z):
    hi = _bf(z)
    return hi, _bf(z - hi.astype(F32))


def _s5_prep_kernel(b_ref, c_ref, ct_ref, lam_ref, wg_ref, kc_ref, kst_ref, lam8_ref):
    n = S5_SLAB_STATE
    b, c, ct, lam = b_ref[0], c_ref[0], ct_ref[0], lam_ref[0]
    ar, ai = lam[:, :n], lam[:, n:]
    br, bi = b[:, :n], b[:, n:]
    ctr, cti = ct[:, :n], ct[:, n:]
    c_hi, c_lo = _split_bf(c)
    pr, pi = jnp.ones_like(ar), jnp.zeros_like(ai)
    for j in range(S5_CHUNK):
        w = jnp.concatenate([br * pr - bi * pi, br * pi + bi * pr], axis=1)
        w_hi, w_lo = _split_bf(w)
        rows = slice(j * LANES, (j + 1) * LANES)
        wg_ref[0, rows, :] = w_hi
        kc_ref[0, rows, :] = _bf(_dot(w_hi, c_hi) + _dot(w_lo, c_hi) + _dot(w_hi, c_lo))
        pr, pi = pr * ar - pi * ai, pr * ai + pi * ar
        kst_ref[0, rows, :] = _bf(jnp.concatenate([ctr * pr + cti * pi, cti * pr - ctr * pi], axis=1))
    lam8_ref[0] = jnp.concatenate([pr, pi], axis=1)


def _s5_prep(bmat, cmat, lam):
    n2 = 2 * S5_SLAB_STATE
    kq = S5_CHUNK * LANES
    slab = lambda s: (s, 0, 0)
    return pl.pallas_call(
        _s5_prep_kernel,
        grid=(S5_SLABS,),
        in_specs=[pl.BlockSpec((1, LANES, n2), slab), pl.BlockSpec((1, n2, LANES), slab),
                  pl.BlockSpec((1, LANES, n2), slab), pl.BlockSpec((1, 1, n2), slab)],
        out_specs=[pl.BlockSpec((1, kq, n2), slab), pl.BlockSpec((1, kq, LANES), slab),
                   pl.BlockSpec((1, kq, n2), slab), pl.BlockSpec((1, 1, n2), slab)],
        out_shape=[jax.ShapeDtypeStruct((S5_SLABS, kq, n2), BF16),
                   jax.ShapeDtypeStruct((S5_SLABS, kq, LANES), BF16),
                   jax.ShapeDtypeStruct((S5_SLABS, kq, n2), BF16),
                   jax.ShapeDtypeStruct((S5_SLABS, 1, n2), F32)],
        compiler_params=_params("parallel"),
        name="s5_prep",
    )(bmat, cmat, jnp.swapaxes(cmat, 1, 2), lam)


def _s5_kernel(x_ref, w_ref, wg_ref, kc_ref, kst_ref, lam8_ref, d_ref, glu_ref, o_ref,
               carry_ref, u_ref, ys_ref, *, ts):
    @pl.when(pl.program_id(1) == 0)
    def _():
        carry_ref[...] = jnp.zeros_like(carry_ref)

    n = S5_SLAB_STATE
    nc = ts // S5_CHUNK
    u = _dot(x_ref[...], w_ref[...])
    for sl in range(S5_SLABS):
        u_ref[sl] = u[:, sl * LANES:(sl + 1) * LANES]
    rin = lax.broadcasted_iota(jnp.int32, (ts, 1), 0) & (S5_CHUNK - 1)
    crow = lax.broadcasted_iota(jnp.int32, (nc, 1), 0)
    first = crow == 0
    slabs = range(S5_SLABS)
    x_end = [jnp.concatenate(
        [_bf(u_ref[sl, pl.ds(S5_CHUNK - 1 - j, nc, stride=S5_CHUNK), :]) for j in range(S5_CHUNK)], axis=1)
        for sl in slabs]
    gain = [_dot(x_end[sl], wg_ref[sl]) for sl in slabs]
    lam8 = [lam8_ref[sl] for sl in slabs]
    ar = [z[:, :n] for z in lam8]
    ai = [z[:, n:] for z in lam8]
    cr = [carry_ref[sl, 0:1, :n] for sl in slabs]
    cim = [carry_ref[sl, 0:1, n:] for sl in slabs]
    er = [gain[sl][:, :n] + jnp.where(first, ar[sl] * cr[sl] - ai[sl] * cim[sl], 0.0) for sl in slabs]
    ei = [gain[sl][:, n:] + jnp.where(first, ar[sl] * cim[sl] + ai[sl] * cr[sl], 0.0) for sl in slabs]
    sh = 1
    while sh < nc:
        keep = crow >= sh
        for sl in slabs:
            sr = jnp.where(keep, pltpu.roll(er[sl], sh, axis=0), 0.0)
            si = jnp.where(keep, pltpu.roll(ei[sl], sh, axis=0), 0.0)
            er[sl], ei[sl] = er[sl] + ar[sl] * sr - ai[sl] * si, ei[sl] + ar[sl] * si + ai[sl] * sr
            ar[sl], ai[sl] = ar[sl] * ar[sl] - ai[sl] * ai[sl], 2.0 * ar[sl] * ai[sl]
        sh *= 2
    for sl in slabs:
        carry_ref[sl, 0:1, :n] = er[sl][nc - 1:nc, :]
        carry_ref[sl, 0:1, n:] = ei[sl][nc - 1:nc, :]
    h0 = [_bf(jnp.concatenate([jnp.where(first, cr[sl], pltpu.roll(er[sl], 1, axis=0)),
                               jnp.where(first, cim[sl], pltpu.roll(ei[sl], 1, axis=0))], axis=1)) for sl in slabs]
    z = [_dot_nt(h0[sl], kst_ref[sl]) for sl in slabs]
    for sl in slabs:
        for i in range(S5_CHUNK):
            ys_ref[sl, pl.ds(i, nc, stride=S5_CHUNK), :] = z[sl][:, i * LANES:(i + 1) * LANES]
    ys = []
    for sl in slabs:
        us = u[:, sl * LANES:(sl + 1) * LANES]
        lagged = [_bf(us)] + [_bf(jnp.where(rin >= j, pltpu.roll(us, j, axis=0), 0.0))
                              for j in range(1, S5_CHUNK)]
        ys.append(_dot(jnp.concatenate(lagged, axis=1), kc_ref[sl]) + ys_ref[sl])
    y = jnp.concatenate(ys, axis=1) + d_ref[...] * u
    y = 0.5 * y * (1.0 + jnp.tanh(math.sqrt(2.0 / math.pi) * (y + 0.044715 * (y * y * y))))
    h = _dot(_bf(y), glu_ref[...])
    o_ref[...] = (h[:, :D_MODEL] * _sigmoid(h[:, D_MODEL:])).astype(o_ref.dtype)


def _s5(x_bf, w_s5, wg, kc, kst, lam8, d_skip, w_glu, ts):
    bsz, t, _ = x_bf.shape
    tok = lambda b, i: (b, i, 0)
    f2 = lambda b, i: (0, 0)
    f3 = lambda b, i: (0, 0, 0)
    return pl.pallas_call(
        functools.partial(_s5_kernel, ts=ts),
        grid=(bsz, t // ts),
        in_specs=[
            pl.BlockSpec((None, ts, D_MODEL), tok),
            pl.BlockSpec(w_s5.shape, f2),
            pl.BlockSpec(wg.shape, f3),
            pl.BlockSpec(kc.shape, f3),
            pl.BlockSpec(kst.shape, f3),
            pl.BlockSpec(lam8.shape, f3),
            pl.BlockSpec((1, S5_DIM), f2),
            pl.BlockSpec(w_glu.shape, f2),
        ],
        out_specs=pl.BlockSpec((None, ts, D_MODEL), tok),
        out_shape=jax.ShapeDtypeStruct((bsz, t, D_MODEL), BF16),
        scratch_shapes=[pltpu.VMEM((S5_SLABS, 8, 2 * S5_SLAB_STATE), F32),
                        pltpu.VMEM((S5_SLABS, ts, LANES), F32), pltpu.VMEM((S5_SLABS, ts, LANES), F32)],
        compiler_params=_params("parallel", "arbitrary"),
        name="s5_scan",
    )(x_bf, w_s5, wg, kc, kst, lam8, d_skip, w_glu)


def _s5_matrices(lb_re, lb_im, bb_re, bb_im, c_re, c_im):
    eye = jnp.eye(S5_SLAB_GROUPS, dtype=F32)
    s, g, c, p = S5_SLABS, S5_SLAB_GROUPS, S5_GROUP, S5_STATE

    def b_blocks(bb):
        return jnp.einsum("sgcp,gh->sgchp", bb.reshape(s, g, c, p), eye).reshape(s, g * c, g * p)

    def c_blocks(cc):
        return jnp.einsum("sgcp,gh->sgphc", cc.reshape(s, g, c, p), eye).reshape(s, g * p, g * c)

    bmat = jnp.concatenate([b_blocks(bb_re), b_blocks(bb_im)], axis=2)
    cmat = jnp.concatenate([c_blocks(c_re), -c_blocks(c_im)], axis=1)
    row = lambda z: z[::S5_GROUP].reshape(s, 1, g * p)
    lam = jnp.concatenate([row(lb_re), row(lb_im)], axis=2)
    return bmat, cmat, lam


def _merge_kernel(x_ref, xb_ref, ya_ref, ob_ref, yc_ref, wg_ref, gb_ref, wra_ref, wmo_ref, wo_ref,
                  lg_ref, lb_ref, o_ref, ob16_ref, *, alpha):
    gates = _sigmoid(_dot(xb_ref[...], wg_ref[...]) + gb_ref[...])
    y_a = _dot(ya_ref[...], wra_ref[...])
    y_b = _dot(ob_ref[...], wmo_ref[...])
    y_c = yc_ref[...].astype(F32)
    merged = (gates[:, :D_MODEL] * y_a + gates[:, D_MODEL:2 * D_MODEL] * y_b
              + gates[:, 2 * D_MODEL:] * y_c)
    y = alpha * x_ref[...] + _dot(_bf(merged), wo_ref[...])
    out = _layer_norm(y, lg_ref[...], lb_ref[...])
    o_ref[...] = out
    ob16_ref[...] = _bf(out)


def _merge(x, x_bf, ya, ob, yc, w_gate, gate_b, w_ra, w_mo, w_out, ln_g, ln_b, alpha, tm):
    n = x.shape[0]
    tok = lambda i: (i, 0)
    fixed = lambda i: (0, 0)
    full = lambda a: pl.BlockSpec(a.shape, fixed)
    return pl.pallas_call(
        functools.partial(_merge_kernel, alpha=alpha),
        grid=(n // tm,),
        in_specs=[
            pl.BlockSpec((tm, D_MODEL), tok), pl.BlockSpec((tm, D_MODEL), tok),
            pl.BlockSpec((tm, RWKV_DIM), tok), pl.BlockSpec((tm, MLA_DIM), tok),
            pl.BlockSpec((tm, D_MODEL), tok),
            full(w_gate), full(gate_b), full(w_ra), full(w_mo), full(w_out), full(ln_g), full(ln_b),
        ],
        out_specs=[pl.BlockSpec((tm, D_MODEL), tok)] * 2,
        out_shape=[jax.ShapeDtypeStruct((n, D_MODEL), F32), jax.ShapeDtypeStruct((n, D_MODEL), BF16)],
        compiler_params=_params("parallel"),
        name="merge_ln",
    )(x, x_bf, ya, ob, yc, w_gate, gate_b, w_ra, w_mo, w_out, ln_g, ln_b)


def _ffn_kernel(x_ref, xb_ref, w1_ref, w3_ref, w2_ref, lg_ref, lb_ref, o_ref, ob16_ref, acc_ref, *, alpha):
    j = pl.program_id(1)

    @pl.when(j == 0)
    def _():
        acc_ref[...] = jnp.zeros_like(acc_ref)

    xb = xb_ref[...]
    h1 = _dot(xb, w1_ref[...])
    h3 = _dot(xb, w3_ref[...])
    h = h1 * _sigmoid(h1) * h3
    acc_ref[...] += _dot(_bf(h), w2_ref[...])

    @pl.when(j == pl.num_programs(1) - 1)
    def _():
        out = _layer_norm(alpha * x_ref[...] + acc_ref[...], lg_ref[...], lb_ref[...])
        o_ref[...] = out
        ob16_ref[...] = _bf(out)


def _ffn(x, x_bf, w1, w3, w2, ln_g, ln_b, alpha, tm, tf):
    n = x.shape[0]
    d_ff = w1.shape[1]
    tok = lambda i, j: (i, 0)
    fixed = lambda i, j: (0, 0)
    return pl.pallas_call(
        functools.partial(_ffn_kernel, alpha=alpha),
        grid=(n // tm, d_ff // tf),
        in_specs=[
            pl.BlockSpec((tm, D_MODEL), tok), pl.BlockSpec((tm, D_MODEL), tok),
            pl.BlockSpec((D_MODEL, tf), lambda i, j: (0, j)),
            pl.BlockSpec((D_MODEL, tf), lambda i, j: (0, j)),
            pl.BlockSpec((tf, D_MODEL), lambda i, j: (j, 0)),
            pl.BlockSpec((1, D_MODEL), fixed), pl.BlockSpec((1, D_MODEL), fixed),
        ],
        out_specs=[pl.BlockSpec((tm, D_MODEL), tok)] * 2,
        out_shape=[jax.ShapeDtypeStruct((n, D_MODEL), F32), jax.ShapeDtypeStruct((n, D_MODEL), BF16)],
        scratch_shapes=[pltpu.VMEM((tm, D_MODEL), F32)],
        compiler_params=_params("parallel", "arbitrary"),
        name="ffn_ln",
    )(x, x_bf, w1, w3, w2, ln_g, ln_b)


def _rope_tables(t):
    pos = jnp.arange(t, dtype=F32)
    inv_freq = ROPE_THETA ** (-jnp.arange(0, QK_ROPE_DIM, 2, dtype=F32) / QK_ROPE_DIM)
    ang = pos[:, None] * inv_freq[None, :]
    cos, sin = jnp.cos(ang), jnp.sin(ang)
    half = QK_ROPE_DIM // 2
    ones = jnp.ones((t, QK_NOPE_DIM), F32)
    zeros = jnp.zeros((t, QK_NOPE_DIM), F32)
    tail1 = jnp.ones((t, LANES - QK_NOPE_DIM - QK_ROPE_DIM), F32)
    tail0 = jnp.zeros((t, LANES - QK_NOPE_DIM - QK_ROPE_DIM), F32)
    zh = jnp.zeros((t, half), F32)
    cos_t = jnp.concatenate([ones, cos, cos, tail1], axis=1)
    sin_lo = jnp.concatenate([zeros, -sin, zh, tail0], axis=1)
    sin_hi = jnp.concatenate([zeros, zh, sin, tail0], axis=1)
    return cos_t, sin_lo, sin_hi


def _row(z):
    return z.reshape(1, -1).astype(F32)


def _tile(t, want):
    return min(t, want)


def kernel(x, w_in, rwkv_mu, rwkv_w0, rwkv_w2, rwkv_a0, rwkv_a2, rwkv_g2, rwkv_k_k, rwkv_k_a, rwkv_r_k,
           rwkv_gn_g, rwkv_gn_b, rwkv_out, mla_q_norm, mla_q_up, mla_kv_norm, mla_kv_up, mla_out,
           s5_lambda_re, s5_lambda_im, s5_log_step, s5_b_re, s5_b_im, s5_c_re, s5_c_im, s5_d, s5_glu,
           gate_b, w_out, ln1_g, ln1_b, ffn_w1, ffn_w3, ffn_w2, ln2_g, ln2_b):
    bsz, t, _ = x.shape
    depth = w_in.shape[0]
    alpha = (2.0 * depth) ** 0.25
    n = bsz * t
    o_mla = RWKV_COLS
    o_s5 = o_mla + MLA_COLS
    o_gate = o_s5 + S5_COLS
    cos_t, sin_lo, sin_hi = _rope_tables(t)
    d_ff = ffn_w1.shape[2]
    tf = d_ff // 2 if (d_ff // 2) % LANES == 0 else d_ff

    x_bf = _bf(x)
    for l in range(depth):
        wl = w_in[l]
        zl = jnp.zeros((DECAY_LORA, RWKV_DIM), F32)
        w_lora = jnp.concatenate([jnp.concatenate([rwkv_w2[l], zl], axis=1),
                                  jnp.concatenate([zl, rwkv_a2[l]], axis=1)], axis=0)
        r, k, v, ld, a, g = _rwkv_prep(
            x_bf, _bf(wl[:, :o_mla]), _row(rwkv_mu[l]), _row(rwkv_w0[l]), _row(rwkv_a0[l]),
            _bf(w_lora), _bf(rwkv_g2[l]), _tile(t, 512))
        ya = _rwkv_scan(r, k, v, ld, a, g, _row(rwkv_k_k[l]), _row(rwkv_k_a[l]), _row(rwkv_r_k[l]),
                        _row(rwkv_gn_g[l]), _row(rwkv_gn_b[l]))
        nq = Q_LORA_RANK + KV_LORA_RANK
        w_kr = jnp.zeros((D_MODEL, LANES), F32).at[:, QK_NOPE_DIM:QK_NOPE_DIM + QK_ROPE_DIM].set(
            wl[:, o_mla + nq:o_s5])
        w_mla = _bf(jnp.concatenate([wl[:, o_mla:o_mla + nq], w_kr], axis=1))
        q_up = mla_q_up[l].reshape(Q_LORA_RANK, MLA_HEADS, QK_NOPE_DIM + QK_ROPE_DIM)
        q_up = jnp.pad(q_up, ((0, 0), (0, 0), (0, LANES - QK_NOPE_DIM - QK_ROPE_DIM)))
        q_up = _bf(q_up.reshape(Q_LORA_RANK, MLA_HEADS * LANES))
        kv_up = mla_kv_up[l].reshape(KV_LORA_RANK, MLA_HEADS, QK_NOPE_DIM + V_HEAD_DIM)
        k_up = jnp.pad(kv_up[:, :, :QK_NOPE_DIM], ((0, 0), (0, 0), (0, LANES - QK_NOPE_DIM)))
        kv_up = _bf(jnp.concatenate([k_up.reshape(KV_LORA_RANK, MLA_HEADS * LANES),
                                     kv_up[:, :, QK_NOPE_DIM:].reshape(KV_LORA_RANK, MLA_DIM)], axis=1))
        q, kx, vx = _mla_prep(x_bf, w_mla, _row(mla_q_norm[l]), q_up, _row(mla_kv_norm[l]), kv_up,
                              cos_t, sin_lo, sin_hi, _tile(t, 512))
        ob = _attention(q, kx, vx, _tile(t, 512))
        lb_re, lb_im, bb_re, bb_im = _s5_discretize(s5_lambda_re[l], s5_lambda_im[l], s5_log_step[l],
                                                    s5_b_re[l], s5_b_im[l])
        tr = lambda z: z.reshape(S5_DIM, S5_STATE)
        bmat, cmat, lam = _s5_matrices(lb_re, lb_im, bb_re, bb_im, tr(s5_c_re[l]), tr(s5_c_im[l]))
        wg, kc, kst, lam8 = _s5_prep(bmat, cmat, lam)
        yc = _s5(x_bf, _bf(wl[:, o_s5:o_gate]), wg, kc, kst, lam8, _row(s5_d[l]), _bf(s5_glu[l]),
                 _tile(t, 512))
        xf = x.reshape(n, D_MODEL)
        x1, x1_bf = _merge(xf, x_bf.reshape(n, D_MODEL), ya.reshape(n, RWKV_DIM), ob.reshape(n, MLA_DIM),
                           yc.reshape(n, D_MODEL), _bf(wl[:, o_gate:]), _row(gate_b[l]),
                           _bf(rwkv_out[l]), _bf(mla_out[l]), _bf(w_out[l]), _row(ln1_g[l]), _row(ln1_b[l]),
                           alpha, _tile(n, 256))
        x2, x2_bf = _ffn(x1, x1_bf, _bf(ffn_w1[l]), _bf(ffn_w3[l]), _bf(ffn_w2[l]), _row(ln2_g[l]),
                         _row(ln2_b[l]), alpha, _tile(n, 512), tf)
        x = x2.reshape(bsz, t, D_MODEL)
        x_bf = x2_bf.reshape(bsz, t, D_MODEL)
    return x
```

```python
import functools
import math

import jax
import jax.numpy as jnp
from jax import lax
from jax.experimental import pallas as pl
from jax.experimental.pallas import tpu as pltpu

F32 = jnp.float32
BF16 = jnp.bfloat16

D_MODEL = 1024
RWKV_HEADS = 8
RWKV_HEAD_DIM = 64
RWKV_DIM = RWKV_HEADS * RWKV_HEAD_DIM
DECAY_LORA = 64
AAA_LORA = 64
GATE_LORA = 128
RWKV_GN_EPS = 64e-5
MLA_HEADS = 8
QK_NOPE_DIM = 64
QK_ROPE_DIM = 32
V_HEAD_DIM = 64
Q_LORA_RANK = 256
KV_LORA_RANK = 128
MLA_DIM = MLA_HEADS * V_HEAD_DIM
ROPE_THETA = 10000.0
ATTN_SCALE = 1.0 / math.sqrt(QK_NOPE_DIM + QK_ROPE_DIM)
LOG2_E = math.log2(math.e)
S5_DIM = 512
S5_GROUP = 16
S5_GROUPS = S5_DIM // S5_GROUP
S5_STATE = 64
N_BRANCHES = 3
LN_EPS = 1e-5
RMS_EPS = 1e-6
RWKV_COLS = 3 * RWKV_DIM + DECAY_LORA + AAA_LORA + GATE_LORA
MLA_COLS = Q_LORA_RANK + KV_LORA_RANK + QK_ROPE_DIM
S5_COLS = S5_DIM

LANES = 128
HEAD_PAIR = LANES // RWKV_HEAD_DIM
N_PAIRS = RWKV_HEADS // HEAD_PAIR
S5_SLAB_GROUPS = LANES // S5_GROUP
S5_SLABS = S5_GROUPS // S5_SLAB_GROUPS
S5_SLAB_STATE = S5_SLAB_GROUPS * S5_STATE
S5_CHUNK = 8
VMEM_LIMIT = 48 * 1024 * 1024

RWKV_CHUNK = 64
RWKV_BLOCK = 256
NEG_BIG = -1e30


def _bf(x):
    return x.astype(BF16)


def _dot(a, b):
    return jnp.dot(a, b, preferred_element_type=F32)


def _dot_nt(a, b):
    return lax.dot_general(a, b, (((1,), (1,)), ((), ())), preferred_element_type=F32)


def _dot_tn(a, b):
    return lax.dot_general(a, b, (((0,), (0,)), ((), ())), preferred_element_type=F32)


def _sigmoid(x):
    return 1.0 / (1.0 + jnp.exp(-x))


def _params(*sem):
    return pltpu.CompilerParams(dimension_semantics=sem, vmem_limit_bytes=VMEM_LIMIT)


def _layer_norm(y, g, b):
    mu = jnp.mean(y, -1, keepdims=True)
    d = y - mu
    var = jnp.mean(d * d, -1, keepdims=True)
    return d * lax.rsqrt(var + LN_EPS) * g + b


def _pair_sum(x, m0):
    s0 = jnp.sum(jnp.where(m0, x, 0.0), -1, keepdims=True)
    s1 = jnp.sum(jnp.where(m0, 0.0, x), -1, keepdims=True)
    return jnp.where(m0, s0, s1)


def _rwkv_prep_kernel(x_ref, w_ref, mu_ref, w0_ref, a0_ref, wl_ref, g2_ref,
                      r_ref, k_ref, v_ref, ld_ref, a_ref, g_ref, prev_ref):
    t = pl.program_id(1)

    @pl.when(t == 0)
    def _():
        prev_ref[...] = jnp.zeros_like(prev_ref)

    p = _dot(x_ref[...], w_ref[...])
    rolled = pltpu.roll(p, 1, axis=0)
    row = lax.broadcasted_iota(jnp.int32, (p.shape[0], 1), 0)
    shifted = jnp.where(row == 0, prev_ref[0:1, :], rolled)
    prev_ref[0:1, :] = rolled[0:1, :]
    p = p + (shifted - p) * mu_ref[...]

    c0, c1, c2 = RWKV_DIM, 2 * RWKV_DIM, 3 * RWKV_DIM
    r_ref[...] = p[:, :c0].astype(r_ref.dtype)
    k_ref[...] = p[:, c0:c1].astype(k_ref.dtype)
    v_ref[...] = p[:, c1:c2].astype(v_ref.dtype)
    lora = p[:, c2:c2 + DECAY_LORA + AAA_LORA]
    lane = lax.broadcasted_iota(jnp.int32, (1, DECAY_LORA + AAA_LORA), 1)
    lora = jnp.where(lane < DECAY_LORA, jnp.tanh(lora), lora)
    wa = _dot(_bf(lora), wl_ref[...])
    ld_ref[...] = (-math.exp(-0.5)) * _sigmoid(w0_ref[...] + wa[:, :RWKV_DIM])
    a_ref[...] = _sigmoid(a0_ref[...] + wa[:, RWKV_DIM:]).astype(a_ref.dtype)
    dg = p[:, c2 + DECAY_LORA + AAA_LORA:]
    g_ref[...] = _dot(_bf(_sigmoid(dg)), g2_ref[...]).astype(g_ref.dtype)


def _rwkv_prep(x_bf, w_rwkv, mu, w0, a0, w_lora, g2, tm):
    bsz, t, _ = x_bf.shape
    outs = [jax.ShapeDtypeStruct((bsz, t, RWKV_DIM), dt) for dt in (BF16, BF16, BF16, F32, BF16, BF16)]
    tok = lambda b, i: (b, i, 0)
    fixed = lambda b, i: (0, 0)
    o_spec = pl.BlockSpec((None, tm, RWKV_DIM), tok)
    return pl.pallas_call(
        _rwkv_prep_kernel,
        grid=(bsz, t // tm),
        in_specs=[
            pl.BlockSpec((None, tm, D_MODEL), tok),
            pl.BlockSpec((D_MODEL, RWKV_COLS), fixed),
            pl.BlockSpec((1, RWKV_COLS), fixed),
            pl.BlockSpec((1, RWKV_DIM), fixed),
            pl.BlockSpec((1, RWKV_DIM), fixed),
            pl.BlockSpec((DECAY_LORA + AAA_LORA, 2 * RWKV_DIM), fixed),
            pl.BlockSpec((GATE_LORA, RWKV_DIM), fixed),
        ],
        out_specs=[o_spec] * 6,
        out_shape=outs,
        scratch_shapes=[pltpu.VMEM((8, RWKV_COLS), F32)],
        compiler_params=_params("parallel", "arbitrary"),
        name="rwkv_prep",
    )(x_bf, w_rwkv, mu, w0, a0, w_lora, g2)


def _rwkv_scan_kernel(r_ref, k_ref, v_ref, ld_ref, a_ref, g_ref,
                      kk_ref, ka_ref, rk_ref, gng_ref, gnb_ref, o_ref, s_ref):
    @pl.when(pl.program_id(1) == 0)
    def _():
        s_ref[...] = jnp.zeros_like(s_ref)

    par = (kk_ref[...], ka_ref[...], rk_ref[...], gng_ref[...], gnb_ref[...])
    nb = r_ref.shape[0]
    ins = [tuple(z[b].astype(F32) for z in (r_ref, k_ref, v_ref, ld_ref, a_ref, g_ref)) for b in range(nb)]
    outs, states = _rwkv_blocks(ins, par, [s_ref[b] for b in range(nb)])
    for b in range(nb):
        o_ref[b] = outs[b].astype(o_ref.dtype)
        s_ref[b] = states[b]


def _rwkv_blocks(ins, par, states):
    rows, chunk = RWKV_BLOCK, RWKV_CHUNK
    n_chunks = rows // chunk
    k_k, k_a, r_k, gn_g, gn_b = par
    nb = len(ins)
    inst = [(b, h) for b in range(nb) for h in range(HEAD_PAIR)]

    lane = lax.broadcasted_iota(jnp.int32, (1, LANES), 1)
    m0 = lane < RWKV_HEAD_DIM
    rin = lax.broadcasted_iota(jnp.int32, (rows, 1), 0) & (chunk - 1)

    pre = []
    for r, k, v, ld, a, g in ins:
        kk = k * k_k
        kk = kk * lax.rsqrt(jnp.maximum(_pair_sum(kk * kk, m0), 1e-12))
        k2 = k * (1.0 + (a - 1.0) * k_a)
        bv = kk * a
        cl = ld
        sh = 1
        while sh < chunk:
            cl = cl + jnp.where(rin >= sh, pltpu.roll(cl, sh, axis=0), 0.0)
            sh *= 2
        w_inc = jnp.exp(cl)
        w_inv = jnp.exp(-cl)
        at = -kk * jnp.exp(cl - ld)
        rt = r * w_inc
        pre.append(dict(
            k2=k2, w_inc=w_inc, rt=rt, bt=_bf(bv * w_inv), kt=_bf(k2 * w_inv), v_bf=_bf(v),
            at_h=(_bf(jnp.where(m0, at, 0.0)), _bf(jnp.where(m0, 0.0, at))),
            rt_h=(_bf(jnp.where(m0, rt, 0.0)), _bf(jnp.where(m0, 0.0, rt)))))

    grams = []
    for d in pre:
        lhs = jnp.concatenate([d["at_h"][0], d["at_h"][1], d["rt_h"][0], d["rt_h"][1]], axis=0)
        rhs = jnp.concatenate([d["bt"], d["kt"]], axis=0)
        grams.append(_dot_nt(lhs, rhs))

    ri = lax.broadcasted_iota(jnp.int32, (rows, rows), 0)
    ci = lax.broadcasted_iota(jnp.int32, (rows, rows), 1)
    same = (ri // chunk) == (ci // chunk)
    strict = same & (ri > ci)
    incl = same & (ri >= ci)
    eye = jnp.where(ri == ci, 1.0, 0.0)

    xs = [jnp.where(strict, grams[b][h * rows:(h + 1) * rows, :rows], 0.0) for b, h in inst]
    t_inv = [eye + x for x in xs]
    n = 2
    while n < chunk:
        xb = [_bf(x) for x in xs]
        xs = [_dot(x, x) for x in xb]
        t_inv = [t + _dot(_bf(t), _bf(x)) for t, x in zip(t_inv, xs)]
        n *= 2
    t_bf = [_bf(t) for t in t_inv]

    akv = [_dot(_bf(jnp.where(strict, grams[b][h * rows:(h + 1) * rows, rows:], 0.0)), pre[b]["v_bf"])
           for b, h in inst]
    up_h = [_dot(t, jnp.concatenate([_bf(z), pre[b]["at_h"][h]], axis=1))
            for t, z, (b, h) in zip(t_bf, akv, inst)]
    u0_h = [z[:, :LANES] for z in up_h]
    p_h = [z[:, LANES:] for z in up_h]
    a_rb = [_bf(jnp.where(incl, grams[b][(2 + h) * rows:(3 + h) * rows, :rows], 0.0)) for b, h in inst]
    a_rk = [_bf(jnp.where(incl, grams[b][(2 + h) * rows:(3 + h) * rows, rows:], 0.0)) for b, h in inst]
    rb_up = [_dot(a_rb[i], _bf(up_h[i])) for i in range(len(inst))]
    y0_h = [rb_up[i][:, :LANES] + _dot(a_rk[i], pre[b]["v_bf"]) for i, (b, h) in enumerate(inst)]
    qp_h = [z[:, LANES:] for z in rb_up]

    bi = lax.broadcasted_iota(jnp.int32, (LANES, LANES), 0) // RWKV_HEAD_DIM
    bj = lax.broadcasted_iota(jnp.int32, (LANES, LANES), 1) // RWKV_HEAD_DIM
    blockdiag = bi == bj

    p_all, q_all, u0_all, y0_all = [], [], [], []
    for b in range(nb):
        i0, i1 = HEAD_PAIR * b, HEAD_PAIR * b + 1
        p_all.append(_bf(p_h[i0] + p_h[i1]))
        q_all.append(_bf(pre[b]["rt"] + qp_h[i0] + qp_h[i1]))
        u0_all.append(_bf(jnp.where(m0, u0_h[i0], u0_h[i1])))
        y0_all.append(jnp.where(m0, y0_h[i0], y0_h[i1]))

    m_c, c_c = {}, {}
    for c in range(n_chunks):
        lo, hi = c * chunk, (c + 1) * chunk
        for b in range(nb):
            bt_c, kt_c = pre[b]["bt"][lo:hi], pre[b]["kt"][lo:hi]
            m_c[b, c] = _bf(jnp.where(blockdiag, _dot_tn(p_all[b][lo:hi], bt_c), 0.0))
            c_c[b, c] = jnp.where(
                blockdiag, _dot_tn(u0_all[b][lo:hi], bt_c) + _dot_tn(pre[b]["v_bf"][lo:hi], kt_c), 0.0)

    ys = [[] for _ in range(nb)]
    states = list(states)
    for c in range(n_chunks):
        lo, hi = c * chunk, (c + 1) * chunk
        for b in range(nb):
            s_bf = _bf(states[b])
            ys[b].append(_dot_nt(q_all[b][lo:hi], s_bf) + y0_all[b][lo:hi])
            states[b] = (states[b] + _dot(s_bf, m_c[b, c]) + c_c[b, c]) * pre[b]["w_inc"][hi - 1:hi, :]

    outs = []
    inv_n = 1.0 / RWKV_HEAD_DIM
    for b, (r, k, v, ld, a, g) in enumerate(ins):
        y = jnp.concatenate(ys[b], axis=0)
        mean = _pair_sum(y, m0) * inv_n
        d = y - mean
        var = _pair_sum(d * d, m0) * inv_n
        yn = d * lax.rsqrt(var + RWKV_GN_EPS) * gn_g + gn_b
        bonus = _pair_sum(r * pre[b]["k2"] * r_k, m0) * v
        outs.append((yn + bonus) * g)
    return outs, states


def _rwkv_scan(r, k, v, ld, a, g, k_k, k_a, r_k, gn_g, gn_b):
    bsz, t, _ = r.shape
    tok = pl.BlockSpec((bsz, RWKV_BLOCK, LANES), lambda h, c: (0, c, h))
    par = pl.BlockSpec((1, LANES), lambda h, c: (0, h))
    return pl.pallas_call(
        _rwkv_scan_kernel,
        grid=(N_PAIRS, t // RWKV_BLOCK),
        in_specs=[tok] * 6 + [par] * 5,
        out_specs=tok,
        out_shape=jax.ShapeDtypeStruct((bsz, t, RWKV_DIM), BF16),
        scratch_shapes=[pltpu.VMEM((bsz, LANES, LANES), F32)],
        compiler_params=_params("parallel", "arbitrary"),
        name="rwkv_scan",
    )(r, k, v, ld, a, g, k_k, k_a, r_k, gn_g, gn_b)


def _rope(z, cos_t, sin_lo, sin_hi):
    n = z.shape[-1]
    half = QK_ROPE_DIM // 2
    return z * cos_t + pltpu.roll(z, n - half, axis=1) * sin_lo + pltpu.roll(z, half, axis=1) * sin_hi


def _mla_prep_kernel(x_ref, w_ref, qn_ref, qup_ref, kvn_ref, kvup_ref, cos_ref, slo_ref, shi_ref,
                     q_ref, k_ref, v_ref):
    p = _dot(x_ref[...], w_ref[...])
    c_q = p[:, :Q_LORA_RANK]
    c_kv = p[:, Q_LORA_RANK:Q_LORA_RANK + KV_LORA_RANK]
    kr = p[:, Q_LORA_RANK + KV_LORA_RANK:]
    c_q = c_q * lax.rsqrt(jnp.mean(c_q * c_q, -1, keepdims=True) + RMS_EPS) * qn_ref[...]
    c_kv = c_kv * lax.rsqrt(jnp.mean(c_kv * c_kv, -1, keepdims=True) + RMS_EPS) * kvn_ref[...]
    q = _dot(_bf(c_q), qup_ref[...])
    kv = _dot(_bf(c_kv), kvup_ref[...])
    cos_t, slo, shi = cos_ref[...], slo_ref[...], shi_ref[...]
    reps = MLA_HEADS
    q = _rope(q, jnp.tile(cos_t, (1, reps)), jnp.tile(slo, (1, reps)), jnp.tile(shi, (1, reps)))
    q_ref[...] = (q * (ATTN_SCALE * LOG2_E)).astype(q_ref.dtype)
    kr = _rope(kr, cos_t, slo, shi)
    k_ref[...] = (kv[:, :MLA_HEADS * LANES] + jnp.tile(kr, (1, reps))).astype(k_ref.dtype)
    v_ref[...] = kv[:, MLA_HEADS * LANES:].astype(v_ref.dtype)


def _mla_prep(x_bf, w_mla, q_norm, q_up, kv_norm, kv_up, cos_t, sin_lo, sin_hi, tm):
    bsz, t, _ = x_bf.shape
    tok = lambda b, i: (b, i, 0)
    fixed = lambda b, i: (0, 0)
    tab = pl.BlockSpec((tm, LANES), lambda b, i: (i, 0))
    wq = MLA_HEADS * LANES
    return pl.pallas_call(
        _mla_prep_kernel,
        grid=(bsz, t // tm),
        in_specs=[
            pl.BlockSpec((None, tm, D_MODEL), tok),
            pl.BlockSpec(w_mla.shape, fixed),
            pl.BlockSpec((1, Q_LORA_RANK), fixed),
            pl.BlockSpec(q_up.shape, fixed),
            pl.BlockSpec((1, KV_LORA_RANK), fixed),
            pl.BlockSpec(kv_up.shape, fixed),
            tab, tab, tab,
        ],
        out_specs=[pl.BlockSpec((None, tm, wq), tok), pl.BlockSpec((None, tm, wq), tok),
                   pl.BlockSpec((None, tm, MLA_DIM), tok)],
        out_shape=[jax.ShapeDtypeStruct((bsz, t, wq), BF16), jax.ShapeDtypeStruct((bsz, t, wq), BF16),
                   jax.ShapeDtypeStruct((bsz, t, MLA_DIM), BF16)],
        compiler_params=_params("parallel", "parallel"),
        name="mla_prep",
    )(x_bf, w_mla, q_norm, q_up, kv_norm, kv_up, cos_t, sin_lo, sin_hi)


def _attn_kernel(q_ref, k_ref, v_ref, o_ref, m_ref, l_ref, acc_ref, *, tq):
    i = pl.program_id(2)
    lane = lax.broadcasted_iota(jnp.int32, (1, LANES), 1)
    m0 = lane < V_HEAD_DIM
    m_ref[...] = jnp.full_like(m_ref, NEG_BIG)
    l_ref[...] = jnp.zeros_like(l_ref)
    acc_ref[...] = jnp.zeros_like(acc_ref)

    def step(start, nk, row_lo, masked):
        rows = slice(row_lo, tq)
        nrow = tq - row_lo
        v_blk = v_ref[pl.ds(start, nk), :]
        heads = range(HEAD_PAIR)
        s_h = [_dot_nt(q_ref[rows, h * LANES:(h + 1) * LANES], k_ref[pl.ds(start, nk), h * LANES:(h + 1) * LANES])
               for h in heads]
        alphas, ps = [], []
        for h in heads:
            s = s_h[h]
            if masked:
                qi = lax.broadcasted_iota(jnp.int32, (nrow, nk), 0) + (i * tq + row_lo)
                ki = lax.broadcasted_iota(jnp.int32, (nrow, nk), 1) + start
                s = jnp.where(ki <= qi, s, NEG_BIG)
            m_old = m_ref[h, rows]
            m_new = jnp.maximum(m_old, jnp.max(s, -1, keepdims=True))
            alpha = jnp.exp2(m_old - m_new)
            p = jnp.exp2(s - jnp.tile(m_new, (1, nk // LANES)))
            p_sum = p[:, :LANES]
            for c in range(1, nk // LANES):
                p_sum = p_sum + p[:, c * LANES:(c + 1) * LANES]
            l_ref[h, rows] = alpha * l_ref[h, rows] + p_sum
            m_ref[h, rows] = m_new
            alphas.append(alpha)
            ps.append(_bf(p))
        for h in heads:
            acc_ref[h, rows] = alphas[h] * acc_ref[h, rows] + _dot(ps[h], v_blk)

    def body(j, carry):
        step(pl.multiple_of(j * tq, tq), tq, 0, False)
        return carry

    lax.fori_loop(0, i, body, 0)
    half = tq // 2
    base = pl.multiple_of(i * tq, tq)
    step(base, half, 0, True)
    step(base + half, half, half, True)
    l0 = jnp.sum(l_ref[0], -1, keepdims=True)
    l1 = jnp.sum(l_ref[1], -1, keepdims=True)
    o_ref[...] = jnp.where(m0, acc_ref[0] / l0, acc_ref[1] / l1).astype(o_ref.dtype)


def _attention(q, k, v, tq):
    bsz, t, _ = q.shape
    return pl.pallas_call(
        functools.partial(_attn_kernel, tq=tq),
        grid=(bsz, N_PAIRS, t // tq),
        in_specs=[
            pl.BlockSpec((None, tq, HEAD_PAIR * LANES), lambda b, h, i: (b, i, h)),
            pl.BlockSpec((None, t, HEAD_PAIR * LANES), lambda b, h, i: (b, 0, h)),
            pl.BlockSpec((None, t, LANES), lambda b, h, i: (b, 0, h)),
        ],
        out_specs=pl.BlockSpec((None, tq, LANES), lambda b, h, i: (b, i, h)),
        out_shape=jax.ShapeDtypeStruct((bsz, t, MLA_DIM), BF16),
        scratch_shapes=[pltpu.VMEM((HEAD_PAIR, tq, LANES), F32)] * 3,
        compiler_params=_params("parallel", "parallel", "arbitrary"),
        name="mla_attention",
    )(q, k, v)


def _s5_disc_kernel(lre_ref, lim_ref, ls_ref, br_ref, bi_ref, lbr_ref, lbi_ref, bbr_ref, bbi_ref):
    lam_re = jnp.minimum(lre_ref[...], -1e-4)
    lam_im = lim_ref[...]
    step = jnp.exp(ls_ref[...])
    mag = jnp.exp(lam_re * step)
    ang = lam_im * step
    lb_re, lb_im = mag * jnp.cos(ang), mag * jnp.sin(ang)
    den = lam_re * lam_re + lam_im * lam_im
    n_re = lb_re - 1.0
    f_re = (n_re * lam_re + lb_im * lam_im) / den
    f_im = (lb_im * lam_re - n_re * lam_im) / den
    br, bi = br_ref[...], bi_ref[...]
    lbr_ref[...] = lb_re
    lbi_ref[...] = lb_im
    bbr_ref[...] = f_re * br - f_im * bi
    bbi_ref[...] = f_re * bi + f_im * br


def _s5_discretize(lambda_re, lambda_im, log_step, b_re, b_im):
    rep = lambda z: jnp.repeat(z, S5_GROUP, axis=0)
    ls = rep(jnp.broadcast_to(log_step[:, None], (S5_GROUPS, S5_STATE)))
    tr = lambda z: jnp.transpose(z, (0, 2, 1)).reshape(S5_DIM, S5_STATE)
    out = jax.ShapeDtypeStruct((S5_DIM, S5_STATE), F32)
    return pl.pallas_call(_s5_disc_kernel, out_shape=[out] * 4, name="s5_discretize")(
        rep(lambda_re), rep(lambda_im), ls, tr(b_re), tr(b_im))


def _split_bf(z):
    hi = _bf(z)
    return hi, _bf(z - hi.astype(F32))


def _s5_prep_kernel(b_ref, c_ref, ct_ref, lam_ref, wg_ref, kc_ref, kst_ref, lam8_ref):
    n = S5_SLAB_STATE
    b, c, ct, lam = b_ref[0], c_ref[0], ct_ref[0], lam_ref[0]
    ar, ai = lam[:, :n], lam[:, n:]
    br, bi = b[:, :n], b[:, n:]
    ctr, cti = ct[:, :n], ct[:, n:]
    c_hi, c_lo = _split_bf(c)
    pr, pi = jnp.ones_like(ar), jnp.zeros_like(ai)
    for j in range(S5_CHUNK):
        w = jnp.concatenate([br * pr - bi * pi, br * pi + bi * pr], axis=1)
        w_hi, w_lo = _split_bf(w)
        rows = slice(j * LANES, (j + 1) * LANES)
        wg_ref[0, rows, :] = w_hi
        kc_ref[0, rows, :] = _bf(_dot(w_hi, c_hi) + _dot(w_lo, c_hi) + _dot(w_hi, c_lo))
        pr, pi = pr * ar - pi * ai, pr * ai + pi * ar
        kst_ref[0, rows, :] = _bf(jnp.concatenate([ctr * pr + cti * pi, cti * pr - ctr * pi], axis=1))
    lam8_ref[0] = jnp.concatenate([pr, pi], axis=1)


def _s5_prep(bmat, cmat, lam):
    n2 = 2 * S5_SLAB_STATE
    kq = S5_CHUNK * LANES
    slab = lambda s: (s, 0, 0)
    return pl.pallas_call(
        _s5_prep_kernel,
        grid=(S5_SLABS,),
        in_specs=[pl.BlockSpec((1, LANES, n2), slab), pl.BlockSpec((1, n2, LANES), slab),
                  pl.BlockSpec((1, LANES, n2), slab), pl.BlockSpec((1, 1, n2), slab)],
        out_specs=[pl.BlockSpec((1, kq, n2), slab), pl.BlockSpec((1, kq, LANES), slab),
                   pl.BlockSpec((1, kq, n2), slab), pl.BlockSpec((1, 1, n2), slab)],
        out_shape=[jax.ShapeDtypeStruct((S5_SLABS, kq, n2), BF16),
                   jax.ShapeDtypeStruct((S5_SLABS, kq, LANES), BF16),
                   jax.ShapeDtypeStruct((S5_SLABS, kq, n2), BF16),
                   jax.ShapeDtypeStruct((S5_SLABS, 1, n2), F32)],
        compiler_params=_params("parallel"),
        name="s5_prep",
    )(bmat, cmat, jnp.swapaxes(cmat, 1, 2), lam)


def _s5_kernel(x_ref, w_ref, wg_ref, kc_ref, kst_ref, lam8_ref, d_ref, glu_ref, o_ref,
               carry_ref, u_ref, ys_ref, *, ts):
    @pl.when(pl.program_id(1) == 0)
    def _():
        carry_ref[...] = jnp.zeros_like(carry_ref)

    n = S5_SLAB_STATE
    nc = ts // S5_CHUNK
    u = _dot(x_ref[...], w_ref[...])
    for sl in range(S5_SLABS):
        u_ref[sl] = u[:, sl * LANES:(sl + 1) * LANES]
    rin = lax.broadcasted_iota(jnp.int32, (ts, 1), 0) & (S5_CHUNK - 1)
    crow = lax.broadcasted_iota(jnp.int32, (nc, 1), 0)
    first = crow == 0
    slabs = range(S5_SLABS)
    x_end = [jnp.concatenate(
        [_bf(u_ref[sl, pl.ds(S5_CHUNK - 1 - j, nc, stride=S5_CHUNK), :]) for j in range(S5_CHUNK)], axis=1)
        for sl in slabs]
    gain = [_dot(x_end[sl], wg_ref[sl]) for sl in slabs]
    lam8 = [lam8_ref[sl] for sl in slabs]
    ar = [z[:, :n] for z in lam8]
    ai = [z[:, n:] for z in lam8]
    cr = [carry_ref[sl, 0:1, :n] for sl in slabs]
    cim = [carry_ref[sl, 0:1, n:] for sl in slabs]
    er = [gain[sl][:, :n] + jnp.where(first, ar[sl] * cr[sl] - ai[sl] * cim[sl], 0.0) for sl in slabs]
    ei = [gain[sl][:, n:] + jnp.where(first, ar[sl] * cim[sl] + ai[sl] * cr[sl], 0.0) for sl in slabs]
    sh = 1
    while sh < nc:
        keep = crow >= sh
        for sl in slabs:
            sr = jnp.where(keep, pltpu.roll(er[sl], sh, axis=0), 0.0)
            si = jnp.where(keep, pltpu.roll(ei[sl], sh, axis=0), 0.0)
            er[sl], ei[sl] = er[sl] + ar[sl] * sr - ai[sl] * si, ei[sl] + ar[sl] * si + ai[sl] * sr
            ar[sl], ai[sl] = ar[sl] * ar[sl] - ai[sl] * ai[sl], 2.0 * ar[sl] * ai[sl]
        sh *= 2
    for sl in slabs:
        carry_ref[sl, 0:1, :n] = er[sl][nc - 1:nc, :]
        carry_ref[sl, 0:1, n:] = ei[sl][nc - 1:nc, :]
    h0 = [_bf(jnp.concatenate([jnp.where(first, cr[sl], pltpu.roll(er[sl], 1, axis=0)),
                               jnp.where(first, cim[sl], pltpu.roll(ei[sl], 1, axis=0))], axis=1)) for sl in slabs]
    z = [_dot_nt(h0[sl], kst_ref[sl]) for sl in slabs]
    for sl in slabs:
        for i in range(S5_CHUNK):
            ys_ref[sl, pl.ds(i, nc, stride=S5_CHUNK), :] = z[sl][:, i * LANES:(i + 1) * LANES]
    ys = []
    for sl in slabs:
        us = u[:, sl * LANES:(sl + 1) * LANES]
        lagged = [_bf(us)] + [_bf(jnp.where(rin >= j, pltpu.roll(us, j, axis=0), 0.0))
                              for j in range(1, S5_CHUNK)]
        ys.append(_dot(jnp.concatenate(lagged, axis=1), kc_ref[sl]) + ys_ref[sl])
    y = jnp.concatenate(ys, axis=1) + d_ref[...] * u
    y = 0.5 * y * (1.0 + jnp.tanh(math.sqrt(2.0 / math.pi) * (y + 0.044715 * (y * y * y))))
    h = _dot(_bf(y), glu_ref[...])
    o_ref[...] = (h[:, :D_MODEL] * _sigmoid(h[:, D_MODEL:])).astype(o_ref.dtype)


def _s5(x_bf, w_s5, wg, kc, kst, lam8, d_skip, w_glu, ts):
    bsz, t, _ = x_bf.shape
    tok = lambda b, i: (b, i, 0)
    f2 = lambda b, i: (0, 0)
    f3 = lambda b, i: (0, 0, 0)
    return pl.pallas_call(
        functools.partial(_s5_kernel, ts=ts),
        grid=(bsz, t // ts),
        in_specs=[
            pl.BlockSpec((None, ts, D_MODEL), tok),
            pl.BlockSpec(w_s5.shape, f2),
            pl.BlockSpec(wg.shape, f3),
            pl.BlockSpec(kc.shape, f3),
            pl.BlockSpec(kst.shape, f3),
            pl.BlockSpec(lam8.shape, f3),
            pl.BlockSpec((1, S5_DIM), f2),
            pl.BlockSpec(w_glu.shape, f2),
        ],
        out_specs=pl.BlockSpec((None, ts, D_MODEL), tok),
        out_shape=jax.ShapeDtypeStruct((bsz, t, D_MODEL), BF16),
        scratch_shapes=[pltpu.VMEM((S5_SLABS, 8, 2 * S5_SLAB_STATE), F32),
                        pltpu.VMEM((S5_SLABS, ts, LANES), F32), pltpu.VMEM((S5_SLABS, ts, LANES), F32)],
        compiler_params=_params("parallel", "arbitrary"),
        name="s5_scan",
    )(x_bf, w_s5, wg, kc, kst, lam8, d_skip, w_glu)


def _s5_matrices(lb_re, lb_im, bb_re, bb_im, c_re, c_im):
    eye = jnp.eye(S5_SLAB_GROUPS, dtype=F32)
    s, g, c, p = S5_SLABS, S5_SLAB_GROUPS, S5_GROUP, S5_STATE

    def b_blocks(bb):
        return jnp.einsum("sgcp,gh->sgchp", bb.reshape(s, g, c, p), eye).reshape(s, g * c, g * p)

    def c_blocks(cc):
        return jnp.einsum("sgcp,gh->sgphc", cc.reshape(s, g, c, p), eye).reshape(s, g * p, g * c)

    bmat = jnp.concatenate([b_blocks(bb_re), b_blocks(bb_im)], axis=2)
    cmat = jnp.concatenate([c_blocks(c_re), -c_blocks(c_im)], axis=1)
    row = lambda z: z[::S5_GROUP].reshape(s, 1, g * p)
    lam = jnp.concatenate([row(lb_re), row(lb_im)], axis=2)
    return bmat, cmat, lam


def _merge_kernel(x_ref, xb_ref, ya_ref, ob_ref, yc_ref, wg_ref, gb_ref, wra_ref, wmo_ref, wo_ref,
                  lg_ref, lb_ref, o_ref, ob16_ref, *, alpha):
    gates = _sigmoid(_dot(xb_ref[...], wg_ref[...]) + gb_ref[...])
    y_a = _dot(ya_ref[...], wra_ref[...])
    y_b = _dot(ob_ref[...], wmo_ref[...])
    y_c = yc_ref[...].astype(F32)
    merged = (gates[:, :D_MODEL] * y_a + gates[:, D_MODEL:2 * D_MODEL] * y_b
              + gates[:, 2 * D_MODEL:] * y_c)
    y = alpha * x_ref[...] + _dot(_bf(merged), wo_ref[...])
    out = _layer_norm(y, lg_ref[...], lb_ref[...])
    o_ref[...] = out
    ob16_ref[...] = _bf(out)


def _merge(x, x_bf, ya, ob, yc, w_gate, gate_b, w_ra, w_mo, w_out, ln_g, ln_b, alpha, tm):
    n = x.shape[0]
    tok = lambda i: (i, 0)
    fixed = lambda i: (0, 0)
    full = lambda a: pl.BlockSpec(a.shape, fixed)
    return pl.pallas_call(
        functools.partial(_merge_kernel, alpha=alpha),
        grid=(n // tm,),
        in_specs=[
            pl.BlockSpec((tm, D_MODEL), tok), pl.BlockSpec((tm, D_MODEL), tok),
            pl.BlockSpec((tm, RWKV_DIM), tok), pl.BlockSpec((tm, MLA_DIM), tok),
            pl.BlockSpec((tm, D_MODEL), tok),
            full(w_gate), full(gate_b), full(w_ra), full(w_mo), full(w_out), full(ln_g), full(ln_b),
        ],
        out_specs=[pl.BlockSpec((tm, D_MODEL), tok)] * 2,
        out_shape=[jax.ShapeDtypeStruct((n, D_MODEL), F32), jax.ShapeDtypeStruct((n, D_MODEL), BF16)],
        compiler_params=_params("parallel"),
        name="merge_ln",
    )(x, x_bf, ya, ob, yc, w_gate, gate_b, w_ra, w_mo, w_out, ln_g, ln_b)


def _ffn_kernel(x_ref, xb_ref, w1_ref, w3_ref, w2_ref, lg_ref, lb_ref, o_ref, ob16_ref, acc_ref, *, alpha):
    j = pl.program_id(1)

    @pl.when(j == 0)
    def _():
        acc_ref[...] = jnp.zeros_like(acc_ref)

    xb = xb_ref[...]
    h1 = _dot(xb, w1_ref[...])
    h3 = _dot(xb, w3_ref[...])
    h = h1 * _sigmoid(h1) * h3
    acc_ref[...] += _dot(_bf(h), w2_ref[...])

    @pl.when(j == pl.num_programs(1) - 1)
    def _():
        out = _layer_norm(alpha * x_ref[...] + acc_ref[...], lg_ref[...], lb_ref[...])
        o_ref[...] = out
        ob16_ref[...] = _bf(out)


def _ffn(x, x_bf, w1, w3, w2, ln_g, ln_b, alpha, tm, tf):
    n = x.shape[0]
    d_ff = w1.shape[1]
    tok = lambda i, j: (i, 0)
    fixed = lambda i, j: (0, 0)
    return pl.pallas_call(
        functools.partial(_ffn_kernel, alpha=alpha),
        grid=(n // tm, d_ff // tf),
        in_specs=[
            pl.BlockSpec((tm, D_MODEL), tok), pl.BlockSpec((tm, D_MODEL), tok),
            pl.BlockSpec((D_MODEL, tf), lambda i, j: (0, j)),
            pl.BlockSpec((D_MODEL, tf), lambda i, j: (0, j)),
            pl.BlockSpec((tf, D_MODEL), lambda i, j: (j, 0)),
            pl.BlockSpec((1, D_MODEL), fixed), pl.BlockSpec((1, D_MODEL), fixed),
        ],
        out_specs=[pl.BlockSpec((tm, D_MODEL), tok)] * 2,
        out_shape=[jax.ShapeDtypeStruct((n, D_MODEL), F32), jax.ShapeDtypeStruct((n, D_MODEL), BF16)],
        scratch_shapes=[pltpu.VMEM((tm, D_MODEL), F32)],
        compiler_params=_params("parallel", "arbitrary"),
        name="ffn_ln",
    )(x, x_bf, w1, w3, w2, ln_g, ln_b)


def _rope_tables(t):
    pos = jnp.arange(t, dtype=F32)
    inv_freq = ROPE_THETA ** (-jnp.arange(0, QK_ROPE_DIM, 2, dtype=F32) / QK_ROPE_DIM)
    ang = pos[:, None] * inv_freq[None, :]
    cos, sin = jnp.cos(ang), jnp.sin(ang)
    half = QK_ROPE_DIM // 2
    ones = jnp.ones((t, QK_NOPE_DIM), F32)
    zeros = jnp.zeros((t, QK_NOPE_DIM), F32)
    tail1 = jnp.ones((t, LANES - QK_NOPE_DIM - QK_ROPE_DIM), F32)
    tail0 = jnp.zeros((t, LANES - QK_NOPE_DIM - QK_ROPE_DIM), F32)
    zh = jnp.zeros((t, half), F32)
    cos_t = jnp.concatenate([ones, cos, cos, tail1], axis=1)
    sin_lo = jnp.concatenate([zeros, -sin, zh, tail0], axis=1)
    sin_hi = jnp.concatenate([zeros, zh, sin, tail0], axis=1)
    return cos_t, sin_lo, sin_hi


def _row(z):
    return z.reshape(1, -1).astype(F32)


def _tile(t, want):
    return min(t, want)


def kernel(x, w_in, rwkv_mu, rwkv_w0, rwkv_w2, rwkv_a0, rwkv_a2, rwkv_g2, rwkv_k_k, rwkv_k_a, rwkv_r_k,
           rwkv_gn_g, rwkv_gn_b, rwkv_out, mla_q_norm, mla_q_up, mla_kv_norm, mla_kv_up, mla_out,
           s5_lambda_re, s5_lambda_im, s5_log_step, s5_b_re, s5_b_im, s5_c_re, s5_c_im, s5_d, s5_glu,
           gate_b, w_out, ln1_g, ln1_b, ffn_w1, ffn_w3, ffn_w2, ln2_g, ln2_b):
    bsz, t, _ = x.shape
    depth = w_in.shape[0]
    alpha = (2.0 * depth) ** 0.25
    n = bsz * t
    o_mla = RWKV_COLS
    o_s5 = o_mla + MLA_COLS
    o_gate = o_s5 + S5_COLS
    cos_t, sin_lo, sin_hi = _rope_tables(t)
    d_ff = ffn_w1.shape[2]
    tf = d_ff // 2 if (d_ff // 2) % LANES == 0 else d_ff

    x_bf = _bf(x)
    for l in range(depth):
        wl = w_in[l]
        zl = jnp.zeros((DECAY_LORA, RWKV_DIM), F32)
        w_lora = jnp.concatenate([jnp.concatenate([rwkv_w2[l], zl], axis=1),
                                  jnp.concatenate([zl, rwkv_a2[l]], axis=1)], axis=0)
        r, k, v, ld, a, g = _rwkv_prep(
            x_bf, _bf(wl[:, :o_mla]), _row(rwkv_mu[l]), _row(rwkv_w0[l]), _row(rwkv_a0[l]),
            _bf(w_lora), _bf(rwkv_g2[l]), _tile(t, 512))
        ya = _rwkv_scan(r, k, v, ld, a, g, _row(rwkv_k_k[l]), _row(rwkv_k_a[l]), _row(rwkv_r_k[l]),
                        _row(rwkv_gn_g[l]), _row(rwkv_gn_b[l]))
        nq = Q_LORA_RANK + KV_LORA_RANK
        w_kr = jnp.zeros((D_MODEL, LANES), F32).at[:, QK_NOPE_DIM:QK_NOPE_DIM + QK_ROPE_DIM].set(
            wl[:, o_mla + nq:o_s5])
        w_mla = _bf(jnp.concatenate([wl[:, o_mla:o_mla + nq], w_kr], axis=1))
        q_up = mla_q_up[l].reshape(Q_LORA_RANK, MLA_HEADS, QK_NOPE_DIM + QK_ROPE_DIM)
        q_up = jnp.pad(q_up, ((0, 0), (0, 0), (0, LANES - QK_NOPE_DIM - QK_ROPE_DIM)))
        q_up = _bf(q_up.reshape(Q_LORA_RANK, MLA_HEADS * LANES))
        kv_up = mla_kv_up[l].reshape(KV_LORA_RANK, MLA_HEADS, QK_NOPE_DIM + V_HEAD_DIM)
        k_up = jnp.pad(kv_up[:, :, :QK_NOPE_DIM], ((0, 0), (0, 0), (0, LANES - QK_NOPE_DIM)))
        kv_up = _bf(jnp.concatenate([k_up.reshape(KV_LORA_RANK, MLA_HEADS * LANES),
                                     kv_up[:, :, QK_NOPE_DIM:].reshape(KV_LORA_RANK, MLA_DIM)], axis=1))
        q, kx, vx = _mla_prep(x_bf, w_mla, _row(mla_q_norm[l]), q_up, _row(mla_kv_norm[l]), kv_up,
                              cos_t, sin_lo, sin_hi, _tile(t, 512))
        ob = _attention(q, kx, vx, _tile(t, 1024))
        lb_re, lb_im, bb_re, bb_im = _s5_discretize(s5_lambda_re[l], s5_lambda_im[l], s5_log_step[l],
                                                    s5_b_re[l], s5_b_im[l])
        tr = lambda z: z.reshape(S5_DIM, S5_STATE)
        bmat, cmat, lam = _s5_matrices(lb_re, lb_im, bb_re, bb_im, tr(s5_c_re[l]), tr(s5_c_im[l]))
        wg, kc, kst, lam8 = _s5_prep(bmat, cmat, lam)
        yc = _s5(x_bf, _bf(wl[:, o_s5:o_gate]), wg, kc, kst, lam8, _row(s5_d[l]), _bf(s5_glu[l]),
                 _tile(t, 512))
        xf = x.reshape(n, D_MODEL)
        x1, x1_bf = _merge(xf, x_bf.reshape(n, D_MODEL), ya.reshape(n, RWKV_DIM), ob.reshape(n, MLA_DIM),
                           yc.reshape(n, D_MODEL), _bf(wl[:, o_gate:]), _row(gate_b[l]),
                           _bf(rwkv_out[l]), _bf(mla_out[l]), _bf(w_out[l]), _row(ln1_g[l]), _row(ln1_b[l]),
                           alpha, _tile(n, 256))
        x2, x2_bf = _ffn(x1, x1_bf, _bf(ffn_w1[l]), _bf(ffn_w3[l]), _bf(ffn_w2[l]), _row(ln2_g[l]),
                         _row(ln2_b[l]), alpha, _tile(n, 512), tf)
        x = x2.reshape(bsz, t, D_MODEL)
        x_bf = x2_bf.reshape(bsz, t, D_MODEL)
    return x
```

```python
import functools
import math

import jax
import jax.numpy as jnp
from jax import lax
from jax.experimental import pallas as pl
from jax.experimental.pallas import tpu as pltpu

F32 = jnp.float32
BF16 = jnp.bfloat16

D_MODEL = 1024
RWKV_HEADS = 8
RWKV_HEAD_DIM = 64
RWKV_DIM = RWKV_HEADS * RWKV_HEAD_DIM
DECAY_LORA = 64
AAA_LORA = 64
GATE_LORA = 128
RWKV_GN_EPS = 64e-5
MLA_HEADS = 8
QK_NOPE_DIM = 64
QK_ROPE_DIM = 32
V_HEAD_DIM = 64
Q_LORA_RANK = 256
KV_LORA_RANK = 128
MLA_DIM = MLA_HEADS * V_HEAD_DIM
ROPE_THETA = 10000.0
ATTN_SCALE = 1.0 / math.sqrt(QK_NOPE_DIM + QK_ROPE_DIM)
LOG2_E = math.log2(math.e)
S5_DIM = 512
S5_GROUP = 16
S5_GROUPS = S5_DIM // S5_GROUP
S5_STATE = 64
N_BRANCHES = 3
LN_EPS = 1e-5
RMS_EPS = 1e-6
RWKV_COLS = 3 * RWKV_DIM + DECAY_LORA + AAA_LORA + GATE_LORA
MLA_COLS = Q_LORA_RANK + KV_LORA_RANK + QK_ROPE_DIM
S5_COLS = S5_DIM

LANES = 128
HEAD_PAIR = LANES // RWKV_HEAD_DIM
N_PAIRS = RWKV_HEADS // HEAD_PAIR
S5_SLAB_GROUPS = LANES // S5_GROUP
S5_SLABS = S5_GROUPS // S5_SLAB_GROUPS
S5_SLAB_STATE = S5_SLAB_GROUPS * S5_STATE
S5_CHUNK = 8
VMEM_LIMIT = 48 * 1024 * 1024

RWKV_CHUNK = 64
RWKV_BLOCK = 256
NEG_BIG = -1e30


def _bf(x):
    return x.astype(BF16)


def _dot(a, b):
    return jnp.dot(a, b, preferred_element_type=F32)


def _dot_nt(a, b):
    return lax.dot_general(a, b, (((1,), (1,)), ((), ())), preferred_element_type=F32)


def _dot_tn(a, b):
    return lax.dot_general(a, b, (((0,), (0,)), ((), ())), preferred_element_type=F32)


def _sigmoid(x):
    return 1.0 / (1.0 + jnp.exp(-x))


def _params(*sem):
    return pltpu.CompilerParams(dimension_semantics=sem, vmem_limit_bytes=VMEM_LIMIT)


def _layer_norm(y, g, b):
    mu = jnp.mean(y, -1, keepdims=True)
    d = y - mu
    var = jnp.mean(d * d, -1, keepdims=True)
    return d * lax.rsqrt(var + LN_EPS) * g + b


def _pair_sum(x, m0):
    s0 = jnp.sum(jnp.where(m0, x, 0.0), -1, keepdims=True)
    s1 = jnp.sum(jnp.where(m0, 0.0, x), -1, keepdims=True)
    return jnp.where(m0, s0, s1)


def _rwkv_prep_kernel(x_ref, w_ref, mu_ref, w0_ref, a0_ref, wl_ref, g2_ref,
                      xb_ref, r_ref, k_ref, v_ref, ld_ref, a_ref, g_ref, prev_ref):
    t = pl.program_id(1)

    @pl.when(t == 0)
    def _():
        prev_ref[...] = jnp.zeros_like(prev_ref)

    xb = _bf(x_ref[...])
    xb_ref[...] = xb
    p = _dot(xb, w_ref[...])
    rolled = pltpu.roll(p, 1, axis=0)
    row = lax.broadcasted_iota(jnp.int32, (p.shape[0], 1), 0)
    shifted = jnp.where(row == 0, prev_ref[0:1, :], rolled)
    prev_ref[0:1, :] = rolled[0:1, :]
    p = p + (shifted - p) * mu_ref[...]

    c0, c1, c2 = RWKV_DIM, 2 * RWKV_DIM, 3 * RWKV_DIM
    r_ref[...] = p[:, :c0].astype(r_ref.dtype)
    k_ref[...] = p[:, c0:c1].astype(k_ref.dtype)
    v_ref[...] = p[:, c1:c2].astype(v_ref.dtype)
    lora = p[:, c2:c2 + DECAY_LORA + AAA_LORA]
    lane = lax.broadcasted_iota(jnp.int32, (1, DECAY_LORA + AAA_LORA), 1)
    lora = jnp.where(lane < DECAY_LORA, jnp.tanh(lora), lora)
    wa = _dot(_bf(lora), wl_ref[...])
    ld_ref[...] = (-math.exp(-0.5)) * _sigmoid(w0_ref[...] + wa[:, :RWKV_DIM])
    a_ref[...] = _sigmoid(a0_ref[...] + wa[:, RWKV_DIM:]).astype(a_ref.dtype)
    dg = p[:, c2 + DECAY_LORA + AAA_LORA:]
    g_ref[...] = _dot(_bf(_sigmoid(dg)), g2_ref[...]).astype(g_ref.dtype)


def _rwkv_prep(x, w_rwkv, mu, w0, a0, w_lora, g2, tm):
    bsz, t, _ = x.shape
    outs = [jax.ShapeDtypeStruct((bsz, t, D_MODEL), BF16)] + [
        jax.ShapeDtypeStruct((bsz, t, RWKV_DIM), dt) for dt in (BF16, BF16, BF16, F32, BF16, BF16)]
    tok = lambda b, i: (b, i, 0)
    fixed = lambda b, i: (0, 0)
    o_spec = pl.BlockSpec((None, tm, RWKV_DIM), tok)
    return pl.pallas_call(
        _rwkv_prep_kernel,
        grid=(bsz, t // tm),
        in_specs=[
            pl.BlockSpec((None, tm, D_MODEL), tok),
            pl.BlockSpec((D_MODEL, RWKV_COLS), fixed),
            pl.BlockSpec((1, RWKV_COLS), fixed),
            pl.BlockSpec((1, RWKV_DIM), fixed),
            pl.BlockSpec((1, RWKV_DIM), fixed),
            pl.BlockSpec((DECAY_LORA + AAA_LORA, 2 * RWKV_DIM), fixed),
            pl.BlockSpec((GATE_LORA, RWKV_DIM), fixed),
        ],
        out_specs=[pl.BlockSpec((None, tm, D_MODEL), tok)] + [o_spec] * 6,
        out_shape=outs,
        scratch_shapes=[pltpu.VMEM((8, RWKV_COLS), F32)],
        compiler_params=_params("parallel", "arbitrary"),
        name="rwkv_prep",
    )(x, w_rwkv, mu, w0, a0, w_lora, g2)


def _rwkv_scan_kernel(r_ref, k_ref, v_ref, ld_ref, a_ref, g_ref,
                      kk_ref, ka_ref, rk_ref, gng_ref, gnb_ref, o_ref, s_ref):
    @pl.when(pl.program_id(1) == 0)
    def _():
        s_ref[...] = jnp.zeros_like(s_ref)

    par = (kk_ref[...], ka_ref[...], rk_ref[...], gng_ref[...], gnb_ref[...])
    nb = r_ref.shape[0]
    ins = [tuple(z[b].astype(F32) for z in (r_ref, k_ref, v_ref, ld_ref, a_ref, g_ref)) for b in range(nb)]
    outs, states = _rwkv_blocks(ins, par, [s_ref[b] for b in range(nb)])
    for b in range(nb):
        o_ref[b] = outs[b].astype(o_ref.dtype)
        s_ref[b] = states[b]


def _rwkv_blocks(ins, par, states):
    rows, chunk = RWKV_BLOCK, RWKV_CHUNK
    n_chunks = rows // chunk
    k_k, k_a, r_k, gn_g, gn_b = par
    nb = len(ins)
    inst = [(b, h) for b in range(nb) for h in range(HEAD_PAIR)]

    lane = lax.broadcasted_iota(jnp.int32, (1, LANES), 1)
    m0 = lane < RWKV_HEAD_DIM
    rin = lax.broadcasted_iota(jnp.int32, (rows, 1), 0) & (chunk - 1)

    pre = []
    for r, k, v, ld, a, g in ins:
        kk = k * k_k
        kk = kk * lax.rsqrt(jnp.maximum(_pair_sum(kk * kk, m0), 1e-12))
        k2 = k * (1.0 + (a - 1.0) * k_a)
        bv = kk * a
        cl = ld
        sh = 1
        while sh < chunk:
            cl = cl + jnp.where(rin >= sh, pltpu.roll(cl, sh, axis=0), 0.0)
            sh *= 2
        w_inc = jnp.exp(cl)
        w_inv = jnp.exp(-cl)
        at = -kk * jnp.exp(cl - ld)
        rt = r * w_inc
        pre.append(dict(
            k2=k2, w_inc=w_inc, rt=rt, bt=_bf(bv * w_inv), kt=_bf(k2 * w_inv), v_bf=_bf(v),
            at_h=(_bf(jnp.where(m0, at, 0.0)), _bf(jnp.where(m0, 0.0, at))),
            rt_h=(_bf(jnp.where(m0, rt, 0.0)), _bf(jnp.where(m0, 0.0, rt)))))

    grams = []
    for d in pre:
        lhs = jnp.concatenate([d["at_h"][0], d["at_h"][1], d["rt_h"][0], d["rt_h"][1]], axis=0)
        rhs = jnp.concatenate([d["bt"], d["kt"]], axis=0)
        grams.append(_dot_nt(lhs, rhs))

    ri = lax.broadcasted_iota(jnp.int32, (rows, rows), 0)
    ci = lax.broadcasted_iota(jnp.int32, (rows, rows), 1)
    same = (ri // chunk) == (ci // chunk)
    strict = same & (ri > ci)
    incl = same & (ri >= ci)
    eye = jnp.where(ri == ci, 1.0, 0.0)

    xs = [jnp.where(strict, grams[b][h * rows:(h + 1) * rows, :rows], 0.0) for b, h in inst]
    t_inv = [eye + x for x in xs]
    n = 2
    while n < chunk:
        xb = [_bf(x) for x in xs]
        xs = [_dot(x, x) for x in xb]
        t_inv = [t + _dot(_bf(t), _bf(x)) for t, x in zip(t_inv, xs)]
        n *= 2
    t_bf = [_bf(t) for t in t_inv]

    akv = [_dot(_bf(jnp.where(strict, grams[b][h * rows:(h + 1) * rows, rows:], 0.0)), pre[b]["v_bf"])
           for b, h in inst]
    up_h = [_dot(t, jnp.concatenate([_bf(z), pre[b]["at_h"][h]], axis=1))
            for t, z, (b, h) in zip(t_bf, akv, inst)]
    u0_h = [z[:, :LANES] for z in up_h]
    p_h = [z[:, LANES:] for z in up_h]
    a_rb = [_bf(jnp.where(incl, grams[b][(2 + h) * rows:(3 + h) * rows, :rows], 0.0)) for b, h in inst]
    a_rk = [_bf(jnp.where(incl, grams[b][(2 + h) * rows:(3 + h) * rows, rows:], 0.0)) for b, h in inst]
    rb_up = [_dot(a_rb[i], _bf(up_h[i])) for i in range(len(inst))]
    y0_h = [rb_up[i][:, :LANES] + _dot(a_rk[i], pre[b]["v_bf"]) for i, (b, h) in enumerate(inst)]
    qp_h = [z[:, LANES:] for z in rb_up]

    bi = lax.broadcasted_iota(jnp.int32, (LANES, LANES), 0) // RWKV_HEAD_DIM
    bj = lax.broadcasted_iota(jnp.int32, (LANES, LANES), 1) // RWKV_HEAD_DIM
    blockdiag = bi == bj

    p_all, q_all, u0_all, y0_all = [], [], [], []
    for b in range(nb):
        i0, i1 = HEAD_PAIR * b, HEAD_PAIR * b + 1
        p_all.append(_bf(p_h[i0] + p_h[i1]))
        q_all.append(_bf(pre[b]["rt"] + qp_h[i0] + qp_h[i1]))
        u0_all.append(_bf(jnp.where(m0, u0_h[i0], u0_h[i1])))
        y0_all.append(jnp.where(m0, y0_h[i0], y0_h[i1]))

    m_c, c_c = {}, {}
    for c in range(n_chunks):
        lo, hi = c * chunk, (c + 1) * chunk
        for b in range(nb):
            bt_c, kt_c = pre[b]["bt"][lo:hi], pre[b]["kt"][lo:hi]
            m_c[b, c] = _bf(jnp.where(blockdiag, _dot_tn(p_all[b][lo:hi], bt_c), 0.0))
            c_c[b, c] = jnp.where(
                blockdiag, _dot_tn(u0_all[b][lo:hi], bt_c) + _dot_tn(pre[b]["v_bf"][lo:hi], kt_c), 0.0)

    ys = [[] for _ in range(nb)]
    states = list(states)
    for c in range(n_chunks):
        lo, hi = c * chunk, (c + 1) * chunk
        for b in range(nb):
            s_bf = _bf(states[b])
            ys[b].append(_dot_nt(q_all[b][lo:hi], s_bf) + y0_all[b][lo:hi])
            states[b] = (states[b] + _dot(s_bf, m_c[b, c]) + c_c[b, c]) * pre[b]["w_inc"][hi - 1:hi, :]

    outs = []
    inv_n = 1.0 / RWKV_HEAD_DIM
    for b, (r, k, v, ld, a, g) in enumerate(ins):
        y = jnp.concatenate(ys[b], axis=0)
        mean = _pair_sum(y, m0) * inv_n
        d = y - mean
        var = _pair_sum(d * d, m0) * inv_n
        yn = d * lax.rsqrt(var + RWKV_GN_EPS) * gn_g + gn_b
        bonus = _pair_sum(r * pre[b]["k2"] * r_k, m0) * v
        outs.append((yn + bonus) * g)
    return outs, states


def _rwkv_scan(r, k, v, ld, a, g, k_k, k_a, r_k, gn_g, gn_b):
    bsz, t, _ = r.shape
    tok = pl.BlockSpec((bsz, RWKV_BLOCK, LANES), lambda h, c: (0, c, h))
    par = pl.BlockSpec((1, LANES), lambda h, c: (0, h))
    return pl.pallas_call(
        _rwkv_scan_kernel,
        grid=(N_PAIRS, t // RWKV_BLOCK),
        in_specs=[tok] * 6 + [par] * 5,
        out_specs=tok,
        out_shape=jax.ShapeDtypeStruct((bsz, t, RWKV_DIM), BF16),
        scratch_shapes=[pltpu.VMEM((bsz, LANES, LANES), F32)],
        compiler_params=_params("parallel", "arbitrary"),
        name="rwkv_scan",
    )(r, k, v, ld, a, g, k_k, k_a, r_k, gn_g, gn_b)


def _rope_rotated(w):
    lo, mid, hi = QK_NOPE_DIM, QK_NOPE_DIM + QK_ROPE_DIM // 2, QK_NOPE_DIM + QK_ROPE_DIM
    return jnp.zeros_like(w).at[..., lo:mid].set(-w[..., mid:hi]).at[..., mid:hi].set(w[..., lo:mid])


def _mla_prep_kernel(x_ref, w_ref, qn_ref, qup_ref, kvn_ref, kvup_ref, cos_ref, sin_ref,
                     q_ref, k_ref, v_ref):
    nq, nkv, wq = Q_LORA_RANK, KV_LORA_RANK, MLA_HEADS * LANES
    p = _dot(x_ref[...], w_ref[...])
    c_q = p[:, :nq]
    c_kv = p[:, nq:nq + nkv]
    c_q = c_q * lax.rsqrt(jnp.mean(c_q * c_q, -1, keepdims=True) + RMS_EPS) * qn_ref[...]
    c_kv = c_kv * lax.rsqrt(jnp.mean(c_kv * c_kv, -1, keepdims=True) + RMS_EPS) * kvn_ref[...]
    q = _dot(_bf(c_q), qup_ref[...])
    kv = _dot(_bf(c_kv), kvup_ref[...])
    cos_t, sin_t = cos_ref[...], sin_ref[...]
    reps = MLA_HEADS
    q = q[:, :wq] * jnp.tile(cos_t, (1, reps)) + q[:, wq:] * jnp.tile(sin_t, (1, reps))
    q_ref[...] = (q * (ATTN_SCALE * LOG2_E)).astype(q_ref.dtype)
    kr = p[:, nq + nkv:nq + nkv + LANES] * cos_t + p[:, nq + nkv + LANES:] * sin_t
    k_ref[...] = (kv[:, :wq] + jnp.tile(kr, (1, reps))).astype(k_ref.dtype)
    v_ref[...] = kv[:, wq:].astype(v_ref.dtype)


def _mla_prep(x_bf, w_mla, q_norm, q_up, kv_norm, kv_up, cos_t, sin_t, tm):
    bsz, t, _ = x_bf.shape
    tok = lambda b, i: (b, i, 0)
    fixed = lambda b, i: (0, 0)
    tab = pl.BlockSpec((tm, LANES), lambda b, i: (i, 0))
    wq = MLA_HEADS * LANES
    return pl.pallas_call(
        _mla_prep_kernel,
        grid=(bsz, t // tm),
        in_specs=[
            pl.BlockSpec((None, tm, D_MODEL), tok),
            pl.BlockSpec(w_mla.shape, fixed),
            pl.BlockSpec((1, Q_LORA_RANK), fixed),
            pl.BlockSpec(q_up.shape, fixed),
            pl.BlockSpec((1, KV_LORA_RANK), fixed),
            pl.BlockSpec(kv_up.shape, fixed),
            tab, tab,
        ],
        out_specs=[pl.BlockSpec((None, tm, wq), tok), pl.BlockSpec((None, tm, wq), tok),
                   pl.BlockSpec((None, tm, MLA_DIM), tok)],
        out_shape=[jax.ShapeDtypeStruct((bsz, t, wq), BF16), jax.ShapeDtypeStruct((bsz, t, wq), BF16),
                   jax.ShapeDtypeStruct((bsz, t, MLA_DIM), BF16)],
        compiler_params=_params("parallel", "parallel"),
        name="mla_prep",
    )(x_bf, w_mla, q_norm, q_up, kv_norm, kv_up, cos_t, sin_t)


def _attn_kernel(q_ref, k_ref, v_ref, o_ref, m_ref, l_ref, acc_ref, *, tq):
    i = pl.program_id(2)
    lane = lax.broadcasted_iota(jnp.int32, (1, LANES), 1)
    m0 = lane < V_HEAD_DIM
    m_ref[...] = jnp.full_like(m_ref, NEG_BIG)
    l_ref[...] = jnp.zeros_like(l_ref)
    acc_ref[...] = jnp.zeros_like(acc_ref)

    def step(start, nk, row_lo, masked):
        rows = slice(row_lo, tq)
        nrow = tq - row_lo
        v_blk = v_ref[pl.ds(start, nk), :]
        heads = range(HEAD_PAIR)
        s_h = [_dot_nt(q_ref[rows, h * LANES:(h + 1) * LANES], k_ref[pl.ds(start, nk), h * LANES:(h + 1) * LANES])
               for h in heads]
        alphas, ps = [], []
        for h in heads:
            s = s_h[h]
            if masked:
                qi = lax.broadcasted_iota(jnp.int32, (nrow, nk), 0) + (i * tq + row_lo)
                ki = lax.broadcasted_iota(jnp.int32, (nrow, nk), 1) + start
                s = jnp.where(ki <= qi, s, NEG_BIG)
            m_old = m_ref[h, rows]
            m_new = jnp.maximum(m_old, jnp.max(s, -1, keepdims=True))
            alpha = jnp.exp2(m_old - m_new)
            p = jnp.exp2(s - jnp.tile(m_new, (1, nk // LANES)))
            p_sum = p[:, :LANES]
            for c in range(1, nk // LANES):
                p_sum = p_sum + p[:, c * LANES:(c + 1) * LANES]
            l_ref[h, rows] = alpha * l_ref[h, rows] + p_sum
            m_ref[h, rows] = m_new
            alphas.append(alpha)
            ps.append(_bf(p))
        for h in heads:
            acc_ref[h, rows] = alphas[h] * acc_ref[h, rows] + _dot(ps[h], v_blk)

    def body(j, carry):
        step(pl.multiple_of(j * tq, tq), tq, 0, False)
        return carry

    lax.fori_loop(0, i, body, 0)
    half = tq // 2
    base = pl.multiple_of(i * tq, tq)
    step(base, half, 0, True)
    step(base + half, half, half, True)
    l0 = jnp.sum(l_ref[0], -1, keepdims=True)
    l1 = jnp.sum(l_ref[1], -1, keepdims=True)
    o_ref[...] = jnp.where(m0, acc_ref[0] / l0, acc_ref[1] / l1).astype(o_ref.dtype)


def _attention(q, k, v, tq):
    bsz, t, _ = q.shape
    return pl.pallas_call(
        functools.partial(_attn_kernel, tq=tq),
        grid=(bsz, N_PAIRS, t // tq),
        in_specs=[
            pl.BlockSpec((None, tq, HEAD_PAIR * LANES), lambda b, h, i: (b, i, h)),
            pl.BlockSpec((None, t, HEAD_PAIR * LANES), lambda b, h, i: (b, 0, h)),
            pl.BlockSpec((None, t, LANES), lambda b, h, i: (b, 0, h)),
        ],
        out_specs=pl.BlockSpec((None, tq, LANES), lambda b, h, i: (b, i, h)),
        out_shape=jax.ShapeDtypeStruct((bsz, t, MLA_DIM), BF16),
        scratch_shapes=[pltpu.VMEM((HEAD_PAIR, tq, LANES), F32)] * 3,
        compiler_params=_params("parallel", "parallel", "arbitrary"),
        name="mla_attention",
    )(q, k, v)


def _s5_disc_kernel(lre_ref, lim_ref, ls_ref, br_ref, bi_ref, lbr_ref, lbi_ref, bbr_ref, bbi_ref):
    lam_re = jnp.minimum(lre_ref[...], -1e-4)
    lam_im = lim_ref[...]
    step = jnp.exp(ls_ref[...])
    mag = jnp.exp(lam_re * step)
    ang = lam_im * step
    lb_re, lb_im = mag * jnp.cos(ang), mag * jnp.sin(ang)
    den = lam_re * lam_re + lam_im * lam_im
    n_re = lb_re - 1.0
    f_re = (n_re * lam_re + lb_im * lam_im) / den
    f_im = (lb_im * lam_re - n_re * lam_im) / den
    br, bi = br_ref[...], bi_ref[...]
    lbr_ref[...] = lb_re
    lbi_ref[...] = lb_im
    bbr_ref[...] = f_re * br - f_im * bi
    bbi_ref[...] = f_re * bi + f_im * br


def _s5_discretize(lambda_re, lambda_im, log_step, b_re, b_im):
    rep = lambda z: jnp.repeat(z, S5_GROUP, axis=0)
    ls = rep(jnp.broadcast_to(log_step[:, None], (S5_GROUPS, S5_STATE)))
    tr = lambda z: jnp.transpose(z, (0, 2, 1)).reshape(S5_DIM, S5_STATE)
    out = jax.ShapeDtypeStruct((S5_DIM, S5_STATE), F32)
    return pl.pallas_call(_s5_disc_kernel, out_shape=[out] * 4, name="s5_discretize")(
        rep(lambda_re), rep(lambda_im), ls, tr(b_re), tr(b_im))


def _split_bf(z):
    hi = _bf(z)
    return hi, _bf(z - hi.astype(F32))


def _s5_prep_kernel(b_ref, c_ref, ct_ref, lam_ref, wg_ref, kc_ref, kst_ref, lam8_ref):
    n = S5_SLAB_STATE
    b, c, ct, lam = b_ref[0], c_ref[0], ct_ref[0], lam_ref[0]
    ar, ai = lam[:, :n], lam[:, n:]
    br, bi = b[:, :n], b[:, n:]
    ctr, cti = ct[:, :n], ct[:, n:]
    c_hi, c_lo = _split_bf(c)
    pr, pi = jnp.ones_like(ar), jnp.zeros_like(ai)
    for j in range(S5_CHUNK):
        w = jnp.concatenate([br * pr - bi * pi, br * pi + bi * pr], axis=1)
        w_hi, w_lo = _split_bf(w)
        rows = slice(j * LANES, (j + 1) * LANES)
        wg_ref[0, rows, :] = w_hi
        kc_ref[0, rows, :] = _bf(_dot(w_hi, c_hi) + _dot(w_lo, c_hi) + _dot(w_hi, c_lo))
        pr, pi = pr * ar - pi * ai, pr * ai + pi * ar
        kst_ref[0, rows, :] = _bf(jnp.concatenate([ctr * pr + cti * pi, cti * pr - ctr * pi], axis=1))
    lam8_ref[0] = jnp.concatenate([pr, pi], axis=1)


def _s5_prep(bmat, cmat, lam):
    n2 = 2 * S5_SLAB_STATE
    kq = S5_CHUNK * LANES
    slab = lambda s: (s, 0, 0)
    return pl.pallas_call(
        _s5_prep_kernel,
        grid=(S5_SLABS,),
        in_specs=[pl.BlockSpec((1, LANES, n2), slab), pl.BlockSpec((1, n2, LANES), slab),
                  pl.BlockSpec((1, LANES, n2), slab), pl.BlockSpec((1, 1, n2), slab)],
        out_specs=[pl.BlockSpec((1, kq, n2), slab), pl.BlockSpec((1, kq, LANES), slab),
                   pl.BlockSpec((1, kq, n2), slab), pl.BlockSpec((1, 1, n2), slab)],
        out_shape=[jax.ShapeDtypeStruct((S5_SLABS, kq, n2), BF16),
                   jax.ShapeDtypeStruct((S5_SLABS, kq, LANES), BF16),
                   jax.ShapeDtypeStruct((S5_SLABS, kq, n2), BF16),
                   jax.ShapeDtypeStruct((S5_SLABS, 1, n2), F32)],
        compiler_params=_params("parallel"),
        name="s5_prep",
    )(bmat, cmat, jnp.swapaxes(cmat, 1, 2), lam)


def _s5_kernel(x_ref, w_ref, wg_ref, kc_ref, kst_ref, lam8_ref, d_ref, glu_ref, o_ref,
               carry_ref, u_ref, ys_ref, *, ts):
    @pl.when(pl.program_id(1) == 0)
    def _():
        carry_ref[...] = jnp.zeros_like(carry_ref)

    n = S5_SLAB_STATE
    nc = ts // S5_CHUNK
    u = _dot(x_ref[...], w_ref[...])
    for sl in range(S5_SLABS):
        u_ref[sl] = u[:, sl * LANES:(sl + 1) * LANES]
    rin = lax.broadcasted_iota(jnp.int32, (ts, 1), 0) & (S5_CHUNK - 1)
    crow = lax.broadcasted_iota(jnp.int32, (nc, 1), 0)
    first = crow == 0
    slabs = range(S5_SLABS)
    x_end = [jnp.concatenate(
        [_bf(u_ref[sl, pl.ds(S5_CHUNK - 1 - j, nc, stride=S5_CHUNK), :]) for j in range(S5_CHUNK)], axis=1)
        for sl in slabs]
    gain = [_dot(x_end[sl], wg_ref[sl]) for sl in slabs]
    lam8 = [lam8_ref[sl] for sl in slabs]
    ar = [z[:, :n] for z in lam8]
    ai = [z[:, n:] for z in lam8]
    cr = [carry_ref[sl, 0:1, :n] for sl in slabs]
    cim = [carry_ref[sl, 0:1, n:] for sl in slabs]
    er = [gain[sl][:, :n] + jnp.where(first, ar[sl] * cr[sl] - ai[sl] * cim[sl], 0.0) for sl in slabs]
    ei = [gain[sl][:, n:] + jnp.where(first, ar[sl] * cim[sl] + ai[sl] * cr[sl], 0.0) for sl in slabs]
    sh = 1
    while sh < nc:
        keep = crow >= sh
        for sl in slabs:
            sr = jnp.where(keep, pltpu.roll(er[sl], sh, axis=0), 0.0)
            si = jnp.where(keep, pltpu.roll(ei[sl], sh, axis=0), 0.0)
            er[sl], ei[sl] = er[sl] + ar[sl] * sr - ai[sl] * si, ei[sl] + ar[sl] * si + ai[sl] * sr
            ar[sl], ai[sl] = ar[sl] * ar[sl] - ai[sl] * ai[sl], 2.0 * ar[sl] * ai[sl]
        sh *= 2
    for sl in slabs:
        carry_ref[sl, 0:1, :n] = er[sl][nc - 1:nc, :]
        carry_ref[sl, 0:1, n:] = ei[sl][nc - 1:nc, :]
    h0 = [_bf(jnp.concatenate([jnp.where(first, cr[sl], pltpu.roll(er[sl], 1, axis=0)),
                               jnp.where(first, cim[sl], pltpu.roll(ei[sl], 1, axis=0))], axis=1)) for sl in slabs]
    z = [_dot_nt(h0[sl], kst_ref[sl]) for sl in slabs]
    for sl in slabs:
        for i in range(S5_CHUNK):
            ys_ref[sl, pl.ds(i, nc, stride=S5_CHUNK), :] = z[sl][:, i * LANES:(i + 1) * LANES]
    ys = []
    for sl in slabs:
        us = u[:, sl * LANES:(sl + 1) * LANES]
        lagged = [_bf(us)] + [_bf(jnp.where(rin >= j, pltpu.roll(us, j, axis=0), 0.0))
                              for j in range(1, S5_CHUNK)]
        ys.append(_dot(jnp.concatenate(lagged, axis=1), kc_ref[sl]) + ys_ref[sl])
    y = jnp.concatenate(ys, axis=1) + d_ref[...] * u
    y = 0.5 * y * (1.0 + jnp.tanh(math.sqrt(2.0 / math.pi) * (y + 0.044715 * (y * y * y))))
    h = _dot(_bf(y), glu_ref[...])
    o_ref[...] = (h[:, :D_MODEL] * _sigmoid(h[:, D_MODEL:])).astype(o_ref.dtype)


def _s5(x_bf, w_s5, wg, kc, kst, lam8, d_skip, w_glu, ts):
    bsz, t, _ = x_bf.shape
    tok = lambda b, i: (b, i, 0)
    f2 = lambda b, i: (0, 0)
    f3 = lambda b, i: (0, 0, 0)
    return pl.pallas_call(
        functools.partial(_s5_kernel, ts=ts),
        grid=(bsz, t // ts),
        in_specs=[
            pl.BlockSpec((None, ts, D_MODEL), tok),
            pl.BlockSpec(w_s5.shape, f2),
            pl.BlockSpec(wg.shape, f3),
            pl.BlockSpec(kc.shape, f3),
            pl.BlockSpec(kst.shape, f3),
            pl.BlockSpec(lam8.shape, f3),
            pl.BlockSpec((1, S5_DIM), f2),
            pl.BlockSpec(w_glu.shape, f2),
        ],
        out_specs=pl.BlockSpec((None, ts, D_MODEL), tok),
        out_shape=jax.ShapeDtypeStruct((bsz, t, D_MODEL), BF16),
        scratch_shapes=[pltpu.VMEM((S5_SLABS, 8, 2 * S5_SLAB_STATE), F32),
                        pltpu.VMEM((S5_SLABS, ts, LANES), F32), pltpu.VMEM((S5_SLABS, ts, LANES), F32)],
        compiler_params=_params("parallel", "arbitrary"),
        name="s5_scan",
    )(x_bf, w_s5, wg, kc, kst, lam8, d_skip, w_glu)


def _s5_matrices(lb_re, lb_im, bb_re, bb_im, c_re, c_im):
    eye = jnp.eye(S5_SLAB_GROUPS, dtype=F32)
    s, g, c, p = S5_SLABS, S5_SLAB_GROUPS, S5_GROUP, S5_STATE

    def b_blocks(bb):
        return jnp.einsum("sgcp,gh->sgchp", bb.reshape(s, g, c, p), eye).reshape(s, g * c, g * p)

    def c_blocks(cc):
        return jnp.einsum("sgcp,gh->sgphc", cc.reshape(s, g, c, p), eye).reshape(s, g * p, g * c)

    bmat = jnp.concatenate([b_blocks(bb_re), b_blocks(bb_im)], axis=2)
    cmat = jnp.concatenate([c_blocks(c_re), -c_blocks(c_im)], axis=1)
    row = lambda z: z[::S5_GROUP].reshape(s, 1, g * p)
    lam = jnp.concatenate([row(lb_re), row(lb_im)], axis=2)
    return bmat, cmat, lam


def _merge_kernel(x_ref, ya_ref, ob_ref, yc_ref, wg_ref, gb_ref, wra_ref, wmo_ref, wo_ref,
                  lg_ref, lb_ref, o_ref, *, alpha):
    gates = _sigmoid(_dot(_bf(x_ref[...]), wg_ref[...]) + gb_ref[...])
    y_a = _dot(ya_ref[...], wra_ref[...])
    y_b = _dot(ob_ref[...], wmo_ref[...])
    y_c = yc_ref[...].astype(F32)
    merged = (gates[:, :D_MODEL] * y_a + gates[:, D_MODEL:2 * D_MODEL] * y_b
              + gates[:, 2 * D_MODEL:] * y_c)
    y = alpha * x_ref[...] + _dot(_bf(merged), wo_ref[...])
    o_ref[...] = _layer_norm(y, lg_ref[...], lb_ref[...])


def _merge(x, ya, ob, yc, w_gate, gate_b, w_ra, w_mo, w_out, ln_g, ln_b, alpha, tm):
    n = x.shape[0]
    tok = lambda i: (i, 0)
    fixed = lambda i: (0, 0)
    full = lambda a: pl.BlockSpec(a.shape, fixed, pipeline_mode=pl.Buffered(1))
    return pl.pallas_call(
        functools.partial(_merge_kernel, alpha=alpha),
        grid=(n // tm,),
        in_specs=[
            pl.BlockSpec((tm, D_MODEL), tok),
            pl.BlockSpec((tm, RWKV_DIM), tok), pl.BlockSpec((tm, MLA_DIM), tok),
            pl.BlockSpec((tm, D_MODEL), tok),
            full(w_gate), full(gate_b), full(w_ra), full(w_mo), full(w_out), full(ln_g), full(ln_b),
        ],
        out_specs=pl.BlockSpec((tm, D_MODEL), tok),
        out_shape=jax.ShapeDtypeStruct((n, D_MODEL), F32),
        compiler_params=_params("parallel"),
        name="merge_ln",
    )(x, ya, ob, yc, w_gate, gate_b, w_ra, w_mo, w_out, ln_g, ln_b)


def _ffn_kernel(x_ref, w1_ref, w3_ref, w2_ref, lg_ref, lb_ref, o_ref, *, alpha, tf):
    x = x_ref[...]
    xb = _bf(x)
    acc = alpha * x
    for j in range(w1_ref.shape[1] // tf):
        cols = slice(j * tf, (j + 1) * tf)
        h1 = _dot(xb, w1_ref[:, cols])
        h3 = _dot(xb, w3_ref[:, cols])
        acc = acc + _dot(_bf(h1 * _sigmoid(h1) * h3), w2_ref[cols, :])
    o_ref[...] = _layer_norm(acc, lg_ref[...], lb_ref[...])


def _ffn(x, w1, w3, w2, ln_g, ln_b, alpha, tm, tf):
    n = x.shape[0]
    tok = lambda i: (i, 0)
    fixed = lambda i: (0, 0)
    resident = lambda a: pl.BlockSpec(a.shape, fixed, pipeline_mode=pl.Buffered(1))
    return pl.pallas_call(
        functools.partial(_ffn_kernel, alpha=alpha, tf=tf),
        grid=(n // tm,),
        in_specs=[
            pl.BlockSpec((tm, D_MODEL), tok),
            resident(w1), resident(w3), resident(w2),
            pl.BlockSpec((1, D_MODEL), fixed), pl.BlockSpec((1, D_MODEL), fixed),
        ],
        out_specs=pl.BlockSpec((tm, D_MODEL), tok),
        out_shape=jax.ShapeDtypeStruct((n, D_MODEL), F32),
        compiler_params=_params("parallel"),
        name="ffn_ln",
    )(x, w1, w3, w2, ln_g, ln_b)


def _rope_tables(t):
    pos = jnp.arange(t, dtype=F32)
    inv_freq = ROPE_THETA ** (-jnp.arange(0, QK_ROPE_DIM, 2, dtype=F32) / QK_ROPE_DIM)
    ang = pos[:, None] * inv_freq[None, :]
    cos, sin = jnp.cos(ang), jnp.sin(ang)
    ones = jnp.ones((t, QK_NOPE_DIM), F32)
    zeros = jnp.zeros((t, QK_NOPE_DIM), F32)
    tail1 = jnp.ones((t, LANES - QK_NOPE_DIM - QK_ROPE_DIM), F32)
    tail0 = jnp.zeros((t, LANES - QK_NOPE_DIM - QK_ROPE_DIM), F32)
    cos_t = jnp.concatenate([ones, cos, cos, tail1], axis=1)
    sin_t = jnp.concatenate([zeros, sin, sin, tail0], axis=1)
    return cos_t, sin_t


def _row(z):
    return z.reshape(1, -1).astype(F32)


ROW_TILE = dict(rwkv_prep=1024, mla_prep=512, attention=1024, s5=512, merge=512, ffn=512)


def _tile(t, name):
    return min(t, ROW_TILE[name])


def kernel(x, w_in, rwkv_mu, rwkv_w0, rwkv_w2, rwkv_a0, rwkv_a2, rwkv_g2, rwkv_k_k, rwkv_k_a, rwkv_r_k,
           rwkv_gn_g, rwkv_gn_b, rwkv_out, mla_q_norm, mla_q_up, mla_kv_norm, mla_kv_up, mla_out,
           s5_lambda_re, s5_lambda_im, s5_log_step, s5_b_re, s5_b_im, s5_c_re, s5_c_im, s5_d, s5_glu,
           gate_b, w_out, ln1_g, ln1_b, ffn_w1, ffn_w3, ffn_w2, ln2_g, ln2_b):
    bsz, t, _ = x.shape
    depth = w_in.shape[0]
    alpha = (2.0 * depth) ** 0.25
    n = bsz * t
    o_mla = RWKV_COLS
    o_s5 = o_mla + MLA_COLS
    o_gate = o_s5 + S5_COLS
    cos_t, sin_t = _rope_tables(t)
    d_ff = ffn_w1.shape[2]
    tf = d_ff // 2 if (d_ff // 2) % LANES == 0 else d_ff

    for l in range(depth):
        wl = w_in[l]
        zl = jnp.zeros((DECAY_LORA, RWKV_DIM), F32)
        w_lora = jnp.concatenate([jnp.concatenate([rwkv_w2[l], zl], axis=1),
                                  jnp.concatenate([zl, rwkv_a2[l]], axis=1)], axis=0)
        x_bf, r, k, v, ld, a, g = _rwkv_prep(
            x, _bf(wl[:, :o_mla]), _row(rwkv_mu[l]), _row(rwkv_w0[l]), _row(rwkv_a0[l]),
            _bf(w_lora), _bf(rwkv_g2[l]), _tile(t, "rwkv_prep"))
        ya = _rwkv_scan(r, k, v, ld, a, g, _row(rwkv_k_k[l]), _row(rwkv_k_a[l]), _row(rwkv_r_k[l]),
                        _row(rwkv_gn_g[l]), _row(rwkv_gn_b[l]))
        nq = Q_LORA_RANK + KV_LORA_RANK
        w_kr = jnp.zeros((D_MODEL, LANES), F32).at[:, QK_NOPE_DIM:QK_NOPE_DIM + QK_ROPE_DIM].set(
            wl[:, o_mla + nq:o_s5])
        w_mla = _bf(jnp.concatenate([wl[:, o_mla:o_mla + nq], w_kr, _rope_rotated(w_kr)], axis=1))
        q_up = mla_q_up[l].reshape(Q_LORA_RANK, MLA_HEADS, QK_NOPE_DIM + QK_ROPE_DIM)
        q_up = jnp.pad(q_up, ((0, 0), (0, 0), (0, LANES - QK_NOPE_DIM - QK_ROPE_DIM)))
        q_up = _bf(jnp.concatenate([q_up.reshape(Q_LORA_RANK, MLA_HEADS * LANES),
                                    _rope_rotated(q_up).reshape(Q_LORA_RANK, MLA_HEADS * LANES)], axis=1))
        kv_up = mla_kv_up[l].reshape(KV_LORA_RANK, MLA_HEADS, QK_NOPE_DIM + V_HEAD_DIM)
        k_up = jnp.pad(kv_up[:, :, :QK_NOPE_DIM], ((0, 0), (0, 0), (0, LANES - QK_NOPE_DIM)))
        kv_up = _bf(jnp.concatenate([k_up.reshape(KV_LORA_RANK, MLA_HEADS * LANES),
                                     kv_up[:, :, QK_NOPE_DIM:].reshape(KV_LORA_RANK, MLA_DIM)], axis=1))
        q, kx, vx = _mla_prep(x_bf, w_mla, _row(mla_q_norm[l]), q_up, _row(mla_kv_norm[l]), kv_up,
                              cos_t, sin_t, _tile(t, "mla_prep"))
        ob = _attention(q, kx, vx, _tile(t, "attention"))
        lb_re, lb_im, bb_re, bb_im = _s5_discretize(s5_lambda_re[l], s5_lambda_im[l], s5_log_step[l],
                                                    s5_b_re[l], s5_b_im[l])
        tr = lambda z: z.reshape(S5_DIM, S5_STATE)
        bmat, cmat, lam = _s5_matrices(lb_re, lb_im, bb_re, bb_im, tr(s5_c_re[l]), tr(s5_c_im[l]))
        wg, kc, kst, lam8 = _s5_prep(bmat, cmat, lam)
        yc = _s5(x_bf, _bf(wl[:, o_s5:o_gate]), wg, kc, kst, lam8, _row(s5_d[l]), _bf(s5_glu[l]),
                 _tile(t, "s5"))
        x1 = _merge(x.reshape(n, D_MODEL), ya.reshape(n, RWKV_DIM), ob.reshape(n, MLA_DIM),
                    yc.reshape(n, D_MODEL), _bf(wl[:, o_gate:]), _row(gate_b[l]),
                    _bf(rwkv_out[l]), _bf(mla_out[l]), _bf(w_out[l]), _row(ln1_g[l]), _row(ln1_b[l]),
                    alpha, _tile(n, "merge"))
        x2 = _ffn(x1, _bf(ffn_w1[l]), _bf(ffn_w3[l]), _bf(ffn_w2[l]), _row(ln2_g[l]),
                  _row(ln2_b[l]), alpha, _tile(n, "ffn"), tf)
        x = x2.reshape(bsz, t, D_MODEL)
    return x
```

```python
import functools
import math

import jax
import jax.numpy as jnp
from jax import lax
from jax.experimental import pallas as pl
from jax.experimental.pallas import tpu as pltpu

F32 = jnp.float32
BF16 = jnp.bfloat16

D_MODEL = 1024
RWKV_HEADS = 8
RWKV_HEAD_DIM = 64
RWKV_DIM = RWKV_HEADS * RWKV_HEAD_DIM
DECAY_LORA = 64
AAA_LORA = 64
GATE_LORA = 128
RWKV_GN_EPS = 64e-5
MLA_HEADS = 8
QK_NOPE_DIM = 64
QK_ROPE_DIM = 32
V_HEAD_DIM = 64
Q_LORA_RANK = 256
KV_LORA_RANK = 128
MLA_DIM = MLA_HEADS * V_HEAD_DIM
ROPE_THETA = 10000.0
ATTN_SCALE = 1.0 / math.sqrt(QK_NOPE_DIM + QK_ROPE_DIM)
LOG2_E = math.log2(math.e)
S5_DIM = 512
S5_GROUP = 16
S5_GROUPS = S5_DIM // S5_GROUP
S5_STATE = 64
N_BRANCHES = 3
LN_EPS = 1e-5
RMS_EPS = 1e-6
RWKV_COLS = 3 * RWKV_DIM + DECAY_LORA + AAA_LORA + GATE_LORA
MLA_COLS = Q_LORA_RANK + KV_LORA_RANK + QK_ROPE_DIM
S5_COLS = S5_DIM

LANES = 128
HEAD_PAIR = LANES // RWKV_HEAD_DIM
N_PAIRS = RWKV_HEADS // HEAD_PAIR
S5_SLAB_GROUPS = LANES // S5_GROUP
S5_SLABS = S5_GROUPS // S5_SLAB_GROUPS
S5_SLAB_STATE = S5_SLAB_GROUPS * S5_STATE
S5_CHUNK = 8
VMEM_LIMIT = 48 * 1024 * 1024

RWKV_CHUNK = 64
RWKV_BLOCK = 256
NEG_BIG = -1e30
CAST_ROWS = 512


def _bf(x):
    return x.astype(BF16)


def _dot(a, b):
    return jnp.dot(a, b, preferred_element_type=F32)


def _dot_nt(a, b):
    return lax.dot_general(a, b, (((1,), (1,)), ((), ())), preferred_element_type=F32)


def _dot_tn(a, b):
    return lax.dot_general(a, b, (((0,), (0,)), ((), ())), preferred_element_type=F32)


def _sigmoid(x):
    return 1.0 / (1.0 + jnp.exp(-x))


def _params(*sem):
    return pltpu.CompilerParams(dimension_semantics=sem, vmem_limit_bytes=VMEM_LIMIT)


def _cast_kernel(w_ref, o_ref):
    o_ref[...] = w_ref[...].astype(o_ref.dtype)


def _to_bf16(w):
    depth, rows, cols = w.shape
    tr = math.gcd(rows, CAST_ROWS)
    spec = pl.BlockSpec((None, tr, cols), lambda l, i: (l, i, 0))
    return pl.pallas_call(
        _cast_kernel, grid=(depth, rows // tr), in_specs=[spec], out_specs=spec,
        out_shape=jax.ShapeDtypeStruct(w.shape, BF16),
        compiler_params=_params("parallel", "parallel"), name="cast_bf16",
    )(w)


def _layer_norm(y, g, b):
    mu = jnp.mean(y, -1, keepdims=True)
    d = y - mu
    var = jnp.mean(d * d, -1, keepdims=True)
    return d * lax.rsqrt(var + LN_EPS) * g + b


def _pair_sum(x, m0):
    s0 = jnp.sum(jnp.where(m0, x, 0.0), -1, keepdims=True)
    s1 = jnp.sum(jnp.where(m0, 0.0, x), -1, keepdims=True)
    return jnp.where(m0, s0, s1)


def _rwkv_prep_kernel(x_ref, w_ref, mu_ref, w0_ref, a0_ref, wl_ref, g2_ref,
                      xb_ref, r_ref, k_ref, v_ref, ld_ref, a_ref, g_ref, prev_ref):
    t = pl.program_id(1)

    @pl.when(t == 0)
    def _():
        prev_ref[...] = jnp.zeros_like(prev_ref)

    xb = _bf(x_ref[...])
    xb_ref[...] = xb
    p = _dot(xb, w_ref[...])
    rolled = pltpu.roll(p, 1, axis=0)
    row = lax.broadcasted_iota(jnp.int32, (p.shape[0], 1), 0)
    shifted = jnp.where(row == 0, prev_ref[0:1, :], rolled)
    prev_ref[0:1, :] = rolled[0:1, :]
    p = p + (shifted - p) * mu_ref[...]

    c0, c1, c2 = RWKV_DIM, 2 * RWKV_DIM, 3 * RWKV_DIM
    r_ref[...] = p[:, :c0].astype(r_ref.dtype)
    k_ref[...] = p[:, c0:c1].astype(k_ref.dtype)
    v_ref[...] = p[:, c1:c2].astype(v_ref.dtype)
    lora = p[:, c2:c2 + DECAY_LORA + AAA_LORA]
    lane = lax.broadcasted_iota(jnp.int32, (1, DECAY_LORA + AAA_LORA), 1)
    lora = jnp.where(lane < DECAY_LORA, jnp.tanh(lora), lora)
    wa = _dot(_bf(lora), wl_ref[...])
    ld_ref[...] = (-math.exp(-0.5)) * _sigmoid(w0_ref[...] + wa[:, :RWKV_DIM])
    a_ref[...] = _sigmoid(a0_ref[...] + wa[:, RWKV_DIM:]).astype(a_ref.dtype)
    dg = p[:, c2 + DECAY_LORA + AAA_LORA:]
    g_ref[...] = _dot(_bf(_sigmoid(dg)), g2_ref[...]).astype(g_ref.dtype)


def _rwkv_prep(x, w_rwkv, mu, w0, a0, w_lora, g2, tm):
    bsz, t, _ = x.shape
    outs = [jax.ShapeDtypeStruct((bsz, t, D_MODEL), BF16)] + [
        jax.ShapeDtypeStruct((bsz, t, RWKV_DIM), dt) for dt in (BF16, BF16, BF16, F32, BF16, BF16)]
    tok = lambda b, i: (b, i, 0)
    fixed = lambda b, i: (0, 0)
    o_spec = pl.BlockSpec((None, tm, RWKV_DIM), tok)
    return pl.pallas_call(
        _rwkv_prep_kernel,
        grid=(bsz, t // tm),
        in_specs=[
            pl.BlockSpec((None, tm, D_MODEL), tok),
            pl.BlockSpec((D_MODEL, RWKV_COLS), fixed),
            pl.BlockSpec((1, RWKV_COLS), fixed),
            pl.BlockSpec((1, RWKV_DIM), fixed),
            pl.BlockSpec((1, RWKV_DIM), fixed),
            pl.BlockSpec((DECAY_LORA + AAA_LORA, 2 * RWKV_DIM), fixed),
            pl.BlockSpec((GATE_LORA, RWKV_DIM), fixed),
        ],
        out_specs=[pl.BlockSpec((None, tm, D_MODEL), tok)] + [o_spec] * 6,
        out_shape=outs,
        scratch_shapes=[pltpu.VMEM((8, RWKV_COLS), F32)],
        compiler_params=_params("parallel", "arbitrary"),
        name="rwkv_prep",
    )(x, w_rwkv, mu, w0, a0, w_lora, g2)


def _rwkv_scan_kernel(r_ref, k_ref, v_ref, ld_ref, a_ref, g_ref,
                      kk_ref, ka_ref, rk_ref, gng_ref, gnb_ref, o_ref, s_ref):
    @pl.when(pl.program_id(1) == 0)
    def _():
        s_ref[...] = jnp.zeros_like(s_ref)

    par = (kk_ref[...], ka_ref[...], rk_ref[...], gng_ref[...], gnb_ref[...])
    nb = r_ref.shape[0]
    ins = [tuple(z[b].astype(F32) for z in (r_ref, k_ref, v_ref, ld_ref, a_ref, g_ref)) for b in range(nb)]
    outs, states = _rwkv_blocks(ins, par, [s_ref[b] for b in range(nb)])
    for b in range(nb):
        o_ref[b] = outs[b].astype(o_ref.dtype)
        s_ref[b] = states[b]


def _rwkv_blocks(ins, par, states):
    rows, chunk = RWKV_BLOCK, RWKV_CHUNK
    n_chunks = rows // chunk
    k_k, k_a, r_k, gn_g, gn_b = par
    nb = len(ins)
    inst = [(b, h) for b in range(nb) for h in range(HEAD_PAIR)]

    lane = lax.broadcasted_iota(jnp.int32, (1, LANES), 1)
    m0 = lane < RWKV_HEAD_DIM
    rin = lax.broadcasted_iota(jnp.int32, (rows, 1), 0) & (chunk - 1)

    pre = []
    for r, k, v, ld, a, g in ins:
        kk = k * k_k
        kk = kk * lax.rsqrt(jnp.maximum(_pair_sum(kk * kk, m0), 1e-12))
        k2 = k * (1.0 + (a - 1.0) * k_a)
        bv = kk * a
        cl = ld
        sh = 1
        while sh < chunk:
            cl = cl + jnp.where(rin >= sh, pltpu.roll(cl, sh, axis=0), 0.0)
            sh *= 2
        w_inc = jnp.exp(cl)
        w_inv = jnp.exp(-cl)
        at = -kk * jnp.exp(cl - ld)
        rt = r * w_inc
        pre.append(dict(
            k2=k2, w_inc=w_inc, rt=rt, bt=_bf(bv * w_inv), kt=_bf(k2 * w_inv), v_bf=_bf(v),
            at_h=(_bf(jnp.where(m0, at, 0.0)), _bf(jnp.where(m0, 0.0, at))),
            rt_h=(_bf(jnp.where(m0, rt, 0.0)), _bf(jnp.where(m0, 0.0, rt)))))

    grams = []
    for d in pre:
        lhs = jnp.concatenate([d["at_h"][0], d["at_h"][1], d["rt_h"][0], d["rt_h"][1]], axis=0)
        rhs = jnp.concatenate([d["bt"], d["kt"]], axis=0)
        grams.append(_dot_nt(lhs, rhs))

    ri = lax.broadcasted_iota(jnp.int32, (rows, rows), 0)
    ci = lax.broadcasted_iota(jnp.int32, (rows, rows), 1)
    same = (ri // chunk) == (ci // chunk)
    strict = same & (ri > ci)
    incl = same & (ri >= ci)
    eye = jnp.where(ri == ci, 1.0, 0.0)

    xs = [jnp.where(strict, grams[b][h * rows:(h + 1) * rows, :rows], 0.0) for b, h in inst]
    t_inv = [eye + x for x in xs]
    n = 2
    while n < chunk:
        xb = [_bf(x) for x in xs]
        xs = [_dot(x, x) for x in xb]
        t_inv = [t + _dot(_bf(t), _bf(x)) for t, x in zip(t_inv, xs)]
        n *= 2
    t_bf = [_bf(t) for t in t_inv]

    akv = [_dot(_bf(jnp.where(strict, grams[b][h * rows:(h + 1) * rows, rows:], 0.0)), pre[b]["v_bf"])
           for b, h in inst]
    up_h = [_dot(t, jnp.concatenate([_bf(z), pre[b]["at_h"][h]], axis=1))
            for t, z, (b, h) in zip(t_bf, akv, inst)]
    u0_h = [z[:, :LANES] for z in up_h]
    p_h = [z[:, LANES:] for z in up_h]
    a_rb = [_bf(jnp.where(incl, grams[b][(2 + h) * rows:(3 + h) * rows, :rows], 0.0)) for b, h in inst]
    a_rk = [_bf(jnp.where(incl, grams[b][(2 + h) * rows:(3 + h) * rows, rows:], 0.0)) for b, h in inst]
    rb_up = [_dot(a_rb[i], _bf(up_h[i])) for i in range(len(inst))]
    y0_h = [rb_up[i][:, :LANES] + _dot(a_rk[i], pre[b]["v_bf"]) for i, (b, h) in enumerate(inst)]
    qp_h = [z[:, LANES:] for z in rb_up]

    bi = lax.broadcasted_iota(jnp.int32, (LANES, LANES), 0) // RWKV_HEAD_DIM
    bj = lax.broadcasted_iota(jnp.int32, (LANES, LANES), 1) // RWKV_HEAD_DIM
    blockdiag = bi == bj

    p_all, q_all, u0_all, y0_all = [], [], [], []
    for b in range(nb):
        i0, i1 = HEAD_PAIR * b, HEAD_PAIR * b + 1
        p_all.append(_bf(p_h[i0] + p_h[i1]))
        q_all.append(_bf(pre[b]["rt"] + qp_h[i0] + qp_h[i1]))
        u0_all.append(_bf(jnp.where(m0, u0_h[i0], u0_h[i1])))
        y0_all.append(jnp.where(m0, y0_h[i0], y0_h[i1]))

    m_c, c_c = {}, {}
    for c in range(n_chunks):
        lo, hi = c * chunk, (c + 1) * chunk
        for b in range(nb):
            bt_c, kt_c = pre[b]["bt"][lo:hi], pre[b]["kt"][lo:hi]
            m_c[b, c] = _bf(jnp.where(blockdiag, _dot_tn(p_all[b][lo:hi], bt_c), 0.0))
            c_c[b, c] = jnp.where(blockdiag, _dot_tn(
                jnp.concatenate([u0_all[b][lo:hi], pre[b]["v_bf"][lo:hi]], axis=0),
                jnp.concatenate([bt_c, kt_c], axis=0)), 0.0)

    ys = [[] for _ in range(nb)]
    states = list(states)
    for c in range(n_chunks):
        lo, hi = c * chunk, (c + 1) * chunk
        for b in range(nb):
            s_bf = _bf(states[b])
            ys[b].append(_dot_nt(q_all[b][lo:hi], s_bf) + y0_all[b][lo:hi])
            states[b] = (states[b] + _dot(s_bf, m_c[b, c]) + c_c[b, c]) * pre[b]["w_inc"][hi - 1:hi, :]

    outs = []
    inv_n = 1.0 / RWKV_HEAD_DIM
    for b, (r, k, v, ld, a, g) in enumerate(ins):
        y = jnp.concatenate(ys[b], axis=0)
        mean = _pair_sum(y, m0) * inv_n
        d = y - mean
        var = _pair_sum(d * d, m0) * inv_n
        yn = d * lax.rsqrt(var + RWKV_GN_EPS) * gn_g + gn_b
        bonus = _pair_sum(r * pre[b]["k2"] * r_k, m0) * v
        outs.append((yn + bonus) * g)
    return outs, states


def _rwkv_scan(r, k, v, ld, a, g, k_k, k_a, r_k, gn_g, gn_b):
    bsz, t, _ = r.shape
    tok = pl.BlockSpec((bsz, RWKV_BLOCK, LANES), lambda h, c: (0, c, h))
    par = pl.BlockSpec((1, LANES), lambda h, c: (0, h))
    return pl.pallas_call(
        _rwkv_scan_kernel,
        grid=(N_PAIRS, t // RWKV_BLOCK),
        in_specs=[tok] * 6 + [par] * 5,
        out_specs=tok,
        out_shape=jax.ShapeDtypeStruct((bsz, t, RWKV_DIM), BF16),
        scratch_shapes=[pltpu.VMEM((bsz, LANES, LANES), F32)],
        compiler_params=_params("parallel", "arbitrary"),
        name="rwkv_scan",
    )(r, k, v, ld, a, g, k_k, k_a, r_k, gn_g, gn_b)


def _rope_rotated(w):
    lo, mid, hi = QK_NOPE_DIM, QK_NOPE_DIM + QK_ROPE_DIM // 2, QK_NOPE_DIM + QK_ROPE_DIM
    return jnp.zeros_like(w).at[..., lo:mid].set(-w[..., mid:hi]).at[..., mid:hi].set(w[..., lo:mid])


def _mla_prep_kernel(x_ref, w_ref, qn_ref, qup_ref, kvn_ref, kvup_ref, cos_ref, sin_ref,
                     q_ref, k_ref, v_ref):
    nq, nkv, wq = Q_LORA_RANK, KV_LORA_RANK, MLA_HEADS * LANES
    p = _dot(x_ref[...], w_ref[...])
    c_q = p[:, :nq]
    c_kv = p[:, nq:nq + nkv]
    c_q = c_q * lax.rsqrt(jnp.mean(c_q * c_q, -1, keepdims=True) + RMS_EPS) * qn_ref[...]
    c_kv = c_kv * lax.rsqrt(jnp.mean(c_kv * c_kv, -1, keepdims=True) + RMS_EPS) * kvn_ref[...]
    q = _dot(_bf(c_q), qup_ref[...])
    kv = _dot(_bf(c_kv), kvup_ref[...])
    cos_t, sin_t = cos_ref[...], sin_ref[...]
    reps = MLA_HEADS
    q = q[:, :wq] * jnp.tile(cos_t, (1, reps)) + q[:, wq:] * jnp.tile(sin_t, (1, reps))
    q_ref[...] = (q * (ATTN_SCALE * LOG2_E)).astype(q_ref.dtype)
    kr = p[:, nq + nkv:nq + nkv + LANES] * cos_t + p[:, nq + nkv + LANES:] * sin_t
    k_ref[...] = (kv[:, :wq] + jnp.tile(kr, (1, reps))).astype(k_ref.dtype)
    v_ref[...] = kv[:, wq:].astype(v_ref.dtype)


def _mla_prep(x_bf, w_mla, q_norm, q_up, kv_norm, kv_up, cos_t, sin_t, tm):
    bsz, t, _ = x_bf.shape
    tok = lambda b, i: (b, i, 0)
    fixed = lambda b, i: (0, 0)
    tab = pl.BlockSpec((tm, LANES), lambda b, i: (i, 0))
    wq = MLA_HEADS * LANES
    return pl.pallas_call(
        _mla_prep_kernel,
        grid=(bsz, t // tm),
        in_specs=[
            pl.BlockSpec((None, tm, D_MODEL), tok),
            pl.BlockSpec(w_mla.shape, fixed),
            pl.BlockSpec((1, Q_LORA_RANK), fixed),
            pl.BlockSpec(q_up.shape, fixed),
            pl.BlockSpec((1, KV_LORA_RANK), fixed),
            pl.BlockSpec(kv_up.shape, fixed),
            tab, tab,
        ],
        out_specs=[pl.BlockSpec((None, tm, wq), tok), pl.BlockSpec((None, tm, wq), tok),
                   pl.BlockSpec((None, tm, MLA_DIM), tok)],
        out_shape=[jax.ShapeDtypeStruct((bsz, t, wq), BF16), jax.ShapeDtypeStruct((bsz, t, wq), BF16),
                   jax.ShapeDtypeStruct((bsz, t, MLA_DIM), BF16)],
        compiler_params=_params("parallel", "parallel"),
        name="mla_prep",
    )(x_bf, w_mla, q_norm, q_up, kv_norm, kv_up, cos_t, sin_t)


def _attn_kernel(q_ref, k_ref, v_ref, o_ref, m_ref, l_ref, acc_ref, *, tq):
    i = pl.program_id(2)
    lane = lax.broadcasted_iota(jnp.int32, (1, LANES), 1)
    m0 = lane < V_HEAD_DIM
    m_ref[...] = jnp.full_like(m_ref, NEG_BIG)
    l_ref[...] = jnp.zeros_like(l_ref)
    acc_ref[...] = jnp.zeros_like(acc_ref)

    def step(start, nk, row_lo, masked):
        rows = slice(row_lo, tq)
        nrow = tq - row_lo
        v_blk = v_ref[pl.ds(start, nk), :]
        heads = range(HEAD_PAIR)
        s_h = [_dot_nt(q_ref[rows, h * LANES:(h + 1) * LANES], k_ref[pl.ds(start, nk), h * LANES:(h + 1) * LANES])
               for h in heads]
        alphas, ps = [], []
        for h in heads:
            s = s_h[h]
            if masked:
                qi = lax.broadcasted_iota(jnp.int32, (nrow, nk), 0) + (i * tq + row_lo)
                ki = lax.broadcasted_iota(jnp.int32, (nrow, nk), 1) + start
                s = jnp.where(ki <= qi, s, NEG_BIG)
            m_old = m_ref[h, rows]
            m_new = jnp.maximum(m_old, jnp.max(s, -1, keepdims=True))
            alpha = jnp.exp2(m_old - m_new)
            p = jnp.exp2(s - jnp.tile(m_new, (1, nk // LANES)))
            p_sum = p[:, :LANES]
            for c in range(1, nk // LANES):
                p_sum = p_sum + p[:, c * LANES:(c + 1) * LANES]
            l_ref[h, rows] = alpha * l_ref[h, rows] + p_sum
            m_ref[h, rows] = m_new
            alphas.append(alpha)
            ps.append(_bf(p))
        for h in heads:
            acc_ref[h, rows] = alphas[h] * acc_ref[h, rows] + _dot(ps[h], v_blk)

    def body(j, carry):
        step(pl.multiple_of(j * tq, tq), tq, 0, False)
        return carry

    lax.fori_loop(0, i, body, 0)
    half = tq // 2
    base = pl.multiple_of(i * tq, tq)
    step(base, half, 0, True)
    step(base + half, half, half, True)
    l0 = jnp.sum(l_ref[0], -1, keepdims=True)
    l1 = jnp.sum(l_ref[1], -1, keepdims=True)
    o_ref[...] = jnp.where(m0, acc_ref[0] / l0, acc_ref[1] / l1).astype(o_ref.dtype)


def _attention(q, k, v, tq):
    bsz, t, _ = q.shape
    return pl.pallas_call(
        functools.partial(_attn_kernel, tq=tq),
        grid=(bsz, N_PAIRS, t // tq),
        in_specs=[
            pl.BlockSpec((None, tq, HEAD_PAIR * LANES), lambda b, h, i: (b, i, h)),
            pl.BlockSpec((None, t, HEAD_PAIR * LANES), lambda b, h, i: (b, 0, h)),
            pl.BlockSpec((None, t, LANES), lambda b, h, i: (b, 0, h)),
        ],
        out_specs=pl.BlockSpec((None, tq, LANES), lambda b, h, i: (b, i, h)),
        out_shape=jax.ShapeDtypeStruct((bsz, t, MLA_DIM), BF16),
        scratch_shapes=[pltpu.VMEM((HEAD_PAIR, tq, LANES), F32)] * 3,
        compiler_params=_params("parallel", "parallel", "arbitrary"),
        name="mla_attention",
    )(q, k, v)


def _s5_disc_kernel(lre_ref, lim_ref, ls_ref, br_ref, bi_ref, lbr_ref, lbi_ref, bbr_ref, bbi_ref):
    lam_re = jnp.minimum(lre_ref[...], -1e-4)
    lam_im = lim_ref[...]
    step = jnp.exp(ls_ref[...])
    mag = jnp.exp(lam_re * step)
    ang = lam_im * step
    lb_re, lb_im = mag * jnp.cos(ang), mag * jnp.sin(ang)
    den = lam_re * lam_re + lam_im * lam_im
    n_re = lb_re - 1.0
    f_re = (n_re * lam_re + lb_im * lam_im) / den
    f_im = (lb_im * lam_re - n_re * lam_im) / den
    br, bi = br_ref[...], bi_ref[...]
    lbr_ref[...] = lb_re
    lbi_ref[...] = lb_im
    bbr_ref[...] = f_re * br - f_im * bi
    bbi_ref[...] = f_re * bi + f_im * br


def _s5_discretize(lambda_re, lambda_im, log_step, b_re, b_im):
    rep = lambda z: jnp.repeat(z, S5_GROUP, axis=0)
    ls = rep(jnp.broadcast_to(log_step[:, None], (S5_GROUPS, S5_STATE)))
    tr = lambda z: jnp.transpose(z, (0, 2, 1)).reshape(S5_DIM, S5_STATE)
    out = jax.ShapeDtypeStruct((S5_DIM, S5_STATE), F32)
    return pl.pallas_call(_s5_disc_kernel, out_shape=[out] * 4, name="s5_discretize")(
        rep(lambda_re), rep(lambda_im), ls, tr(b_re), tr(b_im))


def _split_bf(z):
    hi = _bf(z)
    return hi, _bf(z - hi.astype(F32))


def _s5_prep_kernel(b_ref, c_ref, ct_ref, lam_ref, wg_ref, kc_ref, kst_ref, lam8_ref):
    n = S5_SLAB_STATE
    b, c, ct, lam = b_ref[0], c_ref[0], ct_ref[0], lam_ref[0]
    ar, ai = lam[:, :n], lam[:, n:]
    br, bi = b[:, :n], b[:, n:]
    ctr, cti = ct[:, :n], ct[:, n:]
    c_hi, c_lo = _split_bf(c)
    pr, pi = jnp.ones_like(ar), jnp.zeros_like(ai)
    for j in range(S5_CHUNK):
        w = jnp.concatenate([br * pr - bi * pi, br * pi + bi * pr], axis=1)
        w_hi, w_lo = _split_bf(w)
        rows = slice(j * LANES, (j + 1) * LANES)
        wg_ref[0, rows, :] = w_hi
        kc_ref[0, rows, :] = _bf(_dot(w_hi, c_hi) + _dot(w_lo, c_hi) + _dot(w_hi, c_lo))
        pr, pi = pr * ar - pi * ai, pr * ai + pi * ar
        kst_ref[0, rows, :] = _bf(jnp.concatenate([ctr * pr + cti * pi, cti * pr - ctr * pi], axis=1))
    lam8_ref[0] = jnp.concatenate([pr, pi], axis=1)


def _s5_prep(bmat, cmat, lam):
    n2 = 2 * S5_SLAB_STATE
    kq = S5_CHUNK * LANES
    slab = lambda s: (s, 0, 0)
    return pl.pallas_call(
        _s5_prep_kernel,
        grid=(S5_SLABS,),
        in_specs=[pl.BlockSpec((1, LANES, n2), slab), pl.BlockSpec((1, n2, LANES), slab),
                  pl.BlockSpec((1, LANES, n2), slab), pl.BlockSpec((1, 1, n2), slab)],
        out_specs=[pl.BlockSpec((1, kq, n2), slab), pl.BlockSpec((1, kq, LANES), slab),
                   pl.BlockSpec((1, kq, n2), slab), pl.BlockSpec((1, 1, n2), slab)],
        out_shape=[jax.ShapeDtypeStruct((S5_SLABS, kq, n2), BF16),
                   jax.ShapeDtypeStruct((S5_SLABS, kq, LANES), BF16),
                   jax.ShapeDtypeStruct((S5_SLABS, kq, n2), BF16),
                   jax.ShapeDtypeStruct((S5_SLABS, 1, n2), F32)],
        compiler_params=_params("parallel"),
        name="s5_prep",
    )(bmat, cmat, jnp.swapaxes(cmat, 1, 2), lam)


def _s5_kernel(x_ref, w_ref, wg_ref, kc_ref, kst_ref, lam8_ref, d_ref, glu_ref, o_ref,
               carry_ref, u_ref, ys_ref, *, ts):
    @pl.when(pl.program_id(1) == 0)
    def _():
        carry_ref[...] = jnp.zeros_like(carry_ref)

    n = S5_SLAB_STATE
    nc = ts // S5_CHUNK
    nb = x_ref.shape[0]
    inst = [(b, sl) for b in range(nb) for sl in range(S5_SLABS)]
    ids = range(len(inst))
    u = [_dot(x_ref[b], w_ref[...]) for b in range(nb)]
    for b, sl in inst:
        u_ref[b, sl] = u[b][:, sl * LANES:(sl + 1) * LANES]
    rin = lax.broadcasted_iota(jnp.int32, (ts, 1), 0) & (S5_CHUNK - 1)
    crow = lax.broadcasted_iota(jnp.int32, (nc, 1), 0)
    first = crow == 0
    x_end = [jnp.concatenate(
        [_bf(u_ref[b, sl, pl.ds(S5_CHUNK - 1 - j, nc, stride=S5_CHUNK), :]) for j in range(S5_CHUNK)], axis=1)
        for b, sl in inst]
    gain = [_dot(x_end[i], wg_ref[sl]) for i, (b, sl) in enumerate(inst)]
    lam8 = [lam8_ref[sl] for b, sl in inst]
    ar = [z[:, :n] for z in lam8]
    ai = [z[:, n:] for z in lam8]
    cr = [carry_ref[b, sl, 0:1, :n] for b, sl in inst]
    cim = [carry_ref[b, sl, 0:1, n:] for b, sl in inst]
    er = [gain[i][:, :n] + jnp.where(first, ar[i] * cr[i] - ai[i] * cim[i], 0.0) for i in ids]
    ei = [gain[i][:, n:] + jnp.where(first, ar[i] * cim[i] + ai[i] * cr[i], 0.0) for i in ids]
    sh = 1
    while sh < nc:
        keep = crow >= sh
        for i in ids:
            sr = jnp.where(keep, pltpu.roll(er[i], sh, axis=0), 0.0)
            si = jnp.where(keep, pltpu.roll(ei[i], sh, axis=0), 0.0)
            er[i], ei[i] = er[i] + ar[i] * sr - ai[i] * si, ei[i] + ar[i] * si + ai[i] * sr
            ar[i], ai[i] = ar[i] * ar[i] - ai[i] * ai[i], 2.0 * ar[i] * ai[i]
        sh *= 2
    for i, (b, sl) in enumerate(inst):
        carry_ref[b, sl, 0:1, :n] = er[i][nc - 1:nc, :]
        carry_ref[b, sl, 0:1, n:] = ei[i][nc - 1:nc, :]
    h0 = [_bf(jnp.concatenate([jnp.where(first, cr[i], pltpu.roll(er[i], 1, axis=0)),
                               jnp.where(first, cim[i], pltpu.roll(ei[i], 1, axis=0))], axis=1)) for i in ids]
    z = [_dot_nt(h0[i], kst_ref[sl]) for i, (b, sl) in enumerate(inst)]
    for i, (b, sl) in enumerate(inst):
        for k in range(S5_CHUNK):
            ys_ref[b, sl, pl.ds(k, nc, stride=S5_CHUNK), :] = z[i][:, k * LANES:(k + 1) * LANES]
    ys = []
    for b, sl in inst:
        us = u[b][:, sl * LANES:(sl + 1) * LANES]
        lagged = [_bf(us)] + [_bf(jnp.where(rin >= j, pltpu.roll(us, j, axis=0), 0.0))
                              for j in range(1, S5_CHUNK)]
        ys.append(_dot(jnp.concatenate(lagged, axis=1), kc_ref[sl]) + ys_ref[b, sl])
    for b in range(nb):
        y = jnp.concatenate(ys[b * S5_SLABS:(b + 1) * S5_SLABS], axis=1) + d_ref[...] * u[b]
        y = 0.5 * y * (1.0 + jnp.tanh(math.sqrt(2.0 / math.pi) * (y + 0.044715 * (y * y * y))))
        h = _dot(_bf(y), glu_ref[...])
        o_ref[b] = (h[:, :D_MODEL] * _sigmoid(h[:, D_MODEL:])).astype(o_ref.dtype)


def _s5(x_bf, w_s5, wg, kc, kst, lam8, d_skip, w_glu, ts):
    bsz, t, _ = x_bf.shape
    nb = 1
    tok = lambda b, i: (b, i, 0)
    resident = lambda a: pl.BlockSpec(a.shape, lambda b, i: (0,) * a.ndim, pipeline_mode=pl.Buffered(1))
    return pl.pallas_call(
        functools.partial(_s5_kernel, ts=ts),
        grid=(bsz // nb, t // ts),
        in_specs=[pl.BlockSpec((nb, ts, D_MODEL), tok)] + [
            resident(a) for a in (w_s5, wg, kc, kst, lam8, d_skip, w_glu)],
        out_specs=pl.BlockSpec((nb, ts, D_MODEL), tok),
        out_shape=jax.ShapeDtypeStruct((bsz, t, D_MODEL), BF16),
        scratch_shapes=[pltpu.VMEM((nb, S5_SLABS, 8, 2 * S5_SLAB_STATE), F32),
                        pltpu.VMEM((nb, S5_SLABS, ts, LANES), F32), pltpu.VMEM((nb, S5_SLABS, ts, LANES), F32)],
        compiler_params=_params("parallel", "arbitrary"),
        name="s5_scan",
    )(x_bf, w_s5, wg, kc, kst, lam8, d_skip, w_glu)


def _s5_matrices(lb_re, lb_im, bb_re, bb_im, c_re, c_im):
    eye = jnp.eye(S5_SLAB_GROUPS, dtype=F32)
    s, g, c, p = S5_SLABS, S5_SLAB_GROUPS, S5_GROUP, S5_STATE

    def b_blocks(bb):
        return jnp.einsum("sgcp,gh->sgchp", bb.reshape(s, g, c, p), eye).reshape(s, g * c, g * p)

    def c_blocks(cc):
        return jnp.einsum("sgcp,gh->sgphc", cc.reshape(s, g, c, p), eye).reshape(s, g * p, g * c)

    bmat = jnp.concatenate([b_blocks(bb_re), b_blocks(bb_im)], axis=2)
    cmat = jnp.concatenate([c_blocks(c_re), -c_blocks(c_im)], axis=1)
    row = lambda z: z[::S5_GROUP].reshape(s, 1, g * p)
    lam = jnp.concatenate([row(lb_re), row(lb_im)], axis=2)
    return bmat, cmat, lam


def _merge_kernel(x_ref, ya_ref, ob_ref, yc_ref, wg_ref, gb_ref, wra_ref, wmo_ref, wo_ref,
                  lg_ref, lb_ref, o_ref, *, alpha):
    gates = _sigmoid(_dot(_bf(x_ref[...]), wg_ref[...]) + gb_ref[...])
    y_a = _dot(ya_ref[...], wra_ref[...])
    y_b = _dot(ob_ref[...], wmo_ref[...])
    y_c = yc_ref[...].astype(F32)
    merged = (gates[:, :D_MODEL] * y_a + gates[:, D_MODEL:2 * D_MODEL] * y_b
              + gates[:, 2 * D_MODEL:] * y_c)
    y = alpha * x_ref[...] + _dot(_bf(merged), wo_ref[...])
    o_ref[...] = _layer_norm(y, lg_ref[...], lb_ref[...])


def _merge(x, ya, ob, yc, w_gate, gate_b, w_ra, w_mo, w_out, ln_g, ln_b, alpha, tm):
    n = x.shape[0]
    tok = lambda i: (i, 0)
    fixed = lambda i: (0, 0)
    full = lambda a: pl.BlockSpec(a.shape, fixed, pipeline_mode=pl.Buffered(1))
    return pl.pallas_call(
        functools.partial(_merge_kernel, alpha=alpha),
        grid=(n // tm,),
        in_specs=[
            pl.BlockSpec((tm, D_MODEL), tok),
            pl.BlockSpec((tm, RWKV_DIM), tok), pl.BlockSpec((tm, MLA_DIM), tok),
            pl.BlockSpec((tm, D_MODEL), tok),
            full(w_gate), full(gate_b), full(w_ra), full(w_mo), full(w_out), full(ln_g), full(ln_b),
        ],
        out_specs=pl.BlockSpec((tm, D_MODEL), tok),
        out_shape=jax.ShapeDtypeStruct((n, D_MODEL), F32),
        compiler_params=_params("parallel"),
        name="merge_ln",
    )(x, ya, ob, yc, w_gate, gate_b, w_ra, w_mo, w_out, ln_g, ln_b)


def _ffn_kernel(x_ref, w1_ref, w3_ref, w2_ref, lg_ref, lb_ref, o_ref, *, alpha, tf):
    x = x_ref[...]
    xb = _bf(x)
    acc = alpha * x
    for j in range(w1_ref.shape[1] // tf):
        cols = slice(j * tf, (j + 1) * tf)
        h1 = _dot(xb, w1_ref[:, cols])
        h3 = _dot(xb, w3_ref[:, cols])
        acc = acc + _dot(_bf(h1 * _sigmoid(h1) * h3), w2_ref[cols, :])
    o_ref[...] = _layer_norm(acc, lg_ref[...], lb_ref[...])


def _ffn(x, w1, w3, w2, ln_g, ln_b, alpha, tm, tf):
    n = x.shape[0]
    tok = lambda i: (i, 0)
    fixed = lambda i: (0, 0)
    resident = lambda a: pl.BlockSpec(a.shape, fixed, pipeline_mode=pl.Buffered(1))
    return pl.pallas_call(
        functools.partial(_ffn_kernel, alpha=alpha, tf=tf),
        grid=(n // tm,),
        in_specs=[
            pl.BlockSpec((tm, D_MODEL), tok),
            resident(w1), resident(w3), resident(w2),
            pl.BlockSpec((1, D_MODEL), fixed), pl.BlockSpec((1, D_MODEL), fixed),
        ],
        out_specs=pl.BlockSpec((tm, D_MODEL), tok),
        out_shape=jax.ShapeDtypeStruct((n, D_MODEL), F32),
        compiler_params=_params("parallel"),
        name="ffn_ln",
    )(x, w1, w3, w2, ln_g, ln_b)


def _rope_tables(t):
    pos = jnp.arange(t, dtype=F32)
    inv_freq = ROPE_THETA ** (-jnp.arange(0, QK_ROPE_DIM, 2, dtype=F32) / QK_ROPE_DIM)
    ang = pos[:, None] * inv_freq[None, :]
    cos, sin = jnp.cos(ang), jnp.sin(ang)
    ones = jnp.ones((t, QK_NOPE_DIM), F32)
    zeros = jnp.zeros((t, QK_NOPE_DIM), F32)
    tail1 = jnp.ones((t, LANES - QK_NOPE_DIM - QK_ROPE_DIM), F32)
    tail0 = jnp.zeros((t, LANES - QK_NOPE_DIM - QK_ROPE_DIM), F32)
    cos_t = jnp.concatenate([ones, cos, cos, tail1], axis=1)
    sin_t = jnp.concatenate([zeros, sin, sin, tail0], axis=1)
    return cos_t, sin_t


def _row(z):
    return z.reshape(1, -1).astype(F32)


ROW_TILE = dict(rwkv_prep=1024, mla_prep=512, attention=1024, s5=1024, merge=512, ffn=512)


def _tile(t, name):
    return min(t, ROW_TILE[name])


def kernel(x, w_in, rwkv_mu, rwkv_w0, rwkv_w2, rwkv_a0, rwkv_a2, rwkv_g2, rwkv_k_k, rwkv_k_a, rwkv_r_k,
           rwkv_gn_g, rwkv_gn_b, rwkv_out, mla_q_norm, mla_q_up, mla_kv_norm, mla_kv_up, mla_out,
           s5_lambda_re, s5_lambda_im, s5_log_step, s5_b_re, s5_b_im, s5_c_re, s5_c_im, s5_d, s5_glu,
           gate_b, w_out, ln1_g, ln1_b, ffn_w1, ffn_w3, ffn_w2, ln2_g, ln2_b):
    bsz, t, _ = x.shape
    depth = w_in.shape[0]
    alpha = (2.0 * depth) ** 0.25
    n = bsz * t
    o_mla = RWKV_COLS
    o_s5 = o_mla + MLA_COLS
    o_gate = o_s5 + S5_COLS
    cos_t, sin_t = _rope_tables(t)
    d_ff = ffn_w1.shape[2]
    tf = d_ff // 2 if (d_ff // 2) % LANES == 0 else d_ff

    w_in, ffn_w1, ffn_w3, ffn_w2, s5_glu, w_out, rwkv_out, mla_out = map(
        _to_bf16, (w_in, ffn_w1, ffn_w3, ffn_w2, s5_glu, w_out, rwkv_out, mla_out))

    for l in range(depth):
        wl = w_in[l]
        zl = jnp.zeros((DECAY_LORA, RWKV_DIM), F32)
        w_lora = jnp.concatenate([jnp.concatenate([rwkv_w2[l], zl], axis=1),
                                  jnp.concatenate([zl, rwkv_a2[l]], axis=1)], axis=0)
        x_bf, r, k, v, ld, a, g = _rwkv_prep(
            x, wl[:, :o_mla], _row(rwkv_mu[l]), _row(rwkv_w0[l]), _row(rwkv_a0[l]),
            _bf(w_lora), _bf(rwkv_g2[l]), _tile(t, "rwkv_prep"))
        ya = _rwkv_scan(r, k, v, ld, a, g, _row(rwkv_k_k[l]), _row(rwkv_k_a[l]), _row(rwkv_r_k[l]),
                        _row(rwkv_gn_g[l]), _row(rwkv_gn_b[l]))
        nq = Q_LORA_RANK + KV_LORA_RANK
        w_kr = jnp.zeros((D_MODEL, LANES), BF16).at[:, QK_NOPE_DIM:QK_NOPE_DIM + QK_ROPE_DIM].set(
            wl[:, o_mla + nq:o_s5])
        w_mla = jnp.concatenate([wl[:, o_mla:o_mla + nq], w_kr, _rope_rotated(w_kr)], axis=1)
        q_up = mla_q_up[l].reshape(Q_LORA_RANK, MLA_HEADS, QK_NOPE_DIM + QK_ROPE_DIM)
        q_up = jnp.pad(q_up, ((0, 0), (0, 0), (0, LANES - QK_NOPE_DIM - QK_ROPE_DIM)))
        q_up = _bf(jnp.concatenate([q_up.reshape(Q_LORA_RANK, MLA_HEADS * LANES),
                                    _rope_rotated(q_up).reshape(Q_LORA_RANK, MLA_HEADS * LANES)], axis=1))
        kv_up = mla_kv_up[l].reshape(KV_LORA_RANK, MLA_HEADS, QK_NOPE_DIM + V_HEAD_DIM)
        k_up = jnp.pad(kv_up[:, :, :QK_NOPE_DIM], ((0, 0), (0, 0), (0, LANES - QK_NOPE_DIM)))
        kv_up = _bf(jnp.concatenate([k_up.reshape(KV_LORA_RANK, MLA_HEADS * LANES),
                                     kv_up[:, :, QK_NOPE_DIM:].reshape(KV_LORA_RANK, MLA_DIM)], axis=1))
        q, kx, vx = _mla_prep(x_bf, w_mla, _row(mla_q_norm[l]), q_up, _row(mla_kv_norm[l]), kv_up,
                              cos_t, sin_t, _tile(t, "mla_prep"))
        ob = _attention(q, kx, vx, _tile(t, "attention"))
        lb_re, lb_im, bb_re, bb_im = _s5_discretize(s5_lambda_re[l], s5_lambda_im[l], s5_log_step[l],
                                                    s5_b_re[l], s5_b_im[l])
        tr = lambda z: z.reshape(S5_DIM, S5_STATE)
        bmat, cmat, lam = _s5_matrices(lb_re, lb_im, bb_re, bb_im, tr(s5_c_re[l]), tr(s5_c_im[l]))
        wg, kc, kst, lam8 = _s5_prep(bmat, cmat, lam)
        yc = _s5(x_bf, wl[:, o_s5:o_gate], wg, kc, kst, lam8, _row(s5_d[l]), s5_glu[l], _tile(t, "s5"))
        x1 = _merge(x.reshape(n, D_MODEL), ya.reshape(n, RWKV_DIM), ob.reshape(n, MLA_DIM),
                    yc.reshape(n, D_MODEL), wl[:, o_gate:], _row(gate_b[l]),
                    rwkv_out[l], mla_out[l], w_out[l], _row(ln1_g[l]), _row(ln1_b[l]),
                    alpha, _tile(n, "merge"))
        x2 = _ffn(x1, ffn_w1[l], ffn_w3[l], ffn_w2[l], _row(ln2_g[l]),
                  _row(ln2_b[l]), alpha, _tile(n, "ffn"), tf)
        x = x2.reshape(bsz, t, D_MODEL)
    return x
```

```python
import functools
import math

import jax
import jax.numpy as jnp
from jax import lax
from jax.experimental import pallas as pl
from jax.experimental.pallas import tpu as pltpu

F32 = jnp.float32
BF16 = jnp.bfloat16

D_MODEL = 1024
RWKV_HEADS = 8
RWKV_HEAD_DIM = 64
RWKV_DIM = RWKV_HEADS * RWKV_HEAD_DIM
DECAY_LORA = 64
AAA_LORA = 64
GATE_LORA = 128
RWKV_GN_EPS = 64e-5
MLA_HEADS = 8
QK_NOPE_DIM = 64
QK_ROPE_DIM = 32
V_HEAD_DIM = 64
Q_LORA_RANK = 256
KV_LORA_RANK = 128
MLA_DIM = MLA_HEADS * V_HEAD_DIM
ROPE_THETA = 10000.0
ATTN_SCALE = 1.0 / math.sqrt(QK_NOPE_DIM + QK_ROPE_DIM)
LOG2_E = math.log2(math.e)
S5_DIM = 512
S5_GROUP = 16
S5_GROUPS = S5_DIM // S5_GROUP
S5_STATE = 64
N_BRANCHES = 3
LN_EPS = 1e-5
RMS_EPS = 1e-6
RWKV_COLS = 3 * RWKV_DIM + DECAY_LORA + AAA_LORA + GATE_LORA
MLA_COLS = Q_LORA_RANK + KV_LORA_RANK + QK_ROPE_DIM
S5_COLS = S5_DIM

LANES = 128
HEAD_PAIR = LANES // RWKV_HEAD_DIM
N_PAIRS = RWKV_HEADS // HEAD_PAIR
S5_SLAB_GROUPS = LANES // S5_GROUP
S5_SLABS = S5_GROUPS // S5_SLAB_GROUPS
S5_SLAB_STATE = S5_SLAB_GROUPS * S5_STATE
S5_CHUNK = 8
VMEM_LIMIT = 48 * 1024 * 1024

RWKV_CHUNK = 64
RWKV_BLOCK = 256
ATTN_PAIRS = 1
ATTN_KEY_BLOCK = 1024
RWKV_SCAN_PAIRS = 4
NEG_BIG = -1e30
CAST_BLOCK_BYTES = 6 * 1024 * 1024


def _bf(x):
    return x.astype(BF16)


def _dot(a, b):
    return jnp.dot(a, b, preferred_element_type=F32)


def _dot_nt(a, b):
    return lax.dot_general(a, b, (((1,), (1,)), ((), ())), preferred_element_type=F32)


def _dot_tn(a, b):
    return lax.dot_general(a, b, (((0,), (0,)), ((), ())), preferred_element_type=F32)


def _sigmoid(x):
    return 1.0 / (1.0 + jnp.exp(-x))


def _params(*sem):
    return pltpu.CompilerParams(dimension_semantics=sem, vmem_limit_bytes=VMEM_LIMIT)


def _cast_kernel(w_ref, o_ref):
    o_ref[...] = w_ref[...].astype(o_ref.dtype)


def _to_bf16(w):
    depth, rows, cols = w.shape
    cap = CAST_BLOCK_BYTES // (4 * cols)
    tr = max(d for d in range(8, rows + 1, 8) if rows % d == 0 and d <= max(cap, 8))
    spec = pl.BlockSpec((None, tr, cols), lambda l, i: (l, i, 0))
    return pl.pallas_call(
        _cast_kernel, grid=(depth, rows // tr), in_specs=[spec], out_specs=spec,
        out_shape=jax.ShapeDtypeStruct(w.shape, BF16),
        compiler_params=_params("parallel", "parallel"), name="cast_bf16",
    )(w)


def _layer_norm(y, g, b):
    mu = jnp.mean(y, -1, keepdims=True)
    d = y - mu
    var = jnp.mean(d * d, -1, keepdims=True)
    return d * lax.rsqrt(var + LN_EPS) * g + b


def _pair_sum(x, m0):
    s0 = jnp.sum(jnp.where(m0, x, 0.0), -1, keepdims=True)
    s1 = jnp.sum(jnp.where(m0, 0.0, x), -1, keepdims=True)
    return jnp.where(m0, s0, s1)


def _rwkv_prep_kernel(x_ref, w_ref, mu_ref, w0_ref, a0_ref, wl_ref, g2_ref,
                      xb_ref, r_ref, k_ref, v_ref, ld_ref, a_ref, g_ref, prev_ref):
    t = pl.program_id(1)

    @pl.when(t == 0)
    def _():
        prev_ref[...] = jnp.zeros_like(prev_ref)

    xb = _bf(x_ref[...])
    xb_ref[...] = xb
    p = _dot(xb, w_ref[...])
    rolled = pltpu.roll(p, 1, axis=0)
    row = lax.broadcasted_iota(jnp.int32, (p.shape[0], 1), 0)
    shifted = jnp.where(row == 0, prev_ref[0:1, :], rolled)
    prev_ref[0:1, :] = rolled[0:1, :]
    p = p + (shifted - p) * mu_ref[...]

    c0, c1, c2 = RWKV_DIM, 2 * RWKV_DIM, 3 * RWKV_DIM
    r_ref[...] = p[:, :c0].astype(r_ref.dtype)
    k_ref[...] = p[:, c0:c1].astype(k_ref.dtype)
    v_ref[...] = p[:, c1:c2].astype(v_ref.dtype)
    lora = p[:, c2:c2 + DECAY_LORA + AAA_LORA]
    lane = lax.broadcasted_iota(jnp.int32, (1, DECAY_LORA + AAA_LORA), 1)
    lora = jnp.where(lane < DECAY_LORA, jnp.tanh(lora), lora)
    wa = _dot(_bf(lora), wl_ref[...])
    ld_ref[...] = (-math.exp(-0.5)) * _sigmoid(w0_ref[...] + wa[:, :RWKV_DIM])
    a_ref[...] = _sigmoid(a0_ref[...] + wa[:, RWKV_DIM:]).astype(a_ref.dtype)
    dg = p[:, c2 + DECAY_LORA + AAA_LORA:]
    g_ref[...] = _dot(_bf(_sigmoid(dg)), g2_ref[...]).astype(g_ref.dtype)


def _rwkv_prep(x, w_rwkv, mu, w0, a0, w_lora, g2, tm):
    bsz, t, _ = x.shape
    outs = [jax.ShapeDtypeStruct((bsz, t, D_MODEL), BF16)] + [
        jax.ShapeDtypeStruct((bsz, t, RWKV_DIM), dt) for dt in (BF16, BF16, BF16, F32, BF16, BF16)]
    tok = lambda b, i: (b, i, 0)
    fixed = lambda b, i: (0, 0)
    o_spec = pl.BlockSpec((None, tm, RWKV_DIM), tok)
    return pl.pallas_call(
        _rwkv_prep_kernel,
        grid=(bsz, t // tm),
        in_specs=[
            pl.BlockSpec((None, tm, D_MODEL), tok),
            pl.BlockSpec((D_MODEL, RWKV_COLS), fixed),
            pl.BlockSpec((1, RWKV_COLS), fixed),
            pl.BlockSpec((1, RWKV_DIM), fixed),
            pl.BlockSpec((1, RWKV_DIM), fixed),
            pl.BlockSpec((DECAY_LORA + AAA_LORA, 2 * RWKV_DIM), fixed),
            pl.BlockSpec((GATE_LORA, RWKV_DIM), fixed),
        ],
        out_specs=[pl.BlockSpec((None, tm, D_MODEL), tok)] + [o_spec] * 6,
        out_shape=outs,
        scratch_shapes=[pltpu.VMEM((8, RWKV_COLS), F32)],
        compiler_params=_params("parallel", "arbitrary"),
        name="rwkv_prep",
    )(x, w_rwkv, mu, w0, a0, w_lora, g2)


def _rwkv_scan_kernel(r_ref, k_ref, v_ref, ld_ref, a_ref, g_ref,
                      kk_ref, ka_ref, rk_ref, gng_ref, gnb_ref, o_ref, s_ref):
    @pl.when(pl.program_id(1) == 0)
    def _():
        s_ref[...] = jnp.zeros_like(s_ref)

    seqs = [(b, slice(p * LANES, (p + 1) * LANES)) for b in range(r_ref.shape[0])
            for p in range(r_ref.shape[2] // LANES)]
    ins = [tuple(z[b, :, cols].astype(F32) for z in (r_ref, k_ref, v_ref, ld_ref, a_ref, g_ref))
           for b, cols in seqs]
    pars = [tuple(z[:, cols] for z in (kk_ref, ka_ref, rk_ref, gng_ref, gnb_ref)) for b, cols in seqs]
    outs, states = _rwkv_blocks(ins, pars, [s_ref[i] for i in range(len(seqs))])
    for i, (b, cols) in enumerate(seqs):
        o_ref[b, :, cols] = outs[i].astype(o_ref.dtype)
        s_ref[i] = states[i]


def _rwkv_blocks(ins, pars, states):
    rows, chunk = RWKV_BLOCK, RWKV_CHUNK
    n_chunks = rows // chunk
    nb = len(ins)
    inst = [(b, h) for b in range(nb) for h in range(HEAD_PAIR)]

    lane = lax.broadcasted_iota(jnp.int32, (1, LANES), 1)
    m0 = lane < RWKV_HEAD_DIM
    rin = lax.broadcasted_iota(jnp.int32, (rows, 1), 0) & (chunk - 1)

    pre = []
    for (r, k, v, ld, a, g), (k_k, k_a, _, _, _) in zip(ins, pars):
        kk = k * k_k
        kk = kk * lax.rsqrt(jnp.maximum(_pair_sum(kk * kk, m0), 1e-12))
        k2 = k * (1.0 + (a - 1.0) * k_a)
        bv = kk * a
        cl = ld
        sh = 1
        while sh < chunk:
            cl = cl + jnp.where(rin >= sh, pltpu.roll(cl, sh, axis=0), 0.0)
            sh *= 2
        w_inc = jnp.exp(cl)
        w_inv = jnp.exp(-cl)
        at = -kk * jnp.exp(cl - ld)
        rt = r * w_inc
        pre.append(dict(
            k2=k2, w_inc=w_inc, rt=rt, bt=_bf(bv * w_inv), kt=_bf(k2 * w_inv), v_bf=_bf(v),
            at_h=(_bf(jnp.where(m0, at, 0.0)), _bf(jnp.where(m0, 0.0, at))),
            rt_h=(_bf(jnp.where(m0, rt, 0.0)), _bf(jnp.where(m0, 0.0, rt)))))

    grams = []
    for d in pre:
        lhs = jnp.concatenate([d["at_h"][0], d["at_h"][1], d["rt_h"][0], d["rt_h"][1]], axis=0)
        rhs = jnp.concatenate([d["bt"], d["kt"]], axis=0)
        grams.append(_dot_nt(lhs, rhs))

    ri = lax.broadcasted_iota(jnp.int32, (rows, rows), 0)
    ci = lax.broadcasted_iota(jnp.int32, (rows, rows), 1)
    same = (ri // chunk) == (ci // chunk)
    strict = same & (ri > ci)
    incl = same & (ri >= ci)
    eye = jnp.where(ri == ci, 1.0, 0.0)

    xs = [jnp.where(strict, grams[b][h * rows:(h + 1) * rows, :rows], 0.0) for b, h in inst]
    t_inv = [eye + x for x in xs]
    n = 2
    while n < chunk:
        xb = [_bf(x) for x in xs]
        xs = [_dot(x, x) for x in xb]
        t_inv = [t + _dot(_bf(t), _bf(x)) for t, x in zip(t_inv, xs)]
        n *= 2
    t_bf = [_bf(t) for t in t_inv]

    akv = [_dot(_bf(jnp.where(strict, grams[b][h * rows:(h + 1) * rows, rows:], 0.0)), pre[b]["v_bf"])
           for b, h in inst]
    up_h = [_dot(t, jnp.concatenate([_bf(z), pre[b]["at_h"][h]], axis=1))
            for t, z, (b, h) in zip(t_bf, akv, inst)]
    u0_h = [z[:, :LANES] for z in up_h]
    p_h = [z[:, LANES:] for z in up_h]
    a_rb = [_bf(jnp.where(incl, grams[b][(2 + h) * rows:(3 + h) * rows, :rows], 0.0)) for b, h in inst]
    a_rk = [_bf(jnp.where(incl, grams[b][(2 + h) * rows:(3 + h) * rows, rows:], 0.0)) for b, h in inst]
    rb_up = [_dot(a_rb[i], _bf(up_h[i])) for i in range(len(inst))]
    y0_h = [rb_up[i][:, :LANES] + _dot(a_rk[i], pre[b]["v_bf"]) for i, (b, h) in enumerate(inst)]
    qp_h = [z[:, LANES:] for z in rb_up]

    bi = lax.broadcasted_iota(jnp.int32, (LANES, LANES), 0) // RWKV_HEAD_DIM
    bj = lax.broadcasted_iota(jnp.int32, (LANES, LANES), 1) // RWKV_HEAD_DIM
    blockdiag = bi == bj

    p_all, q_all, u0_all, y0_all = [], [], [], []
    for b in range(nb):
        i0, i1 = HEAD_PAIR * b, HEAD_PAIR * b + 1
        p_all.append(_bf(p_h[i0] + p_h[i1]))
        q_all.append(_bf(pre[b]["rt"] + qp_h[i0] + qp_h[i1]))
        u0_all.append(_bf(jnp.where(m0, u0_h[i0], u0_h[i1])))
        y0_all.append(jnp.where(m0, y0_h[i0], y0_h[i1]))

    m_c, c_c = {}, {}
    for c in range(n_chunks):
        lo, hi = c * chunk, (c + 1) * chunk
        for b in range(nb):
            bt_c, kt_c = pre[b]["bt"][lo:hi], pre[b]["kt"][lo:hi]
            m_c[b, c] = _bf(jnp.where(blockdiag, _dot_tn(p_all[b][lo:hi], bt_c), 0.0))
            c_c[b, c] = jnp.where(blockdiag, _dot_tn(
                jnp.concatenate([u0_all[b][lo:hi], pre[b]["v_bf"][lo:hi]], axis=0),
                jnp.concatenate([bt_c, kt_c], axis=0)), 0.0)

    ys = [[] for _ in range(nb)]
    states = list(states)
    for c in range(n_chunks):
        lo, hi = c * chunk, (c + 1) * chunk
        for b in range(nb):
            s_bf = _bf(states[b])
            ys[b].append(_dot_nt(q_all[b][lo:hi], s_bf) + y0_all[b][lo:hi])
            states[b] = (states[b] + _dot(s_bf, m_c[b, c]) + c_c[b, c]) * pre[b]["w_inc"][hi - 1:hi, :]

    outs = []
    inv_n = 1.0 / RWKV_HEAD_DIM
    for b, (r, k, v, ld, a, g) in enumerate(ins):
        _, _, r_k, gn_g, gn_b = pars[b]
        y = jnp.concatenate(ys[b], axis=0)
        mean = _pair_sum(y, m0) * inv_n
        d = y - mean
        var = _pair_sum(d * d, m0) * inv_n
        yn = d * lax.rsqrt(var + RWKV_GN_EPS) * gn_g + gn_b
        bonus = _pair_sum(r * pre[b]["k2"] * r_k, m0) * v
        outs.append((yn + bonus) * g)
    return outs, states


def _rwkv_scan(r, k, v, ld, a, g, k_k, k_a, r_k, gn_g, gn_b):
    bsz, t, _ = r.shape
    width = RWKV_SCAN_PAIRS * LANES
    tok = pl.BlockSpec((bsz, RWKV_BLOCK, width), lambda h, c: (0, c, h))
    par = pl.BlockSpec((1, width), lambda h, c: (0, h))
    return pl.pallas_call(
        _rwkv_scan_kernel,
        grid=(N_PAIRS // RWKV_SCAN_PAIRS, t // RWKV_BLOCK),
        in_specs=[tok] * 6 + [par] * 5,
        out_specs=tok,
        out_shape=jax.ShapeDtypeStruct((bsz, t, RWKV_DIM), BF16),
        scratch_shapes=[pltpu.VMEM((bsz * RWKV_SCAN_PAIRS, LANES, LANES), F32)],
        compiler_params=_params("parallel", "arbitrary"),
        name="rwkv_scan",
    )(r, k, v, ld, a, g, k_k, k_a, r_k, gn_g, gn_b)


def _rope_rotated(w):
    lo, mid, hi = QK_NOPE_DIM, QK_NOPE_DIM + QK_ROPE_DIM // 2, QK_NOPE_DIM + QK_ROPE_DIM
    return jnp.zeros_like(w).at[..., lo:mid].set(-w[..., mid:hi]).at[..., mid:hi].set(w[..., lo:mid])


def _mla_prep_kernel(x_ref, w_ref, qn_ref, qup_ref, kvn_ref, kvup_ref, cos_ref, sin_ref,
                     q_ref, k_ref, v_ref):
    nq, nkv, wq = Q_LORA_RANK, KV_LORA_RANK, MLA_HEADS * LANES
    p = _dot(x_ref[...], w_ref[...])
    c_q = p[:, :nq]
    c_kv = p[:, nq:nq + nkv]
    c_q = c_q * lax.rsqrt(jnp.mean(c_q * c_q, -1, keepdims=True) + RMS_EPS) * qn_ref[...]
    c_kv = c_kv * lax.rsqrt(jnp.mean(c_kv * c_kv, -1, keepdims=True) + RMS_EPS) * kvn_ref[...]
    q = _dot(_bf(c_q), qup_ref[...])
    kv = _dot(_bf(c_kv), kvup_ref[...])
    cos_t, sin_t = cos_ref[...], sin_ref[...]
    reps = MLA_HEADS
    q = q[:, :wq] * jnp.tile(cos_t, (1, reps)) + q[:, wq:] * jnp.tile(sin_t, (1, reps))
    q_ref[...] = (q * (ATTN_SCALE * LOG2_E)).astype(q_ref.dtype)
    kr = p[:, nq + nkv:nq + nkv + LANES] * cos_t + p[:, nq + nkv + LANES:] * sin_t
    k_ref[...] = (kv[:, :wq] + jnp.tile(kr, (1, reps))).astype(k_ref.dtype)
    v_ref[...] = kv[:, wq:].astype(v_ref.dtype)


def _mla_prep(x_bf, w_mla, q_norm, q_up, kv_norm, kv_up, cos_t, sin_t, tm):
    bsz, t, _ = x_bf.shape
    tok = lambda b, i: (b, i, 0)
    fixed = lambda b, i: (0, 0)
    tab = pl.BlockSpec((tm, LANES), lambda b, i: (i, 0))
    wq = MLA_HEADS * LANES
    return pl.pallas_call(
        _mla_prep_kernel,
        grid=(bsz, t // tm),
        in_specs=[
            pl.BlockSpec((None, tm, D_MODEL), tok),
            pl.BlockSpec(w_mla.shape, fixed),
            pl.BlockSpec((1, Q_LORA_RANK), fixed),
            pl.BlockSpec(q_up.shape, fixed),
            pl.BlockSpec((1, KV_LORA_RANK), fixed),
            pl.BlockSpec(kv_up.shape, fixed),
            tab, tab,
        ],
        out_specs=[pl.BlockSpec((None, tm, wq), tok), pl.BlockSpec((None, tm, wq), tok),
                   pl.BlockSpec((None, tm, MLA_DIM), tok)],
        out_shape=[jax.ShapeDtypeStruct((bsz, t, wq), BF16), jax.ShapeDtypeStruct((bsz, t, wq), BF16),
                   jax.ShapeDtypeStruct((bsz, t, MLA_DIM), BF16)],
        compiler_params=_params("parallel", "parallel"),
        name="mla_prep",
    )(x_bf, w_mla, q_norm, q_up, kv_norm, kv_up, cos_t, sin_t)


def _attn_kernel(q_ref, k_ref, v_ref, o_ref, m_ref, l_ref, acc_ref, *, tq, tk):
    i = pl.program_id(2)
    lane = lax.broadcasted_iota(jnp.int32, (1, LANES), 1)
    m0 = lane < V_HEAD_DIM
    m_ref[...] = jnp.full_like(m_ref, NEG_BIG)
    l_ref[...] = jnp.zeros_like(l_ref)
    acc_ref[...] = jnp.zeros_like(acc_ref)

    def step(start, nk, row_lo, masked):
        rows = slice(row_lo, tq)
        nrow = tq - row_lo
        heads = range(q_ref.shape[1] // LANES)
        s_h = [_dot_nt(q_ref[rows, h * LANES:(h + 1) * LANES], k_ref[pl.ds(start, nk), h * LANES:(h + 1) * LANES])
               for h in heads]
        alphas, ps = [], []
        for h in heads:
            s = s_h[h]
            if masked:
                qi = lax.broadcasted_iota(jnp.int32, (nrow, nk), 0) + (i * tq + row_lo)
                ki = lax.broadcasted_iota(jnp.int32, (nrow, nk), 1) + start
                s = jnp.where(ki <= qi, s, NEG_BIG)
            m_old = m_ref[h, rows]
            m_new = jnp.maximum(m_old, jnp.max(s, -1, keepdims=True))
            alpha = jnp.exp2(m_old - m_new)
            p = jnp.exp2(s - jnp.tile(m_new, (1, nk // LANES)))
            p_sum = p[:, :LANES]
            for c in range(1, nk // LANES):
                p_sum = p_sum + p[:, c * LANES:(c + 1) * LANES]
            l_ref[h, rows] = alpha * l_ref[h, rows] + p_sum
            m_ref[h, rows] = m_new
            alphas.append(alpha)
            ps.append(_bf(p))
        for h in heads:
            pair = slice((h // HEAD_PAIR) * LANES, (h // HEAD_PAIR + 1) * LANES)
            acc_ref[h, rows] = alphas[h] * acc_ref[h, rows] + _dot(ps[h], v_ref[pl.ds(start, nk), pair])

    def body(j, carry):
        step(pl.multiple_of(j * tk, tk), tk, 0, False)
        return carry

    lax.fori_loop(0, i * (tq // tk), body, 0)
    half = tq // 2
    base = pl.multiple_of(i * tq, tq)
    step(base, half, 0, True)
    step(base + half, half, half, True)
    for p in range(o_ref.shape[1] // LANES):
        out = [acc_ref[h] / jnp.sum(l_ref[h], -1, keepdims=True) for h in (HEAD_PAIR * p, HEAD_PAIR * p + 1)]
        o_ref[:, p * LANES:(p + 1) * LANES] = jnp.where(m0, out[0], out[1]).astype(o_ref.dtype)


def _attention(q, k, v, tq, tk):
    bsz, t, _ = q.shape
    npair = ATTN_PAIRS
    heads = npair * HEAD_PAIR
    resident = lambda width: pl.BlockSpec((None, t, width), lambda b, h, i: (b, 0, h), pipeline_mode=pl.Buffered(1))
    return pl.pallas_call(
        functools.partial(_attn_kernel, tq=tq, tk=tk),
        grid=(bsz, N_PAIRS // npair, t // tq),
        in_specs=[
            pl.BlockSpec((None, tq, heads * LANES), lambda b, h, i: (b, i, h)),
            resident(heads * LANES),
            resident(npair * LANES),
        ],
        out_specs=pl.BlockSpec((None, tq, npair * LANES), lambda b, h, i: (b, i, h)),
        out_shape=jax.ShapeDtypeStruct((bsz, t, MLA_DIM), BF16),
        scratch_shapes=[pltpu.VMEM((heads, tq, LANES), F32)] * 3,
        compiler_params=_params("parallel", "parallel", "arbitrary"),
        name="mla_attention",
    )(q, k, v)


def _s5_disc_kernel(lre_ref, lim_ref, ls_ref, br_ref, bi_ref, lbr_ref, lbi_ref, bbr_ref, bbi_ref):
    lam_re = jnp.minimum(lre_ref[...], -1e-4)
    lam_im = lim_ref[...]
    step = jnp.exp(ls_ref[...])
    mag = jnp.exp(lam_re * step)
    ang = lam_im * step
    lb_re, lb_im = mag * jnp.cos(ang), mag * jnp.sin(ang)
    den = lam_re * lam_re + lam_im * lam_im
    n_re = lb_re - 1.0
    f_re = (n_re * lam_re + lb_im * lam_im) / den
    f_im = (lb_im * lam_re - n_re * lam_im) / den
    br, bi = br_ref[...], bi_ref[...]
    lbr_ref[...] = lb_re
    lbi_ref[...] = lb_im
    bbr_ref[...] = f_re * br - f_im * bi
    bbi_ref[...] = f_re * bi + f_im * br


def _s5_discretize(lambda_re, lambda_im, log_step, b_re, b_im):
    rep = lambda z: jnp.repeat(z, S5_GROUP, axis=0)
    ls = rep(jnp.broadcast_to(log_step[:, None], (S5_GROUPS, S5_STATE)))
    tr = lambda z: jnp.transpose(z, (0, 2, 1)).reshape(S5_DIM, S5_STATE)
    out = jax.ShapeDtypeStruct((S5_DIM, S5_STATE), F32)
    return pl.pallas_call(_s5_disc_kernel, out_shape=[out] * 4, name="s5_discretize")(
        rep(lambda_re), rep(lambda_im), ls, tr(b_re), tr(b_im))


def _split_bf(z):
    hi = _bf(z)
    return hi, _bf(z - hi.astype(F32))


def _s5_prep_kernel(b_ref, c_ref, ct_ref, lam_ref, wg_ref, kc_ref, kst_ref, lam8_ref):
    n = S5_SLAB_STATE
    b, c, ct, lam = b_ref[0], c_ref[0], ct_ref[0], lam_ref[0]
    ar, ai = lam[:, :n], lam[:, n:]
    br, bi = b[:, :n], b[:, n:]
    ctr, cti = ct[:, :n], ct[:, n:]
    c_hi, c_lo = _split_bf(c)
    pr, pi = jnp.ones_like(ar), jnp.zeros_like(ai)
    for j in range(S5_CHUNK):
        w = jnp.concatenate([br * pr - bi * pi, br * pi + bi * pr], axis=1)
        w_hi, w_lo = _split_bf(w)
        rows = slice(j * LANES, (j + 1) * LANES)
        wg_ref[0, rows, :] = w_hi
        kc_ref[0, rows, :] = _bf(_dot(w_hi, c_hi) + _dot(w_lo, c_hi) + _dot(w_hi, c_lo))
        pr, pi = pr * ar - pi * ai, pr * ai + pi * ar
        kst_ref[0, rows, :] = _bf(jnp.concatenate([ctr * pr + cti * pi, cti * pr - ctr * pi], axis=1))
    lam8_ref[0] = jnp.concatenate([pr, pi], axis=1)


def _s5_prep(bmat, cmat, lam):
    n2 = 2 * S5_SLAB_STATE
    kq = S5_CHUNK * LANES
    slab = lambda s: (s, 0, 0)
    return pl.pallas_call(
        _s5_prep_kernel,
        grid=(S5_SLABS,),
        in_specs=[pl.BlockSpec((1, LANES, n2), slab), pl.BlockSpec((1, n2, LANES), slab),
                  pl.BlockSpec((1, LANES, n2), slab), pl.BlockSpec((1, 1, n2), slab)],
        out_specs=[pl.BlockSpec((1, kq, n2), slab), pl.BlockSpec((1, kq, LANES), slab),
                   pl.BlockSpec((1, kq, n2), slab), pl.BlockSpec((1, 1, n2), slab)],
        out_shape=[jax.ShapeDtypeStruct((S5_SLABS, kq, n2), BF16),
                   jax.ShapeDtypeStruct((S5_SLABS, kq, LANES), BF16),
                   jax.ShapeDtypeStruct((S5_SLABS, kq, n2), BF16),
                   jax.ShapeDtypeStruct((S5_SLABS, 1, n2), F32)],
        compiler_params=_params("parallel"),
        name="s5_prep",
    )(bmat, cmat, jnp.swapaxes(cmat, 1, 2), lam)


def _s5_kernel(x_ref, w_ref, wg_ref, kc_ref, kst_ref, lam8_ref, d_ref, glu_ref, o_ref,
               carry_ref, u_ref, ys_ref, *, ts):
    @pl.when(pl.program_id(1) == 0)
    def _():
        carry_ref[...] = jnp.zeros_like(carry_ref)

    n = S5_SLAB_STATE
    nc = ts // S5_CHUNK
    nb = x_ref.shape[0]
    inst = [(b, sl) for b in range(nb) for sl in range(S5_SLABS)]
    ids = range(len(inst))
    u = [_dot(x_ref[b], w_ref[...]) for b in range(nb)]
    for b, sl in inst:
        u_ref[b, sl] = u[b][:, sl * LANES:(sl + 1) * LANES]
    rin = lax.broadcasted_iota(jnp.int32, (ts, 1), 0) & (S5_CHUNK - 1)
    crow = lax.broadcasted_iota(jnp.int32, (nc, 1), 0)
    first = crow == 0
    x_end = [jnp.concatenate(
        [_bf(u_ref[b, sl, pl.ds(S5_CHUNK - 1 - j, nc, stride=S5_CHUNK), :]) for j in range(S5_CHUNK)], axis=1)
        for b, sl in inst]
    gain = [_dot(x_end[i], wg_ref[sl]) for i, (b, sl) in enumerate(inst)]
    lam8 = [lam8_ref[sl] for b, sl in inst]
    ar = [z[:, :n] for z in lam8]
    ai = [z[:, n:] for z in lam8]
    cr = [carry_ref[b, sl, 0:1, :n] for b, sl in inst]
    cim = [carry_ref[b, sl, 0:1, n:] for b, sl in inst]
    er = [gain[i][:, :n] + jnp.where(first, ar[i] * cr[i] - ai[i] * cim[i], 0.0) for i in ids]
    ei = [gain[i][:, n:] + jnp.where(first, ar[i] * cim[i] + ai[i] * cr[i], 0.0) for i in ids]
    sh = 1
    while sh < nc:
        keep = crow >= sh
        for i in ids:
            sr = jnp.where(keep, pltpu.roll(er[i], sh, axis=0), 0.0)
            si = jnp.where(keep, pltpu.roll(ei[i], sh, axis=0), 0.0)
            er[i], ei[i] = er[i] + ar[i] * sr - ai[i] * si, ei[i] + ar[i] * si + ai[i] * sr
            ar[i], ai[i] = ar[i] * ar[i] - ai[i] * ai[i], 2.0 * ar[i] * ai[i]
        sh *= 2
    for i, (b, sl) in enumerate(inst):
        carry_ref[b, sl, 0:1, :n] = er[i][nc - 1:nc, :]
        carry_ref[b, sl, 0:1, n:] = ei[i][nc - 1:nc, :]
    h0 = [_bf(jnp.concatenate([jnp.where(first, cr[i], pltpu.roll(er[i], 1, axis=0)),
                               jnp.where(first, cim[i], pltpu.roll(ei[i], 1, axis=0))], axis=1)) for i in ids]
    z = [_dot_nt(h0[i], kst_ref[sl]) for i, (b, sl) in enumerate(inst)]
    for i, (b, sl) in enumerate(inst):
        for k in range(S5_CHUNK):
            ys_ref[b, sl, pl.ds(k, nc, stride=S5_CHUNK), :] = z[i][:, k * LANES:(k + 1) * LANES]
    ys = []
    for b, sl in inst:
        us = u[b][:, sl * LANES:(sl + 1) * LANES]
        lagged = [_bf(us)] + [_bf(jnp.where(rin >= j, pltpu.roll(us, j, axis=0), 0.0))
                              for j in range(1, S5_CHUNK)]
        ys.append(_dot(jnp.concatenate(lagged, axis=1), kc_ref[sl]) + ys_ref[b, sl])
    for b in range(nb):
        y = jnp.concatenate(ys[b * S5_SLABS:(b + 1) * S5_SLABS], axis=1) + d_ref[...] * u[b]
        y = 0.5 * y * (1.0 + jnp.tanh(math.sqrt(2.0 / math.pi) * (y + 0.044715 * (y * y * y))))
        h = _dot(_bf(y), glu_ref[...])
        o_ref[b] = (h[:, :D_MODEL] * _sigmoid(h[:, D_MODEL:])).astype(o_ref.dtype)


def _s5(x_bf, w_s5, wg, kc, kst, lam8, d_skip, w_glu, ts):
    bsz, t, _ = x_bf.shape
    nb = 1
    tok = lambda b, i: (b, i, 0)
    resident = lambda a: pl.BlockSpec(a.shape, lambda b, i: (0,) * a.ndim, pipeline_mode=pl.Buffered(1))
    return pl.pallas_call(
        functools.partial(_s5_kernel, ts=ts),
        grid=(bsz // nb, t // ts),
        in_specs=[pl.BlockSpec((nb, ts, D_MODEL), tok)] + [
            resident(a) for a in (w_s5, wg, kc, kst, lam8, d_skip, w_glu)],
        out_specs=pl.BlockSpec((nb, ts, D_MODEL), tok),
        out_shape=jax.ShapeDtypeStruct((bsz, t, D_MODEL), BF16),
        scratch_shapes=[pltpu.VMEM((nb, S5_SLABS, 8, 2 * S5_SLAB_STATE), F32),
                        pltpu.VMEM((nb, S5_SLABS, ts, LANES), F32), pltpu.VMEM((nb, S5_SLABS, ts, LANES), F32)],
        compiler_params=_params("parallel", "arbitrary"),
        name="s5_scan",
    )(x_bf, w_s5, wg, kc, kst, lam8, d_skip, w_glu)


def _s5_matrices(lb_re, lb_im, bb_re, bb_im, c_re, c_im):
    eye = jnp.eye(S5_SLAB_GROUPS, dtype=F32)
    s, g, c, p = S5_SLABS, S5_SLAB_GROUPS, S5_GROUP, S5_STATE

    def b_blocks(bb):
        return jnp.einsum("sgcp,gh->sgchp", bb.reshape(s, g, c, p), eye).reshape(s, g * c, g * p)

    def c_blocks(cc):
        return jnp.einsum("sgcp,gh->sgphc", cc.reshape(s, g, c, p), eye).reshape(s, g * p, g * c)

    bmat = jnp.concatenate([b_blocks(bb_re), b_blocks(bb_im)], axis=2)
    cmat = jnp.concatenate([c_blocks(c_re), -c_blocks(c_im)], axis=1)
    row = lambda z: z[::S5_GROUP].reshape(s, 1, g * p)
    lam = jnp.concatenate([row(lb_re), row(lb_im)], axis=2)
    return bmat, cmat, lam


def _merge_kernel(x_ref, ya_ref, ob_ref, yc_ref, wg_ref, gb_ref, wra_ref, wmo_ref, wo_ref,
                  lg_ref, lb_ref, o_ref, *, alpha):
    gates = _sigmoid(_dot(_bf(x_ref[...]), wg_ref[...]) + gb_ref[...])
    y_a = _dot(ya_ref[...], wra_ref[...])
    y_b = _dot(ob_ref[...], wmo_ref[...])
    y_c = yc_ref[...].astype(F32)
    merged = (gates[:, :D_MODEL] * y_a + gates[:, D_MODEL:2 * D_MODEL] * y_b
              + gates[:, 2 * D_MODEL:] * y_c)
    y = alpha * x_ref[...] + _dot(_bf(merged), wo_ref[...])
    o_ref[...] = _layer_norm(y, lg_ref[...], lb_ref[...])


def _merge(x, ya, ob, yc, w_gate, gate_b, w_ra, w_mo, w_out, ln_g, ln_b, alpha, tm):
    n = x.shape[0]
    tok = lambda i: (i, 0)
    fixed = lambda i: (0, 0)
    full = lambda a: pl.BlockSpec(a.shape, fixed, pipeline_mode=pl.Buffered(1))
    return pl.pallas_call(
        functools.partial(_merge_kernel, alpha=alpha),
        grid=(n // tm,),
        in_specs=[
            pl.BlockSpec((tm, D_MODEL), tok),
            pl.BlockSpec((tm, RWKV_DIM), tok), pl.BlockSpec((tm, MLA_DIM), tok),
            pl.BlockSpec((tm, D_MODEL), tok),
            full(w_gate), full(gate_b), full(w_ra), full(w_mo), full(w_out), full(ln_g), full(ln_b),
        ],
        out_specs=pl.BlockSpec((tm, D_MODEL), tok),
        out_shape=jax.ShapeDtypeStruct((n, D_MODEL), F32),
        compiler_params=_params("parallel"),
        name="merge_ln",
    )(x, ya, ob, yc, w_gate, gate_b, w_ra, w_mo, w_out, ln_g, ln_b)


def _ffn_kernel(x_ref, w1_ref, w3_ref, w2_ref, lg_ref, lb_ref, o_ref, *, alpha, tf):
    x = x_ref[...]
    xb = _bf(x)
    acc = alpha * x
    for j in range(w1_ref.shape[1] // tf):
        cols = slice(j * tf, (j + 1) * tf)
        h1 = _dot(xb, w1_ref[:, cols])
        h3 = _dot(xb, w3_ref[:, cols])
        acc = acc + _dot(_bf(h1 * _sigmoid(h1) * h3), w2_ref[cols, :])
    o_ref[...] = _layer_norm(acc, lg_ref[...], lb_ref[...])


def _ffn(x, w1, w3, w2, ln_g, ln_b, alpha, tm, tf):
    n = x.shape[0]
    tok = lambda i: (i, 0)
    fixed = lambda i: (0, 0)
    resident = lambda a: pl.BlockSpec(a.shape, fixed, pipeline_mode=pl.Buffered(1))
    return pl.pallas_call(
        functools.partial(_ffn_kernel, alpha=alpha, tf=tf),
        grid=(n // tm,),
        in_specs=[
            pl.BlockSpec((tm, D_MODEL), tok),
            resident(w1), resident(w3), resident(w2),
            pl.BlockSpec((1, D_MODEL), fixed), pl.BlockSpec((1, D_MODEL), fixed),
        ],
        out_specs=pl.BlockSpec((tm, D_MODEL), tok),
        out_shape=jax.ShapeDtypeStruct((n, D_MODEL), F32),
        compiler_params=_params("parallel"),
        name="ffn_ln",
    )(x, w1, w3, w2, ln_g, ln_b)


def _rope_tables(t):
    pos = jnp.arange(t, dtype=F32)
    inv_freq = ROPE_THETA ** (-jnp.arange(0, QK_ROPE_DIM, 2, dtype=F32) / QK_ROPE_DIM)
    ang = pos[:, None] * inv_freq[None, :]
    cos, sin = jnp.cos(ang), jnp.sin(ang)
    ones = jnp.ones((t, QK_NOPE_DIM), F32)
    zeros = jnp.zeros((t, QK_NOPE_DIM), F32)
    tail1 = jnp.ones((t, LANES - QK_NOPE_DIM - QK_ROPE_DIM), F32)
    tail0 = jnp.zeros((t, LANES - QK_NOPE_DIM - QK_ROPE_DIM), F32)
    cos_t = jnp.concatenate([ones, cos, cos, tail1], axis=1)
    sin_t = jnp.concatenate([zeros, sin, sin, tail0], axis=1)
    return cos_t, sin_t


def _row(z):
    return z.reshape(1, -1).astype(F32)


ROW_TILE = dict(rwkv_prep=1024, mla_prep=512, attention=1024, s5=1024, merge=512, ffn=512)


def _tile(t, name):
    return min(t, ROW_TILE[name])


def kernel(x, w_in, rwkv_mu, rwkv_w0, rwkv_w2, rwkv_a0, rwkv_a2, rwkv_g2, rwkv_k_k, rwkv_k_a, rwkv_r_k,
           rwkv_gn_g, rwkv_gn_b, rwkv_out, mla_q_norm, mla_q_up, mla_kv_norm, mla_kv_up, mla_out,
           s5_lambda_re, s5_lambda_im, s5_log_step, s5_b_re, s5_b_im, s5_c_re, s5_c_im, s5_d, s5_glu,
           gate_b, w_out, ln1_g, ln1_b, ffn_w1, ffn_w3, ffn_w2, ln2_g, ln2_b):
    bsz, t, _ = x.shape
    depth = w_in.shape[0]
    alpha = (2.0 * depth) ** 0.25
    n = bsz * t
    o_mla = RWKV_COLS
    o_s5 = o_mla + MLA_COLS
    o_gate = o_s5 + S5_COLS
    cos_t, sin_t = _rope_tables(t)
    d_ff = ffn_w1.shape[2]
    tf = d_ff // 2 if (d_ff // 2) % LANES == 0 else d_ff

    ffn_w1, ffn_w3, ffn_w2, s5_glu, w_out, rwkv_out, mla_out = map(
        _to_bf16, (ffn_w1, ffn_w3, ffn_w2, s5_glu, w_out, rwkv_out, mla_out))

    for l in range(depth):
        wl = w_in[l]
        zl = jnp.zeros((DECAY_LORA, RWKV_DIM), F32)
        w_lora = jnp.concatenate([jnp.concatenate([rwkv_w2[l], zl], axis=1),
                                  jnp.concatenate([zl, rwkv_a2[l]], axis=1)], axis=0)
        x_bf, r, k, v, ld, a, g = _rwkv_prep(
            x, _bf(wl[:, :o_mla]), _row(rwkv_mu[l]), _row(rwkv_w0[l]), _row(rwkv_a0[l]),
            _bf(w_lora), _bf(rwkv_g2[l]), _tile(t, "rwkv_prep"))
        ya = _rwkv_scan(r, k, v, ld, a, g, _row(rwkv_k_k[l]), _row(rwkv_k_a[l]), _row(rwkv_r_k[l]),
                        _row(rwkv_gn_g[l]), _row(rwkv_gn_b[l]))
        nq = Q_LORA_RANK + KV_LORA_RANK
        w_kr = jnp.zeros((D_MODEL, LANES), F32).at[:, QK_NOPE_DIM:QK_NOPE_DIM + QK_ROPE_DIM].set(
            wl[:, o_mla + nq:o_s5])
        w_mla = _bf(jnp.concatenate([wl[:, o_mla:o_mla + nq], w_kr, _rope_rotated(w_kr)], axis=1))
        q_up = mla_q_up[l].reshape(Q_LORA_RANK, MLA_HEADS, QK_NOPE_DIM + QK_ROPE_DIM)
        q_up = jnp.pad(q_up, ((0, 0), (0, 0), (0, LANES - QK_NOPE_DIM - QK_ROPE_DIM)))
        q_up = _bf(jnp.concatenate([q_up.reshape(Q_LORA_RANK, MLA_HEADS * LANES),
                                    _rope_rotated(q_up).reshape(Q_LORA_RANK, MLA_HEADS * LANES)], axis=1))
        kv_up = mla_kv_up[l].reshape(KV_LORA_RANK, MLA_HEADS, QK_NOPE_DIM + V_HEAD_DIM)
        k_up = jnp.pad(kv_up[:, :, :QK_NOPE_DIM], ((0, 0), (0, 0), (0, LANES - QK_NOPE_DIM)))
        kv_up = _bf(jnp.concatenate([k_up.reshape(KV_LORA_RANK, MLA_HEADS * LANES),
                                     kv_up[:, :, QK_NOPE_DIM:].reshape(KV_LORA_RANK, MLA_DIM)], axis=1))
        q, kx, vx = _mla_prep(x_bf, w_mla, _row(mla_q_norm[l]), q_up, _row(mla_kv_norm[l]), kv_up,
                              cos_t, sin_t, _tile(t, "mla_prep"))
        ob = _attention(q, kx, vx, _tile(t, "attention"), min(t, ATTN_KEY_BLOCK))
        lb_re, lb_im, bb_re, bb_im = _s5_discretize(s5_lambda_re[l], s5_lambda_im[l], s5_log_step[l],
                                                    s5_b_re[l], s5_b_im[l])
        tr = lambda z: z.reshape(S5_DIM, S5_STATE)
        bmat, cmat, lam = _s5_matrices(lb_re, lb_im, bb_re, bb_im, tr(s5_c_re[l]), tr(s5_c_im[l]))
        wg, kc, kst, lam8 = _s5_prep(bmat, cmat, lam)
        yc = _s5(x_bf, _bf(wl[:, o_s5:o_gate]), wg, kc, kst, lam8, _row(s5_d[l]), s5_glu[l], _tile(t, "s5"))
        x1 = _merge(x.reshape(n, D_MODEL), ya.reshape(n, RWKV_DIM), ob.reshape(n, MLA_DIM),
                    yc.reshape(n, D_MODEL), _bf(wl[:, o_gate:]), _row(gate_b[l]),
                    rwkv_out[l], mla_out[l], w_out[l], _row(ln1_g[l]), _row(ln1_b[l]),
                    alpha, _tile(n, "merge"))
        x2 = _ffn(x1, ffn_w1[l], ffn_w3[l], ffn_w2[l], _row(ln2_g[l]),
                  _row(ln2_b[l]), alpha, _tile(n, "ffn"), tf)
        x = x2.reshape(bsz, t, D_MODEL)
    return x
```

```python
import functools
import math

import jax
import jax.numpy as jnp
from jax import lax
from jax.experimental import pallas as pl
from jax.experimental.pallas import tpu as pltpu

F32 = jnp.float32
BF16 = jnp.bfloat16

D_MODEL = 1024
RWKV_HEADS = 8
RWKV_HEAD_DIM = 64
RWKV_DIM = RWKV_HEADS * RWKV_HEAD_DIM
DECAY_LORA = 64
AAA_LORA = 64
GATE_LORA = 128
RWKV_GN_EPS = 64e-5
MLA_HEADS = 8
QK_NOPE_DIM = 64
QK_ROPE_DIM = 32
V_HEAD_DIM = 64
Q_LORA_RANK = 256
KV_LORA_RANK = 128
MLA_DIM = MLA_HEADS * V_HEAD_DIM
ROPE_THETA = 10000.0
ATTN_SCALE = 1.0 / math.sqrt(QK_NOPE_DIM + QK_ROPE_DIM)
LOG2_E = math.log2(math.e)
S5_DIM = 512
S5_GROUP = 16
S5_GROUPS = S5_DIM // S5_GROUP
S5_STATE = 64
N_BRANCHES = 3
LN_EPS = 1e-5
RMS_EPS = 1e-6
RWKV_COLS = 3 * RWKV_DIM + DECAY_LORA + AAA_LORA + GATE_LORA
MLA_COLS = Q_LORA_RANK + KV_LORA_RANK + QK_ROPE_DIM
S5_COLS = S5_DIM

LANES = 128
HEAD_PAIR = LANES // RWKV_HEAD_DIM
N_PAIRS = RWKV_HEADS // HEAD_PAIR
S5_SLAB_GROUPS = LANES // S5_GROUP
S5_SLABS = S5_GROUPS // S5_SLAB_GROUPS
S5_SLAB_STATE = S5_SLAB_GROUPS * S5_STATE
S5_CHUNK = 8
VMEM_LIMIT = 48 * 1024 * 1024

RWKV_CHUNK = 64
RWKV_BLOCK = 256
ATTN_PAIRS = 1
ATTN_KEY_BLOCK = 1024
RWKV_SCAN_PAIRS = 4
NEG_BIG = -1e30
CAST_BLOCK_BYTES = 6 * 1024 * 1024


def _bf(x):
    return x.astype(BF16)


def _dot(a, b):
    return jnp.dot(a, b, preferred_element_type=F32)


def _dot_nt(a, b):
    return lax.dot_general(a, b, (((1,), (1,)), ((), ())), preferred_element_type=F32)


def _dot_tn(a, b):
    return lax.dot_general(a, b, (((0,), (0,)), ((), ())), preferred_element_type=F32)


def _sigmoid(x):
    return 1.0 / (1.0 + jnp.exp(-x))


def _params(*sem):
    return pltpu.CompilerParams(dimension_semantics=sem, vmem_limit_bytes=VMEM_LIMIT)


def _cast_kernel(w_ref, o_ref):
    o_ref[...] = w_ref[...].astype(o_ref.dtype)


def _to_bf16(w):
    depth, rows, cols = w.shape
    cap = CAST_BLOCK_BYTES // (4 * cols)
    tr = max(d for d in range(8, rows + 1, 8) if rows % d == 0 and d <= max(cap, 8))
    spec = pl.BlockSpec((None, tr, cols), lambda l, i: (l, i, 0))
    return pl.pallas_call(
        _cast_kernel, grid=(depth, rows // tr), in_specs=[spec], out_specs=spec,
        out_shape=jax.ShapeDtypeStruct(w.shape, BF16),
        compiler_params=_params("parallel", "parallel"), name="cast_bf16",
    )(w)


def _layer_norm(y, g, b):
    mu = jnp.mean(y, -1, keepdims=True)
    d = y - mu
    var = jnp.mean(d * d, -1, keepdims=True)
    return d * lax.rsqrt(var + LN_EPS) * g + b


def _pair_sum(x, m0):
    s0 = jnp.sum(jnp.where(m0, x, 0.0), -1, keepdims=True)
    s1 = jnp.sum(jnp.where(m0, 0.0, x), -1, keepdims=True)
    return jnp.where(m0, s0, s1)


def _rwkv_prep_kernel(x_ref, w_ref, mu_ref, w0_ref, a0_ref, wl_ref, g2_ref,
                      xb_ref, r_ref, k_ref, v_ref, ld_ref, a_ref, g_ref, prev_ref):
    t = pl.program_id(1)

    @pl.when(t == 0)
    def _():
        prev_ref[...] = jnp.zeros_like(prev_ref)

    xb = _bf(x_ref[...])
    xb_ref[...] = xb
    p = _dot(xb, w_ref[...])
    rolled = pltpu.roll(p, 1, axis=0)
    row = lax.broadcasted_iota(jnp.int32, (p.shape[0], 1), 0)
    shifted = jnp.where(row == 0, prev_ref[0:1, :], rolled)
    prev_ref[0:1, :] = rolled[0:1, :]
    p = p + (shifted - p) * mu_ref[...]

    c0, c1, c2 = RWKV_DIM, 2 * RWKV_DIM, 3 * RWKV_DIM
    r_ref[...] = p[:, :c0].astype(r_ref.dtype)
    k_ref[...] = p[:, c0:c1].astype(k_ref.dtype)
    v_ref[...] = p[:, c1:c2].astype(v_ref.dtype)
    lora = p[:, c2:c2 + DECAY_LORA + AAA_LORA]
    lane = lax.broadcasted_iota(jnp.int32, (1, DECAY_LORA + AAA_LORA), 1)
    lora = jnp.where(lane < DECAY_LORA, jnp.tanh(lora), lora)
    wa = _dot(_bf(lora), wl_ref[...])
    ld_ref[...] = (-math.exp(-0.5)) * _sigmoid(w0_ref[...] + wa[:, :RWKV_DIM])
    a_ref[...] = _sigmoid(a0_ref[...] + wa[:, RWKV_DIM:]).astype(a_ref.dtype)
    dg = p[:, c2 + DECAY_LORA + AAA_LORA:]
    g_ref[...] = _dot(_bf(_sigmoid(dg)), g2_ref[...]).astype(g_ref.dtype)


def _rwkv_prep(x, w_rwkv, mu, w0, a0, w_lora, g2, tm):
    bsz, t, _ = x.shape
    outs = [jax.ShapeDtypeStruct((bsz, t, D_MODEL), BF16)] + [
        jax.ShapeDtypeStruct((bsz, t, RWKV_DIM), dt) for dt in (BF16, BF16, BF16, F32, BF16, BF16)]
    tok = lambda b, i: (b, i, 0)
    fixed = lambda b, i: (0, 0)
    o_spec = pl.BlockSpec((None, tm, RWKV_DIM), tok)
    return pl.pallas_call(
        _rwkv_prep_kernel,
        grid=(bsz, t // tm),
        in_specs=[
            pl.BlockSpec((None, tm, D_MODEL), tok),
            pl.BlockSpec((D_MODEL, RWKV_COLS), fixed),
            pl.BlockSpec((1, RWKV_COLS), fixed),
            pl.BlockSpec((1, RWKV_DIM), fixed),
            pl.BlockSpec((1, RWKV_DIM), fixed),
            pl.BlockSpec((DECAY_LORA + AAA_LORA, 2 * RWKV_DIM), fixed),
            pl.BlockSpec((GATE_LORA, RWKV_DIM), fixed),
        ],
        out_specs=[pl.BlockSpec((None, tm, D_MODEL), tok)] + [o_spec] * 6,
        out_shape=outs,
        scratch_shapes=[pltpu.VMEM((8, RWKV_COLS), F32)],
        compiler_params=_params("parallel", "arbitrary"),
        name="rwkv_prep",
    )(x, w_rwkv, mu, w0, a0, w_lora, g2)


def _rwkv_scan_kernel(r_ref, k_ref, v_ref, ld_ref, a_ref, g_ref,
                      kk_ref, ka_ref, rk_ref, gng_ref, gnb_ref, o_ref, s_ref):
    @pl.when(pl.program_id(1) == 0)
    def _():
        s_ref[...] = jnp.zeros_like(s_ref)

    seqs = [(b, slice(p * LANES, (p + 1) * LANES)) for b in range(r_ref.shape[0])
            for p in range(r_ref.shape[2] // LANES)]
    ins = [tuple(z[b, :, cols].astype(F32) for z in (r_ref, k_ref, v_ref, ld_ref, a_ref, g_ref))
           for b, cols in seqs]
    pars = [tuple(z[:, cols] for z in (kk_ref, ka_ref, rk_ref, gng_ref, gnb_ref)) for b, cols in seqs]
    outs, states = _rwkv_blocks(ins, pars, [s_ref[i] for i in range(len(seqs))])
    for i, (b, cols) in enumerate(seqs):
        o_ref[b, :, cols] = outs[i].astype(o_ref.dtype)
        s_ref[i] = states[i]


def _rwkv_blocks(ins, pars, states):
    rows, chunk = RWKV_BLOCK, RWKV_CHUNK
    n_chunks = rows // chunk
    nb = len(ins)
    inst = [(b, h) for b in range(nb) for h in range(HEAD_PAIR)]

    lane = lax.broadcasted_iota(jnp.int32, (1, LANES), 1)
    m0 = lane < RWKV_HEAD_DIM
    rin = lax.broadcasted_iota(jnp.int32, (rows, 1), 0) & (chunk - 1)

    pre = []
    for (r, k, v, ld, a, g), (k_k, k_a, _, _, _) in zip(ins, pars):
        kk = k * k_k
        kk = kk * lax.rsqrt(jnp.maximum(_pair_sum(kk * kk, m0), 1e-12))
        k2 = k * (1.0 + (a - 1.0) * k_a)
        bv = kk * a
        cl = ld
        sh = 1
        while sh < chunk:
            cl = cl + jnp.where(rin >= sh, pltpu.roll(cl, sh, axis=0), 0.0)
            sh *= 2
        w_inc = jnp.exp(cl)
        w_inv = jnp.exp(-cl)
        at = -kk * jnp.exp(cl - ld)
        rt = r * w_inc
        pre.append(dict(
            k2=k2, w_inc=w_inc, rt=rt, bt=_bf(bv * w_inv), kt=_bf(k2 * w_inv), v_bf=_bf(v),
            at_h=(_bf(jnp.where(m0, at, 0.0)), _bf(jnp.where(m0, 0.0, at))),
            rt_h=(_bf(jnp.where(m0, rt, 0.0)), _bf(jnp.where(m0, 0.0, rt)))))

    grams = []
    for d in pre:
        lhs = jnp.concatenate([d["at_h"][0], d["at_h"][1], d["rt_h"][0], d["rt_h"][1]], axis=0)
        rhs = jnp.concatenate([d["bt"], d["kt"]], axis=0)
        grams.append(_dot_nt(lhs, rhs))

    ri = lax.broadcasted_iota(jnp.int32, (rows, rows), 0)
    ci = lax.broadcasted_iota(jnp.int32, (rows, rows), 1)
    same = (ri // chunk) == (ci // chunk)
    strict = same & (ri > ci)
    incl = same & (ri >= ci)
    eye = jnp.where(ri == ci, 1.0, 0.0)

    xs = [jnp.where(strict, grams[b][h * rows:(h + 1) * rows, :rows], 0.0) for b, h in inst]
    t_inv = [eye + x for x in xs]
    n = 2
    while n < chunk:
        xb = [_bf(x) for x in xs]
        xs = [_dot(x, x) for x in xb]
        t_inv = [t + _dot(_bf(t), _bf(x)) for t, x in zip(t_inv, xs)]
        n *= 2
    t_bf = [_bf(t) for t in t_inv]

    akv = [_dot(_bf(jnp.where(strict, grams[b][h * rows:(h + 1) * rows, rows:], 0.0)), pre[b]["v_bf"])
           for b, h in inst]
    up_h = [_dot(t, jnp.concatenate([_bf(z), pre[b]["at_h"][h]], axis=1))
            for t, z, (b, h) in zip(t_bf, akv, inst)]
    u0_h = [z[:, :LANES] for z in up_h]
    p_h = [z[:, LANES:] for z in up_h]
    a_rb = [_bf(jnp.where(incl, grams[b][(2 + h) * rows:(3 + h) * rows, :rows], 0.0)) for b, h in inst]
    a_rk = [_bf(jnp.where(incl, grams[b][(2 + h) * rows:(3 + h) * rows, rows:], 0.0)) for b, h in inst]
    rb_up = [_dot(a_rb[i], _bf(up_h[i])) for i in range(len(inst))]
    y0_h = [rb_up[i][:, :LANES] + _dot(a_rk[i], pre[b]["v_bf"]) for i, (b, h) in enumerate(inst)]
    qp_h = [z[:, LANES:] for z in rb_up]

    bi = lax.broadcasted_iota(jnp.int32, (LANES, LANES), 0) // RWKV_HEAD_DIM
    bj = lax.broadcasted_iota(jnp.int32, (LANES, LANES), 1) // RWKV_HEAD_DIM
    blockdiag = bi == bj

    p_all, q_all, u0_all, y0_all = [], [], [], []
    for b in range(nb):
        i0, i1 = HEAD_PAIR * b, HEAD_PAIR * b + 1
        p_all.append(_bf(p_h[i0] + p_h[i1]))
        q_all.append(_bf(pre[b]["rt"] + qp_h[i0] + qp_h[i1]))
        u0_all.append(_bf(jnp.where(m0, u0_h[i0], u0_h[i1])))
        y0_all.append(jnp.where(m0, y0_h[i0], y0_h[i1]))

    m_c, c_c = {}, {}
    for c in range(n_chunks):
        lo, hi = c * chunk, (c + 1) * chunk
        for b in range(nb):
            bt_c, kt_c = pre[b]["bt"][lo:hi], pre[b]["kt"][lo:hi]
            m_c[b, c] = _bf(jnp.where(blockdiag, _dot_tn(p_all[b][lo:hi], bt_c), 0.0))
            c_c[b, c] = jnp.where(blockdiag, _dot_tn(
                jnp.concatenate([u0_all[b][lo:hi], pre[b]["v_bf"][lo:hi]], axis=0),
                jnp.concatenate([bt_c, kt_c], axis=0)), 0.0)

    ys = [[] for _ in range(nb)]
    states = list(states)
    for c in range(n_chunks):
        lo, hi = c * chunk, (c + 1) * chunk
        for b in range(nb):
            s_bf = _bf(states[b])
            ys[b].append(_dot_nt(q_all[b][lo:hi], s_bf) + y0_all[b][lo:hi])
            states[b] = (states[b] + _dot(s_bf, m_c[b, c]) + c_c[b, c]) * pre[b]["w_inc"][hi - 1:hi, :]

    outs = []
    inv_n = 1.0 / RWKV_HEAD_DIM
    for b, (r, k, v, ld, a, g) in enumerate(ins):
        _, _, r_k, gn_g, gn_b = pars[b]
        y = jnp.concatenate(ys[b], axis=0)
        mean = _pair_sum(y, m0) * inv_n
        d = y - mean
        var = _pair_sum(d * d, m0) * inv_n
        yn = d * lax.rsqrt(var + RWKV_GN_EPS) * gn_g + gn_b
        bonus = _pair_sum(r * pre[b]["k2"] * r_k, m0) * v
        outs.append((yn + bonus) * g)
    return outs, states


def _rwkv_scan(r, k, v, ld, a, g, k_k, k_a, r_k, gn_g, gn_b):
    bsz, t, _ = r.shape
    width = RWKV_SCAN_PAIRS * LANES
    tok = pl.BlockSpec((bsz, RWKV_BLOCK, width), lambda h, c: (0, c, h))
    par = pl.BlockSpec((1, width), lambda h, c: (0, h))
    return pl.pallas_call(
        _rwkv_scan_kernel,
        grid=(N_PAIRS // RWKV_SCAN_PAIRS, t // RWKV_BLOCK),
        in_specs=[tok] * 6 + [par] * 5,
        out_specs=tok,
        out_shape=jax.ShapeDtypeStruct((bsz, t, RWKV_DIM), BF16),
        scratch_shapes=[pltpu.VMEM((bsz * RWKV_SCAN_PAIRS, LANES, LANES), F32)],
        compiler_params=_params("parallel", "arbitrary"),
        name="rwkv_scan",
    )(r, k, v, ld, a, g, k_k, k_a, r_k, gn_g, gn_b)


def _rope_rotated(w):
    lo, mid, hi = QK_NOPE_DIM, QK_NOPE_DIM + QK_ROPE_DIM // 2, QK_NOPE_DIM + QK_ROPE_DIM
    return jnp.zeros_like(w).at[..., lo:mid].set(-w[..., mid:hi]).at[..., mid:hi].set(w[..., lo:mid])


def _mla_prep_kernel(x_ref, w_ref, qn_ref, qup_ref, kvn_ref, kvup_ref, cos_ref, sin_ref,
                     q_ref, k_ref, v_ref):
    nq, nkv, wq = Q_LORA_RANK, KV_LORA_RANK, MLA_HEADS * LANES
    p = _dot(x_ref[...], w_ref[...])
    c_q = p[:, :nq]
    c_kv = p[:, nq:nq + nkv]
    c_q = c_q * lax.rsqrt(jnp.mean(c_q * c_q, -1, keepdims=True) + RMS_EPS) * qn_ref[...]
    c_kv = c_kv * lax.rsqrt(jnp.mean(c_kv * c_kv, -1, keepdims=True) + RMS_EPS) * kvn_ref[...]
    q = _dot(_bf(c_q), qup_ref[...])
    kv = _dot(_bf(c_kv), kvup_ref[...])
    cos_t, sin_t = cos_ref[...], sin_ref[...]
    reps = MLA_HEADS
    q = q[:, :wq] * jnp.tile(cos_t, (1, reps)) + q[:, wq:] * jnp.tile(sin_t, (1, reps))
    q_ref[...] = (q * (ATTN_SCALE * LOG2_E)).astype(q_ref.dtype)
    kr = p[:, nq + nkv:nq + nkv + LANES] * cos_t + p[:, nq + nkv + LANES:] * sin_t
    k_ref[...] = (kv[:, :wq] + jnp.tile(kr, (1, reps))).astype(k_ref.dtype)
    v_ref[...] = kv[:, wq:].astype(v_ref.dtype)


def _mla_prep(x_bf, w_mla, q_norm, q_up, kv_norm, kv_up, cos_t, sin_t, tm):
    bsz, t, _ = x_bf.shape
    tok = lambda b, i: (b, i, 0)
    fixed = lambda b, i: (0, 0)
    tab = pl.BlockSpec((tm, LANES), lambda b, i: (i, 0))
    wq = MLA_HEADS * LANES
    return pl.pallas_call(
        _mla_prep_kernel,
        grid=(bsz, t // tm),
        in_specs=[
            pl.BlockSpec((None, tm, D_MODEL), tok),
            pl.BlockSpec(w_mla.shape, fixed),
            pl.BlockSpec((1, Q_LORA_RANK), fixed),
            pl.BlockSpec(q_up.shape, fixed),
            pl.BlockSpec((1, KV_LORA_RANK), fixed),
            pl.BlockSpec(kv_up.shape, fixed),
            tab, tab,
        ],
        out_specs=[pl.BlockSpec((None, tm, wq), tok), pl.BlockSpec((None, tm, wq), tok),
                   pl.BlockSpec((None, tm, MLA_DIM), tok)],
        out_shape=[jax.ShapeDtypeStruct((bsz, t, wq), BF16), jax.ShapeDtypeStruct((bsz, t, wq), BF16),
                   jax.ShapeDtypeStruct((bsz, t, MLA_DIM), BF16)],
        compiler_params=_params("parallel", "parallel"),
        name="mla_prep",
    )(x_bf, w_mla, q_norm, q_up, kv_norm, kv_up, cos_t, sin_t)


def _attn_kernel(q_ref, k_ref, v_ref, o_ref, m_ref, l_ref, acc_ref, *, tq, tk):
    i = pl.program_id(2)
    lane = lax.broadcasted_iota(jnp.int32, (1, LANES), 1)
    m0 = lane < V_HEAD_DIM
    m_ref[...] = jnp.full_like(m_ref, NEG_BIG)
    l_ref[...] = jnp.zeros_like(l_ref)
    acc_ref[...] = jnp.zeros_like(acc_ref)

    def step(start, nk, row_lo, masked):
        rows = slice(row_lo, tq)
        nrow = tq - row_lo
        heads = range(q_ref.shape[1] // LANES)
        s_h = [_dot_nt(q_ref[rows, h * LANES:(h + 1) * LANES], k_ref[pl.ds(start, nk), h * LANES:(h + 1) * LANES])
               for h in heads]
        alphas, ps = [], []
        for h in heads:
            s = s_h[h]
            if masked:
                qi = lax.broadcasted_iota(jnp.int32, (nrow, nk), 0) + (i * tq + row_lo)
                ki = lax.broadcasted_iota(jnp.int32, (nrow, nk), 1) + start
                s = jnp.where(ki <= qi, s, NEG_BIG)
            m_old = m_ref[h, rows]
            m_new = jnp.maximum(m_old, jnp.max(s, -1, keepdims=True))
            alpha = jnp.exp2(m_old - m_new)
            p = jnp.exp2(s - jnp.tile(m_new, (1, nk // LANES)))
            p_sum = p[:, :LANES]
            for c in range(1, nk // LANES):
                p_sum = p_sum + p[:, c * LANES:(c + 1) * LANES]
            l_ref[h, rows] = alpha * l_ref[h, rows] + p_sum
            m_ref[h, rows] = m_new
            alphas.append(alpha)
            ps.append(_bf(p))
        for h in heads:
            pair = slice((h // HEAD_PAIR) * LANES, (h // HEAD_PAIR + 1) * LANES)
            acc_ref[h, rows] = alphas[h] * acc_ref[h, rows] + _dot(ps[h], v_ref[pl.ds(start, nk), pair])

    def body(j, carry):
        step(pl.multiple_of(j * tk, tk), tk, 0, False)
        return carry

    lax.fori_loop(0, i * (tq // tk), body, 0)
    half = tq // 2
    base = pl.multiple_of(i * tq, tq)
    step(base, half, 0, True)
    step(base + half, half, half, True)
    for p in range(o_ref.shape[1] // LANES):
        out = [acc_ref[h] / jnp.sum(l_ref[h], -1, keepdims=True) for h in (HEAD_PAIR * p, HEAD_PAIR * p + 1)]
        o_ref[:, p * LANES:(p + 1) * LANES] = jnp.where(m0, out[0], out[1]).astype(o_ref.dtype)


def _attention(q, k, v, tq, tk):
    bsz, t, _ = q.shape
    npair = ATTN_PAIRS
    heads = npair * HEAD_PAIR
    resident = lambda width: pl.BlockSpec((None, t, width), lambda b, h, i: (b, 0, h))
    return pl.pallas_call(
        functools.partial(_attn_kernel, tq=tq, tk=tk),
        grid=(bsz, N_PAIRS // npair, t // tq),
        in_specs=[
            pl.BlockSpec((None, tq, heads * LANES), lambda b, h, i: (b, i, h)),
            resident(heads * LANES),
            resident(npair * LANES),
        ],
        out_specs=pl.BlockSpec((None, tq, npair * LANES), lambda b, h, i: (b, i, h)),
        out_shape=jax.ShapeDtypeStruct((bsz, t, MLA_DIM), BF16),
        scratch_shapes=[pltpu.VMEM((heads, tq, LANES), F32)] * 3,
        compiler_params=_params("parallel", "parallel", "arbitrary"),
        name="mla_attention",
    )(q, k, v)


def _s5_disc_kernel(lre_ref, lim_ref, ls_ref, br_ref, bi_ref, lbr_ref, lbi_ref, bbr_ref, bbi_ref):
    lam_re = jnp.minimum(lre_ref[...], -1e-4)
    lam_im = lim_ref[...]
    step = jnp.exp(ls_ref[...])
    mag = jnp.exp(lam_re * step)
    ang = lam_im * step
    lb_re, lb_im = mag * jnp.cos(ang), mag * jnp.sin(ang)
    den = lam_re * lam_re + lam_im * lam_im
    n_re = lb_re - 1.0
    f_re = (n_re * lam_re + lb_im * lam_im) / den
    f_im = (lb_im * lam_re - n_re * lam_im) / den
    br, bi = br_ref[...], bi_ref[...]
    lbr_ref[...] = lb_re
    lbi_ref[...] = lb_im
    bbr_ref[...] = f_re * br - f_im * bi
    bbi_ref[...] = f_re * bi + f_im * br


def _s5_discretize(lambda_re, lambda_im, log_step, b_re, b_im):
    depth = lambda_re.shape[0]
    rep = lambda z: jnp.repeat(z, S5_GROUP, axis=1)
    ls = rep(jnp.broadcast_to(log_step[:, :, None], (depth, S5_GROUPS, S5_STATE)))
    tr = lambda z: jnp.transpose(z, (0, 1, 3, 2)).reshape(depth, S5_DIM, S5_STATE)
    spec = pl.BlockSpec((None, S5_DIM, S5_STATE), lambda l: (l, 0, 0))
    out = jax.ShapeDtypeStruct((depth, S5_DIM, S5_STATE), F32)
    return pl.pallas_call(
        _s5_disc_kernel, grid=(depth,), in_specs=[spec] * 5, out_specs=[spec] * 4, out_shape=[out] * 4,
        compiler_params=_params("parallel"), name="s5_discretize",
    )(rep(lambda_re), rep(lambda_im), ls, tr(b_re), tr(b_im))


def _split_bf(z):
    hi = _bf(z)
    return hi, _bf(z - hi.astype(F32))


def _s5_prep_kernel(b_ref, c_ref, ct_ref, lam_ref, wg_ref, kc_ref, kst_ref, lam8_ref):
    n = S5_SLAB_STATE
    b, c, ct, lam = b_ref[0], c_ref[0], ct_ref[0], lam_ref[0]
    ar, ai = lam[:, :n], lam[:, n:]
    br, bi = b[:, :n], b[:, n:]
    ctr, cti = ct[:, :n], ct[:, n:]
    c_hi, c_lo = _split_bf(c)
    pr, pi = jnp.ones_like(ar), jnp.zeros_like(ai)
    for j in range(S5_CHUNK):
        w = jnp.concatenate([br * pr - bi * pi, br * pi + bi * pr], axis=1)
        w_hi, w_lo = _split_bf(w)
        rows = slice(j * LANES, (j + 1) * LANES)
        wg_ref[0, rows, :] = w_hi
        kc_ref[0, rows, :] = _bf(_dot(w_hi, c_hi) + _dot(w_lo, c_hi) + _dot(w_hi, c_lo))
        pr, pi = pr * ar - pi * ai, pr * ai + pi * ar
        kst_ref[0, rows, :] = _bf(jnp.concatenate([ctr * pr + cti * pi, cti * pr - ctr * pi], axis=1))
    lam8_ref[0] = jnp.concatenate([pr, pi], axis=1)


def _s5_prep(bmat, cmat, lam):
    depth = bmat.shape[0]
    n2 = 2 * S5_SLAB_STATE
    kq = S5_CHUNK * LANES
    blk = lambda r, c: pl.BlockSpec((None, 1, r, c), lambda l, s: (l, s, 0, 0))
    shape = lambda r, c, dt: jax.ShapeDtypeStruct((depth, S5_SLABS, r, c), dt)
    return pl.pallas_call(
        _s5_prep_kernel,
        grid=(depth, S5_SLABS),
        in_specs=[blk(LANES, n2), blk(n2, LANES), blk(LANES, n2), blk(1, n2)],
        out_specs=[blk(kq, n2), blk(kq, LANES), blk(kq, n2), blk(1, n2)],
        out_shape=[shape(kq, n2, BF16), shape(kq, LANES, BF16), shape(kq, n2, BF16), shape(1, n2, F32)],
        compiler_params=_params("parallel", "parallel"),
        name="s5_prep",
    )(bmat, cmat, jnp.swapaxes(cmat, 2, 3), lam)


def _s5_kernel(x_ref, w_ref, wg_ref, kc_ref, kst_ref, lam8_ref, d_ref, glu_ref, o_ref,
               carry_ref, u_ref, ys_ref, *, ts):
    @pl.when(pl.program_id(1) == 0)
    def _():
        carry_ref[...] = jnp.zeros_like(carry_ref)

    n = S5_SLAB_STATE
    nc = ts // S5_CHUNK
    nb = x_ref.shape[0]
    inst = [(b, sl) for b in range(nb) for sl in range(S5_SLABS)]
    ids = range(len(inst))
    u = [_dot(x_ref[b], w_ref[...]) for b in range(nb)]
    for b, sl in inst:
        u_ref[b, sl] = u[b][:, sl * LANES:(sl + 1) * LANES]
    rin = lax.broadcasted_iota(jnp.int32, (ts, 1), 0) & (S5_CHUNK - 1)
    crow = lax.broadcasted_iota(jnp.int32, (nc, 1), 0)
    first = crow == 0
    x_end = [jnp.concatenate(
        [_bf(u_ref[b, sl, pl.ds(S5_CHUNK - 1 - j, nc, stride=S5_CHUNK), :]) for j in range(S5_CHUNK)], axis=1)
        for b, sl in inst]
    gain = [_dot(x_end[i], wg_ref[sl]) for i, (b, sl) in enumerate(inst)]
    lam8 = [lam8_ref[sl] for b, sl in inst]
    ar = [z[:, :n] for z in lam8]
    ai = [z[:, n:] for z in lam8]
    cr = [carry_ref[b, sl, 0:1, :n] for b, sl in inst]
    cim = [carry_ref[b, sl, 0:1, n:] for b, sl in inst]
    er = [gain[i][:, :n] + jnp.where(first, ar[i] * cr[i] - ai[i] * cim[i], 0.0) for i in ids]
    ei = [gain[i][:, n:] + jnp.where(first, ar[i] * cim[i] + ai[i] * cr[i], 0.0) for i in ids]
    sh = 1
    while sh < nc:
        keep = crow >= sh
        for i in ids:
            sr = jnp.where(keep, pltpu.roll(er[i], sh, axis=0), 0.0)
            si = jnp.where(keep, pltpu.roll(ei[i], sh, axis=0), 0.0)
            er[i], ei[i] = er[i] + ar[i] * sr - ai[i] * si, ei[i] + ar[i] * si + ai[i] * sr
            ar[i], ai[i] = ar[i] * ar[i] - ai[i] * ai[i], 2.0 * ar[i] * ai[i]
        sh *= 2
    for i, (b, sl) in enumerate(inst):
        carry_ref[b, sl, 0:1, :n] = er[i][nc - 1:nc, :]
        carry_ref[b, sl, 0:1, n:] = ei[i][nc - 1:nc, :]
    h0 = [_bf(jnp.concatenate([jnp.where(first, cr[i], pltpu.roll(er[i], 1, axis=0)),
                               jnp.where(first, cim[i], pltpu.roll(ei[i], 1, axis=0))], axis=1)) for i in ids]
    z = [_dot_nt(h0[i], kst_ref[sl]) for i, (b, sl) in enumerate(inst)]
    for i, (b, sl) in enumerate(inst):
        for k in range(S5_CHUNK):
            ys_ref[b, sl, pl.ds(k, nc, stride=S5_CHUNK), :] = z[i][:, k * LANES:(k + 1) * LANES]
    ys = []
    for b, sl in inst:
        us = u[b][:, sl * LANES:(sl + 1) * LANES]
        lagged = [_bf(us)] + [_bf(jnp.where(rin >= j, pltpu.roll(us, j, axis=0), 0.0))
                              for j in range(1, S5_CHUNK)]
        ys.append(_dot(jnp.concatenate(lagged, axis=1), kc_ref[sl]) + ys_ref[b, sl])
    for b in range(nb):
        y = jnp.concatenate(ys[b * S5_SLABS:(b + 1) * S5_SLABS], axis=1) + d_ref[...] * u[b]
        y = 0.5 * y * (1.0 + jnp.tanh(math.sqrt(2.0 / math.pi) * (y + 0.044715 * (y * y * y))))
        h = _dot(_bf(y), glu_ref[...])
        o_ref[b] = (h[:, :D_MODEL] * _sigmoid(h[:, D_MODEL:])).astype(o_ref.dtype)


def _s5(x_bf, w_s5, wg, kc, kst, lam8, d_skip, w_glu, ts):
    bsz, t, _ = x_bf.shape
    nb = 1
    tok = lambda b, i: (b, i, 0)
    resident = lambda a: pl.BlockSpec(a.shape, lambda b, i: (0,) * a.ndim, pipeline_mode=pl.Buffered(1))
    return pl.pallas_call(
        functools.partial(_s5_kernel, ts=ts),
        grid=(bsz // nb, t // ts),
        in_specs=[pl.BlockSpec((nb, ts, D_MODEL), tok)] + [
            resident(a) for a in (w_s5, wg, kc, kst, lam8, d_skip, w_glu)],
        out_specs=pl.BlockSpec((nb, ts, D_MODEL), tok),
        out_shape=jax.ShapeDtypeStruct((bsz, t, D_MODEL), BF16),
        scratch_shapes=[pltpu.VMEM((nb, S5_SLABS, 8, 2 * S5_SLAB_STATE), F32),
                        pltpu.VMEM((nb, S5_SLABS, ts, LANES), F32), pltpu.VMEM((nb, S5_SLABS, ts, LANES), F32)],
        compiler_params=_params("parallel", "arbitrary"),
        name="s5_scan",
    )(x_bf, w_s5, wg, kc, kst, lam8, d_skip, w_glu)


def _s5_matrices(lb_re, lb_im, bb_re, bb_im, c_re, c_im):
    eye = jnp.eye(S5_SLAB_GROUPS, dtype=F32)
    s, g, c, p = S5_SLABS, S5_SLAB_GROUPS, S5_GROUP, S5_STATE

    def b_blocks(bb):
        return jnp.einsum("sgcp,gh->sgchp", bb.reshape(s, g, c, p), eye).reshape(s, g * c, g * p)

    def c_blocks(cc):
        return jnp.einsum("sgcp,gh->sgphc", cc.reshape(s, g, c, p), eye).reshape(s, g * p, g * c)

    bmat = jnp.concatenate([b_blocks(bb_re), b_blocks(bb_im)], axis=2)
    cmat = jnp.concatenate([c_blocks(c_re), -c_blocks(c_im)], axis=1)
    row = lambda z: z[::S5_GROUP].reshape(s, 1, g * p)
    lam = jnp.concatenate([row(lb_re), row(lb_im)], axis=2)
    return bmat, cmat, lam


def _merge_kernel(x_ref, ya_ref, ob_ref, yc_ref, wg_ref, gb_ref, wra_ref, wmo_ref, wo_ref,
                  lg_ref, lb_ref, o_ref, *, alpha):
    gates = _sigmoid(_dot(_bf(x_ref[...]), wg_ref[...]) + gb_ref[...])
    y_a = _dot(ya_ref[...], wra_ref[...])
    y_b = _dot(ob_ref[...], wmo_ref[...])
    y_c = yc_ref[...].astype(F32)
    merged = (gates[:, :D_MODEL] * y_a + gates[:, D_MODEL:2 * D_MODEL] * y_b
              + gates[:, 2 * D_MODEL:] * y_c)
    y = alpha * x_ref[...] + _dot(_bf(merged), wo_ref[...])
    o_ref[...] = _layer_norm(y, lg_ref[...], lb_ref[...])


def _merge(x, ya, ob, yc, w_gate, gate_b, w_ra, w_mo, w_out, ln_g, ln_b, alpha, tm):
    n = x.shape[0]
    tok = lambda i: (i, 0)
    fixed = lambda i: (0, 0)
    full = lambda a: pl.BlockSpec(a.shape, fixed, pipeline_mode=pl.Buffered(1))
    return pl.pallas_call(
        functools.partial(_merge_kernel, alpha=alpha),
        grid=(n // tm,),
        in_specs=[
            pl.BlockSpec((tm, D_MODEL), tok),
            pl.BlockSpec((tm, RWKV_DIM), tok), pl.BlockSpec((tm, MLA_DIM), tok),
            pl.BlockSpec((tm, D_MODEL), tok),
            full(w_gate), full(gate_b), full(w_ra), full(w_mo), full(w_out), full(ln_g), full(ln_b),
        ],
        out_specs=pl.BlockSpec((tm, D_MODEL), tok),
        out_shape=jax.ShapeDtypeStruct((n, D_MODEL), F32),
        compiler_params=_params("parallel"),
        name="merge_ln",
    )(x, ya, ob, yc, w_gate, gate_b, w_ra, w_mo, w_out, ln_g, ln_b)


def _ffn_kernel(x_ref, w1_ref, w3_ref, w2_ref, lg_ref, lb_ref, o_ref, *, alpha, tf):
    x = x_ref[...]
    xb = _bf(x)
    acc = alpha * x
    for j in range(w1_ref.shape[1] // tf):
        cols = slice(j * tf, (j + 1) * tf)
        h1 = _dot(xb, w1_ref[:, cols])
        h3 = _dot(xb, w3_ref[:, cols])
        acc = acc + _dot(_bf(h1 * _sigmoid(h1) * h3), w2_ref[cols, :])
    o_ref[...] = _layer_norm(acc, lg_ref[...], lb_ref[...])


def _ffn(x, w1, w3, w2, ln_g, ln_b, alpha, tm, tf):
    n = x.shape[0]
    tok = lambda i: (i, 0)
    fixed = lambda i: (0, 0)
    resident = lambda a: pl.BlockSpec(a.shape, fixed, pipeline_mode=pl.Buffered(1))
    return pl.pallas_call(
        functools.partial(_ffn_kernel, alpha=alpha, tf=tf),
        grid=(n // tm,),
        in_specs=[
            pl.BlockSpec((tm, D_MODEL), tok),
            resident(w1), resident(w3), resident(w2),
            pl.BlockSpec((1, D_MODEL), fixed), pl.BlockSpec((1, D_MODEL), fixed),
        ],
        out_specs=pl.BlockSpec((tm, D_MODEL), tok),
        out_shape=jax.ShapeDtypeStruct((n, D_MODEL), F32),
        compiler_params=_params("parallel"),
        name="ffn_ln",
    )(x, w1, w3, w2, ln_g, ln_b)


def _rope_tables(t):
    pos = jnp.arange(t, dtype=F32)
    inv_freq = ROPE_THETA ** (-jnp.arange(0, QK_ROPE_DIM, 2, dtype=F32) / QK_ROPE_DIM)
    ang = pos[:, None] * inv_freq[None, :]
    cos, sin = jnp.cos(ang), jnp.sin(ang)
    ones = jnp.ones((t, QK_NOPE_DIM), F32)
    zeros = jnp.zeros((t, QK_NOPE_DIM), F32)
    tail1 = jnp.ones((t, LANES - QK_NOPE_DIM - QK_ROPE_DIM), F32)
    tail0 = jnp.zeros((t, LANES - QK_NOPE_DIM - QK_ROPE_DIM), F32)
    cos_t = jnp.concatenate([ones, cos, cos, tail1], axis=1)
    sin_t = jnp.concatenate([zeros, sin, sin, tail0], axis=1)
    return cos_t, sin_t


def _branch_weights(wl, w2, a2, q_up, kv_up):
    o_mla = RWKV_COLS
    o_s5 = o_mla + MLA_COLS
    o_gate = o_s5 + S5_COLS
    nq = Q_LORA_RANK + KV_LORA_RANK
    zl = jnp.zeros((DECAY_LORA, RWKV_DIM), F32)
    w_lora = jnp.concatenate([jnp.concatenate([w2, zl], axis=1), jnp.concatenate([zl, a2], axis=1)], axis=0)
    w_kr = jnp.zeros((D_MODEL, LANES), F32).at[:, QK_NOPE_DIM:QK_NOPE_DIM + QK_ROPE_DIM].set(
        wl[:, o_mla + nq:o_s5])
    w_mla = jnp.concatenate([wl[:, o_mla:o_mla + nq], w_kr, _rope_rotated(w_kr)], axis=1)
    q_up = q_up.reshape(Q_LORA_RANK, MLA_HEADS, QK_NOPE_DIM + QK_ROPE_DIM)
    q_up = jnp.pad(q_up, ((0, 0), (0, 0), (0, LANES - QK_NOPE_DIM - QK_ROPE_DIM)))
    q_up = jnp.concatenate([q_up.reshape(Q_LORA_RANK, MLA_HEADS * LANES),
                            _rope_rotated(q_up).reshape(Q_LORA_RANK, MLA_HEADS * LANES)], axis=1)
    kv_up = kv_up.reshape(KV_LORA_RANK, MLA_HEADS, QK_NOPE_DIM + V_HEAD_DIM)
    k_up = jnp.pad(kv_up[:, :, :QK_NOPE_DIM], ((0, 0), (0, 0), (0, LANES - QK_NOPE_DIM)))
    kv_up = jnp.concatenate([k_up.reshape(KV_LORA_RANK, MLA_HEADS * LANES),
                             kv_up[:, :, QK_NOPE_DIM:].reshape(KV_LORA_RANK, MLA_DIM)], axis=1)
    return dict(rwkv=_bf(wl[:, :o_mla]), lora=_bf(w_lora), mla=_bf(w_mla), q_up=_bf(q_up), kv_up=_bf(kv_up),
                s5=_bf(wl[:, o_s5:o_gate]), gate=_bf(wl[:, o_gate:]))


def _s5_params(lambda_re, lambda_im, log_step, b_re, b_im, c_re, c_im):
    lb_re, lb_im, bb_re, bb_im = _s5_discretize(lambda_re, lambda_im, log_step, b_re, b_im)
    rows = lambda z: z.reshape(-1, S5_DIM, S5_STATE)
    return _s5_prep(*jax.vmap(_s5_matrices)(lb_re, lb_im, bb_re, bb_im, rows(c_re), rows(c_im)))


def _row(z):
    return z.reshape(1, -1).astype(F32)


ROW_TILE = dict(rwkv_prep=1024, mla_prep=512, attention=1024, s5=1024, merge=512, ffn=512)


def _tile(t, name):
    return min(t, ROW_TILE[name])


def kernel(x, w_in, rwkv_mu, rwkv_w0, rwkv_w2, rwkv_a0, rwkv_a2, rwkv_g2, rwkv_k_k, rwkv_k_a, rwkv_r_k,
           rwkv_gn_g, rwkv_gn_b, rwkv_out, mla_q_norm, mla_q_up, mla_kv_norm, mla_kv_up, mla_out,
           s5_lambda_re, s5_lambda_im, s5_log_step, s5_b_re, s5_b_im, s5_c_re, s5_c_im, s5_d, s5_glu,
           gate_b, w_out, ln1_g, ln1_b, ffn_w1, ffn_w3, ffn_w2, ln2_g, ln2_b):
    bsz, t, _ = x.shape
    depth = w_in.shape[0]
    alpha = (2.0 * depth) ** 0.25
    n = bsz * t
    cos_t, sin_t = _rope_tables(t)
    d_ff = ffn_w1.shape[2]
    tf = d_ff // 2 if (d_ff // 2) % LANES == 0 else d_ff

    ffn_w1, ffn_w3, ffn_w2, s5_glu, w_out, rwkv_out, mla_out = map(
        _to_bf16, (ffn_w1, ffn_w3, ffn_w2, s5_glu, w_out, rwkv_out, mla_out))

    w = jax.vmap(_branch_weights)(w_in, rwkv_w2, rwkv_a2, mla_q_up, mla_kv_up)
    wg, kc, kst, lam8 = _s5_params(s5_lambda_re, s5_lambda_im, s5_log_step, s5_b_re, s5_b_im, s5_c_re, s5_c_im)

    for l in range(depth):
        x_bf, r, k, v, ld, a, g = _rwkv_prep(
            x, w["rwkv"][l], _row(rwkv_mu[l]), _row(rwkv_w0[l]), _row(rwkv_a0[l]),
            w["lora"][l], _bf(rwkv_g2[l]), _tile(t, "rwkv_prep"))
        ya = _rwkv_scan(r, k, v, ld, a, g, _row(rwkv_k_k[l]), _row(rwkv_k_a[l]), _row(rwkv_r_k[l]),
                        _row(rwkv_gn_g[l]), _row(rwkv_gn_b[l]))
        q, kx, vx = _mla_prep(x_bf, w["mla"][l], _row(mla_q_norm[l]), w["q_up"][l], _row(mla_kv_norm[l]),
                              w["kv_up"][l], cos_t, sin_t, _tile(t, "mla_prep"))
        ob = _attention(q, kx, vx, _tile(t, "attention"), min(t, ATTN_KEY_BLOCK))
        yc = _s5(x_bf, w["s5"][l], wg[l], kc[l], kst[l], lam8[l], _row(s5_d[l]), s5_glu[l], _tile(t, "s5"))
        x1 = _merge(x.reshape(n, D_MODEL), ya.reshape(n, RWKV_DIM), ob.reshape(n, MLA_DIM),
                    yc.reshape(n, D_MODEL), w["gate"][l], _row(gate_b[l]),
                    rwkv_out[l], mla_out[l], w_out[l], _row(ln1_g[l]), _row(ln1_b[l]),
                    alpha, _tile(n, "merge"))
        x2 = _ffn(x1, ffn_w1[l], ffn_w3[l], ffn_w2[l], _row(ln2_g[l]),
                  _row(ln2_b[l]), alpha, _tile(n, "ffn"), tf)
        x = x2.reshape(bsz, t, D_MODEL)
    return x
```

```python
import functools
import math

import jax
import jax.numpy as jnp
from jax import lax
from jax.experimental import pallas as pl
from jax.experimental.pallas import tpu as pltpu

F32 = jnp.float32
BF16 = jnp.bfloat16

D_MODEL = 1024
RWKV_HEADS = 8
RWKV_HEAD_DIM = 64
RWKV_DIM = RWKV_HEADS * RWKV_HEAD_DIM
DECAY_LORA = 64
AAA_LORA = 64
GATE_LORA = 128
RWKV_GN_EPS = 64e-5
MLA_HEADS = 8
QK_NOPE_DIM = 64
QK_ROPE_DIM = 32
V_HEAD_DIM = 64
Q_LORA_RANK = 256
KV_LORA_RANK = 128
MLA_DIM = MLA_HEADS * V_HEAD_DIM
ROPE_THETA = 10000.0
ATTN_SCALE = 1.0 / math.sqrt(QK_NOPE_DIM + QK_ROPE_DIM)
LOG2_E = math.log2(math.e)
S5_DIM = 512
S5_GROUP = 16
S5_GROUPS = S5_DIM // S5_GROUP
S5_STATE = 64
N_BRANCHES = 3
LN_EPS = 1e-5
RMS_EPS = 1e-6
RWKV_COLS = 3 * RWKV_DIM + DECAY_LORA + AAA_LORA + GATE_LORA
MLA_COLS = Q_LORA_RANK + KV_LORA_RANK + QK_ROPE_DIM
S5_COLS = S5_DIM

LANES = 128
HEAD_PAIR = LANES // RWKV_HEAD_DIM
N_PAIRS = RWKV_HEADS // HEAD_PAIR
S5_SLAB_GROUPS = LANES // S5_GROUP
S5_SLABS = S5_GROUPS // S5_SLAB_GROUPS
S5_SLAB_STATE = S5_SLAB_GROUPS * S5_STATE
S5_CHUNK = 8
VMEM_LIMIT = 48 * 1024 * 1024

RWKV_CHUNK = 64
RWKV_BLOCK = 256
ATTN_PAIRS = 1
ATTN_KEY_BLOCK = 1024
RWKV_SCAN_PAIRS = 4
NEG_BIG = -1e30
CAST_BLOCK_BYTES = 6 * 1024 * 1024


def _bf(x):
    return x.astype(BF16)


def _dot(a, b):
    return jnp.dot(a, b, preferred_element_type=F32)


def _dot_nt(a, b):
    return lax.dot_general(a, b, (((1,), (1,)), ((), ())), preferred_element_type=F32)


def _dot_tn(a, b):
    return lax.dot_general(a, b, (((0,), (0,)), ((), ())), preferred_element_type=F32)


def _sigmoid(x):
    return 1.0 / (1.0 + jnp.exp(-x))


def _params(*sem):
    return pltpu.CompilerParams(dimension_semantics=sem, vmem_limit_bytes=VMEM_LIMIT)


def _cast_kernel(w_ref, o_ref):
    o_ref[...] = w_ref[...].astype(o_ref.dtype)


def _to_bf16(w):
    depth, rows, cols = w.shape
    cap = CAST_BLOCK_BYTES // (4 * cols)
    tr = max(d for d in range(8, rows + 1, 8) if rows % d == 0 and d <= max(cap, 8))
    spec = pl.BlockSpec((None, tr, cols), lambda l, i: (l, i, 0))
    return pl.pallas_call(
        _cast_kernel, grid=(depth, rows // tr), in_specs=[spec], out_specs=spec,
        out_shape=jax.ShapeDtypeStruct(w.shape, BF16),
        compiler_params=_params("parallel", "parallel"), name="cast_bf16",
    )(w)


def _layer_norm(y, g, b):
    mu = jnp.mean(y, -1, keepdims=True)
    d = y - mu
    var = jnp.mean(d * d, -1, keepdims=True)
    return d * lax.rsqrt(var + LN_EPS) * g + b


def _pair_sum(x, m0):
    s0 = jnp.sum(jnp.where(m0, x, 0.0), -1, keepdims=True)
    s1 = jnp.sum(jnp.where(m0, 0.0, x), -1, keepdims=True)
    return jnp.where(m0, s0, s1)


def _rwkv_prep_kernel(x_ref, w_ref, mu_ref, w0_ref, a0_ref, wl_ref, g2_ref,
                      xb_ref, r_ref, k_ref, v_ref, ld_ref, a_ref, g_ref, prev_ref):
    t = pl.program_id(1)

    @pl.when(t == 0)
    def _():
        prev_ref[...] = jnp.zeros_like(prev_ref)

    xb = _bf(x_ref[...])
    xb_ref[...] = xb
    p = _dot_nt(xb, w_ref[...])
    rolled = pltpu.roll(p, 1, axis=0)
    row = lax.broadcasted_iota(jnp.int32, (p.shape[0], 1), 0)
    shifted = jnp.where(row == 0, prev_ref[0:1, :], rolled)
    prev_ref[0:1, :] = rolled[0:1, :]
    p = p + (shifted - p) * mu_ref[...]

    c0, c1, c2 = RWKV_DIM, 2 * RWKV_DIM, 3 * RWKV_DIM
    r_ref[...] = p[:, :c0].astype(r_ref.dtype)
    k_ref[...] = p[:, c0:c1].astype(k_ref.dtype)
    v_ref[...] = p[:, c1:c2].astype(v_ref.dtype)
    lora = p[:, c2:c2 + DECAY_LORA + AAA_LORA]
    lane = lax.broadcasted_iota(jnp.int32, (1, DECAY_LORA + AAA_LORA), 1)
    lora = jnp.where(lane < DECAY_LORA, jnp.tanh(lora), lora)
    wa = _dot(_bf(lora), wl_ref[...])
    ld_ref[...] = (-math.exp(-0.5)) * _sigmoid(w0_ref[...] + wa[:, :RWKV_DIM])
    a_ref[...] = _sigmoid(a0_ref[...] + wa[:, RWKV_DIM:]).astype(a_ref.dtype)
    dg = p[:, c2 + DECAY_LORA + AAA_LORA:]
    g_ref[...] = _dot(_bf(_sigmoid(dg)), g2_ref[...]).astype(g_ref.dtype)


def _rwkv_prep(x, w_rwkv, mu, w0, a0, w_lora, g2, tm):
    bsz, t, _ = x.shape
    outs = [jax.ShapeDtypeStruct((bsz, t, D_MODEL), BF16)] + [
        jax.ShapeDtypeStruct((bsz, t, RWKV_DIM), dt) for dt in (BF16, BF16, BF16, F32, BF16, BF16)]
    tok = lambda b, i: (b, i, 0)
    fixed = lambda b, i: (0, 0)
    o_spec = pl.BlockSpec((None, tm, RWKV_DIM), tok)
    return pl.pallas_call(
        _rwkv_prep_kernel,
        grid=(bsz, t // tm),
        in_specs=[
            pl.BlockSpec((None, tm, D_MODEL), tok),
            pl.BlockSpec((RWKV_COLS, D_MODEL), fixed),
            pl.BlockSpec((1, RWKV_COLS), fixed),
            pl.BlockSpec((1, RWKV_DIM), fixed),
            pl.BlockSpec((1, RWKV_DIM), fixed),
            pl.BlockSpec((DECAY_LORA + AAA_LORA, 2 * RWKV_DIM), fixed),
            pl.BlockSpec((GATE_LORA, RWKV_DIM), fixed),
        ],
        out_specs=[pl.BlockSpec((None, tm, D_MODEL), tok)] + [o_spec] * 6,
        out_shape=outs,
        scratch_shapes=[pltpu.VMEM((8, RWKV_COLS), F32)],
        compiler_params=_params("parallel", "arbitrary"),
        name="rwkv_prep",
    )(x, w_rwkv, mu, w0, a0, w_lora, g2)


def _rwkv_scan_kernel(r_ref, k_ref, v_ref, ld_ref, a_ref, g_ref,
                      kk_ref, ka_ref, rk_ref, gng_ref, gnb_ref, o_ref, s_ref):
    @pl.when(pl.program_id(1) == 0)
    def _():
        s_ref[...] = jnp.zeros_like(s_ref)

    seqs = [(b, slice(p * LANES, (p + 1) * LANES)) for b in range(r_ref.shape[0])
            for p in range(r_ref.shape[2] // LANES)]
    ins = [tuple(z[b, :, cols].astype(F32) for z in (r_ref, k_ref, v_ref, ld_ref, a_ref, g_ref))
           for b, cols in seqs]
    pars = [tuple(z[:, cols] for z in (kk_ref, ka_ref, rk_ref, gng_ref, gnb_ref)) for b, cols in seqs]
    outs, states = _rwkv_blocks(ins, pars, [s_ref[i] for i in range(len(seqs))])
    for i, (b, cols) in enumerate(seqs):
        o_ref[b, :, cols] = outs[i].astype(o_ref.dtype)
        s_ref[i] = states[i]


def _rwkv_blocks(ins, pars, states):
    rows, chunk = RWKV_BLOCK, RWKV_CHUNK
    n_chunks = rows // chunk
    nb = len(ins)
    inst = [(b, h) for b in range(nb) for h in range(HEAD_PAIR)]

    lane = lax.broadcasted_iota(jnp.int32, (1, LANES), 1)
    m0 = lane < RWKV_HEAD_DIM
    rin = lax.broadcasted_iota(jnp.int32, (rows, 1), 0) & (chunk - 1)

    pre = []
    for (r, k, v, ld, a, g), (k_k, k_a, _, _, _) in zip(ins, pars):
        kk = k * k_k
        kk = kk * lax.rsqrt(jnp.maximum(_pair_sum(kk * kk, m0), 1e-12))
        k2 = k * (1.0 + (a - 1.0) * k_a)
        bv = kk * a
        cl = ld
        sh = 1
        while sh < chunk:
            cl = cl + jnp.where(rin >= sh, pltpu.roll(cl, sh, axis=0), 0.0)
            sh *= 2
        w_inc = jnp.exp(cl)
        w_inv = jnp.exp(-cl)
        at = -kk * jnp.exp(cl - ld)
        rt = r * w_inc
        pre.append(dict(
            k2=k2, w_inc=w_inc, rt=rt, bt=_bf(bv * w_inv), kt=_bf(k2 * w_inv), v_bf=_bf(v),
            at_h=(_bf(jnp.where(m0, at, 0.0)), _bf(jnp.where(m0, 0.0, at))),
            rt_h=(_bf(jnp.where(m0, rt, 0.0)), _bf(jnp.where(m0, 0.0, rt)))))

    grams = []
    for d in pre:
        lhs = jnp.concatenate([d["at_h"][0], d["at_h"][1], d["rt_h"][0], d["rt_h"][1]], axis=0)
        rhs = jnp.concatenate([d["bt"], d["kt"]], axis=0)
        grams.append(_dot_nt(lhs, rhs))

    ri = lax.broadcasted_iota(jnp.int32, (rows, rows), 0)
    ci = lax.broadcasted_iota(jnp.int32, (rows, rows), 1)
    same = (ri // chunk) == (ci // chunk)
    strict = same & (ri > ci)
    incl = same & (ri >= ci)
    eye = jnp.where(ri == ci, 1.0, 0.0)

    xs = [jnp.where(strict, grams[b][h * rows:(h + 1) * rows, :rows], 0.0) for b, h in inst]
    t_inv = [eye + x for x in xs]
    n = 2
    while n < chunk:
        xb = [_bf(x) for x in xs]
        xs = [_dot(x, x) for x in xb]
        t_inv = [t + _dot(_bf(t), _bf(x)) for t, x in zip(t_inv, xs)]
        n *= 2
    t_bf = [_bf(t) for t in t_inv]

    akv = [_dot(_bf(jnp.where(strict, grams[b][h * rows:(h + 1) * rows, rows:], 0.0)), pre[b]["v_bf"])
           for b, h in inst]
    up_h = [_dot(t, jnp.concatenate([_bf(z), pre[b]["at_h"][h]], axis=1))
            for t, z, (b, h) in zip(t_bf, akv, inst)]
    u0_h = [z[:, :LANES] for z in up_h]
    p_h = [z[:, LANES:] for z in up_h]
    a_rb = [_bf(jnp.where(incl, grams[b][(2 + h) * rows:(3 + h) * rows, :rows], 0.0)) for b, h in inst]
    a_rk = [_bf(jnp.where(incl, grams[b][(2 + h) * rows:(3 + h) * rows, rows:], 0.0)) for b, h in inst]
    rb_up = [_dot(a_rb[i], _bf(up_h[i])) for i in range(len(inst))]
    y0_h = [rb_up[i][:, :LANES] + _dot(a_rk[i], pre[b]["v_bf"]) for i, (b, h) in enumerate(inst)]
    qp_h = [z[:, LANES:] for z in rb_up]

    bi = lax.broadcasted_iota(jnp.int32, (LANES, LANES), 0) // RWKV_HEAD_DIM
    bj = lax.broadcasted_iota(jnp.int32, (LANES, LANES), 1) // RWKV_HEAD_DIM
    blockdiag = bi == bj

    p_all, q_all, u0_all, y0_all = [], [], [], []
    for b in range(nb):
        i0, i1 = HEAD_PAIR * b, HEAD_PAIR * b + 1
        p_all.append(_bf(p_h[i0] + p_h[i1]))
        q_all.append(_bf(pre[b]["rt"] + qp_h[i0] + qp_h[i1]))
        u0_all.append(_bf(jnp.where(m0, u0_h[i0], u0_h[i1])))
        y0_all.append(jnp.where(m0, y0_h[i0], y0_h[i1]))

    m_c, c_c = {}, {}
    for c in range(n_chunks):
        lo, hi = c * chunk, (c + 1) * chunk
        for b in range(nb):
            bt_c, kt_c = pre[b]["bt"][lo:hi], pre[b]["kt"][lo:hi]
            m_c[b, c] = _bf(jnp.where(blockdiag, _dot_tn(p_all[b][lo:hi], bt_c), 0.0))
            c_c[b, c] = jnp.where(blockdiag, _dot_tn(
                jnp.concatenate([u0_all[b][lo:hi], pre[b]["v_bf"][lo:hi]], axis=0),
                jnp.concatenate([bt_c, kt_c], axis=0)), 0.0)

    ys = [[] for _ in range(nb)]
    states = list(states)
    for c in range(n_chunks):
        lo, hi = c * chunk, (c + 1) * chunk
        for b in range(nb):
            s_bf = _bf(states[b])
            ys[b].append(_dot_nt(q_all[b][lo:hi], s_bf) + y0_all[b][lo:hi])
            states[b] = (states[b] + _dot(s_bf, m_c[b, c]) + c_c[b, c]) * pre[b]["w_inc"][hi - 1:hi, :]

    outs = []
    inv_n = 1.0 / RWKV_HEAD_DIM
    for b, (r, k, v, ld, a, g) in enumerate(ins):
        _, _, r_k, gn_g, gn_b = pars[b]
        y = jnp.concatenate(ys[b], axis=0)
        mean = _pair_sum(y, m0) * inv_n
        d = y - mean
        var = _pair_sum(d * d, m0) * inv_n
        yn = d * lax.rsqrt(var + RWKV_GN_EPS) * gn_g + gn_b
        bonus = _pair_sum(r * pre[b]["k2"] * r_k, m0) * v
        outs.append((yn + bonus) * g)
    return outs, states


def _rwkv_scan(r, k, v, ld, a, g, k_k, k_a, r_k, gn_g, gn_b):
    bsz, t, _ = r.shape
    width = RWKV_SCAN_PAIRS * LANES
    tok = pl.BlockSpec((bsz, RWKV_BLOCK, width), lambda h, c: (0, c, h))
    par = pl.BlockSpec((1, width), lambda h, c: (0, h))
    return pl.pallas_call(
        _rwkv_scan_kernel,
        grid=(N_PAIRS // RWKV_SCAN_PAIRS, t // RWKV_BLOCK),
        in_specs=[tok] * 6 + [par] * 5,
        out_specs=tok,
        out_shape=jax.ShapeDtypeStruct((bsz, t, RWKV_DIM), BF16),
        scratch_shapes=[pltpu.VMEM((bsz * RWKV_SCAN_PAIRS, LANES, LANES), F32)],
        compiler_params=_params("parallel", "arbitrary"),
        name="rwkv_scan",
    )(r, k, v, ld, a, g, k_k, k_a, r_k, gn_g, gn_b)


def _rope_rotated(w):
    lo, mid, hi = QK_NOPE_DIM, QK_NOPE_DIM + QK_ROPE_DIM // 2, QK_NOPE_DIM + QK_ROPE_DIM
    return jnp.zeros_like(w).at[..., lo:mid].set(-w[..., mid:hi]).at[..., mid:hi].set(w[..., lo:mid])


def _mla_prep_kernel(x_ref, w_ref, qn_ref, qup_ref, kvn_ref, kvup_ref, cos_ref, sin_ref,
                     q_ref, k_ref, v_ref):
    nq, nkv, wq = Q_LORA_RANK, KV_LORA_RANK, MLA_HEADS * LANES
    p = _dot_nt(x_ref[...], w_ref[...])
    c_q = p[:, :nq]
    c_kv = p[:, nq:nq + nkv]
    c_q = c_q * lax.rsqrt(jnp.mean(c_q * c_q, -1, keepdims=True) + RMS_EPS) * qn_ref[...]
    c_kv = c_kv * lax.rsqrt(jnp.mean(c_kv * c_kv, -1, keepdims=True) + RMS_EPS) * kvn_ref[...]
    q = _dot(_bf(c_q), qup_ref[...])
    kv = _dot(_bf(c_kv), kvup_ref[...])
    cos_t, sin_t = cos_ref[...], sin_ref[...]
    reps = MLA_HEADS
    q = q[:, :wq] * jnp.tile(cos_t, (1, reps)) + q[:, wq:] * jnp.tile(sin_t, (1, reps))
    q_ref[...] = (q * (ATTN_SCALE * LOG2_E)).astype(q_ref.dtype)
    kr = p[:, nq + nkv:nq + nkv + LANES] * cos_t + p[:, nq + nkv + LANES:] * sin_t
    k_ref[...] = (kv[:, :wq] + jnp.tile(kr, (1, reps))).astype(k_ref.dtype)
    v_ref[...] = kv[:, wq:].astype(v_ref.dtype)


def _mla_prep(x_bf, w_mla, q_norm, q_up, kv_norm, kv_up, cos_t, sin_t, tm):
    bsz, t, _ = x_bf.shape
    tok = lambda b, i: (b, i, 0)
    fixed = lambda b, i: (0, 0)
    tab = pl.BlockSpec((tm, LANES), lambda b, i: (i, 0))
    wq = MLA_HEADS * LANES
    return pl.pallas_call(
        _mla_prep_kernel,
        grid=(bsz, t // tm),
        in_specs=[
            pl.BlockSpec((None, tm, D_MODEL), tok),
            pl.BlockSpec(w_mla.shape, fixed),
            pl.BlockSpec((1, Q_LORA_RANK), fixed),
            pl.BlockSpec(q_up.shape, fixed),
            pl.BlockSpec((1, KV_LORA_RANK), fixed),
            pl.BlockSpec(kv_up.shape, fixed),
            tab, tab,
        ],
        out_specs=[pl.BlockSpec((None, tm, wq), tok), pl.BlockSpec((None, tm, wq), tok),
                   pl.BlockSpec((None, tm, MLA_DIM), tok)],
        out_shape=[jax.ShapeDtypeStruct((bsz, t, wq), BF16), jax.ShapeDtypeStruct((bsz, t, wq), BF16),
                   jax.ShapeDtypeStruct((bsz, t, MLA_DIM), BF16)],
        compiler_params=_params("parallel", "parallel"),
        name="mla_prep",
    )(x_bf, w_mla, q_norm, q_up, kv_norm, kv_up, cos_t, sin_t)


def _attn_kernel(q_ref, k_ref, v_ref, o_ref, m_ref, l_ref, acc_ref, *, tq, tk):
    i = pl.program_id(2)
    lane = lax.broadcasted_iota(jnp.int32, (1, LANES), 1)
    m0 = lane < V_HEAD_DIM
    m_ref[...] = jnp.full_like(m_ref, NEG_BIG)
    l_ref[...] = jnp.zeros_like(l_ref)
    acc_ref[...] = jnp.zeros_like(acc_ref)

    def step(start, nk, row_lo, masked):
        rows = slice(row_lo, tq)
        nrow = tq - row_lo
        heads = range(q_ref.shape[1] // LANES)
        s_h = [_dot_nt(q_ref[rows, h * LANES:(h + 1) * LANES], k_ref[pl.ds(start, nk), h * LANES:(h + 1) * LANES])
               for h in heads]
        alphas, ps = [], []
        for h in heads:
            s = s_h[h]
            if masked:
                qi = lax.broadcasted_iota(jnp.int32, (nrow, nk), 0) + (i * tq + row_lo)
                ki = lax.broadcasted_iota(jnp.int32, (nrow, nk), 1) + start
                s = jnp.where(ki <= qi, s, NEG_BIG)
            m_old = m_ref[h, rows]
            m_new = jnp.maximum(m_old, jnp.max(s, -1, keepdims=True))
            alpha = jnp.exp2(m_old - m_new)
            p = jnp.exp2(s - jnp.tile(m_new, (1, nk // LANES)))
            p_sum = p[:, :LANES]
            for c in range(1, nk // LANES):
                p_sum = p_sum + p[:, c * LANES:(c + 1) * LANES]
            l_ref[h, rows] = alpha * l_ref[h, rows] + p_sum
            m_ref[h, rows] = m_new
            alphas.append(alpha)
            ps.append(_bf(p))
        for h in heads:
            pair = slice((h // HEAD_PAIR) * LANES, (h // HEAD_PAIR + 1) * LANES)
            acc_ref[h, rows] = alphas[h] * acc_ref[h, rows] + _dot(ps[h], v_ref[pl.ds(start, nk), pair])

    def body(j, carry):
        step(pl.multiple_of(j * tk, tk), tk, 0, False)
        return carry

    lax.fori_loop(0, i * (tq // tk), body, 0)
    half = tq // 2
    base = pl.multiple_of(i * tq, tq)
    step(base, half, 0, True)
    step(base + half, half, half, True)
    for p in range(o_ref.shape[1] // LANES):
        out = [acc_ref[h] / jnp.sum(l_ref[h], -1, keepdims=True) for h in (HEAD_PAIR * p, HEAD_PAIR * p + 1)]
        o_ref[:, p * LANES:(p + 1) * LANES] = jnp.where(m0, out[0], out[1]).astype(o_ref.dtype)


def _attention(q, k, v, tq, tk):
    bsz, t, _ = q.shape
    npair = ATTN_PAIRS
    heads = npair * HEAD_PAIR
    resident = lambda width: pl.BlockSpec((None, t, width), lambda b, h, i: (b, 0, h))
    return pl.pallas_call(
        functools.partial(_attn_kernel, tq=tq, tk=tk),
        grid=(bsz, N_PAIRS // npair, t // tq),
        in_specs=[
            pl.BlockSpec((None, tq, heads * LANES), lambda b, h, i: (b, i, h)),
            resident(heads * LANES),
            resident(npair * LANES),
        ],
        out_specs=pl.BlockSpec((None, tq, npair * LANES), lambda b, h, i: (b, i, h)),
        out_shape=jax.ShapeDtypeStruct((bsz, t, MLA_DIM), BF16),
        scratch_shapes=[pltpu.VMEM((heads, tq, LANES), F32)] * 3,
        compiler_params=_params("parallel", "parallel", "arbitrary"),
        name="mla_attention",
    )(q, k, v)


def _s5_disc_kernel(lre_ref, lim_ref, ls_ref, br_ref, bi_ref, lbr_ref, lbi_ref, bbr_ref, bbi_ref):
    lam_re = jnp.minimum(lre_ref[...], -1e-4)
    lam_im = lim_ref[...]
    step = jnp.exp(ls_ref[...])
    mag = jnp.exp(lam_re * step)
    ang = lam_im * step
    lb_re, lb_im = mag * jnp.cos(ang), mag * jnp.sin(ang)
    den = lam_re * lam_re + lam_im * lam_im
    n_re = lb_re - 1.0
    f_re = (n_re * lam_re + lb_im * lam_im) / den
    f_im = (lb_im * lam_re - n_re * lam_im) / den
    br, bi = br_ref[...], bi_ref[...]
    lbr_ref[...] = lb_re
    lbi_ref[...] = lb_im
    bbr_ref[...] = f_re * br - f_im * bi
    bbi_ref[...] = f_re * bi + f_im * br


def _s5_discretize(lambda_re, lambda_im, log_step, b_re, b_im):
    depth = lambda_re.shape[0]
    rep = lambda z: jnp.repeat(z, S5_GROUP, axis=1)
    ls = rep(jnp.broadcast_to(log_step[:, :, None], (depth, S5_GROUPS, S5_STATE)))
    tr = lambda z: jnp.transpose(z, (0, 1, 3, 2)).reshape(depth, S5_DIM, S5_STATE)
    spec = pl.BlockSpec((None, S5_DIM, S5_STATE), lambda l: (l, 0, 0))
    out = jax.ShapeDtypeStruct((depth, S5_DIM, S5_STATE), F32)
    return pl.pallas_call(
        _s5_disc_kernel, grid=(depth,), in_specs=[spec] * 5, out_specs=[spec] * 4, out_shape=[out] * 4,
        compiler_params=_params("parallel"), name="s5_discretize",
    )(rep(lambda_re), rep(lambda_im), ls, tr(b_re), tr(b_im))


def _split_bf(z):
    hi = _bf(z)
    return hi, _bf(z - hi.astype(F32))


def _s5_prep_kernel(b_ref, c_ref, ct_ref, lam_ref, wg_ref, kc_ref, kst_ref, lam8_ref):
    n = S5_SLAB_STATE
    b, c, ct, lam = b_ref[0], c_ref[0], ct_ref[0], lam_ref[0]
    ar, ai = lam[:, :n], lam[:, n:]
    br, bi = b[:, :n], b[:, n:]
    ctr, cti = ct[:, :n], ct[:, n:]
    c_hi, c_lo = _split_bf(c)
    pr, pi = jnp.ones_like(ar), jnp.zeros_like(ai)
    for j in range(S5_CHUNK):
        w = jnp.concatenate([br * pr - bi * pi, br * pi + bi * pr], axis=1)
        w_hi, w_lo = _split_bf(w)
        rows = slice(j * LANES, (j + 1) * LANES)
        wg_ref[0, rows, :] = w_hi
        kc_ref[0, rows, :] = _bf(_dot(w_hi, c_hi) + _dot(w_lo, c_hi) + _dot(w_hi, c_lo))
        pr, pi = pr * ar - pi * ai, pr * ai + pi * ar
        kst_ref[0, rows, :] = _bf(jnp.concatenate([ctr * pr + cti * pi, cti * pr - ctr * pi], axis=1))
    lam8_ref[0] = jnp.concatenate([pr, pi], axis=1)


def _s5_prep(bmat, cmat, lam):
    depth = bmat.shape[0]
    n2 = 2 * S5_SLAB_STATE
    kq = S5_CHUNK * LANES
    blk = lambda r, c: pl.BlockSpec((None, 1, r, c), lambda l, s: (l, s, 0, 0))
    shape = lambda r, c, dt: jax.ShapeDtypeStruct((depth, S5_SLABS, r, c), dt)
    return pl.pallas_call(
        _s5_prep_kernel,
        grid=(depth, S5_SLABS),
        in_specs=[blk(LANES, n2), blk(n2, LANES), blk(LANES, n2), blk(1, n2)],
        out_specs=[blk(kq, n2), blk(kq, LANES), blk(kq, n2), blk(1, n2)],
        out_shape=[shape(kq, n2, BF16), shape(kq, LANES, BF16), shape(kq, n2, BF16), shape(1, n2, F32)],
        compiler_params=_params("parallel", "parallel"),
        name="s5_prep",
    )(bmat, cmat, jnp.swapaxes(cmat, 2, 3), lam)


def _s5_kernel(x_ref, w_ref, wg_ref, kc_ref, kst_ref, lam8_ref, d_ref, glu_ref, o_ref,
               carry_ref, u_ref, ys_ref, *, ts):
    @pl.when(pl.program_id(1) == 0)
    def _():
        carry_ref[...] = jnp.zeros_like(carry_ref)

    n = S5_SLAB_STATE
    nc = ts // S5_CHUNK
    nb = x_ref.shape[0]
    inst = [(b, sl) for b in range(nb) for sl in range(S5_SLABS)]
    ids = range(len(inst))
    u = [_dot_nt(x_ref[b], w_ref[...]) for b in range(nb)]
    for b, sl in inst:
        u_ref[b, sl] = u[b][:, sl * LANES:(sl + 1) * LANES]
    rin = lax.broadcasted_iota(jnp.int32, (ts, 1), 0) & (S5_CHUNK - 1)
    crow = lax.broadcasted_iota(jnp.int32, (nc, 1), 0)
    first = crow == 0
    x_end = [jnp.concatenate(
        [_bf(u_ref[b, sl, pl.ds(S5_CHUNK - 1 - j, nc, stride=S5_CHUNK), :]) for j in range(S5_CHUNK)], axis=1)
        for b, sl in inst]
    gain = [_dot(x_end[i], wg_ref[sl]) for i, (b, sl) in enumerate(inst)]
    lam8 = [lam8_ref[sl] for b, sl in inst]
    ar = [z[:, :n] for z in lam8]
    ai = [z[:, n:] for z in lam8]
    cr = [carry_ref[b, sl, 0:1, :n] for b, sl in inst]
    cim = [carry_ref[b, sl, 0:1, n:] for b, sl in inst]
    er = [gain[i][:, :n] + jnp.where(first, ar[i] * cr[i] - ai[i] * cim[i], 0.0) for i in ids]
    ei = [gain[i][:, n:] + jnp.where(first, ar[i] * cim[i] + ai[i] * cr[i], 0.0) for i in ids]
    sh = 1
    while sh < nc:
        keep = crow >= sh
        for i in ids:
            sr = jnp.where(keep, pltpu.roll(er[i], sh, axis=0), 0.0)
            si = jnp.where(keep, pltpu.roll(ei[i], sh, axis=0), 0.0)
            er[i], ei[i] = er[i] + ar[i] * sr - ai[i] * si, ei[i] + ar[i] * si + ai[i] * sr
            ar[i], ai[i] = ar[i] * ar[i] - ai[i] * ai[i], 2.0 * ar[i] * ai[i]
        sh *= 2
    for i, (b, sl) in enumerate(inst):
        carry_ref[b, sl, 0:1, :n] = er[i][nc - 1:nc, :]
        carry_ref[b, sl, 0:1, n:] = ei[i][nc - 1:nc, :]
    h0 = [_bf(jnp.concatenate([jnp.where(first, cr[i], pltpu.roll(er[i], 1, axis=0)),
                               jnp.where(first, cim[i], pltpu.roll(ei[i], 1, axis=0))], axis=1)) for i in ids]
    z = [_dot_nt(h0[i], kst_ref[sl]) for i, (b, sl) in enumerate(inst)]
    for i, (b, sl) in enumerate(inst):
        for k in range(S5_CHUNK):
            ys_ref[b, sl, pl.ds(k, nc, stride=S5_CHUNK), :] = z[i][:, k * LANES:(k + 1) * LANES]
    ys = []
    for b, sl in inst:
        us = u[b][:, sl * LANES:(sl + 1) * LANES]
        lagged = [_bf(us)] + [_bf(jnp.where(rin >= j, pltpu.roll(us, j, axis=0), 0.0))
                              for j in range(1, S5_CHUNK)]
        ys.append(_dot(jnp.concatenate(lagged, axis=1), kc_ref[sl]) + ys_ref[b, sl])
    for b in range(nb):
        y = jnp.concatenate(ys[b * S5_SLABS:(b + 1) * S5_SLABS], axis=1) + d_ref[...] * u[b]
        y = 0.5 * y * (1.0 + jnp.tanh(math.sqrt(2.0 / math.pi) * (y + 0.044715 * (y * y * y))))
        h = _dot(_bf(y), glu_ref[...])
        o_ref[b] = (h[:, :D_MODEL] * _sigmoid(h[:, D_MODEL:])).astype(o_ref.dtype)


def _s5(x_bf, w_s5, wg, kc, kst, lam8, d_skip, w_glu, ts):
    bsz, t, _ = x_bf.shape
    nb = 1
    tok = lambda b, i: (b, i, 0)
    resident = lambda a: pl.BlockSpec(a.shape, lambda b, i: (0,) * a.ndim, pipeline_mode=pl.Buffered(1))
    return pl.pallas_call(
        functools.partial(_s5_kernel, ts=ts),
        grid=(bsz // nb, t // ts),
        in_specs=[pl.BlockSpec((nb, ts, D_MODEL), tok)] + [
            resident(a) for a in (w_s5, wg, kc, kst, lam8, d_skip, w_glu)],
        out_specs=pl.BlockSpec((nb, ts, D_MODEL), tok),
        out_shape=jax.ShapeDtypeStruct((bsz, t, D_MODEL), BF16),
        scratch_shapes=[pltpu.VMEM((nb, S5_SLABS, 8, 2 * S5_SLAB_STATE), F32),
                        pltpu.VMEM((nb, S5_SLABS, ts, LANES), F32), pltpu.VMEM((nb, S5_SLABS, ts, LANES), F32)],
        compiler_params=_params("parallel", "arbitrary"),
        name="s5_scan",
    )(x_bf, w_s5, wg, kc, kst, lam8, d_skip, w_glu)


def _s5_matrices(lb_re, lb_im, bb_re, bb_im, c_re, c_im):
    eye = jnp.eye(S5_SLAB_GROUPS, dtype=F32)
    depth = lb_re.shape[0]
    s, g, c, p = S5_SLABS, S5_SLAB_GROUPS, S5_GROUP, S5_STATE

    def b_blocks(bb):
        return jnp.einsum("lsgcp,gh->lsgchp", bb.reshape(depth, s, g, c, p), eye).reshape(depth, s, g * c, g * p)

    def c_blocks(cc):
        return jnp.einsum("lsgcp,gh->lsgphc", cc.reshape(depth, s, g, c, p), eye).reshape(depth, s, g * p, g * c)

    bmat = jnp.concatenate([b_blocks(bb_re), b_blocks(bb_im)], axis=3)
    cmat = jnp.concatenate([c_blocks(c_re), -c_blocks(c_im)], axis=2)
    row = lambda z: z[:, ::S5_GROUP].reshape(depth, s, 1, g * p)
    lam = jnp.concatenate([row(lb_re), row(lb_im)], axis=3)
    return bmat, cmat, lam


def _merge_kernel(x_ref, ya_ref, ob_ref, yc_ref, wg_ref, gb_ref, wra_ref, wmo_ref, wo_ref,
                  lg_ref, lb_ref, o_ref, *, alpha):
    gates = _sigmoid(_dot_nt(_bf(x_ref[...]), wg_ref[...]) + gb_ref[...])
    y_a = _dot(ya_ref[...], wra_ref[...])
    y_b = _dot(ob_ref[...], wmo_ref[...])
    y_c = yc_ref[...].astype(F32)
    merged = (gates[:, :D_MODEL] * y_a + gates[:, D_MODEL:2 * D_MODEL] * y_b
              + gates[:, 2 * D_MODEL:] * y_c)
    y = alpha * x_ref[...] + _dot(_bf(merged), wo_ref[...])
    o_ref[...] = _layer_norm(y, lg_ref[...], lb_ref[...])


def _merge(x, ya, ob, yc, w_gate, gate_b, w_ra, w_mo, w_out, ln_g, ln_b, alpha, tm):
    n = x.shape[0]
    tok = lambda i: (i, 0)
    fixed = lambda i: (0, 0)
    full = lambda a: pl.BlockSpec(a.shape, fixed, pipeline_mode=pl.Buffered(1))
    return pl.pallas_call(
        functools.partial(_merge_kernel, alpha=alpha),
        grid=(n // tm,),
        in_specs=[
            pl.BlockSpec((tm, D_MODEL), tok),
            pl.BlockSpec((tm, RWKV_DIM), tok), pl.BlockSpec((tm, MLA_DIM), tok),
            pl.BlockSpec((tm, D_MODEL), tok),
            full(w_gate), full(gate_b), full(w_ra), full(w_mo), full(w_out), full(ln_g), full(ln_b),
        ],
        out_specs=pl.BlockSpec((tm, D_MODEL), tok),
        out_shape=jax.ShapeDtypeStruct((n, D_MODEL), F32),
        compiler_params=_params("parallel"),
        name="merge_ln",
    )(x, ya, ob, yc, w_gate, gate_b, w_ra, w_mo, w_out, ln_g, ln_b)


def _ffn_kernel(x_ref, w1_ref, w3_ref, w2_ref, lg_ref, lb_ref, o_ref, *, alpha, tf):
    x = x_ref[...]
    xb = _bf(x)
    acc = alpha * x
    for j in range(w1_ref.shape[1] // tf):
        cols = slice(j * tf, (j + 1) * tf)
        h1 = _dot(xb, w1_ref[:, cols])
        h3 = _dot(xb, w3_ref[:, cols])
        acc = acc + _dot(_bf(h1 * _sigmoid(h1) * h3), w2_ref[cols, :])
    o_ref[...] = _layer_norm(acc, lg_ref[...], lb_ref[...])


def _ffn(x, w1, w3, w2, ln_g, ln_b, alpha, tm, tf):
    n = x.shape[0]
    tok = lambda i: (i, 0)
    fixed = lambda i: (0, 0)
    resident = lambda a: pl.BlockSpec(a.shape, fixed, pipeline_mode=pl.Buffered(1))
    return pl.pallas_call(
        functools.partial(_ffn_kernel, alpha=alpha, tf=tf),
        grid=(n // tm,),
        in_specs=[
            pl.BlockSpec((tm, D_MODEL), tok),
            resident(w1), resident(w3), resident(w2),
            pl.BlockSpec((1, D_MODEL), fixed), pl.BlockSpec((1, D_MODEL), fixed),
        ],
        out_specs=pl.BlockSpec((tm, D_MODEL), tok),
        out_shape=jax.ShapeDtypeStruct((n, D_MODEL), F32),
        compiler_params=_params("parallel"),
        name="ffn_ln",
    )(x, w1, w3, w2, ln_g, ln_b)


def _rope_tables(t):
    pos = jnp.arange(t, dtype=F32)
    inv_freq = ROPE_THETA ** (-jnp.arange(0, QK_ROPE_DIM, 2, dtype=F32) / QK_ROPE_DIM)
    ang = pos[:, None] * inv_freq[None, :]
    cos, sin = jnp.cos(ang), jnp.sin(ang)
    ones = jnp.ones((t, QK_NOPE_DIM), F32)
    zeros = jnp.zeros((t, QK_NOPE_DIM), F32)
    tail1 = jnp.ones((t, LANES - QK_NOPE_DIM - QK_ROPE_DIM), F32)
    tail0 = jnp.zeros((t, LANES - QK_NOPE_DIM - QK_ROPE_DIM), F32)
    cos_t = jnp.concatenate([ones, cos, cos, tail1], axis=1)
    sin_t = jnp.concatenate([zeros, sin, sin, tail0], axis=1)
    return cos_t, sin_t


def _branch_weights(w_in, w2, a2, q_up, kv_up):
    depth = w_in.shape[0]
    o_mla = RWKV_COLS
    o_s5 = o_mla + MLA_COLS
    o_gate = o_s5 + S5_COLS
    nq = Q_LORA_RANK + KV_LORA_RANK
    zl = jnp.zeros((depth, DECAY_LORA, RWKV_DIM), F32)
    w_lora = jnp.concatenate([jnp.concatenate([w2, zl], axis=2), jnp.concatenate([zl, a2], axis=2)], axis=1)
    wt = jnp.swapaxes(w_in, 1, 2)
    w_kr = jnp.zeros((depth, D_MODEL, LANES), F32).at[:, :, QK_NOPE_DIM:QK_NOPE_DIM + QK_ROPE_DIM].set(
        w_in[:, :, o_mla + nq:o_s5])
    w_mla = jnp.concatenate([wt[:, o_mla:o_mla + nq], jnp.swapaxes(w_kr, 1, 2),
                             jnp.swapaxes(_rope_rotated(w_kr), 1, 2)], axis=1)
    q_up = q_up.reshape(depth, Q_LORA_RANK, MLA_HEADS, QK_NOPE_DIM + QK_ROPE_DIM)
    q_up = jnp.pad(q_up, ((0, 0), (0, 0), (0, 0), (0, LANES - QK_NOPE_DIM - QK_ROPE_DIM)))
    q_up = jnp.concatenate([q_up.reshape(depth, Q_LORA_RANK, MLA_HEADS * LANES),
                            _rope_rotated(q_up).reshape(depth, Q_LORA_RANK, MLA_HEADS * LANES)], axis=2)
    kv_up = kv_up.reshape(depth, KV_LORA_RANK, MLA_HEADS, QK_NOPE_DIM + V_HEAD_DIM)
    k_up = jnp.pad(kv_up[..., :QK_NOPE_DIM], ((0, 0), (0, 0), (0, 0), (0, LANES - QK_NOPE_DIM)))
    kv_up = jnp.concatenate([k_up.reshape(depth, KV_LORA_RANK, MLA_HEADS * LANES),
                             kv_up[..., QK_NOPE_DIM:].reshape(depth, KV_LORA_RANK, MLA_DIM)], axis=2)
    return dict(rwkv=_bf(wt[:, :o_mla]), lora=_bf(w_lora), mla=_bf(w_mla), q_up=_bf(q_up),
                kv_up=_bf(kv_up), s5=_bf(wt[:, o_s5:o_gate]), gate=_bf(wt[:, o_gate:]))


def _s5_params(lambda_re, lambda_im, log_step, b_re, b_im, c_re, c_im):
    lb_re, lb_im, bb_re, bb_im = _s5_discretize(lambda_re, lambda_im, log_step, b_re, b_im)
    rows = lambda z: z.reshape(-1, S5_DIM, S5_STATE)
    return _s5_prep(*_s5_matrices(lb_re, lb_im, bb_re, bb_im, rows(c_re), rows(c_im)))


def _row(z):
    return z.reshape(1, -1).astype(F32)


ROW_TILE = dict(rwkv_prep=1024, mla_prep=512, attention=1024, s5=1024, merge=512, ffn=512)


def _tile(t, name):
    return min(t, ROW_TILE[name])


def kernel(x, w_in, rwkv_mu, rwkv_w0, rwkv_w2, rwkv_a0, rwkv_a2, rwkv_g2, rwkv_k_k, rwkv_k_a, rwkv_r_k,
           rwkv_gn_g, rwkv_gn_b, rwkv_out, mla_q_norm, mla_q_up, mla_kv_norm, mla_kv_up, mla_out,
           s5_lambda_re, s5_lambda_im, s5_log_step, s5_b_re, s5_b_im, s5_c_re, s5_c_im, s5_d, s5_glu,
           gate_b, w_out, ln1_g, ln1_b, ffn_w1, ffn_w3, ffn_w2, ln2_g, ln2_b):
    bsz, t, _ = x.shape
    depth = w_in.shape[0]
    alpha = (2.0 * depth) ** 0.25
    n = bsz * t
    cos_t, sin_t = _rope_tables(t)
    d_ff = ffn_w1.shape[2]
    tf = d_ff // 2 if (d_ff // 2) % LANES == 0 else d_ff

    ffn_w1, ffn_w3, ffn_w2, s5_glu, w_out, rwkv_out, mla_out = map(
        _to_bf16, (ffn_w1, ffn_w3, ffn_w2, s5_glu, w_out, rwkv_out, mla_out))

    w = _branch_weights(w_in, rwkv_w2, rwkv_a2, mla_q_up, mla_kv_up)
    wg, kc, kst, lam8 = _s5_params(s5_lambda_re, s5_lambda_im, s5_log_step, s5_b_re, s5_b_im, s5_c_re, s5_c_im)

    for l in range(depth):
        x_bf, r, k, v, ld, a, g = _rwkv_prep(
            x, w["rwkv"][l], _row(rwkv_mu[l]), _row(rwkv_w0[l]), _row(rwkv_a0[l]),
            w["lora"][l], _bf(rwkv_g2[l]), _tile(t, "rwkv_prep"))
        ya = _rwkv_scan(r, k, v, ld, a, g, _row(rwkv_k_k[l]), _row(rwkv_k_a[l]), _row(rwkv_r_k[l]),
                        _row(rwkv_gn_g[l]), _row(rwkv_gn_b[l]))
        q, kx, vx = _mla_prep(x_bf, w["mla"][l], _row(mla_q_norm[l]), w["q_up"][l], _row(mla_kv_norm[l]),
                              w["kv_up"][l], cos_t, sin_t, _tile(t, "mla_prep"))
        ob = _attention(q, kx, vx, _tile(t, "attention"), min(t, ATTN_KEY_BLOCK))
        yc = _s5(x_bf, w["s5"][l], wg[l], kc[l], kst[l], lam8[l], _row(s5_d[l]), s5_glu[l], _tile(t, "s5"))
        x1 = _merge(x.reshape(n, D_MODEL), ya.reshape(n, RWKV_DIM), ob.reshape(n, MLA_DIM),
                    yc.reshape(n, D_MODEL), w["gate"][l], _row(gate_b[l]),
                    rwkv_out[l], mla_out[l], w_out[l], _row(ln1_g[l]), _row(ln1_b[l]),
                    alpha, _tile(n, "merge"))
        x2 = _ffn(x1, ffn_w1[l], ffn_w3[l], ffn_w2[l], _row(ln2_g[l]),
                  _row(ln2_b[l]), alpha, _tile(n, "ffn"), tf)
        x = x2.reshape(bsz, t, D_MODEL)
    return x
```

```python
import functools
import math

import jax
import jax.numpy as jnp
from jax import lax
from jax.experimental import pallas as pl
from jax.experimental.pallas import tpu as pltpu

F32 = jnp.float32
BF16 = jnp.bfloat16

D_MODEL = 1024
RWKV_HEADS = 8
RWKV_HEAD_DIM = 64
RWKV_DIM = RWKV_HEADS * RWKV_HEAD_DIM
DECAY_LORA = 64
AAA_LORA = 64
GATE_LORA = 128
RWKV_GN_EPS = 64e-5
MLA_HEADS = 8
QK_NOPE_DIM = 64
QK_ROPE_DIM = 32
V_HEAD_DIM = 64
Q_LORA_RANK = 256
KV_LORA_RANK = 128
MLA_DIM = MLA_HEADS * V_HEAD_DIM
ROPE_THETA = 10000.0
ATTN_SCALE = 1.0 / math.sqrt(QK_NOPE_DIM + QK_ROPE_DIM)
LOG2_E = math.log2(math.e)
S5_DIM = 512
S5_GROUP = 16
S5_GROUPS = S5_DIM // S5_GROUP
S5_STATE = 64
N_BRANCHES = 3
LN_EPS = 1e-5
RMS_EPS = 1e-6
RWKV_COLS = 3 * RWKV_DIM + DECAY_LORA + AAA_LORA + GATE_LORA
MLA_COLS = Q_LORA_RANK + KV_LORA_RANK + QK_ROPE_DIM
S5_COLS = S5_DIM

LANES = 128
HEAD_PAIR = LANES // RWKV_HEAD_DIM
N_PAIRS = RWKV_HEADS // HEAD_PAIR
S5_SLAB_GROUPS = LANES // S5_GROUP
S5_SLABS = S5_GROUPS // S5_SLAB_GROUPS
S5_SLAB_STATE = S5_SLAB_GROUPS * S5_STATE
S5_CHUNK = 8
VMEM_LIMIT = 48 * 1024 * 1024

RWKV_CHUNK = 64
RWKV_BLOCK = 256
ATTN_PAIRS = 1
ATTN_KEY_BLOCK = 1024
RWKV_SCAN_PAIRS = 4
NEG_BIG = -1e30
CAST_BLOCK_BYTES = 6 * 1024 * 1024


def _bf(x):
    return x.astype(BF16)


def _dot(a, b):
    return jnp.dot(a, b, preferred_element_type=F32)


def _dot_nt(a, b):
    return lax.dot_general(a, b, (((1,), (1,)), ((), ())), preferred_element_type=F32)


def _dot_tn(a, b):
    return lax.dot_general(a, b, (((0,), (0,)), ((), ())), preferred_element_type=F32)


def _sigmoid(x):
    return 1.0 / (1.0 + jnp.exp(-x))


def _params(*sem):
    return pltpu.CompilerParams(dimension_semantics=sem, vmem_limit_bytes=VMEM_LIMIT)


def _layer_block(a, layer, single=False):
    idx = (layer,) + (0,) * (a.ndim - 1)
    mode = dict(pipeline_mode=pl.Buffered(1)) if single else {}
    return pl.BlockSpec((None,) + a.shape[1:], lambda *_: idx, **mode)


def _cast_kernel(w_ref, o_ref):
    o_ref[...] = w_ref[...].astype(o_ref.dtype)


def _to_bf16(w):
    depth, rows, cols = w.shape
    cap = CAST_BLOCK_BYTES // (4 * cols)
    tr = max(d for d in range(8, rows + 1, 8) if rows % d == 0 and d <= max(cap, 8))
    spec = pl.BlockSpec((None, tr, cols), lambda l, i: (l, i, 0))
    return pl.pallas_call(
        _cast_kernel, grid=(depth, rows // tr), in_specs=[spec], out_specs=spec,
        out_shape=jax.ShapeDtypeStruct(w.shape, BF16),
        compiler_params=_params("parallel", "parallel"), name="cast_bf16",
    )(w)


def _layer_norm(y, g, b):
    mu = jnp.mean(y, -1, keepdims=True)
    d = y - mu
    var = jnp.mean(d * d, -1, keepdims=True)
    return d * lax.rsqrt(var + LN_EPS) * g + b


def _pair_sum(x, m0):
    s0 = jnp.sum(jnp.where(m0, x, 0.0), -1, keepdims=True)
    s1 = jnp.sum(jnp.where(m0, 0.0, x), -1, keepdims=True)
    return jnp.where(m0, s0, s1)


def _rwkv_prep_kernel(x_ref, w_ref, mu_ref, w0_ref, a0_ref, wl_ref, g2_ref,
                      xb_ref, r_ref, k_ref, v_ref, ld_ref, a_ref, g_ref, prev_ref):
    t = pl.program_id(1)

    @pl.when(t == 0)
    def _():
        prev_ref[...] = jnp.zeros_like(prev_ref)

    xb = _bf(x_ref[...])
    xb_ref[...] = xb
    p = _dot(xb, w_ref[...])
    rolled = pltpu.roll(p, 1, axis=0)
    row = lax.broadcasted_iota(jnp.int32, (p.shape[0], 1), 0)
    shifted = jnp.where(row == 0, prev_ref[0:1, :], rolled)
    prev_ref[0:1, :] = rolled[0:1, :]
    p = p + (shifted - p) * mu_ref[...]

    c0, c1, c2 = RWKV_DIM, 2 * RWKV_DIM, 3 * RWKV_DIM
    r_ref[...] = p[:, :c0].astype(r_ref.dtype)
    k_ref[...] = p[:, c0:c1].astype(k_ref.dtype)
    v_ref[...] = p[:, c1:c2].astype(v_ref.dtype)
    lora = p[:, c2:c2 + DECAY_LORA + AAA_LORA]
    lane = lax.broadcasted_iota(jnp.int32, (1, DECAY_LORA + AAA_LORA), 1)
    lora = jnp.where(lane < DECAY_LORA, jnp.tanh(lora), lora)
    wa = _dot(_bf(lora), wl_ref[...])
    ld_ref[...] = (-math.exp(-0.5)) * _sigmoid(w0_ref[...] + wa[:, :RWKV_DIM])
    a_ref[...] = _sigmoid(a0_ref[...] + wa[:, RWKV_DIM:]).astype(a_ref.dtype)
    dg = p[:, c2 + DECAY_LORA + AAA_LORA:]
    g_ref[...] = _dot(_bf(_sigmoid(dg)), g2_ref[...]).astype(g_ref.dtype)


def _rwkv_prep(x, layer, params, tm):
    bsz, t, _ = x.shape
    outs = [jax.ShapeDtypeStruct((bsz, t, D_MODEL), BF16)] + [
        jax.ShapeDtypeStruct((bsz, t, RWKV_DIM), dt) for dt in (BF16, BF16, BF16, F32, BF16, BF16)]
    tok = lambda b, i: (b, i, 0)
    o_spec = pl.BlockSpec((None, tm, RWKV_DIM), tok)
    return pl.pallas_call(
        _rwkv_prep_kernel,
        grid=(bsz, t // tm),
        in_specs=[pl.BlockSpec((None, tm, D_MODEL), tok)] + [_layer_block(a, layer) for a in params],
        out_specs=[pl.BlockSpec((None, tm, D_MODEL), tok)] + [o_spec] * 6,
        out_shape=outs,
        scratch_shapes=[pltpu.VMEM((8, RWKV_COLS), F32)],
        compiler_params=_params("parallel", "arbitrary"),
        name="rwkv_prep",
    )(x, *params)


def _rwkv_scan_kernel(r_ref, k_ref, v_ref, ld_ref, a_ref, g_ref,
                      kk_ref, ka_ref, rk_ref, gng_ref, gnb_ref, o_ref, s_ref):
    @pl.when(pl.program_id(1) == 0)
    def _():
        s_ref[...] = jnp.zeros_like(s_ref)

    seqs = [(b, slice(p * LANES, (p + 1) * LANES)) for b in range(r_ref.shape[0])
            for p in range(r_ref.shape[2] // LANES)]
    ins = [tuple(z[b, :, cols].astype(F32) for z in (r_ref, k_ref, v_ref, ld_ref, a_ref, g_ref))
           for b, cols in seqs]
    pars = [tuple(z[:, cols] for z in (kk_ref, ka_ref, rk_ref, gng_ref, gnb_ref)) for b, cols in seqs]
    outs, states = _rwkv_blocks(ins, pars, [s_ref[i] for i in range(len(seqs))])
    for i, (b, cols) in enumerate(seqs):
        o_ref[b, :, cols] = outs[i].astype(o_ref.dtype)
        s_ref[i] = states[i]


def _rwkv_blocks(ins, pars, states):
    rows, chunk = RWKV_BLOCK, RWKV_CHUNK
    n_chunks = rows // chunk
    nb = len(ins)
    inst = [(b, h) for b in range(nb) for h in range(HEAD_PAIR)]

    lane = lax.broadcasted_iota(jnp.int32, (1, LANES), 1)
    m0 = lane < RWKV_HEAD_DIM
    rin = lax.broadcasted_iota(jnp.int32, (rows, 1), 0) & (chunk - 1)

    pre = []
    for (r, k, v, ld, a, g), (k_k, k_a, _, _, _) in zip(ins, pars):
        kk = k * k_k
        kk = kk * lax.rsqrt(jnp.maximum(_pair_sum(kk * kk, m0), 1e-12))
        k2 = k * (1.0 + (a - 1.0) * k_a)
        bv = kk * a
        cl = ld
        sh = 1
        while sh < chunk:
            cl = cl + jnp.where(rin >= sh, pltpu.roll(cl, sh, axis=0), 0.0)
            sh *= 2
        w_inc = jnp.exp(cl)
        w_inv = jnp.exp(-cl)
        at = -kk * jnp.exp(cl - ld)
        rt = r * w_inc
        pre.append(dict(
            k2=k2, w_inc=w_inc, rt=rt, bt=_bf(bv * w_inv), kt=_bf(k2 * w_inv), v_bf=_bf(v),
            at_h=(_bf(jnp.where(m0, at, 0.0)), _bf(jnp.where(m0, 0.0, at))),
            rt_h=(_bf(jnp.where(m0, rt, 0.0)), _bf(jnp.where(m0, 0.0, rt)))))

    grams = []
    for d in pre:
        lhs = jnp.concatenate([d["at_h"][0], d["at_h"][1], d["rt_h"][0], d["rt_h"][1]], axis=0)
        rhs = jnp.concatenate([d["bt"], d["kt"]], axis=0)
        grams.append(_dot_nt(lhs, rhs))

    ri = lax.broadcasted_iota(jnp.int32, (rows, rows), 0)
    ci = lax.broadcasted_iota(jnp.int32, (rows, rows), 1)
    same = (ri // chunk) == (ci // chunk)
    strict = same & (ri > ci)
    incl = same & (ri >= ci)
    eye = jnp.where(ri == ci, 1.0, 0.0)

    xs = [jnp.where(strict, grams[b][h * rows:(h + 1) * rows, :rows], 0.0) for b, h in inst]
    t_inv = [eye + x for x in xs]
    n = 2
    while n < chunk:
        xb = [_bf(x) for x in xs]
        xs = [_dot(x, x) for x in xb]
        t_inv = [t + _dot(_bf(t), _bf(x)) for t, x in zip(t_inv, xs)]
        n *= 2
    t_bf = [_bf(t) for t in t_inv]

    akv = [_dot(_bf(jnp.where(strict, grams[b][h * rows:(h + 1) * rows, rows:], 0.0)), pre[b]["v_bf"])
           for b, h in inst]
    up_h = [_dot(t, jnp.concatenate([_bf(z), pre[b]["at_h"][h]], axis=1))
            for t, z, (b, h) in zip(t_bf, akv, inst)]
    u0_h = [z[:, :LANES] for z in up_h]
    p_h = [z[:, LANES:] for z in up_h]
    a_rb = [_bf(jnp.where(incl, grams[b][(2 + h) * rows:(3 + h) * rows, :rows], 0.0)) for b, h in inst]
    a_rk = [_bf(jnp.where(incl, grams[b][(2 + h) * rows:(3 + h) * rows, rows:], 0.0)) for b, h in inst]
    rb_up = [_dot(a_rb[i], _bf(up_h[i])) for i in range(len(inst))]
    y0_h = [rb_up[i][:, :LANES] + _dot(a_rk[i], pre[b]["v_bf"]) for i, (b, h) in enumerate(inst)]
    qp_h = [z[:, LANES:] for z in rb_up]

    bi = lax.broadcasted_iota(jnp.int32, (LANES, LANES), 0) // RWKV_HEAD_DIM
    bj = lax.broadcasted_iota(jnp.int32, (LANES, LANES), 1) // RWKV_HEAD_DIM
    blockdiag = bi == bj

    p_all, q_all, u0_all, y0_all = [], [], [], []
    for b in range(nb):
        i0, i1 = HEAD_PAIR * b, HEAD_PAIR * b + 1
        p_all.append(_bf(p_h[i0] + p_h[i1]))
        q_all.append(_bf(pre[b]["rt"] + qp_h[i0] + qp_h[i1]))
        u0_all.append(_bf(jnp.where(m0, u0_h[i0], u0_h[i1])))
        y0_all.append(jnp.where(m0, y0_h[i0], y0_h[i1]))

    m_c, c_c = {}, {}
    for c in range(n_chunks):
        lo, hi = c * chunk, (c + 1) * chunk
        for b in range(nb):
            bt_c, kt_c = pre[b]["bt"][lo:hi], pre[b]["kt"][lo:hi]
            m_c[b, c] = _bf(jnp.where(blockdiag, _dot_tn(p_all[b][lo:hi], bt_c), 0.0))
            c_c[b, c] = jnp.where(blockdiag, _dot_tn(
                jnp.concatenate([u0_all[b][lo:hi], pre[b]["v_bf"][lo:hi]], axis=0),
                jnp.concatenate([bt_c, kt_c], axis=0)), 0.0)

    ys = [[] for _ in range(nb)]
    states = list(states)
    for c in range(n_chunks):
        lo, hi = c * chunk, (c + 1) * chunk
        for b in range(nb):
            s_bf = _bf(states[b])
            ys[b].append(_dot_nt(q_all[b][lo:hi], s_bf) + y0_all[b][lo:hi])
            states[b] = (states[b] + _dot(s_bf, m_c[b, c]) + c_c[b, c]) * pre[b]["w_inc"][hi - 1:hi, :]

    outs = []
    inv_n = 1.0 / RWKV_HEAD_DIM
    for b, (r, k, v, ld, a, g) in enumerate(ins):
        _, _, r_k, gn_g, gn_b = pars[b]
        y = jnp.concatenate(ys[b], axis=0)
        mean = _pair_sum(y, m0) * inv_n
        d = y - mean
        var = _pair_sum(d * d, m0) * inv_n
        yn = d * lax.rsqrt(var + RWKV_GN_EPS) * gn_g + gn_b
        bonus = _pair_sum(r * pre[b]["k2"] * r_k, m0) * v
        outs.append((yn + bonus) * g)
    return outs, states


def _rwkv_scan(r, k, v, ld, a, g, layer, k_k, k_a, r_k, gn_g, gn_b):
    bsz, t, _ = r.shape
    width = RWKV_SCAN_PAIRS * LANES
    tok = pl.BlockSpec((bsz, RWKV_BLOCK, width), lambda h, c: (0, c, h))
    par = pl.BlockSpec((None, 1, width), lambda h, c: (layer, 0, h))
    return pl.pallas_call(
        _rwkv_scan_kernel,
        grid=(N_PAIRS // RWKV_SCAN_PAIRS, t // RWKV_BLOCK),
        in_specs=[tok] * 6 + [par] * 5,
        out_specs=tok,
        out_shape=jax.ShapeDtypeStruct((bsz, t, RWKV_DIM), BF16),
        scratch_shapes=[pltpu.VMEM((bsz * RWKV_SCAN_PAIRS, LANES, LANES), F32)],
        compiler_params=_params("parallel", "arbitrary"),
        name="rwkv_scan",
    )(r, k, v, ld, a, g, k_k, k_a, r_k, gn_g, gn_b)


def _rope_rotated(w):
    lo, mid, hi = QK_NOPE_DIM, QK_NOPE_DIM + QK_ROPE_DIM // 2, QK_NOPE_DIM + QK_ROPE_DIM
    return jnp.zeros_like(w).at[..., lo:mid].set(-w[..., mid:hi]).at[..., mid:hi].set(w[..., lo:mid])


def _mla_prep_kernel(x_ref, w_ref, qn_ref, qup_ref, kvn_ref, kvup_ref, cos_ref, sin_ref,
                     q_ref, k_ref, v_ref):
    nq, nkv, wq = Q_LORA_RANK, KV_LORA_RANK, MLA_HEADS * LANES
    p = _dot(x_ref[...], w_ref[...])
    c_q = p[:, :nq]
    c_kv = p[:, nq:nq + nkv]
    c_q = c_q * lax.rsqrt(jnp.mean(c_q * c_q, -1, keepdims=True) + RMS_EPS) * qn_ref[...]
    c_kv = c_kv * lax.rsqrt(jnp.mean(c_kv * c_kv, -1, keepdims=True) + RMS_EPS) * kvn_ref[...]
    q = _dot(_bf(c_q), qup_ref[...])
    kv = _dot(_bf(c_kv), kvup_ref[...])
    cos_t, sin_t = cos_ref[...], sin_ref[...]
    reps = MLA_HEADS
    q = q[:, :wq] * jnp.tile(cos_t, (1, reps)) + q[:, wq:] * jnp.tile(sin_t, (1, reps))
    q_ref[...] = (q * (ATTN_SCALE * LOG2_E)).astype(q_ref.dtype)
    kr = p[:, nq + nkv:nq + nkv + LANES] * cos_t + p[:, nq + nkv + LANES:] * sin_t
    k_ref[...] = (kv[:, :wq] + jnp.tile(kr, (1, reps))).astype(k_ref.dtype)
    v_ref[...] = kv[:, wq:].astype(v_ref.dtype)


def _mla_prep(x_bf, layer, params, cos_t, sin_t, tm):
    bsz, t, _ = x_bf.shape
    tok = lambda b, i: (b, i, 0)
    tab = pl.BlockSpec((tm, LANES), lambda b, i: (i, 0))
    wq = MLA_HEADS * LANES
    return pl.pallas_call(
        _mla_prep_kernel,
        grid=(bsz, t // tm),
        in_specs=[pl.BlockSpec((None, tm, D_MODEL), tok)] + [_layer_block(a, layer) for a in params] + [tab, tab],
        out_specs=[pl.BlockSpec((None, tm, wq), tok), pl.BlockSpec((None, tm, wq), tok),
                   pl.BlockSpec((None, tm, MLA_DIM), tok)],
        out_shape=[jax.ShapeDtypeStruct((bsz, t, wq), BF16), jax.ShapeDtypeStruct((bsz, t, wq), BF16),
                   jax.ShapeDtypeStruct((bsz, t, MLA_DIM), BF16)],
        compiler_params=_params("parallel", "parallel"),
        name="mla_prep",
    )(x_bf, *params, cos_t, sin_t)


def _attn_kernel(q_ref, k_ref, v_ref, o_ref, m_ref, l_ref, acc_ref, *, tq, tk):
    i = pl.program_id(2)
    lane = lax.broadcasted_iota(jnp.int32, (1, LANES), 1)
    m0 = lane < V_HEAD_DIM
    m_ref[...] = jnp.full_like(m_ref, NEG_BIG)
    l_ref[...] = jnp.zeros_like(l_ref)
    acc_ref[...] = jnp.zeros_like(acc_ref)

    def step(start, nk, row_lo, masked):
        rows = slice(row_lo, tq)
        nrow = tq - row_lo
        heads = range(q_ref.shape[1] // LANES)
        s_h = [_dot_nt(q_ref[rows, h * LANES:(h + 1) * LANES], k_ref[pl.ds(start, nk), h * LANES:(h + 1) * LANES])
               for h in heads]
        alphas, ps = [], []
        for h in heads:
            s = s_h[h]
            if masked:
                qi = lax.broadcasted_iota(jnp.int32, (nrow, nk), 0) + (i * tq + row_lo)
                ki = lax.broadcasted_iota(jnp.int32, (nrow, nk), 1) + start
                s = jnp.where(ki <= qi, s, NEG_BIG)
            m_old = m_ref[h, rows]
            m_new = jnp.maximum(m_old, jnp.max(s, -1, keepdims=True))
            alpha = jnp.exp2(m_old - m_new)
            p = jnp.exp2(s - jnp.tile(m_new, (1, nk // LANES)))
            p_sum = p[:, :LANES]
            for c in range(1, nk // LANES):
                p_sum = p_sum + p[:, c * LANES:(c + 1) * LANES]
            l_ref[h, rows] = alpha * l_ref[h, rows] + p_sum
            m_ref[h, rows] = m_new
            alphas.append(alpha)
            ps.append(_bf(p))
        for h in heads:
            pair = slice((h // HEAD_PAIR) * LANES, (h // HEAD_PAIR + 1) * LANES)
            acc_ref[h, rows] = alphas[h] * acc_ref[h, rows] + _dot(ps[h], v_ref[pl.ds(start, nk), pair])

    def body(j, carry):
        step(pl.multiple_of(j * tk, tk), tk, 0, False)
        return carry

    lax.fori_loop(0, i * (tq // tk), body, 0)
    half = tq // 2
    base = pl.multiple_of(i * tq, tq)
    step(base, half, 0, True)
    step(base + half, half, half, True)
    for p in range(o_ref.shape[1] // LANES):
        out = [acc_ref[h] / jnp.sum(l_ref[h], -1, keepdims=True) for h in (HEAD_PAIR * p, HEAD_PAIR * p + 1)]
        o_ref[:, p * LANES:(p + 1) * LANES] = jnp.where(m0, out[0], out[1]).astype(o_ref.dtype)


def _attention(q, k, v, tq, tk):
    bsz, t, _ = q.shape
    npair = ATTN_PAIRS
    heads = npair * HEAD_PAIR
    resident = lambda width: pl.BlockSpec((None, t, width), lambda b, h, i: (b, 0, h))
    return pl.pallas_call(
        functools.partial(_attn_kernel, tq=tq, tk=tk),
        grid=(bsz, N_PAIRS // npair, t // tq),
        in_specs=[
            pl.BlockSpec((None, tq, heads * LANES), lambda b, h, i: (b, i, h)),
            resident(heads * LANES),
            resident(npair * LANES),
        ],
        out_specs=pl.BlockSpec((None, tq, npair * LANES), lambda b, h, i: (b, i, h)),
        out_shape=jax.ShapeDtypeStruct((bsz, t, MLA_DIM), BF16),
        scratch_shapes=[pltpu.VMEM((heads, tq, LANES), F32)] * 3,
        compiler_params=_params("parallel", "parallel", "arbitrary"),
        name="mla_attention",
    )(q, k, v)


def _s5_disc_kernel(lre_ref, lim_ref, ls_ref, br_ref, bi_ref, lbr_ref, lbi_ref, bbr_ref, bbi_ref):
    lam_re = jnp.minimum(lre_ref[...], -1e-4)
    lam_im = lim_ref[...]
    step = jnp.exp(ls_ref[...])
    mag = jnp.exp(lam_re * step)
    ang = lam_im * step
    lb_re, lb_im = mag * jnp.cos(ang), mag * jnp.sin(ang)
    den = lam_re * lam_re + lam_im * lam_im
    n_re = lb_re - 1.0
    f_re = (n_re * lam_re + lb_im * lam_im) / den
    f_im = (lb_im * lam_re - n_re * lam_im) / den
    br, bi = br_ref[...], bi_ref[...]
    lbr_ref[...] = lb_re
    lbi_ref[...] = lb_im
    bbr_ref[...] = f_re * br - f_im * bi
    bbi_ref[...] = f_re * bi + f_im * br


def _s5_discretize(lambda_re, lambda_im, log_step, b_re, b_im):
    depth = lambda_re.shape[0]
    rep = lambda z: jnp.repeat(z, S5_GROUP, axis=1)
    ls = rep(jnp.broadcast_to(log_step[:, :, None], (depth, S5_GROUPS, S5_STATE)))
    tr = lambda z: jnp.transpose(z, (0, 1, 3, 2)).reshape(depth, S5_DIM, S5_STATE)
    spec = pl.BlockSpec((None, S5_DIM, S5_STATE), lambda l: (l, 0, 0))
    out = jax.ShapeDtypeStruct((depth, S5_DIM, S5_STATE), F32)
    return pl.pallas_call(
        _s5_disc_kernel, grid=(depth,), in_specs=[spec] * 5, out_specs=[spec] * 4, out_shape=[out] * 4,
        compiler_params=_params("parallel"), name="s5_discretize",
    )(rep(lambda_re), rep(lambda_im), ls, tr(b_re), tr(b_im))


def _split_bf(z):
    hi = _bf(z)
    return hi, _bf(z - hi.astype(F32))


def _s5_prep_kernel(b_ref, c_ref, ct_ref, lam_ref, wg_ref, kc_ref, kst_ref, lam8_ref):
    n = S5_SLAB_STATE
    b, c, ct, lam = b_ref[0], c_ref[0], ct_ref[0], lam_ref[0]
    ar, ai = lam[:, :n], lam[:, n:]
    br, bi = b[:, :n], b[:, n:]
    ctr, cti = ct[:, :n], ct[:, n:]
    c_hi, c_lo = _split_bf(c)
    pr, pi = jnp.ones_like(ar), jnp.zeros_like(ai)
    for j in range(S5_CHUNK):
        w = jnp.concatenate([br * pr - bi * pi, br * pi + bi * pr], axis=1)
        w_hi, w_lo = _split_bf(w)
        rows = slice(j * LANES, (j + 1) * LANES)
        wg_ref[0, rows, :] = w_hi
        kc_ref[0, rows, :] = _bf(_dot(w_hi, c_hi) + _dot(w_lo, c_hi) + _dot(w_hi, c_lo))
        pr, pi = pr * ar - pi * ai, pr * ai + pi * ar
        kst_ref[0, rows, :] = _bf(jnp.concatenate([ctr * pr + cti * pi, cti * pr - ctr * pi], axis=1))
    lam8_ref[0] = jnp.concatenate([pr, pi], axis=1)


def _s5_prep(bmat, cmat, lam):
    depth = bmat.shape[0]
    n2 = 2 * S5_SLAB_STATE
    kq = S5_CHUNK * LANES
    blk = lambda r, c: pl.BlockSpec((None, 1, r, c), lambda l, s: (l, s, 0, 0))
    shape = lambda r, c, dt: jax.ShapeDtypeStruct((depth, S5_SLABS, r, c), dt)
    return pl.pallas_call(
        _s5_prep_kernel,
        grid=(depth, S5_SLABS),
        in_specs=[blk(LANES, n2), blk(n2, LANES), blk(LANES, n2), blk(1, n2)],
        out_specs=[blk(kq, n2), blk(kq, LANES), blk(kq, n2), blk(1, n2)],
        out_shape=[shape(kq, n2, BF16), shape(kq, LANES, BF16), shape(kq, n2, BF16), shape(1, n2, F32)],
        compiler_params=_params("parallel", "parallel"),
        name="s5_prep",
    )(bmat, cmat, jnp.swapaxes(cmat, 2, 3), lam)


def _s5_kernel(x_ref, w_ref, wg_ref, kc_ref, kst_ref, lam8_ref, d_ref, glu_ref, o_ref,
               carry_ref, u_ref, ys_ref, *, ts):
    @pl.when(pl.program_id(1) == 0)
    def _():
        carry_ref[...] = jnp.zeros_like(carry_ref)

    n = S5_SLAB_STATE
    nc = ts // S5_CHUNK
    nb = x_ref.shape[0]
    inst = [(b, sl) for b in range(nb) for sl in range(S5_SLABS)]
    ids = range(len(inst))
    u = [_dot(x_ref[b], w_ref[...]) for b in range(nb)]
    for b, sl in inst:
        u_ref[b, sl] = u[b][:, sl * LANES:(sl + 1) * LANES]
    rin = lax.broadcasted_iota(jnp.int32, (ts, 1), 0) & (S5_CHUNK - 1)
    crow = lax.broadcasted_iota(jnp.int32, (nc, 1), 0)
    first = crow == 0
    x_end = [jnp.concatenate(
        [_bf(u_ref[b, sl, pl.ds(S5_CHUNK - 1 - j, nc, stride=S5_CHUNK), :]) for j in range(S5_CHUNK)], axis=1)
        for b, sl in inst]
    gain = [_dot(x_end[i], wg_ref[sl]) for i, (b, sl) in enumerate(inst)]
    lam8 = [lam8_ref[sl] for b, sl in inst]
    ar = [z[:, :n] for z in lam8]
    ai = [z[:, n:] for z in lam8]
    cr = [carry_ref[b, sl, 0:1, :n] for b, sl in inst]
    cim = [carry_ref[b, sl, 0:1, n:] for b, sl in inst]
    er = [gain[i][:, :n] + jnp.where(first, ar[i] * cr[i] - ai[i] * cim[i], 0.0) for i in ids]
    ei = [gain[i][:, n:] + jnp.where(first, ar[i] * cim[i] + ai[i] * cr[i], 0.0) for i in ids]
    sh = 1
    while sh < nc:
        keep = crow >= sh
        for i in ids:
            sr = jnp.where(keep, pltpu.roll(er[i], sh, axis=0), 0.0)
            si = jnp.where(keep, pltpu.roll(ei[i], sh, axis=0), 0.0)
            er[i], ei[i] = er[i] + ar[i] * sr - ai[i] * si, ei[i] + ar[i] * si + ai[i] * sr
            ar[i], ai[i] = ar[i] * ar[i] - ai[i] * ai[i], 2.0 * ar[i] * ai[i]
        sh *= 2
    for i, (b, sl) in enumerate(inst):
        carry_ref[b, sl, 0:1, :n] = er[i][nc - 1:nc, :]
        carry_ref[b, sl, 0:1, n:] = ei[i][nc - 1:nc, :]
    h0 = [_bf(jnp.concatenate([jnp.where(first, cr[i], pltpu.roll(er[i], 1, axis=0)),
                               jnp.where(first, cim[i], pltpu.roll(ei[i], 1, axis=0))], axis=1)) for i in ids]
    z = [_dot_nt(h0[i], kst_ref[sl]) for i, (b, sl) in enumerate(inst)]
    for i, (b, sl) in enumerate(inst):
        for k in range(S5_CHUNK):
            ys_ref[b, sl, pl.ds(k, nc, stride=S5_CHUNK), :] = z[i][:, k * LANES:(k + 1) * LANES]
    ys = []
    for b, sl in inst:
        us = u[b][:, sl * LANES:(sl + 1) * LANES]
        lagged = [_bf(us)] + [_bf(jnp.where(rin >= j, pltpu.roll(us, j, axis=0), 0.0))
                              for j in range(1, S5_CHUNK)]
        ys.append(_dot(jnp.concatenate(lagged, axis=1), kc_ref[sl]) + ys_ref[b, sl])
    for b in range(nb):
        y = jnp.concatenate(ys[b * S5_SLABS:(b + 1) * S5_SLABS], axis=1) + d_ref[...] * u[b]
        y = 0.5 * y * (1.0 + jnp.tanh(math.sqrt(2.0 / math.pi) * (y + 0.044715 * (y * y * y))))
        h = _dot(_bf(y), glu_ref[...])
        o_ref[b] = (h[:, :D_MODEL] * _sigmoid(h[:, D_MODEL:])).astype(o_ref.dtype)


def _s5(x_bf, layer, params, ts):
    bsz, t, _ = x_bf.shape
    nb = 1
    tok = lambda b, i: (b, i, 0)
    return pl.pallas_call(
        functools.partial(_s5_kernel, ts=ts),
        grid=(bsz // nb, t // ts),
        in_specs=[pl.BlockSpec((nb, ts, D_MODEL), tok)] + [_layer_block(a, layer, single=True) for a in params],
        out_specs=pl.BlockSpec((nb, ts, D_MODEL), tok),
        out_shape=jax.ShapeDtypeStruct((bsz, t, D_MODEL), BF16),
        scratch_shapes=[pltpu.VMEM((nb, S5_SLABS, 8, 2 * S5_SLAB_STATE), F32),
                        pltpu.VMEM((nb, S5_SLABS, ts, LANES), F32), pltpu.VMEM((nb, S5_SLABS, ts, LANES), F32)],
        compiler_params=_params("parallel", "arbitrary"),
        name="s5_scan",
    )(x_bf, *params)


def _s5_matrices(lb_re, lb_im, bb_re, bb_im, c_re, c_im):
    eye = jnp.eye(S5_SLAB_GROUPS, dtype=F32)
    depth = lb_re.shape[0]
    s, g, c, p = S5_SLABS, S5_SLAB_GROUPS, S5_GROUP, S5_STATE

    def b_blocks(bb):
        return jnp.einsum("lsgcp,gh->lsgchp", bb.reshape(depth, s, g, c, p), eye).reshape(depth, s, g * c, g * p)

    def c_blocks(cc):
        return jnp.einsum("lsgcp,gh->lsgphc", cc.reshape(depth, s, g, c, p), eye).reshape(depth, s, g * p, g * c)

    bmat = jnp.concatenate([b_blocks(bb_re), b_blocks(bb_im)], axis=3)
    cmat = jnp.concatenate([c_blocks(c_re), -c_blocks(c_im)], axis=2)
    row = lambda z: z[:, ::S5_GROUP].reshape(depth, s, 1, g * p)
    lam = jnp.concatenate([row(lb_re), row(lb_im)], axis=3)
    return bmat, cmat, lam


def _merge_kernel(x_ref, ya_ref, ob_ref, yc_ref, wg_ref, gb_ref, wra_ref, wmo_ref, wo_ref,
                  lg_ref, lb_ref, o_ref, *, alpha):
    gates = _sigmoid(_dot(_bf(x_ref[...]), wg_ref[...]) + gb_ref[...])
    y_a = _dot(ya_ref[...], wra_ref[...])
    y_b = _dot(ob_ref[...], wmo_ref[...])
    y_c = yc_ref[...].astype(F32)
    merged = (gates[:, :D_MODEL] * y_a + gates[:, D_MODEL:2 * D_MODEL] * y_b
              + gates[:, 2 * D_MODEL:] * y_c)
    y = alpha * x_ref[...] + _dot(_bf(merged), wo_ref[...])
    o_ref[...] = _layer_norm(y, lg_ref[...], lb_ref[...])


def _merge(x, ya, ob, yc, layer, params, alpha, tm):
    n = x.shape[0]
    tok = lambda i: (i, 0)
    return pl.pallas_call(
        functools.partial(_merge_kernel, alpha=alpha),
        grid=(n // tm,),
        in_specs=[
            pl.BlockSpec((tm, D_MODEL), tok),
            pl.BlockSpec((tm, RWKV_DIM), tok), pl.BlockSpec((tm, MLA_DIM), tok),
            pl.BlockSpec((tm, D_MODEL), tok),
        ] + [_layer_block(a, layer, single=True) for a in params],
        out_specs=pl.BlockSpec((tm, D_MODEL), tok),
        out_shape=jax.ShapeDtypeStruct((n, D_MODEL), F32),
        compiler_params=_params("parallel"),
        name="merge_ln",
    )(x, ya, ob, yc, *params)


def _ffn_kernel(x_ref, w1_ref, w3_ref, w2_ref, lg_ref, lb_ref, o_ref, *, alpha, tf):
    x = x_ref[...]
    xb = _bf(x)
    acc = alpha * x
    for j in range(w1_ref.shape[1] // tf):
        cols = slice(j * tf, (j + 1) * tf)
        h1 = _dot(xb, w1_ref[:, cols])
        h3 = _dot(xb, w3_ref[:, cols])
        acc = acc + _dot(_bf(h1 * _sigmoid(h1) * h3), w2_ref[cols, :])
    o_ref[...] = _layer_norm(acc, lg_ref[...], lb_ref[...])


def _ffn(x, layer, params, alpha, tm, tf):
    n = x.shape[0]
    tok = lambda i: (i, 0)
    return pl.pallas_call(
        functools.partial(_ffn_kernel, alpha=alpha, tf=tf),
        grid=(n // tm,),
        in_specs=[pl.BlockSpec((tm, D_MODEL), tok)] + [_layer_block(a, layer, single=True) for a in params],
        out_specs=pl.BlockSpec((tm, D_MODEL), tok),
        out_shape=jax.ShapeDtypeStruct((n, D_MODEL), F32),
        compiler_params=_params("parallel"),
        name="ffn_ln",
    )(x, *params)


def _rope_tables(t):
    pos = jnp.arange(t, dtype=F32)
    inv_freq = ROPE_THETA ** (-jnp.arange(0, QK_ROPE_DIM, 2, dtype=F32) / QK_ROPE_DIM)
    ang = pos[:, None] * inv_freq[None, :]
    cos, sin = jnp.cos(ang), jnp.sin(ang)
    ones = jnp.ones((t, QK_NOPE_DIM), F32)
    zeros = jnp.zeros((t, QK_NOPE_DIM), F32)
    tail1 = jnp.ones((t, LANES - QK_NOPE_DIM - QK_ROPE_DIM), F32)
    tail0 = jnp.zeros((t, LANES - QK_NOPE_DIM - QK_ROPE_DIM), F32)
    cos_t = jnp.concatenate([ones, cos, cos, tail1], axis=1)
    sin_t = jnp.concatenate([zeros, sin, sin, tail0], axis=1)
    return cos_t, sin_t


def _branch_weights(w_in, w2, a2, q_up, kv_up):
    depth = w_in.shape[0]
    o_mla = RWKV_COLS
    o_s5 = o_mla + MLA_COLS
    o_gate = o_s5 + S5_COLS
    nq = Q_LORA_RANK + KV_LORA_RANK
    zl = jnp.zeros((depth, DECAY_LORA, RWKV_DIM), F32)
    w_lora = jnp.concatenate([jnp.concatenate([w2, zl], axis=2), jnp.concatenate([zl, a2], axis=2)], axis=1)
    w_kr = jnp.zeros((depth, D_MODEL, LANES), F32).at[:, :, QK_NOPE_DIM:QK_NOPE_DIM + QK_ROPE_DIM].set(
        w_in[:, :, o_mla + nq:o_s5])
    w_mla = jnp.concatenate([w_in[:, :, o_mla:o_mla + nq], w_kr, _rope_rotated(w_kr)], axis=2)
    q_up = q_up.reshape(depth, Q_LORA_RANK, MLA_HEADS, QK_NOPE_DIM + QK_ROPE_DIM)
    q_up = jnp.pad(q_up, ((0, 0), (0, 0), (0, 0), (0, LANES - QK_NOPE_DIM - QK_ROPE_DIM)))
    q_up = jnp.concatenate([q_up.reshape(depth, Q_LORA_RANK, MLA_HEADS * LANES),
                            _rope_rotated(q_up).reshape(depth, Q_LORA_RANK, MLA_HEADS * LANES)], axis=2)
    kv_up = kv_up.reshape(depth, KV_LORA_RANK, MLA_HEADS, QK_NOPE_DIM + V_HEAD_DIM)
    k_up = jnp.pad(kv_up[..., :QK_NOPE_DIM], ((0, 0), (0, 0), (0, 0), (0, LANES - QK_NOPE_DIM)))
    kv_up = jnp.concatenate([k_up.reshape(depth, KV_LORA_RANK, MLA_HEADS * LANES),
                             kv_up[..., QK_NOPE_DIM:].reshape(depth, KV_LORA_RANK, MLA_DIM)], axis=2)
    return dict(rwkv=_bf(w_in[:, :, :o_mla]), lora=_bf(w_lora), mla=_bf(w_mla), q_up=_bf(q_up),
                kv_up=_bf(kv_up), s5=_bf(w_in[:, :, o_s5:o_gate]), gate=_bf(w_in[:, :, o_gate:]))


def _s5_params(lambda_re, lambda_im, log_step, b_re, b_im, c_re, c_im):
    lb_re, lb_im, bb_re, bb_im = _s5_discretize(lambda_re, lambda_im, log_step, b_re, b_im)
    rows = lambda z: z.reshape(-1, S5_DIM, S5_STATE)
    return _s5_prep(*_s5_matrices(lb_re, lb_im, bb_re, bb_im, rows(c_re), rows(c_im)))


def _rows(z):
    return z.reshape(z.shape[0], 1, -1).astype(F32)


ROW_TILE = dict(rwkv_prep=1024, mla_prep=512, attention=1024, s5=1024, merge=512, ffn=512)


def _tile(t, name):
    return min(t, ROW_TILE[name])


def kernel(x, w_in, rwkv_mu, rwkv_w0, rwkv_w2, rwkv_a0, rwkv_a2, rwkv_g2, rwkv_k_k, rwkv_k_a, rwkv_r_k,
           rwkv_gn_g, rwkv_gn_b, rwkv_out, mla_q_norm, mla_q_up, mla_kv_norm, mla_kv_up, mla_out,
           s5_lambda_re, s5_lambda_im, s5_log_step, s5_b_re, s5_b_im, s5_c_re, s5_c_im, s5_d, s5_glu,
           gate_b, w_out, ln1_g, ln1_b, ffn_w1, ffn_w3, ffn_w2, ln2_g, ln2_b):
    bsz, t, _ = x.shape
    depth = w_in.shape[0]
    alpha = (2.0 * depth) ** 0.25
    n = bsz * t
    cos_t, sin_t = _rope_tables(t)
    d_ff = ffn_w1.shape[2]
    tf = d_ff // 2 if (d_ff // 2) % LANES == 0 else d_ff

    ffn_w1, ffn_w3, ffn_w2, s5_glu, w_out, rwkv_out, mla_out = map(
        _to_bf16, (ffn_w1, ffn_w3, ffn_w2, s5_glu, w_out, rwkv_out, mla_out))

    w = _branch_weights(w_in, rwkv_w2, rwkv_a2, mla_q_up, mla_kv_up)
    wg, kc, kst, lam8 = _s5_params(s5_lambda_re, s5_lambda_im, s5_log_step, s5_b_re, s5_b_im, s5_c_re, s5_c_im)

    rwkv_par = (w["rwkv"], _rows(rwkv_mu), _rows(rwkv_w0), _rows(rwkv_a0), w["lora"], _bf(rwkv_g2))
    scan_par = tuple(_rows(z) for z in (rwkv_k_k, rwkv_k_a, rwkv_r_k, rwkv_gn_g, rwkv_gn_b))
    mla_par = (w["mla"], _rows(mla_q_norm), w["q_up"], _rows(mla_kv_norm), w["kv_up"])
    s5_par = (w["s5"], wg, kc, kst, lam8, _rows(s5_d), s5_glu)
    merge_par = (w["gate"], _rows(gate_b), rwkv_out, mla_out, w_out, _rows(ln1_g), _rows(ln1_b))
    ffn_par = (ffn_w1, ffn_w3, ffn_w2, _rows(ln2_g), _rows(ln2_b))

    for l in range(depth):
        x_bf, r, k, v, ld, a, g = _rwkv_prep(x, l, rwkv_par, _tile(t, "rwkv_prep"))
        ya = _rwkv_scan(r, k, v, ld, a, g, l, *scan_par)
        q, kx, vx = _mla_prep(x_bf, l, mla_par, cos_t, sin_t, _tile(t, "mla_prep"))
        ob = _attention(q, kx, vx, _tile(t, "attention"), min(t, ATTN_KEY_BLOCK))
        yc = _s5(x_bf, l, s5_par, _tile(t, "s5"))
        x1 = _merge(x.reshape(n, D_MODEL), ya.reshape(n, RWKV_DIM), ob.reshape(n, MLA_DIM),
                    yc.reshape(n, D_MODEL), l, merge_par, alpha, _tile(n, "merge"))
        x = _ffn(x1, l, ffn_par, alpha, _tile(n, "ffn"), tf).reshape(bsz, t, D_MODEL)
    return x
```

```python
import functools
import math

import jax
import jax.numpy as jnp
from jax import lax
from jax.experimental import pallas as pl
from jax.experimental.pallas import tpu as pltpu

F32 = jnp.float32
BF16 = jnp.bfloat16

D_MODEL = 1024
RWKV_HEADS = 8
RWKV_HEAD_DIM = 64
RWKV_DIM = RWKV_HEADS * RWKV_HEAD_DIM
DECAY_LORA = 64
AAA_LORA = 64
GATE_LORA = 128
RWKV_GN_EPS = 64e-5
MLA_HEADS = 8
QK_NOPE_DIM = 64
QK_ROPE_DIM = 32
V_HEAD_DIM = 64
Q_LORA_RANK = 256
KV_LORA_RANK = 128
MLA_DIM = MLA_HEADS * V_HEAD_DIM
ROPE_THETA = 10000.0
ATTN_SCALE = 1.0 / math.sqrt(QK_NOPE_DIM + QK_ROPE_DIM)
LOG2_E = math.log2(math.e)
S5_DIM = 512
S5_GROUP = 16
S5_GROUPS = S5_DIM // S5_GROUP
S5_STATE = 64
N_BRANCHES = 3
LN_EPS = 1e-5
RMS_EPS = 1e-6
RWKV_COLS = 3 * RWKV_DIM + DECAY_LORA + AAA_LORA + GATE_LORA
MLA_COLS = Q_LORA_RANK + KV_LORA_RANK + QK_ROPE_DIM
S5_COLS = S5_DIM

LANES = 128
HEAD_PAIR = LANES // RWKV_HEAD_DIM
N_PAIRS = RWKV_HEADS // HEAD_PAIR
S5_SLAB_GROUPS = LANES // S5_GROUP
S5_SLABS = S5_GROUPS // S5_SLAB_GROUPS
S5_SLAB_STATE = S5_SLAB_GROUPS * S5_STATE
S5_CHUNK = 8
VMEM_LIMIT = 48 * 1024 * 1024

RWKV_CHUNK = 64
RWKV_BLOCK = 256
ATTN_PAIRS = 1
ATTN_KEY_BLOCK = 1024
FFN_SLICE = 256
RWKV_SCAN_PAIRS = 4
NEG_BIG = -1e30
CAST_BLOCK_BYTES = 6 * 1024 * 1024


def _bf(x):
    return x.astype(BF16)


def _dot(a, b):
    return jnp.dot(a, b, preferred_element_type=F32)


def _dot_nt(a, b):
    return lax.dot_general(a, b, (((1,), (1,)), ((), ())), preferred_element_type=F32)


def _dot_tn(a, b):
    return lax.dot_general(a, b, (((0,), (0,)), ((), ())), preferred_element_type=F32)


def _sigmoid(x):
    return 1.0 / (1.0 + jnp.exp(-x))


def _params(*sem):
    return pltpu.CompilerParams(dimension_semantics=sem, vmem_limit_bytes=VMEM_LIMIT)


def _layer_block(a, layer, single=False):
    idx = (layer,) + (0,) * (a.ndim - 1)
    mode = dict(pipeline_mode=pl.Buffered(1)) if single else {}
    return pl.BlockSpec((None,) + a.shape[1:], lambda *_: idx, **mode)


def _cast_kernel(w_ref, o_ref):
    o_ref[...] = w_ref[...].astype(o_ref.dtype)


def _to_bf16(w):
    depth, rows, cols = w.shape
    cap = CAST_BLOCK_BYTES // (4 * cols)
    tr = max(d for d in range(8, rows + 1, 8) if rows % d == 0 and d <= max(cap, 8))
    spec = pl.BlockSpec((None, tr, cols), lambda l, i: (l, i, 0))
    return pl.pallas_call(
        _cast_kernel, grid=(depth, rows // tr), in_specs=[spec], out_specs=spec,
        out_shape=jax.ShapeDtypeStruct(w.shape, BF16),
        compiler_params=_params("parallel", "parallel"), name="cast_bf16",
    )(w)


def _layer_norm(y, g, b):
    mu = jnp.mean(y, -1, keepdims=True)
    d = y - mu
    var = jnp.mean(d * d, -1, keepdims=True)
    return d * lax.rsqrt(var + LN_EPS) * g + b


def _pair_sum(x, m0):
    s0 = jnp.sum(jnp.where(m0, x, 0.0), -1, keepdims=True)
    s1 = jnp.sum(jnp.where(m0, 0.0, x), -1, keepdims=True)
    return jnp.where(m0, s0, s1)


def _rwkv_prep_kernel(x_ref, w_ref, mu_ref, w0_ref, a0_ref, wl_ref, g2_ref,
                      xb_ref, r_ref, k_ref, v_ref, ld_ref, a_ref, g_ref, prev_ref):
    t = pl.program_id(1)

    @pl.when(t == 0)
    def _():
        prev_ref[...] = jnp.zeros_like(prev_ref)

    xb = _bf(x_ref[...])
    xb_ref[...] = xb
    p = _dot(xb, w_ref[...])
    rolled = pltpu.roll(p, 1, axis=0)
    row = lax.broadcasted_iota(jnp.int32, (p.shape[0], 1), 0)
    shifted = jnp.where(row == 0, prev_ref[0:1, :], rolled)
    prev_ref[0:1, :] = rolled[0:1, :]
    p = p + (shifted - p) * mu_ref[...]

    c0, c1, c2 = RWKV_DIM, 2 * RWKV_DIM, 3 * RWKV_DIM
    r_ref[...] = p[:, :c0].astype(r_ref.dtype)
    k_ref[...] = p[:, c0:c1].astype(k_ref.dtype)
    v_ref[...] = p[:, c1:c2].astype(v_ref.dtype)
    lora = p[:, c2:c2 + DECAY_LORA + AAA_LORA]
    lane = lax.broadcasted_iota(jnp.int32, (1, DECAY_LORA + AAA_LORA), 1)
    lora = jnp.where(lane < DECAY_LORA, jnp.tanh(lora), lora)
    wa = _dot(_bf(lora), wl_ref[...])
    ld_ref[...] = (-math.exp(-0.5)) * _sigmoid(w0_ref[...] + wa[:, :RWKV_DIM])
    a_ref[...] = _sigmoid(a0_ref[...] + wa[:, RWKV_DIM:]).astype(a_ref.dtype)
    dg = p[:, c2 + DECAY_LORA + AAA_LORA:]
    g_ref[...] = _dot(_bf(_sigmoid(dg)), g2_ref[...]).astype(g_ref.dtype)


def _rwkv_prep(x, layer, params, tm):
    bsz, t, _ = x.shape
    outs = [jax.ShapeDtypeStruct((bsz, t, D_MODEL), BF16)] + [
        jax.ShapeDtypeStruct((bsz, t, RWKV_DIM), dt) for dt in (BF16, BF16, BF16, F32, BF16, BF16)]
    tok = lambda b, i: (b, i, 0)
    o_spec = pl.BlockSpec((None, tm, RWKV_DIM), tok)
    return pl.pallas_call(
        _rwkv_prep_kernel,
        grid=(bsz, t // tm),
        in_specs=[pl.BlockSpec((None, tm, D_MODEL), tok)] + [_layer_block(a, layer) for a in params],
        out_specs=[pl.BlockSpec((None, tm, D_MODEL), tok)] + [o_spec] * 6,
        out_shape=outs,
        scratch_shapes=[pltpu.VMEM((8, RWKV_COLS), F32)],
        compiler_params=_params("parallel", "arbitrary"),
        name="rwkv_prep",
    )(x, *params)


def _rwkv_scan_kernel(r_ref, k_ref, v_ref, ld_ref, a_ref, g_ref,
                      kk_ref, ka_ref, rk_ref, gng_ref, gnb_ref, o_ref, s_ref):
    @pl.when(pl.program_id(1) == 0)
    def _():
        s_ref[...] = jnp.zeros_like(s_ref)

    seqs = [(b, slice(p * LANES, (p + 1) * LANES)) for b in range(r_ref.shape[0])
            for p in range(r_ref.shape[2] // LANES)]
    ins = [tuple(z[b, :, cols].astype(F32) for z in (r_ref, k_ref, v_ref, ld_ref, a_ref, g_ref))
           for b, cols in seqs]
    pars = [tuple(z[:, cols] for z in (kk_ref, ka_ref, rk_ref, gng_ref, gnb_ref)) for b, cols in seqs]
    outs, states = _rwkv_blocks(ins, pars, [s_ref[i] for i in range(len(seqs))])
    for i, (b, cols) in enumerate(seqs):
        o_ref[b, :, cols] = outs[i].astype(o_ref.dtype)
        s_ref[i] = states[i]


def _rwkv_blocks(ins, pars, states):
    rows, chunk = RWKV_BLOCK, RWKV_CHUNK
    n_chunks = rows // chunk
    nb = len(ins)
    inst = [(b, h) for b in range(nb) for h in range(HEAD_PAIR)]

    lane = lax.broadcasted_iota(jnp.int32, (1, LANES), 1)
    m0 = lane < RWKV_HEAD_DIM
    rin = lax.broadcasted_iota(jnp.int32, (rows, 1), 0) & (chunk - 1)

    pre = []
    for (r, k, v, ld, a, g), (k_k, k_a, _, _, _) in zip(ins, pars):
        kk = k * k_k
        kk = kk * lax.rsqrt(jnp.maximum(_pair_sum(kk * kk, m0), 1e-12))
        k2 = k * (1.0 + (a - 1.0) * k_a)
        bv = kk * a
        cl = ld
        sh = 1
        while sh < chunk:
            cl = cl + jnp.where(rin >= sh, pltpu.roll(cl, sh, axis=0), 0.0)
            sh *= 2
        w_inc = jnp.exp(cl)
        w_inv = jnp.exp(-cl)
        at = -kk * jnp.exp(cl - ld)
        rt = r * w_inc
        pre.append(dict(
            k2=k2, w_inc=w_inc, rt=rt, bt=_bf(bv * w_inv), kt=_bf(k2 * w_inv), v_bf=_bf(v),
            at_h=(_bf(jnp.where(m0, at, 0.0)), _bf(jnp.where(m0, 0.0, at))),
            rt_h=(_bf(jnp.where(m0, rt, 0.0)), _bf(jnp.where(m0, 0.0, rt)))))

    grams = []
    for d in pre:
        lhs = jnp.concatenate([d["at_h"][0], d["at_h"][1], d["rt_h"][0], d["rt_h"][1]], axis=0)
        rhs = jnp.concatenate([d["bt"], d["kt"]], axis=0)
        grams.append(_dot_nt(lhs, rhs))

    ri = lax.broadcasted_iota(jnp.int32, (rows, rows), 0)
    ci = lax.broadcasted_iota(jnp.int32, (rows, rows), 1)
    same = (ri // chunk) == (ci // chunk)
    strict = same & (ri > ci)
    incl = same & (ri >= ci)
    eye = jnp.where(ri == ci, 1.0, 0.0)

    xs = [jnp.where(strict, grams[b][h * rows:(h + 1) * rows, :rows], 0.0) for b, h in inst]
    t_inv = [eye + x for x in xs]
    n = 2
    while n < chunk:
        xb = [_bf(x) for x in xs]
        xs = [_dot(x, x) for x in xb]
        t_inv = [t + _dot(_bf(t), _bf(x)) for t, x in zip(t_inv, xs)]
        n *= 2
    t_bf = [_bf(t) for t in t_inv]

    akv = [_dot(_bf(jnp.where(strict, grams[b][h * rows:(h + 1) * rows, rows:], 0.0)), pre[b]["v_bf"])
           for b, h in inst]
    up_h = [_dot(t, jnp.concatenate([_bf(z), pre[b]["at_h"][h]], axis=1))
            for t, z, (b, h) in zip(t_bf, akv, inst)]
    u0_h = [z[:, :LANES] for z in up_h]
    p_h = [z[:, LANES:] for z in up_h]
    a_rb = [_bf(jnp.where(incl, grams[b][(2 + h) * rows:(3 + h) * rows, :rows], 0.0)) for b, h in inst]
    a_rk = [_bf(jnp.where(incl, grams[b][(2 + h) * rows:(3 + h) * rows, rows:], 0.0)) for b, h in inst]
    rb_up = [_dot(a_rb[i], _bf(up_h[i])) for i in range(len(inst))]
    y0_h = [rb_up[i][:, :LANES] + _dot(a_rk[i], pre[b]["v_bf"]) for i, (b, h) in enumerate(inst)]
    qp_h = [z[:, LANES:] for z in rb_up]

    bi = lax.broadcasted_iota(jnp.int32, (LANES, LANES), 0) // RWKV_HEAD_DIM
    bj = lax.broadcasted_iota(jnp.int32, (LANES, LANES), 1) // RWKV_HEAD_DIM
    blockdiag = bi == bj

    p_all, q_all, u0_all, y0_all = [], [], [], []
    for b in range(nb):
        i0, i1 = HEAD_PAIR * b, HEAD_PAIR * b + 1
        p_all.append(_bf(p_h[i0] + p_h[i1]))
        q_all.append(_bf(pre[b]["rt"] + qp_h[i0] + qp_h[i1]))
        u0_all.append(_bf(jnp.where(m0, u0_h[i0], u0_h[i1])))
        y0_all.append(jnp.where(m0, y0_h[i0], y0_h[i1]))

    m_c, c_c = {}, {}
    for c in range(n_chunks):
        lo, hi = c * chunk, (c + 1) * chunk
        for b in range(nb):
            bt_c, kt_c = pre[b]["bt"][lo:hi], pre[b]["kt"][lo:hi]
            m_c[b, c] = _bf(jnp.where(blockdiag, _dot_tn(p_all[b][lo:hi], bt_c), 0.0))
            c_c[b, c] = jnp.where(blockdiag, _dot_tn(
                jnp.concatenate([u0_all[b][lo:hi], pre[b]["v_bf"][lo:hi]], axis=0),
                jnp.concatenate([bt_c, kt_c], axis=0)), 0.0)

    ys = [[] for _ in range(nb)]
    states = list(states)
    for c in range(n_chunks):
        lo, hi = c * chunk, (c + 1) * chunk
        for b in range(nb):
            s_bf = _bf(states[b])
            ys[b].append(_dot_nt(q_all[b][lo:hi], s_bf) + y0_all[b][lo:hi])
            states[b] = (states[b] + _dot(s_bf, m_c[b, c]) + c_c[b, c]) * pre[b]["w_inc"][hi - 1:hi, :]

    outs = []
    inv_n = 1.0 / RWKV_HEAD_DIM
    for b, (r, k, v, ld, a, g) in enumerate(ins):
        _, _, r_k, gn_g, gn_b = pars[b]
        y = jnp.concatenate(ys[b], axis=0)
        mean = _pair_sum(y, m0) * inv_n
        d = y - mean
        var = _pair_sum(d * d, m0) * inv_n
        yn = d * lax.rsqrt(var + RWKV_GN_EPS) * gn_g + gn_b
        bonus = _pair_sum(r * pre[b]["k2"] * r_k, m0) * v
        outs.append((yn + bonus) * g)
    return outs, states


def _rwkv_scan(r, k, v, ld, a, g, layer, k_k, k_a, r_k, gn_g, gn_b):
    bsz, t, _ = r.shape
    width = RWKV_SCAN_PAIRS * LANES
    tok = pl.BlockSpec((bsz, RWKV_BLOCK, width), lambda h, c: (0, c, h))
    par = pl.BlockSpec((None, 1, width), lambda h, c: (layer, 0, h))
    return pl.pallas_call(
        _rwkv_scan_kernel,
        grid=(N_PAIRS // RWKV_SCAN_PAIRS, t // RWKV_BLOCK),
        in_specs=[tok] * 6 + [par] * 5,
        out_specs=tok,
        out_shape=jax.ShapeDtypeStruct((bsz, t, RWKV_DIM), BF16),
        scratch_shapes=[pltpu.VMEM((bsz * RWKV_SCAN_PAIRS, LANES, LANES), F32)],
        compiler_params=_params("parallel", "arbitrary"),
        name="rwkv_scan",
    )(r, k, v, ld, a, g, k_k, k_a, r_k, gn_g, gn_b)


def _rope_rotated(w):
    lo, mid, hi = QK_NOPE_DIM, QK_NOPE_DIM + QK_ROPE_DIM // 2, QK_NOPE_DIM + QK_ROPE_DIM
    return jnp.zeros_like(w).at[..., lo:mid].set(-w[..., mid:hi]).at[..., mid:hi].set(w[..., lo:mid])


def _mla_prep_kernel(x_ref, w_ref, qn_ref, qup_ref, kvn_ref, kvup_ref, cos_ref, sin_ref,
                     q_ref, k_ref, v_ref):
    nq, nkv, wq = Q_LORA_RANK, KV_LORA_RANK, MLA_HEADS * LANES
    p = _dot(x_ref[...], w_ref[...])
    c_q = p[:, :nq]
    c_kv = p[:, nq:nq + nkv]
    c_q = c_q * lax.rsqrt(jnp.mean(c_q * c_q, -1, keepdims=True) + RMS_EPS) * qn_ref[...]
    c_kv = c_kv * lax.rsqrt(jnp.mean(c_kv * c_kv, -1, keepdims=True) + RMS_EPS) * kvn_ref[...]
    q = _dot(_bf(c_q), qup_ref[...])
    kv = _dot(_bf(c_kv), kvup_ref[...])
    cos_t, sin_t = cos_ref[...], sin_ref[...]
    reps = MLA_HEADS
    q = q[:, :wq] * jnp.tile(cos_t, (1, reps)) + q[:, wq:] * jnp.tile(sin_t, (1, reps))
    q_ref[...] = (q * (ATTN_SCALE * LOG2_E)).astype(q_ref.dtype)
    kr = p[:, nq + nkv:nq + nkv + LANES] * cos_t + p[:, nq + nkv + LANES:] * sin_t
    k_ref[...] = (kv[:, :wq] + jnp.tile(kr, (1, reps))).astype(k_ref.dtype)
    v_ref[...] = kv[:, wq:].astype(v_ref.dtype)


def _mla_prep(x_bf, layer, params, cos_t, sin_t, tm):
    bsz, t, _ = x_bf.shape
    tok = lambda b, i: (b, i, 0)
    tab = pl.BlockSpec((tm, LANES), lambda b, i: (i, 0))
    wq = MLA_HEADS * LANES
    return pl.pallas_call(
        _mla_prep_kernel,
        grid=(bsz, t // tm),
        in_specs=[pl.BlockSpec((None, tm, D_MODEL), tok)] + [_layer_block(a, layer) for a in params] + [tab, tab],
        out_specs=[pl.BlockSpec((None, tm, wq), tok), pl.BlockSpec((None, tm, wq), tok),
                   pl.BlockSpec((None, tm, MLA_DIM), tok)],
        out_shape=[jax.ShapeDtypeStruct((bsz, t, wq), BF16), jax.ShapeDtypeStruct((bsz, t, wq), BF16),
                   jax.ShapeDtypeStruct((bsz, t, MLA_DIM), BF16)],
        compiler_params=_params("parallel", "parallel"),
        name="mla_prep",
    )(x_bf, *params, cos_t, sin_t)


def _attn_kernel(q_ref, k_ref, v_ref, o_ref, m_ref, l_ref, acc_ref, *, tq, tk):
    i = pl.program_id(2)
    lane = lax.broadcasted_iota(jnp.int32, (1, LANES), 1)
    m0 = lane < V_HEAD_DIM
    m_ref[...] = jnp.full_like(m_ref, NEG_BIG)
    l_ref[...] = jnp.zeros_like(l_ref)
    acc_ref[...] = jnp.zeros_like(acc_ref)

    def step(start, nk, row_lo, masked):
        rows = slice(row_lo, tq)
        nrow = tq - row_lo
        heads = range(q_ref.shape[1] // LANES)
        s_h = [_dot_nt(q_ref[rows, h * LANES:(h + 1) * LANES], k_ref[pl.ds(start, nk), h * LANES:(h + 1) * LANES])
               for h in heads]
        alphas, ps = [], []
        for h in heads:
            s = s_h[h]
            if masked:
                qi = lax.broadcasted_iota(jnp.int32, (nrow, nk), 0) + (i * tq + row_lo)
                ki = lax.broadcasted_iota(jnp.int32, (nrow, nk), 1) + start
                s = jnp.where(ki <= qi, s, NEG_BIG)
            m_old = m_ref[h, rows]
            m_new = jnp.maximum(m_old, jnp.max(s, -1, keepdims=True))
            alpha = jnp.exp2(m_old - m_new)
            p = jnp.exp2(s - jnp.tile(m_new, (1, nk // LANES)))
            p_sum = p[:, :LANES]
            for c in range(1, nk // LANES):
                p_sum = p_sum + p[:, c * LANES:(c + 1) * LANES]
            l_ref[h, rows] = alpha * l_ref[h, rows] + p_sum
            m_ref[h, rows] = m_new
            alphas.append(alpha)
            ps.append(_bf(p))
        for h in heads:
            pair = slice((h // HEAD_PAIR) * LANES, (h // HEAD_PAIR + 1) * LANES)
            acc_ref[h, rows] = alphas[h] * acc_ref[h, rows] + _dot(ps[h], v_ref[pl.ds(start, nk), pair])

    def body(j, carry):
        step(pl.multiple_of(j * tk, tk), tk, 0, False)
        return carry

    lax.fori_loop(0, i * (tq // tk), body, 0)
    half = tq // 2
    base = pl.multiple_of(i * tq, tq)
    step(base, half, 0, True)
    step(base + half, half, half, True)
    for p in range(o_ref.shape[1] // LANES):
        out = [acc_ref[h] / jnp.sum(l_ref[h], -1, keepdims=True) for h in (HEAD_PAIR * p, HEAD_PAIR * p + 1)]
        o_ref[:, p * LANES:(p + 1) * LANES] = jnp.where(m0, out[0], out[1]).astype(o_ref.dtype)


def _attention(q, k, v, tq, tk):
    bsz, t, _ = q.shape
    npair = ATTN_PAIRS
    heads = npair * HEAD_PAIR
    resident = lambda width: pl.BlockSpec((None, t, width), lambda b, h, i: (b, 0, h))
    return pl.pallas_call(
        functools.partial(_attn_kernel, tq=tq, tk=tk),
        grid=(bsz, N_PAIRS // npair, t // tq),
        in_specs=[
            pl.BlockSpec((None, tq, heads * LANES), lambda b, h, i: (b, i, h)),
            resident(heads * LANES),
            resident(npair * LANES),
        ],
        out_specs=pl.BlockSpec((None, tq, npair * LANES), lambda b, h, i: (b, i, h)),
        out_shape=jax.ShapeDtypeStruct((bsz, t, MLA_DIM), BF16),
        scratch_shapes=[pltpu.VMEM((heads, tq, LANES), F32)] * 3,
        compiler_params=_params("parallel", "parallel", "arbitrary"),
        name="mla_attention",
    )(q, k, v)


def _s5_disc_kernel(lre_ref, lim_ref, ls_ref, br_ref, bi_ref, lbr_ref, lbi_ref, bbr_ref, bbi_ref):
    lam_re = jnp.minimum(lre_ref[...], -1e-4)
    lam_im = lim_ref[...]
    step = jnp.exp(ls_ref[...])
    mag = jnp.exp(lam_re * step)
    ang = lam_im * step
    lb_re, lb_im = mag * jnp.cos(ang), mag * jnp.sin(ang)
    den = lam_re * lam_re + lam_im * lam_im
    n_re = lb_re - 1.0
    f_re = (n_re * lam_re + lb_im * lam_im) / den
    f_im = (lb_im * lam_re - n_re * lam_im) / den
    br, bi = br_ref[...], bi_ref[...]
    lbr_ref[...] = lb_re
    lbi_ref[...] = lb_im
    bbr_ref[...] = f_re * br - f_im * bi
    bbi_ref[...] = f_re * bi + f_im * br


def _s5_discretize(lambda_re, lambda_im, log_step, b_re, b_im):
    depth = lambda_re.shape[0]
    rep = lambda z: jnp.repeat(z, S5_GROUP, axis=1)
    ls = rep(jnp.broadcast_to(log_step[:, :, None], (depth, S5_GROUPS, S5_STATE)))
    tr = lambda z: jnp.transpose(z, (0, 1, 3, 2)).reshape(depth, S5_DIM, S5_STATE)
    spec = pl.BlockSpec((None, S5_DIM, S5_STATE), lambda l: (l, 0, 0))
    out = jax.ShapeDtypeStruct((depth, S5_DIM, S5_STATE), F32)
    return pl.pallas_call(
        _s5_disc_kernel, grid=(depth,), in_specs=[spec] * 5, out_specs=[spec] * 4, out_shape=[out] * 4,
        compiler_params=_params("parallel"), name="s5_discretize",
    )(rep(lambda_re), rep(lambda_im), ls, tr(b_re), tr(b_im))


def _split_bf(z):
    hi = _bf(z)
    return hi, _bf(z - hi.astype(F32))


def _s5_prep_kernel(b_ref, c_ref, ct_ref, lam_ref, wg_ref, kc_ref, kst_ref, lam8_ref):
    n = S5_SLAB_STATE
    b, c, ct, lam = b_ref[0], c_ref[0], ct_ref[0], lam_ref[0]
    ar, ai = lam[:, :n], lam[:, n:]
    br, bi = b[:, :n], b[:, n:]
    ctr, cti = ct[:, :n], ct[:, n:]
    c_hi, c_lo = _split_bf(c)
    pr, pi = jnp.ones_like(ar), jnp.zeros_like(ai)
    for j in range(S5_CHUNK):
        w = jnp.concatenate([br * pr - bi * pi, br * pi + bi * pr], axis=1)
        w_hi, w_lo = _split_bf(w)
        rows = slice(j * LANES, (j + 1) * LANES)
        wg_ref[0, rows, :] = w_hi
        kc_ref[0, rows, :] = _bf(_dot(w_hi, c_hi) + _dot(w_lo, c_hi) + _dot(w_hi, c_lo))
        pr, pi = pr * ar - pi * ai, pr * ai + pi * ar
        kst_ref[0, rows, :] = _bf(jnp.concatenate([ctr * pr + cti * pi, cti * pr - ctr * pi], axis=1))
    lam8_ref[0] = jnp.concatenate([pr, pi], axis=1)


def _s5_prep(bmat, cmat, lam):
    depth = bmat.shape[0]
    n2 = 2 * S5_SLAB_STATE
    kq = S5_CHUNK * LANES
    blk = lambda r, c: pl.BlockSpec((None, 1, r, c), lambda l, s: (l, s, 0, 0))
    shape = lambda r, c, dt: jax.ShapeDtypeStruct((depth, S5_SLABS, r, c), dt)
    return pl.pallas_call(
        _s5_prep_kernel,
        grid=(depth, S5_SLABS),
        in_specs=[blk(LANES, n2), blk(n2, LANES), blk(LANES, n2), blk(1, n2)],
        out_specs=[blk(kq, n2), blk(kq, LANES), blk(kq, n2), blk(1, n2)],
        out_shape=[shape(kq, n2, BF16), shape(kq, LANES, BF16), shape(kq, n2, BF16), shape(1, n2, F32)],
        compiler_params=_params("parallel", "parallel"),
        name="s5_prep",
    )(bmat, cmat, jnp.swapaxes(cmat, 2, 3), lam)


def _s5_kernel(x_ref, w_ref, wg_ref, kc_ref, kst_ref, lam8_ref, d_ref, glu_ref, o_ref,
               carry_ref, u_ref, ys_ref, *, ts):
    @pl.when(pl.program_id(1) == 0)
    def _():
        carry_ref[...] = jnp.zeros_like(carry_ref)

    n = S5_SLAB_STATE
    nc = ts // S5_CHUNK
    nb = x_ref.shape[0]
    inst = [(b, sl) for b in range(nb) for sl in range(S5_SLABS)]
    ids = range(len(inst))
    u = [_dot(x_ref[b], w_ref[...]) for b in range(nb)]
    for b, sl in inst:
        u_ref[b, sl] = u[b][:, sl * LANES:(sl + 1) * LANES]
    rin = lax.broadcasted_iota(jnp.int32, (ts, 1), 0) & (S5_CHUNK - 1)
    crow = lax.broadcasted_iota(jnp.int32, (nc, 1), 0)
    first = crow == 0
    x_end = [jnp.concatenate(
        [_bf(u_ref[b, sl, pl.ds(S5_CHUNK - 1 - j, nc, stride=S5_CHUNK), :]) for j in range(S5_CHUNK)], axis=1)
        for b, sl in inst]
    gain = [_dot(x_end[i], wg_ref[sl]) for i, (b, sl) in enumerate(inst)]
    lam8 = [lam8_ref[sl] for b, sl in inst]
    ar = [z[:, :n] for z in lam8]
    ai = [z[:, n:] for z in lam8]
    cr = [carry_ref[b, sl, 0:1, :n] for b, sl in inst]
    cim = [carry_ref[b, sl, 0:1, n:] for b, sl in inst]
    er = [gain[i][:, :n] + jnp.where(first, ar[i] * cr[i] - ai[i] * cim[i], 0.0) for i in ids]
    ei = [gain[i][:, n:] + jnp.where(first, ar[i] * cim[i] + ai[i] * cr[i], 0.0) for i in ids]
    sh = 1
    while sh < nc:
        keep = crow >= sh
        for i in ids:
            sr = jnp.where(keep, pltpu.roll(er[i], sh, axis=0), 0.0)
            si = jnp.where(keep, pltpu.roll(ei[i], sh, axis=0), 0.0)
            er[i], ei[i] = er[i] + ar[i] * sr - ai[i] * si, ei[i] + ar[i] * si + ai[i] * sr
            ar[i], ai[i] = ar[i] * ar[i] - ai[i] * ai[i], 2.0 * ar[i] * ai[i]
        sh *= 2
    for i, (b, sl) in enumerate(inst):
        carry_ref[b, sl, 0:1, :n] = er[i][nc - 1:nc, :]
        carry_ref[b, sl, 0:1, n:] = ei[i][nc - 1:nc, :]
    h0 = [_bf(jnp.concatenate([jnp.where(first, cr[i], pltpu.roll(er[i], 1, axis=0)),
                               jnp.where(first, cim[i], pltpu.roll(ei[i], 1, axis=0))], axis=1)) for i in ids]
    z = [_dot_nt(h0[i], kst_ref[sl]) for i, (b, sl) in enumerate(inst)]
    for i, (b, sl) in enumerate(inst):
        for k in range(S5_CHUNK):
            ys_ref[b, sl, pl.ds(k, nc, stride=S5_CHUNK), :] = z[i][:, k * LANES:(k + 1) * LANES]
    ys = []
    for b, sl in inst:
        us = u[b][:, sl * LANES:(sl + 1) * LANES]
        lagged = [_bf(us)] + [_bf(jnp.where(rin >= j, pltpu.roll(us, j, axis=0), 0.0))
                              for j in range(1, S5_CHUNK)]
        ys.append(_dot(jnp.concatenate(lagged, axis=1), kc_ref[sl]) + ys_ref[b, sl])
    for b in range(nb):
        y = jnp.concatenate(ys[b * S5_SLABS:(b + 1) * S5_SLABS], axis=1) + d_ref[...] * u[b]
        y = 0.5 * y * (1.0 + jnp.tanh(math.sqrt(2.0 / math.pi) * (y + 0.044715 * (y * y * y))))
        h = _dot(_bf(y), glu_ref[...])
        o_ref[b] = (h[:, :D_MODEL] * _sigmoid(h[:, D_MODEL:])).astype(o_ref.dtype)


def _s5(x_bf, layer, params, ts):
    bsz, t, _ = x_bf.shape
    nb = 1
    tok = lambda b, i: (b, i, 0)
    return pl.pallas_call(
        functools.partial(_s5_kernel, ts=ts),
        grid=(bsz // nb, t // ts),
        in_specs=[pl.BlockSpec((nb, ts, D_MODEL), tok)] + [_layer_block(a, layer, single=True) for a in params],
        out_specs=pl.BlockSpec((nb, ts, D_MODEL), tok),
        out_shape=jax.ShapeDtypeStruct((bsz, t, D_MODEL), BF16),
        scratch_shapes=[pltpu.VMEM((nb, S5_SLABS, 8, 2 * S5_SLAB_STATE), F32),
                        pltpu.VMEM((nb, S5_SLABS, ts, LANES), F32), pltpu.VMEM((nb, S5_SLABS, ts, LANES), F32)],
        compiler_params=_params("parallel", "arbitrary"),
        name="s5_scan",
    )(x_bf, *params)


def _s5_matrices(lb_re, lb_im, bb_re, bb_im, c_re, c_im):
    eye = jnp.eye(S5_SLAB_GROUPS, dtype=F32)
    depth = lb_re.shape[0]
    s, g, c, p = S5_SLABS, S5_SLAB_GROUPS, S5_GROUP, S5_STATE

    def b_blocks(bb):
        return jnp.einsum("lsgcp,gh->lsgchp", bb.reshape(depth, s, g, c, p), eye).reshape(depth, s, g * c, g * p)

    def c_blocks(cc):
        return jnp.einsum("lsgcp,gh->lsgphc", cc.reshape(depth, s, g, c, p), eye).reshape(depth, s, g * p, g * c)

    bmat = jnp.concatenate([b_blocks(bb_re), b_blocks(bb_im)], axis=3)
    cmat = jnp.concatenate([c_blocks(c_re), -c_blocks(c_im)], axis=2)
    row = lambda z: z[:, ::S5_GROUP].reshape(depth, s, 1, g * p)
    lam = jnp.concatenate([row(lb_re), row(lb_im)], axis=3)
    return bmat, cmat, lam


def _merge_kernel(x_ref, ya_ref, ob_ref, yc_ref, wg_ref, gb_ref, wra_ref, wmo_ref, wo_ref,
                  lg_ref, lb_ref, o_ref, *, alpha):
    gates = _sigmoid(_dot(_bf(x_ref[...]), wg_ref[...]) + gb_ref[...])
    y_a = _dot(ya_ref[...], wra_ref[...])
    y_b = _dot(ob_ref[...], wmo_ref[...])
    y_c = yc_ref[...].astype(F32)
    merged = (gates[:, :D_MODEL] * y_a + gates[:, D_MODEL:2 * D_MODEL] * y_b
              + gates[:, 2 * D_MODEL:] * y_c)
    y = alpha * x_ref[...] + _dot(_bf(merged), wo_ref[...])
    o_ref[...] = _layer_norm(y, lg_ref[...], lb_ref[...])


def _merge(x, ya, ob, yc, layer, params, alpha, tm):
    n = x.shape[0]
    tok = lambda i: (i, 0)
    return pl.pallas_call(
        functools.partial(_merge_kernel, alpha=alpha),
        grid=(n // tm,),
        in_specs=[
            pl.BlockSpec((tm, D_MODEL), tok),
            pl.BlockSpec((tm, RWKV_DIM), tok), pl.BlockSpec((tm, MLA_DIM), tok),
            pl.BlockSpec((tm, D_MODEL), tok),
        ] + [_layer_block(a, layer, single=True) for a in params],
        out_specs=pl.BlockSpec((tm, D_MODEL), tok),
        out_shape=jax.ShapeDtypeStruct((n, D_MODEL), F32),
        compiler_params=_params("parallel"),
        name="merge_ln",
    )(x, ya, ob, yc, *params)


def _ffn_kernel(x_ref, w1_ref, w3_ref, w2_ref, lg_ref, lb_ref, o_ref, *, alpha, tf):
    x = x_ref[...]
    xb = _bf(x)
    acc = alpha * x
    for j in range(w1_ref.shape[1] // tf):
        cols = slice(j * tf, (j + 1) * tf)
        h1 = _dot(xb, w1_ref[:, cols])
        h3 = _dot(xb, w3_ref[:, cols])
        acc = acc + _dot(_bf(h1 * _sigmoid(h1) * h3), w2_ref[cols, :])
    o_ref[...] = _layer_norm(acc, lg_ref[...], lb_ref[...])


def _ffn(x, layer, params, alpha, tm, tf):
    n = x.shape[0]
    tok = lambda i: (i, 0)
    return pl.pallas_call(
        functools.partial(_ffn_kernel, alpha=alpha, tf=tf),
        grid=(n // tm,),
        in_specs=[pl.BlockSpec((tm, D_MODEL), tok)] + [_layer_block(a, layer, single=True) for a in params],
        out_specs=pl.BlockSpec((tm, D_MODEL), tok),
        out_shape=jax.ShapeDtypeStruct((n, D_MODEL), F32),
        compiler_params=_params("parallel"),
        name="ffn_ln",
    )(x, *params)


def _rope_tables(t):
    pos = jnp.arange(t, dtype=F32)
    inv_freq = ROPE_THETA ** (-jnp.arange(0, QK_ROPE_DIM, 2, dtype=F32) / QK_ROPE_DIM)
    ang = pos[:, None] * inv_freq[None, :]
    cos, sin = jnp.cos(ang), jnp.sin(ang)
    ones = jnp.ones((t, QK_NOPE_DIM), F32)
    zeros = jnp.zeros((t, QK_NOPE_DIM), F32)
    tail1 = jnp.ones((t, LANES - QK_NOPE_DIM - QK_ROPE_DIM), F32)
    tail0 = jnp.zeros((t, LANES - QK_NOPE_DIM - QK_ROPE_DIM), F32)
    cos_t = jnp.concatenate([ones, cos, cos, tail1], axis=1)
    sin_t = jnp.concatenate([zeros, sin, sin, tail0], axis=1)
    return cos_t, sin_t


def _branch_weights(w_in, w2, a2, q_up, kv_up):
    depth = w_in.shape[0]
    o_mla = RWKV_COLS
    o_s5 = o_mla + MLA_COLS
    o_gate = o_s5 + S5_COLS
    nq = Q_LORA_RANK + KV_LORA_RANK
    zl = jnp.zeros((depth, DECAY_LORA, RWKV_DIM), F32)
    w_lora = jnp.concatenate([jnp.concatenate([w2, zl], axis=2), jnp.concatenate([zl, a2], axis=2)], axis=1)
    w_kr = jnp.zeros((depth, D_MODEL, LANES), F32).at[:, :, QK_NOPE_DIM:QK_NOPE_DIM + QK_ROPE_DIM].set(
        w_in[:, :, o_mla + nq:o_s5])
    w_mla = jnp.concatenate([w_in[:, :, o_mla:o_mla + nq], w_kr, _rope_rotated(w_kr)], axis=2)
    q_up = q_up.reshape(depth, Q_LORA_RANK, MLA_HEADS, QK_NOPE_DIM + QK_ROPE_DIM)
    q_up = jnp.pad(q_up, ((0, 0), (0, 0), (0, 0), (0, LANES - QK_NOPE_DIM - QK_ROPE_DIM)))
    q_up = jnp.concatenate([q_up.reshape(depth, Q_LORA_RANK, MLA_HEADS * LANES),
                            _rope_rotated(q_up).reshape(depth, Q_LORA_RANK, MLA_HEADS * LANES)], axis=2)
    kv_up = kv_up.reshape(depth, KV_LORA_RANK, MLA_HEADS, QK_NOPE_DIM + V_HEAD_DIM)
    k_up = jnp.pad(kv_up[..., :QK_NOPE_DIM], ((0, 0), (0, 0), (0, 0), (0, LANES - QK_NOPE_DIM)))
    kv_up = jnp.concatenate([k_up.reshape(depth, KV_LORA_RANK, MLA_HEADS * LANES),
                             kv_up[..., QK_NOPE_DIM:].reshape(depth, KV_LORA_RANK, MLA_DIM)], axis=2)
    return dict(rwkv=_bf(w_in[:, :, :o_mla]), lora=_bf(w_lora), mla=_bf(w_mla), q_up=_bf(q_up),
                kv_up=_bf(kv_up), s5=_bf(w_in[:, :, o_s5:o_gate]), gate=_bf(w_in[:, :, o_gate:]))


def _s5_params(lambda_re, lambda_im, log_step, b_re, b_im, c_re, c_im):
    lb_re, lb_im, bb_re, bb_im = _s5_discretize(lambda_re, lambda_im, log_step, b_re, b_im)
    rows = lambda z: z.reshape(-1, S5_DIM, S5_STATE)
    return _s5_prep(*_s5_matrices(lb_re, lb_im, bb_re, bb_im, rows(c_re), rows(c_im)))


def _rows(z):
    return z.reshape(z.shape[0], 1, -1).astype(F32)


ROW_TILE = dict(rwkv_prep=1024, mla_prep=512, attention=1024, s5=1024, merge=512, ffn=1024)


def _tile(t, name):
    return min(t, ROW_TILE[name])


def kernel(x, w_in, rwkv_mu, rwkv_w0, rwkv_w2, rwkv_a0, rwkv_a2, rwkv_g2, rwkv_k_k, rwkv_k_a, rwkv_r_k,
           rwkv_gn_g, rwkv_gn_b, rwkv_out, mla_q_norm, mla_q_up, mla_kv_norm, mla_kv_up, mla_out,
           s5_lambda_re, s5_lambda_im, s5_log_step, s5_b_re, s5_b_im, s5_c_re, s5_c_im, s5_d, s5_glu,
           gate_b, w_out, ln1_g, ln1_b, ffn_w1, ffn_w3, ffn_w2, ln2_g, ln2_b):
    bsz, t, _ = x.shape
    depth = w_in.shape[0]
    alpha = (2.0 * depth) ** 0.25
    n = bsz * t
    cos_t, sin_t = _rope_tables(t)

    ffn_w1, ffn_w3, ffn_w2, s5_glu, w_out, rwkv_out, mla_out = map(
        _to_bf16, (ffn_w1, ffn_w3, ffn_w2, s5_glu, w_out, rwkv_out, mla_out))

    w = _branch_weights(w_in, rwkv_w2, rwkv_a2, mla_q_up, mla_kv_up)
    wg, kc, kst, lam8 = _s5_params(s5_lambda_re, s5_lambda_im, s5_log_step, s5_b_re, s5_b_im, s5_c_re, s5_c_im)

    rwkv_par = (w["rwkv"], _rows(rwkv_mu), _rows(rwkv_w0), _rows(rwkv_a0), w["lora"], _bf(rwkv_g2))
    scan_par = tuple(_rows(z) for z in (rwkv_k_k, rwkv_k_a, rwkv_r_k, rwkv_gn_g, rwkv_gn_b))
    mla_par = (w["mla"], _rows(mla_q_norm), w["q_up"], _rows(mla_kv_norm), w["kv_up"])
    s5_par = (w["s5"], wg, kc, kst, lam8, _rows(s5_d), s5_glu)
    merge_par = (w["gate"], _rows(gate_b), rwkv_out, mla_out, w_out, _rows(ln1_g), _rows(ln1_b))
    ffn_par = (ffn_w1, ffn_w3, ffn_w2, _rows(ln2_g), _rows(ln2_b))

    for l in range(depth):
        x_bf, r, k, v, ld, a, g = _rwkv_prep(x, l, rwkv_par, _tile(t, "rwkv_prep"))
        ya = _rwkv_scan(r, k, v, ld, a, g, l, *scan_par)
        q, kx, vx = _mla_prep(x_bf, l, mla_par, cos_t, sin_t, _tile(t, "mla_prep"))
        ob = _attention(q, kx, vx, _tile(t, "attention"), min(t, ATTN_KEY_BLOCK))
        yc = _s5(x_bf, l, s5_par, _tile(t, "s5"))
        x1 = _merge(x.reshape(n, D_MODEL), ya.reshape(n, RWKV_DIM), ob.reshape(n, MLA_DIM),
                    yc.reshape(n, D_MODEL), l, merge_par, alpha, _tile(n, "merge"))
        x = _ffn(x1, l, ffn_par, alpha, _tile(n, "ffn"), FFN_SLICE).reshape(bsz, t, D_MODEL)
    return x
```

```python
import functools
import math

import jax
import jax.numpy as jnp
from jax import lax
from jax.experimental import pallas as pl
from jax.experimental.pallas import tpu as pltpu

F32 = jnp.float32
BF16 = jnp.bfloat16

D_MODEL = 1024
RWKV_HEADS = 8
RWKV_HEAD_DIM = 64
RWKV_DIM = RWKV_HEADS * RWKV_HEAD_DIM
DECAY_LORA = 64
AAA_LORA = 64
GATE_LORA = 128
RWKV_GN_EPS = 64e-5
MLA_HEADS = 8
QK_NOPE_DIM = 64
QK_ROPE_DIM = 32
V_HEAD_DIM = 64
Q_LORA_RANK = 256
KV_LORA_RANK = 128
MLA_DIM = MLA_HEADS * V_HEAD_DIM
ROPE_THETA = 10000.0
ATTN_SCALE = 1.0 / math.sqrt(QK_NOPE_DIM + QK_ROPE_DIM)
LOG2_E = math.log2(math.e)
S5_DIM = 512
S5_GROUP = 16
S5_GROUPS = S5_DIM // S5_GROUP
S5_STATE = 64
N_BRANCHES = 3
LN_EPS = 1e-5
RMS_EPS = 1e-6
RWKV_COLS = 3 * RWKV_DIM + DECAY_LORA + AAA_LORA + GATE_LORA
MLA_COLS = Q_LORA_RANK + KV_LORA_RANK + QK_ROPE_DIM
S5_COLS = S5_DIM

LANES = 128
HEAD_PAIR = LANES // RWKV_HEAD_DIM
N_PAIRS = RWKV_HEADS // HEAD_PAIR
S5_SLAB_GROUPS = LANES // S5_GROUP
S5_SLABS = S5_GROUPS // S5_SLAB_GROUPS
S5_SLAB_STATE = S5_SLAB_GROUPS * S5_STATE
S5_CHUNK = 8
VMEM_LIMIT = 48 * 1024 * 1024

RWKV_CHUNK = 64
RWKV_BLOCK = 256
ATTN_PAIRS = 1
ATTN_KEY_BLOCK = 1024
FFN_SLICE = 256
RWKV_SCAN_PAIRS = 4
NEG_BIG = -1e30
CAST_BLOCK_BYTES = 6 * 1024 * 1024


def _bf(x):
    return x.astype(BF16)


def _dot(a, b):
    return jnp.dot(a, b, preferred_element_type=F32)


def _dot_nt(a, b):
    return lax.dot_general(a, b, (((1,), (1,)), ((), ())), preferred_element_type=F32)


def _dot_tn(a, b):
    return lax.dot_general(a, b, (((0,), (0,)), ((), ())), preferred_element_type=F32)


def _sigmoid(x):
    return 1.0 / (1.0 + jnp.exp(-x))


def _params(*sem):
    return pltpu.CompilerParams(dimension_semantics=sem, vmem_limit_bytes=VMEM_LIMIT)


def _layer_block(a, layer, single=False):
    idx = (layer,) + (0,) * (a.ndim - 1)
    mode = dict(pipeline_mode=pl.Buffered(1)) if single else {}
    return pl.BlockSpec((None,) + a.shape[1:], lambda *_: idx, **mode)


def _cast_kernel(w_ref, o_ref):
    o_ref[...] = w_ref[...].astype(o_ref.dtype)


def _to_bf16(w):
    depth, rows, cols = w.shape
    cap = CAST_BLOCK_BYTES // (4 * cols)
    tr = max(d for d in range(8, rows + 1, 8) if rows % d == 0 and d <= max(cap, 8))
    spec = pl.BlockSpec((None, tr, cols), lambda l, i: (l, i, 0))
    return pl.pallas_call(
        _cast_kernel, grid=(depth, rows // tr), in_specs=[spec], out_specs=spec,
        out_shape=jax.ShapeDtypeStruct(w.shape, BF16),
        compiler_params=_params("parallel", "parallel"), name="cast_bf16",
    )(w)


def _layer_norm(y, g, b):
    mu = jnp.mean(y, -1, keepdims=True)
    d = y - mu
    var = jnp.mean(d * d, -1, keepdims=True)
    return d * lax.rsqrt(var + LN_EPS) * g + b


def _pair_sum(x, m0):
    s0 = jnp.sum(jnp.where(m0, x, 0.0), -1, keepdims=True)
    s1 = jnp.sum(jnp.where(m0, 0.0, x), -1, keepdims=True)
    return jnp.where(m0, s0, s1)


def _rwkv_prep_kernel(x_ref, w_ref, mu_ref, w0_ref, a0_ref, wl_ref, g2_ref,
                      xb_ref, r_ref, k_ref, v_ref, ld_ref, a_ref, g_ref, prev_ref):
    @pl.when(pl.program_id(0) == 0)
    def _():
        prev_ref[...] = jnp.zeros_like(prev_ref)

    nb = x_ref.shape[0]
    xb = [_bf(x_ref[b]) for b in range(nb)]
    ps = [_dot(z, w_ref[...]) for z in xb]
    row = lax.broadcasted_iota(jnp.int32, (ps[0].shape[0], 1), 0)
    c0, c1, c2 = RWKV_DIM, 2 * RWKV_DIM, 3 * RWKV_DIM
    lane = lax.broadcasted_iota(jnp.int32, (1, DECAY_LORA + AAA_LORA), 1)
    for b in range(nb):
        xb_ref[b] = xb[b]
        p = ps[b]
        rolled = pltpu.roll(p, 1, axis=0)
        shifted = jnp.where(row == 0, prev_ref[b, 0:1, :], rolled)
        prev_ref[b, 0:1, :] = rolled[0:1, :]
        p = p + (shifted - p) * mu_ref[...]
        r_ref[b] = p[:, :c0].astype(r_ref.dtype)
        k_ref[b] = p[:, c0:c1].astype(k_ref.dtype)
        v_ref[b] = p[:, c1:c2].astype(v_ref.dtype)
        lora = p[:, c2:c2 + DECAY_LORA + AAA_LORA]
        lora = jnp.where(lane < DECAY_LORA, jnp.tanh(lora), lora)
        wa = _dot(_bf(lora), wl_ref[...])
        ld_ref[b] = (-math.exp(-0.5)) * _sigmoid(w0_ref[...] + wa[:, :RWKV_DIM])
        a_ref[b] = _sigmoid(a0_ref[...] + wa[:, RWKV_DIM:]).astype(a_ref.dtype)
        dg = p[:, c2 + DECAY_LORA + AAA_LORA:]
        g_ref[b] = _dot(_bf(_sigmoid(dg)), g2_ref[...]).astype(g_ref.dtype)


def _rwkv_prep(x, layer, params, tm):
    bsz, t, _ = x.shape
    outs = [jax.ShapeDtypeStruct((bsz, t, D_MODEL), BF16)] + [
        jax.ShapeDtypeStruct((bsz, t, RWKV_DIM), dt) for dt in (BF16, BF16, BF16, F32, BF16, BF16)]
    tok = lambda i: (0, i, 0)
    o_spec = pl.BlockSpec((bsz, tm, RWKV_DIM), tok)
    return pl.pallas_call(
        _rwkv_prep_kernel,
        grid=(t // tm,),
        in_specs=[pl.BlockSpec((bsz, tm, D_MODEL), tok)] + [_layer_block(a, layer, single=True) for a in params],
        out_specs=[pl.BlockSpec((bsz, tm, D_MODEL), tok)] + [o_spec] * 6,
        out_shape=outs,
        scratch_shapes=[pltpu.VMEM((bsz, 8, RWKV_COLS), F32)],
        compiler_params=_params("arbitrary"),
        name="rwkv_prep",
    )(x, *params)


def _rwkv_scan_kernel(r_ref, k_ref, v_ref, ld_ref, a_ref, g_ref,
                      kk_ref, ka_ref, rk_ref, gng_ref, gnb_ref, o_ref, s_ref):
    @pl.when(pl.program_id(1) == 0)
    def _():
        s_ref[...] = jnp.zeros_like(s_ref)

    seqs = [(b, slice(p * LANES, (p + 1) * LANES)) for b in range(r_ref.shape[0])
            for p in range(r_ref.shape[2] // LANES)]
    ins = [tuple(z[b, :, cols].astype(F32) for z in (r_ref, k_ref, v_ref, ld_ref, a_ref, g_ref))
           for b, cols in seqs]
    pars = [tuple(z[:, cols] for z in (kk_ref, ka_ref, rk_ref, gng_ref, gnb_ref)) for b, cols in seqs]
    outs, states = _rwkv_blocks(ins, pars, [s_ref[i] for i in range(len(seqs))])
    for i, (b, cols) in enumerate(seqs):
        o_ref[b, :, cols] = outs[i].astype(o_ref.dtype)
        s_ref[i] = states[i]


def _rwkv_blocks(ins, pars, states):
    rows, chunk = RWKV_BLOCK, RWKV_CHUNK
    n_chunks = rows // chunk
    nb = len(ins)
    inst = [(b, h) for b in range(nb) for h in range(HEAD_PAIR)]

    lane = lax.broadcasted_iota(jnp.int32, (1, LANES), 1)
    m0 = lane < RWKV_HEAD_DIM
    rin = lax.broadcasted_iota(jnp.int32, (rows, 1), 0) & (chunk - 1)

    pre = []
    for (r, k, v, ld, a, g), (k_k, k_a, _, _, _) in zip(ins, pars):
        kk = k * k_k
        kk = kk * lax.rsqrt(jnp.maximum(_pair_sum(kk * kk, m0), 1e-12))
        k2 = k * (1.0 + (a - 1.0) * k_a)
        bv = kk * a
        cl = ld
        sh = 1
        while sh < chunk:
            cl = cl + jnp.where(rin >= sh, pltpu.roll(cl, sh, axis=0), 0.0)
            sh *= 2
        w_inc = jnp.exp(cl)
        w_inv = jnp.exp(-cl)
        at = -kk * jnp.exp(cl - ld)
        rt = r * w_inc
        pre.append(dict(
            k2=k2, w_inc=w_inc, rt=rt, bt=_bf(bv * w_inv), kt=_bf(k2 * w_inv), v_bf=_bf(v),
            at_h=(_bf(jnp.where(m0, at, 0.0)), _bf(jnp.where(m0, 0.0, at))),
            rt_h=(_bf(jnp.where(m0, rt, 0.0)), _bf(jnp.where(m0, 0.0, rt)))))

    grams = []
    for d in pre:
        lhs = jnp.concatenate([d["at_h"][0], d["at_h"][1], d["rt_h"][0], d["rt_h"][1]], axis=0)
        rhs = jnp.concatenate([d["bt"], d["kt"]], axis=0)
        grams.append(_dot_nt(lhs, rhs))

    ri = lax.broadcasted_iota(jnp.int32, (rows, rows), 0)
    ci = lax.broadcasted_iota(jnp.int32, (rows, rows), 1)
    same = (ri // chunk) == (ci // chunk)
    strict = same & (ri > ci)
    incl = same & (ri >= ci)
    eye = jnp.where(ri == ci, 1.0, 0.0)

    xs = [jnp.where(strict, grams[b][h * rows:(h + 1) * rows, :rows], 0.0) for b, h in inst]
    t_inv = [eye + x for x in xs]
    n = 2
    while n < chunk:
        xb = [_bf(x) for x in xs]
        xs = [_dot(x, x) for x in xb]
        t_inv = [t + _dot(_bf(t), _bf(x)) for t, x in zip(t_inv, xs)]
        n *= 2
    t_bf = [_bf(t) for t in t_inv]

    akv = [_dot(_bf(jnp.where(strict, grams[b][h * rows:(h + 1) * rows, rows:], 0.0)), pre[b]["v_bf"])
           for b, h in inst]
    up_h = [_dot(t, jnp.concatenate([_bf(z), pre[b]["at_h"][h]], axis=1))
            for t, z, (b, h) in zip(t_bf, akv, inst)]
    u0_h = [z[:, :LANES] for z in up_h]
    p_h = [z[:, LANES:] for z in up_h]
    a_rb = [_bf(jnp.where(incl, grams[b][(2 + h) * rows:(3 + h) * rows, :rows], 0.0)) for b, h in inst]
    a_rk = [_bf(jnp.where(incl, grams[b][(2 + h) * rows:(3 + h) * rows, rows:], 0.0)) for b, h in inst]
    rb_up = [_dot(a_rb[i], _bf(up_h[i])) for i in range(len(inst))]
    y0_h = [rb_up[i][:, :LANES] + _dot(a_rk[i], pre[b]["v_bf"]) for i, (b, h) in enumerate(inst)]
    qp_h = [z[:, LANES:] for z in rb_up]

    bi = lax.broadcasted_iota(jnp.int32, (LANES, LANES), 0) // RWKV_HEAD_DIM
    bj = lax.broadcasted_iota(jnp.int32, (LANES, LANES), 1) // RWKV_HEAD_DIM
    blockdiag = bi == bj

    p_all, q_all, u0_all, y0_all = [], [], [], []
    for b in range(nb):
        i0, i1 = HEAD_PAIR * b, HEAD_PAIR * b + 1
        p_all.append(_bf(p_h[i0] + p_h[i1]))
        q_all.append(_bf(pre[b]["rt"] + qp_h[i0] + qp_h[i1]))
        u0_all.append(_bf(jnp.where(m0, u0_h[i0], u0_h[i1])))
        y0_all.append(jnp.where(m0, y0_h[i0], y0_h[i1]))

    m_c, c_c = {}, {}
    for c in range(n_chunks):
        lo, hi = c * chunk, (c + 1) * chunk
        for b in range(nb):
            bt_c, kt_c = pre[b]["bt"][lo:hi], pre[b]["kt"][lo:hi]
            m_c[b, c] = _bf(jnp.where(blockdiag, _dot_tn(p_all[b][lo:hi], bt_c), 0.0))
            c_c[b, c] = jnp.where(blockdiag, _dot_tn(
                jnp.concatenate([u0_all[b][lo:hi], pre[b]["v_bf"][lo:hi]], axis=0),
                jnp.concatenate([bt_c, kt_c], axis=0)), 0.0)

    ys = [[] for _ in range(nb)]
    states = list(states)
    for c in range(n_chunks):
        lo, hi = c * chunk, (c + 1) * chunk
        for b in range(nb):
            s_bf = _bf(states[b])
            ys[b].append(_dot_nt(q_all[b][lo:hi], s_bf) + y0_all[b][lo:hi])
            states[b] = (states[b] + _dot(s_bf, m_c[b, c]) + c_c[b, c]) * pre[b]["w_inc"][hi - 1:hi, :]

    outs = []
    inv_n = 1.0 / RWKV_HEAD_DIM
    for b, (r, k, v, ld, a, g) in enumerate(ins):
        _, _, r_k, gn_g, gn_b = pars[b]
        y = jnp.concatenate(ys[b], axis=0)
        mean = _pair_sum(y, m0) * inv_n
        d = y - mean
        var = _pair_sum(d * d, m0) * inv_n
        yn = d * lax.rsqrt(var + RWKV_GN_EPS) * gn_g + gn_b
        bonus = _pair_sum(r * pre[b]["k2"] * r_k, m0) * v
        outs.append((yn + bonus) * g)
    return outs, states


def _rwkv_scan(r, k, v, ld, a, g, layer, k_k, k_a, r_k, gn_g, gn_b):
    bsz, t, _ = r.shape
    width = RWKV_SCAN_PAIRS * LANES
    tok = pl.BlockSpec((bsz, RWKV_BLOCK, width), lambda h, c: (0, c, h))
    par = pl.BlockSpec((None, 1, width), lambda h, c: (layer, 0, h))
    return pl.pallas_call(
        _rwkv_scan_kernel,
        grid=(N_PAIRS // RWKV_SCAN_PAIRS, t // RWKV_BLOCK),
        in_specs=[tok] * 6 + [par] * 5,
        out_specs=tok,
        out_shape=jax.ShapeDtypeStruct((bsz, t, RWKV_DIM), BF16),
        scratch_shapes=[pltpu.VMEM((bsz * RWKV_SCAN_PAIRS, LANES, LANES), F32)],
        compiler_params=_params("parallel", "arbitrary"),
        name="rwkv_scan",
    )(r, k, v, ld, a, g, k_k, k_a, r_k, gn_g, gn_b)


def _rope_rotated(w):
    lo, mid, hi = QK_NOPE_DIM, QK_NOPE_DIM + QK_ROPE_DIM // 2, QK_NOPE_DIM + QK_ROPE_DIM
    return jnp.zeros_like(w).at[..., lo:mid].set(-w[..., mid:hi]).at[..., mid:hi].set(w[..., lo:mid])


def _mla_prep_kernel(x_ref, w_ref, qn_ref, qup_ref, kvn_ref, kvup_ref, cos_ref, sin_ref,
                     q_ref, k_ref, v_ref):
    nq, nkv, wq = Q_LORA_RANK, KV_LORA_RANK, MLA_HEADS * LANES
    cos_t, sin_t = cos_ref[...], sin_ref[...]
    reps = MLA_HEADS
    cos_q, sin_q = jnp.tile(cos_t, (1, reps)), jnp.tile(sin_t, (1, reps))
    batch = range(x_ref.shape[0])
    ps = [_dot(x_ref[b], w_ref[...]) for b in batch]
    c_q = [p[:, :nq] for p in ps]
    c_kv = [p[:, nq:nq + nkv] for p in ps]
    c_q = [z * lax.rsqrt(jnp.mean(z * z, -1, keepdims=True) + RMS_EPS) * qn_ref[...] for z in c_q]
    c_kv = [z * lax.rsqrt(jnp.mean(z * z, -1, keepdims=True) + RMS_EPS) * kvn_ref[...] for z in c_kv]
    qs = [_dot(_bf(z), qup_ref[...]) for z in c_q]
    kvs = [_dot(_bf(z), kvup_ref[...]) for z in c_kv]
    for b in batch:
        q = qs[b][:, :wq] * cos_q + qs[b][:, wq:] * sin_q
        q_ref[b] = (q * (ATTN_SCALE * LOG2_E)).astype(q_ref.dtype)
        kr = ps[b][:, nq + nkv:nq + nkv + LANES] * cos_t + ps[b][:, nq + nkv + LANES:] * sin_t
        k_ref[b] = (kvs[b][:, :wq] + jnp.tile(kr, (1, reps))).astype(k_ref.dtype)
        v_ref[b] = kvs[b][:, wq:].astype(v_ref.dtype)


def _mla_prep(x_bf, layer, params, cos_t, sin_t, tm):
    bsz, t, _ = x_bf.shape
    tok = lambda i: (0, i, 0)
    tab = pl.BlockSpec((tm, LANES), lambda i: (i, 0))
    wq = MLA_HEADS * LANES
    return pl.pallas_call(
        _mla_prep_kernel,
        grid=(t // tm,),
        in_specs=[pl.BlockSpec((bsz, tm, D_MODEL), tok)] + [_layer_block(a, layer, single=True) for a in params]
        + [tab, tab],
        out_specs=[pl.BlockSpec((bsz, tm, wq), tok), pl.BlockSpec((bsz, tm, wq), tok),
                   pl.BlockSpec((bsz, tm, MLA_DIM), tok)],
        out_shape=[jax.ShapeDtypeStruct((bsz, t, wq), BF16), jax.ShapeDtypeStruct((bsz, t, wq), BF16),
                   jax.ShapeDtypeStruct((bsz, t, MLA_DIM), BF16)],
        compiler_params=_params("parallel"),
        name="mla_prep",
    )(x_bf, *params, cos_t, sin_t)


def _attn_kernel(q_ref, k_ref, v_ref, o_ref, m_ref, l_ref, acc_ref, *, tq, tk):
    i = pl.program_id(2)
    lane = lax.broadcasted_iota(jnp.int32, (1, LANES), 1)
    m0 = lane < V_HEAD_DIM
    m_ref[...] = jnp.full_like(m_ref, NEG_BIG)
    l_ref[...] = jnp.zeros_like(l_ref)
    acc_ref[...] = jnp.zeros_like(acc_ref)

    def step(start, nk, row_lo, masked):
        rows = slice(row_lo, tq)
        nrow = tq - row_lo
        heads = range(q_ref.shape[1] // LANES)
        s_h = [_dot_nt(q_ref[rows, h * LANES:(h + 1) * LANES], k_ref[pl.ds(start, nk), h * LANES:(h + 1) * LANES])
               for h in heads]
        alphas, ps = [], []
        for h in heads:
            s = s_h[h]
            if masked:
                qi = lax.broadcasted_iota(jnp.int32, (nrow, nk), 0) + (i * tq + row_lo)
                ki = lax.broadcasted_iota(jnp.int32, (nrow, nk), 1) + start
                s = jnp.where(ki <= qi, s, NEG_BIG)
            m_old = m_ref[h, rows]
            m_new = jnp.maximum(m_old, jnp.max(s, -1, keepdims=True))
            alpha = jnp.exp2(m_old - m_new)
            p = jnp.exp2(s - jnp.tile(m_new, (1, nk // LANES)))
            p_sum = p[:, :LANES]
            for c in range(1, nk // LANES):
                p_sum = p_sum + p[:, c * LANES:(c + 1) * LANES]
            l_ref[h, rows] = alpha * l_ref[h, rows] + p_sum
            m_ref[h, rows] = m_new
            alphas.append(alpha)
            ps.append(_bf(p))
        for h in heads:
            pair = slice((h // HEAD_PAIR) * LANES, (h // HEAD_PAIR + 1) * LANES)
            acc_ref[h, rows] = alphas[h] * acc_ref[h, rows] + _dot(ps[h], v_ref[pl.ds(start, nk), pair])

    def body(j, carry):
        step(pl.multiple_of(j * tk, tk), tk, 0, False)
        return carry

    lax.fori_loop(0, i * (tq // tk), body, 0)
    half = tq // 2
    base = pl.multiple_of(i * tq, tq)
    step(base, half, 0, True)
    step(base + half, half, half, True)
    for p in range(o_ref.shape[1] // LANES):
        out = [acc_ref[h] / jnp.sum(l_ref[h], -1, keepdims=True) for h in (HEAD_PAIR * p, HEAD_PAIR * p + 1)]
        o_ref[:, p * LANES:(p + 1) * LANES] = jnp.where(m0, out[0], out[1]).astype(o_ref.dtype)


def _attention(q, k, v, tq, tk):
    bsz, t, _ = q.shape
    npair = ATTN_PAIRS
    heads = npair * HEAD_PAIR
    resident = lambda width: pl.BlockSpec((None, t, width), lambda b, h, i: (b, 0, h))
    return pl.pallas_call(
        functools.partial(_attn_kernel, tq=tq, tk=tk),
        grid=(bsz, N_PAIRS // npair, t // tq),
        in_specs=[
            pl.BlockSpec((None, tq, heads * LANES), lambda b, h, i: (b, i, h)),
            resident(heads * LANES),
            resident(npair * LANES),
        ],
        out_specs=pl.BlockSpec((None, tq, npair * LANES), lambda b, h, i: (b, i, h)),
        out_shape=jax.ShapeDtypeStruct((bsz, t, MLA_DIM), BF16),
        scratch_shapes=[pltpu.VMEM((heads, tq, LANES), F32)] * 3,
        compiler_params=_params("parallel", "parallel", "arbitrary"),
        name="mla_attention",
    )(q, k, v)


def _s5_disc_kernel(lre_ref, lim_ref, ls_ref, br_ref, bi_ref, lbr_ref, lbi_ref, bbr_ref, bbi_ref):
    lam_re = jnp.minimum(lre_ref[...], -1e-4)
    lam_im = lim_ref[...]
    step = jnp.exp(ls_ref[...])
    mag = jnp.exp(lam_re * step)
    ang = lam_im * step
    lb_re, lb_im = mag * jnp.cos(ang), mag * jnp.sin(ang)
    den = lam_re * lam_re + lam_im * lam_im
    n_re = lb_re - 1.0
    f_re = (n_re * lam_re + lb_im * lam_im) / den
    f_im = (lb_im * lam_re - n_re * lam_im) / den
    br, bi = br_ref[...], bi_ref[...]
    lbr_ref[...] = lb_re
    lbi_ref[...] = lb_im
    bbr_ref[...] = f_re * br - f_im * bi
    bbi_ref[...] = f_re * bi + f_im * br


def _s5_discretize(lambda_re, lambda_im, log_step, b_re, b_im):
    depth = lambda_re.shape[0]
    rep = lambda z: jnp.repeat(z, S5_GROUP, axis=1)
    ls = rep(jnp.broadcast_to(log_step[:, :, None], (depth, S5_GROUPS, S5_STATE)))
    tr = lambda z: jnp.transpose(z, (0, 1, 3, 2)).reshape(depth, S5_DIM, S5_STATE)
    spec = pl.BlockSpec((None, S5_DIM, S5_STATE), lambda l: (l, 0, 0))
    out = jax.ShapeDtypeStruct((depth, S5_DIM, S5_STATE), F32)
    return pl.pallas_call(
        _s5_disc_kernel, grid=(depth,), in_specs=[spec] * 5, out_specs=[spec] * 4, out_shape=[out] * 4,
        compiler_params=_params("parallel"), name="s5_discretize",
    )(rep(lambda_re), rep(lambda_im), ls, tr(b_re), tr(b_im))


def _split_bf(z):
    hi = _bf(z)
    return hi, _bf(z - hi.astype(F32))


def _s5_prep_kernel(b_ref, c_ref, ct_ref, lam_ref, wg_ref, kc_ref, kst_ref, lam8_ref):
    n = S5_SLAB_STATE
    b, c, ct, lam = b_ref[0], c_ref[0], ct_ref[0], lam_ref[0]
    ar, ai = lam[:, :n], lam[:, n:]
    br, bi = b[:, :n], b[:, n:]
    ctr, cti = ct[:, :n], ct[:, n:]
    c_hi, c_lo = _split_bf(c)
    pr, pi = jnp.ones_like(ar), jnp.zeros_like(ai)
    for j in range(S5_CHUNK):
        w = jnp.concatenate([br * pr - bi * pi, br * pi + bi * pr], axis=1)
        w_hi, w_lo = _split_bf(w)
        rows = slice(j * LANES, (j + 1) * LANES)
        wg_ref[0, rows, :] = w_hi
        kc_ref[0, rows, :] = _bf(_dot(w_hi, c_hi) + _dot(w_lo, c_hi) + _dot(w_hi, c_lo))
        pr, pi = pr * ar - pi * ai, pr * ai + pi * ar
        kst_ref[0, rows, :] = _bf(jnp.concatenate([ctr * pr + cti * pi, cti * pr - ctr * pi], axis=1))
    lam8_ref[0] = jnp.concatenate([pr, pi], axis=1)


def _s5_prep(bmat, cmat, lam):
    depth = bmat.shape[0]
    n2 = 2 * S5_SLAB_STATE
    kq = S5_CHUNK * LANES
    blk = lambda r, c: pl.BlockSpec((None, 1, r, c), lambda l, s: (l, s, 0, 0))
    shape = lambda r, c, dt: jax.ShapeDtypeStruct((depth, S5_SLABS, r, c), dt)
    return pl.pallas_call(
        _s5_prep_kernel,
        grid=(depth, S5_SLABS),
        in_specs=[blk(LANES, n2), blk(n2, LANES), blk(LANES, n2), blk(1, n2)],
        out_specs=[blk(kq, n2), blk(kq, LANES), blk(kq, n2), blk(1, n2)],
        out_shape=[shape(kq, n2, BF16), shape(kq, LANES, BF16), shape(kq, n2, BF16), shape(1, n2, F32)],
        compiler_params=_params("parallel", "parallel"),
        name="s5_prep",
    )(bmat, cmat, jnp.swapaxes(cmat, 2, 3), lam)


def _s5_kernel(x_ref, w_ref, wg_ref, kc_ref, kst_ref, lam8_ref, d_ref, glu_ref, o_ref,
               carry_ref, u_ref, ys_ref, *, ts):
    @pl.when(pl.program_id(1) == 0)
    def _():
        carry_ref[...] = jnp.zeros_like(carry_ref)

    n = S5_SLAB_STATE
    nc = ts // S5_CHUNK
    nb = x_ref.shape[0]
    inst = [(b, sl) for b in range(nb) for sl in range(S5_SLABS)]
    ids = range(len(inst))
    u = [_dot(x_ref[b], w_ref[...]) for b in range(nb)]
    for b, sl in inst:
        u_ref[b, sl] = u[b][:, sl * LANES:(sl + 1) * LANES]
    rin = lax.broadcasted_iota(jnp.int32, (ts, 1), 0) & (S5_CHUNK - 1)
    crow = lax.broadcasted_iota(jnp.int32, (nc, 1), 0)
    first = crow == 0
    x_end = [jnp.concatenate(
        [_bf(u_ref[b, sl, pl.ds(S5_CHUNK - 1 - j, nc, stride=S5_CHUNK), :]) for j in range(S5_CHUNK)], axis=1)
        for b, sl in inst]
    gain = [_dot(x_end[i], wg_ref[sl]) for i, (b, sl) in enumerate(inst)]
    lam8 = [lam8_ref[sl] for b, sl in inst]
    ar = [z[:, :n] for z in lam8]
    ai = [z[:, n:] for z in lam8]
    cr = [carry_ref[b, sl, 0:1, :n] for b, sl in inst]
    cim = [carry_ref[b, sl, 0:1, n:] for b, sl in inst]
    er = [gain[i][:, :n] + jnp.where(first, ar[i] * cr[i] - ai[i] * cim[i], 0.0) for i in ids]
    ei = [gain[i][:, n:] + jnp.where(first, ar[i] * cim[i] + ai[i] * cr[i], 0.0) for i in ids]
    sh = 1
    while sh < nc:
        keep = crow >= sh
        for i in ids:
            sr = jnp.where(keep, pltpu.roll(er[i], sh, axis=0), 0.0)
            si = jnp.where(keep, pltpu.roll(ei[i], sh, axis=0), 0.0)
            er[i], ei[i] = er[i] + ar[i] * sr - ai[i] * si, ei[i] + ar[i] * si + ai[i] * sr
            ar[i], ai[i] = ar[i] * ar[i] - ai[i] * ai[i], 2.0 * ar[i] * ai[i]
        sh *= 2
    for i, (b, sl) in enumerate(inst):
        carry_ref[b, sl, 0:1, :n] = er[i][nc - 1:nc, :]
        carry_ref[b, sl, 0:1, n:] = ei[i][nc - 1:nc, :]
    h0 = [_bf(jnp.concatenate([jnp.where(first, cr[i], pltpu.roll(er[i], 1, axis=0)),
                               jnp.where(first, cim[i], pltpu.roll(ei[i], 1, axis=0))], axis=1)) for i in ids]
    z = [_dot_nt(h0[i], kst_ref[sl]) for i, (b, sl) in enumerate(inst)]
    for i, (b, sl) in enumerate(inst):
        for k in range(S5_CHUNK):
            ys_ref[b, sl, pl.ds(k, nc, stride=S5_CHUNK), :] = z[i][:, k * LANES:(k + 1) * LANES]
    ys = []
    for b, sl in inst:
        us = u[b][:, sl * LANES:(sl + 1) * LANES]
        lagged = [_bf(us)] + [_bf(jnp.where(rin >= j, pltpu.roll(us, j, axis=0), 0.0))
                              for j in range(1, S5_CHUNK)]
        ys.append(_dot(jnp.concatenate(lagged, axis=1), kc_ref[sl]) + ys_ref[b, sl])
    for b in range(nb):
        y = jnp.concatenate(ys[b * S5_SLABS:(b + 1) * S5_SLABS], axis=1) + d_ref[...] * u[b]
        y = 0.5 * y * (1.0 + jnp.tanh(math.sqrt(2.0 / math.pi) * (y + 0.044715 * (y * y * y))))
        h = _dot(_bf(y), glu_ref[...])
        o_ref[b] = (h[:, :D_MODEL] * _sigmoid(h[:, D_MODEL:])).astype(o_ref.dtype)


def _s5(x_bf, layer, params, ts):
    bsz, t, _ = x_bf.shape
    nb = 1
    tok = lambda b, i: (b, i, 0)
    return pl.pallas_call(
        functools.partial(_s5_kernel, ts=ts),
        grid=(bsz // nb, t // ts),
        in_specs=[pl.BlockSpec((nb, ts, D_MODEL), tok)] + [_layer_block(a, layer, single=True) for a in params],
        out_specs=pl.BlockSpec((nb, ts, D_MODEL), tok),
        out_shape=jax.ShapeDtypeStruct((bsz, t, D_MODEL), BF16),
        scratch_shapes=[pltpu.VMEM((nb, S5_SLABS, 8, 2 * S5_SLAB_STATE), F32),
                        pltpu.VMEM((nb, S5_SLABS, ts, LANES), F32), pltpu.VMEM((nb, S5_SLABS, ts, LANES), F32)],
        compiler_params=_params("parallel", "arbitrary"),
        name="s5_scan",
    )(x_bf, *params)


def _s5_matrices(lb_re, lb_im, bb_re, bb_im, c_re, c_im):
    eye = jnp.eye(S5_SLAB_GROUPS, dtype=F32)
    depth = lb_re.shape[0]
    s, g, c, p = S5_SLABS, S5_SLAB_GROUPS, S5_GROUP, S5_STATE

    def b_blocks(bb):
        return jnp.einsum("lsgcp,gh->lsgchp", bb.reshape(depth, s, g, c, p), eye).reshape(depth, s, g * c, g * p)

    def c_blocks(cc):
        return jnp.einsum("lsgcp,gh->lsgphc", cc.reshape(depth, s, g, c, p), eye).reshape(depth, s, g * p, g * c)

    bmat = jnp.concatenate([b_blocks(bb_re), b_blocks(bb_im)], axis=3)
    cmat = jnp.concatenate([c_blocks(c_re), -c_blocks(c_im)], axis=2)
    row = lambda z: z[:, ::S5_GROUP].reshape(depth, s, 1, g * p)
    lam = jnp.concatenate([row(lb_re), row(lb_im)], axis=3)
    return bmat, cmat, lam


def _merge_kernel(x_ref, ya_ref, ob_ref, yc_ref, wg_ref, gb_ref, wra_ref, wmo_ref, wo_ref,
                  lg_ref, lb_ref, o_ref, *, alpha):
    tm = x_ref.shape[0]
    parts = [slice(0, tm // 2), slice(tm // 2, tm)]
    xs = [x_ref[r, :] for r in parts]
    xbs = [_bf(x) for x in xs]
    branches = (lambda r: _dot(ya_ref[r, :], wra_ref[...]), lambda r: _dot(ob_ref[r, :], wmo_ref[...]),
                lambda r: yc_ref[r, :].astype(F32))
    merged = [None] * len(parts)
    for br, branch in enumerate(branches):
        cols = slice(br * D_MODEL, (br + 1) * D_MODEL)
        for i, r in enumerate(parts):
            term = _sigmoid(_dot(xbs[i], wg_ref[:, cols]) + gb_ref[:, cols]) * branch(r)
            merged[i] = term if merged[i] is None else merged[i] + term
    ys = [alpha * xs[i] + _dot(_bf(merged[i]), wo_ref[...]) for i in range(len(parts))]
    for i, r in enumerate(parts):
        o_ref[r, :] = _layer_norm(ys[i], lg_ref[...], lb_ref[...])


def _merge(x, ya, ob, yc, layer, params, alpha, tm):
    n = x.shape[0]
    tok = lambda i: (i, 0)
    return pl.pallas_call(
        functools.partial(_merge_kernel, alpha=alpha),
        grid=(n // tm,),
        in_specs=[
            pl.BlockSpec((tm, D_MODEL), tok),
            pl.BlockSpec((tm, RWKV_DIM), tok), pl.BlockSpec((tm, MLA_DIM), tok),
            pl.BlockSpec((tm, D_MODEL), tok),
        ] + [_layer_block(a, layer, single=True) for a in params],
        out_specs=pl.BlockSpec((tm, D_MODEL), tok),
        out_shape=jax.ShapeDtypeStruct((n, D_MODEL), F32),
        compiler_params=_params("parallel"),
        name="merge_ln",
    )(x, ya, ob, yc, *params)


def _ffn_kernel(x_ref, w1_ref, w3_ref, w2_ref, lg_ref, lb_ref, o_ref, *, alpha, tf):
    x = x_ref[...]
    xb = _bf(x)
    acc = alpha * x
    for j in range(w1_ref.shape[1] // tf):
        cols = slice(j * tf, (j + 1) * tf)
        h1 = _dot(xb, w1_ref[:, cols])
        h3 = _dot(xb, w3_ref[:, cols])
        acc = acc + _dot(_bf(h1 * _sigmoid(h1) * h3), w2_ref[cols, :])
    o_ref[...] = _layer_norm(acc, lg_ref[...], lb_ref[...])


def _ffn(x, layer, params, alpha, tm, tf):
    n = x.shape[0]
    tok = lambda i: (i, 0)
    return pl.pallas_call(
        functools.partial(_ffn_kernel, alpha=alpha, tf=tf),
        grid=(n // tm,),
        in_specs=[pl.BlockSpec((tm, D_MODEL), tok)] + [_layer_block(a, layer, single=True) for a in params],
        out_specs=pl.BlockSpec((tm, D_MODEL), tok),
        out_shape=jax.ShapeDtypeStruct((n, D_MODEL), F32),
        compiler_params=_params("parallel"),
        name="ffn_ln",
    )(x, *params)


def _rope_tables(t):
    pos = jnp.arange(t, dtype=F32)
    inv_freq = ROPE_THETA ** (-jnp.arange(0, QK_ROPE_DIM, 2, dtype=F32) / QK_ROPE_DIM)
    ang = pos[:, None] * inv_freq[None, :]
    cos, sin = jnp.cos(ang), jnp.sin(ang)
    ones = jnp.ones((t, QK_NOPE_DIM), F32)
    zeros = jnp.zeros((t, QK_NOPE_DIM), F32)
    tail1 = jnp.ones((t, LANES - QK_NOPE_DIM - QK_ROPE_DIM), F32)
    tail0 = jnp.zeros((t, LANES - QK_NOPE_DIM - QK_ROPE_DIM), F32)
    cos_t = jnp.concatenate([ones, cos, cos, tail1], axis=1)
    sin_t = jnp.concatenate([zeros, sin, sin, tail0], axis=1)
    return cos_t, sin_t


def _branch_weights(w_in, w2, a2, q_up, kv_up):
    depth = w_in.shape[0]
    o_mla = RWKV_COLS
    o_s5 = o_mla + MLA_COLS
    o_gate = o_s5 + S5_COLS
    nq = Q_LORA_RANK + KV_LORA_RANK
    zl = jnp.zeros((depth, DECAY_LORA, RWKV_DIM), F32)
    w_lora = jnp.concatenate([jnp.concatenate([w2, zl], axis=2), jnp.concatenate([zl, a2], axis=2)], axis=1)
    w_kr = jnp.zeros((depth, D_MODEL, LANES), F32).at[:, :, QK_NOPE_DIM:QK_NOPE_DIM + QK_ROPE_DIM].set(
        w_in[:, :, o_mla + nq:o_s5])
    w_mla = jnp.concatenate([w_in[:, :, o_mla:o_mla + nq], w_kr, _rope_rotated(w_kr)], axis=2)
    q_up = q_up.reshape(depth, Q_LORA_RANK, MLA_HEADS, QK_NOPE_DIM + QK_ROPE_DIM)
    q_up = jnp.pad(q_up, ((0, 0), (0, 0), (0, 0), (0, LANES - QK_NOPE_DIM - QK_ROPE_DIM)))
    q_up = jnp.concatenate([q_up.reshape(depth, Q_LORA_RANK, MLA_HEADS * LANES),
                            _rope_rotated(q_up).reshape(depth, Q_LORA_RANK, MLA_HEADS * LANES)], axis=2)
    kv_up = kv_up.reshape(depth, KV_LORA_RANK, MLA_HEADS, QK_NOPE_DIM + V_HEAD_DIM)
    k_up = jnp.pad(kv_up[..., :QK_NOPE_DIM], ((0, 0), (0, 0), (0, 0), (0, LANES - QK_NOPE_DIM)))
    kv_up = jnp.concatenate([k_up.reshape(depth, KV_LORA_RANK, MLA_HEADS * LANES),
                             kv_up[..., QK_NOPE_DIM:].reshape(depth, KV_LORA_RANK, MLA_DIM)], axis=2)
    return dict(rwkv=_bf(w_in[:, :, :o_mla]), lora=_bf(w_lora), mla=_bf(w_mla), q_up=_bf(q_up),
                kv_up=_bf(kv_up), s5=_bf(w_in[:, :, o_s5:o_gate]), gate=_bf(w_in[:, :, o_gate:]))


def _s5_params(lambda_re, lambda_im, log_step, b_re, b_im, c_re, c_im):
    lb_re, lb_im, bb_re, bb_im = _s5_discretize(lambda_re, lambda_im, log_step, b_re, b_im)
    rows = lambda z: z.reshape(-1, S5_DIM, S5_STATE)
    return _s5_prep(*_s5_matrices(lb_re, lb_im, bb_re, bb_im, rows(c_re), rows(c_im)))


def _rows(z):
    return z.reshape(z.shape[0], 1, -1).astype(F32)


ROW_TILE = dict(rwkv_prep=512, mla_prep=512, attention=1024, s5=1024, merge=1024, ffn=1024)


def _tile(t, name):
    return min(t, ROW_TILE[name])


def kernel(x, w_in, rwkv_mu, rwkv_w0, rwkv_w2, rwkv_a0, rwkv_a2, rwkv_g2, rwkv_k_k, rwkv_k_a, rwkv_r_k,
           rwkv_gn_g, rwkv_gn_b, rwkv_out, mla_q_norm, mla_q_up, mla_kv_norm, mla_kv_up, mla_out,
           s5_lambda_re, s5_lambda_im, s5_log_step, s5_b_re, s5_b_im, s5_c_re, s5_c_im, s5_d, s5_glu,
           gate_b, w_out, ln1_g, ln1_b, ffn_w1, ffn_w3, ffn_w2, ln2_g, ln2_b):
    bsz, t, _ = x.shape
    depth = w_in.shape[0]
    alpha = (2.0 * depth) ** 0.25
    n = bsz * t
    cos_t, sin_t = _rope_tables(t)

    ffn_w1, ffn_w3, ffn_w2, s5_glu, w_out, rwkv_out, mla_out = map(
        _to_bf16, (ffn_w1, ffn_w3, ffn_w2, s5_glu, w_out, rwkv_out, mla_out))

    w = _branch_weights(w_in, rwkv_w2, rwkv_a2, mla_q_up, mla_kv_up)
    wg, kc, kst, lam8 = _s5_params(s5_lambda_re, s5_lambda_im, s5_log_step, s5_b_re, s5_b_im, s5_c_re, s5_c_im)

    rwkv_par = (w["rwkv"], _rows(rwkv_mu), _rows(rwkv_w0), _rows(rwkv_a0), w["lora"], _bf(rwkv_g2))
    scan_par = tuple(_rows(z) for z in (rwkv_k_k, rwkv_k_a, rwkv_r_k, rwkv_gn_g, rwkv_gn_b))
    mla_par = (w["mla"], _rows(mla_q_norm), w["q_up"], _rows(mla_kv_norm), w["kv_up"])
    s5_par = (w["s5"], wg, kc, kst, lam8, _rows(s5_d), s5_glu)
    merge_par = (w["gate"], _rows(gate_b), rwkv_out, mla_out, w_out, _rows(ln1_g), _rows(ln1_b))
    ffn_par = (ffn_w1, ffn_w3, ffn_w2, _rows(ln2_g), _rows(ln2_b))

    for l in range(depth):
        x_bf, r, k, v, ld, a, g = _rwkv_prep(x, l, rwkv_par, _tile(t, "rwkv_prep"))
        ya = _rwkv_scan(r, k, v, ld, a, g, l, *scan_par)
        q, kx, vx = _mla_prep(x_bf, l, mla_par, cos_t, sin_t, _tile(t, "mla_prep"))
        ob = _attention(q, kx, vx, _tile(t, "attention"), min(t, ATTN_KEY_BLOCK))
        yc = _s5(x_bf, l, s5_par, _tile(t, "s5"))
        x1 = _merge(x.reshape(n, D_MODEL), ya.reshape(n, RWKV_DIM), ob.reshape(n, MLA_DIM),
                    yc.reshape(n, D_MODEL), l, merge_par, alpha, _tile(n, "merge"))
        x = _ffn(x1, l, ffn_par, alpha, _tile(n, "ffn"), FFN_SLICE).reshape(bsz, t, D_MODEL)
    return x
```

```python
import functools
import math

import jax
import jax.numpy as jnp
from jax import lax
from jax.experimental import pallas as pl
from jax.experimental.pallas import tpu as pltpu

F32 = jnp.float32
BF16 = jnp.bfloat16

D_MODEL = 1024
RWKV_HEADS = 8
RWKV_HEAD_DIM = 64
RWKV_DIM = RWKV_HEADS * RWKV_HEAD_DIM
DECAY_LORA = 64
AAA_LORA = 64
GATE_LORA = 128
RWKV_GN_EPS = 64e-5
MLA_HEADS = 8
QK_NOPE_DIM = 64
QK_ROPE_DIM = 32
V_HEAD_DIM = 64
Q_LORA_RANK = 256
KV_LORA_RANK = 128
MLA_DIM = MLA_HEADS * V_HEAD_DIM
ROPE_THETA = 10000.0
ATTN_SCALE = 1.0 / math.sqrt(QK_NOPE_DIM + QK_ROPE_DIM)
LOG2_E = math.log2(math.e)
S5_DIM = 512
S5_GROUP = 16
S5_GROUPS = S5_DIM // S5_GROUP
S5_STATE = 64
N_BRANCHES = 3
LN_EPS = 1e-5
RMS_EPS = 1e-6
RWKV_COLS = 3 * RWKV_DIM + DECAY_LORA + AAA_LORA + GATE_LORA
MLA_COLS = Q_LORA_RANK + KV_LORA_RANK + QK_ROPE_DIM
S5_COLS = S5_DIM

LANES = 128
HEAD_PAIR = LANES // RWKV_HEAD_DIM
N_PAIRS = RWKV_HEADS // HEAD_PAIR
S5_SLAB_GROUPS = LANES // S5_GROUP
S5_SLABS = S5_GROUPS // S5_SLAB_GROUPS
S5_SLAB_STATE = S5_SLAB_GROUPS * S5_STATE
S5_CHUNK = 8
VMEM_LIMIT = 48 * 1024 * 1024

RWKV_CHUNK = 64
RWKV_BLOCK = 256
ATTN_PAIRS = 1
ATTN_KEY_BLOCK = 1024
MERGE_ROW_PARTS = 4
FFN_SLICE = 256
RWKV_SCAN_PAIRS = 4
NEG_BIG = -1e30
CAST_BLOCK_BYTES = 6 * 1024 * 1024


def _bf(x):
    return x.astype(BF16)


def _dot(a, b):
    return jnp.dot(a, b, preferred_element_type=F32)


def _dot_nt(a, b):
    return lax.dot_general(a, b, (((1,), (1,)), ((), ())), preferred_element_type=F32)


def _dot_tn(a, b):
    return lax.dot_general(a, b, (((0,), (0,)), ((), ())), preferred_element_type=F32)


def _sigmoid(x):
    return 1.0 / (1.0 + jnp.exp(-x))


def _params(*sem):
    return pltpu.CompilerParams(dimension_semantics=sem, vmem_limit_bytes=VMEM_LIMIT)


def _layer_block(a, layer, single=False):
    idx = (layer,) + (0,) * (a.ndim - 1)
    mode = dict(pipeline_mode=pl.Buffered(1)) if single else {}
    return pl.BlockSpec((None,) + a.shape[1:], lambda *_: idx, **mode)


def _cast_kernel(w_ref, o_ref):
    o_ref[...] = w_ref[...].astype(o_ref.dtype)


def _to_bf16(w):
    depth, rows, cols = w.shape
    cap = CAST_BLOCK_BYTES // (4 * cols)
    tr = max(d for d in range(8, rows + 1, 8) if rows % d == 0 and d <= max(cap, 8))
    spec = pl.BlockSpec((None, tr, cols), lambda l, i: (l, i, 0))
    return pl.pallas_call(
        _cast_kernel, grid=(depth, rows // tr), in_specs=[spec], out_specs=spec,
        out_shape=jax.ShapeDtypeStruct(w.shape, BF16),
        compiler_params=_params("parallel", "parallel"), name="cast_bf16",
    )(w)


def _layer_norm(y, g, b):
    mu = jnp.mean(y, -1, keepdims=True)
    d = y - mu
    var = jnp.mean(d * d, -1, keepdims=True)
    return d * lax.rsqrt(var + LN_EPS) * g + b


def _pair_sum(x, m0):
    s0 = jnp.sum(jnp.where(m0, x, 0.0), -1, keepdims=True)
    s1 = jnp.sum(jnp.where(m0, 0.0, x), -1, keepdims=True)
    return jnp.where(m0, s0, s1)


def _rwkv_prep_kernel(x_ref, w_ref, mu_ref, w0_ref, a0_ref, wl_ref, g2_ref,
                      xb_ref, r_ref, k_ref, v_ref, ld_ref, a_ref, g_ref, prev_ref):
    @pl.when(pl.program_id(0) == 0)
    def _():
        prev_ref[...] = jnp.zeros_like(prev_ref)

    nb = x_ref.shape[0]
    xb = [_bf(x_ref[b]) for b in range(nb)]
    ps = [_dot(z, w_ref[...]) for z in xb]
    row = lax.broadcasted_iota(jnp.int32, (ps[0].shape[0], 1), 0)
    c0, c1, c2 = RWKV_DIM, 2 * RWKV_DIM, 3 * RWKV_DIM
    lane = lax.broadcasted_iota(jnp.int32, (1, DECAY_LORA + AAA_LORA), 1)
    for b in range(nb):
        xb_ref[b] = xb[b]
        p = ps[b]
        rolled = pltpu.roll(p, 1, axis=0)
        shifted = jnp.where(row == 0, prev_ref[b, 0:1, :], rolled)
        prev_ref[b, 0:1, :] = rolled[0:1, :]
        p = p + (shifted - p) * mu_ref[...]
        r_ref[b] = p[:, :c0].astype(r_ref.dtype)
        k_ref[b] = p[:, c0:c1].astype(k_ref.dtype)
        v_ref[b] = p[:, c1:c2].astype(v_ref.dtype)
        lora = p[:, c2:c2 + DECAY_LORA + AAA_LORA]
        lora = jnp.where(lane < DECAY_LORA, jnp.tanh(lora), lora)
        wa = _dot(_bf(lora), wl_ref[...])
        ld_ref[b] = (-math.exp(-0.5)) * _sigmoid(w0_ref[...] + wa[:, :RWKV_DIM])
        a_ref[b] = _sigmoid(a0_ref[...] + wa[:, RWKV_DIM:]).astype(a_ref.dtype)
        dg = p[:, c2 + DECAY_LORA + AAA_LORA:]
        g_ref[b] = _dot(_bf(_sigmoid(dg)), g2_ref[...]).astype(g_ref.dtype)


def _rwkv_prep(x, layer, params, tm):
    bsz, t, _ = x.shape
    outs = [jax.ShapeDtypeStruct((bsz, t, D_MODEL), BF16)] + [
        jax.ShapeDtypeStruct((bsz, t, RWKV_DIM), dt) for dt in (BF16, BF16, BF16, F32, BF16, BF16)]
    tok = lambda i: (0, i, 0)
    o_spec = pl.BlockSpec((bsz, tm, RWKV_DIM), tok)
    return pl.pallas_call(
        _rwkv_prep_kernel,
        grid=(t // tm,),
        in_specs=[pl.BlockSpec((bsz, tm, D_MODEL), tok)] + [_layer_block(a, layer, single=True) for a in params],
        out_specs=[pl.BlockSpec((bsz, tm, D_MODEL), tok)] + [o_spec] * 6,
        out_shape=outs,
        scratch_shapes=[pltpu.VMEM((bsz, 8, RWKV_COLS), F32)],
        compiler_params=_params("arbitrary"),
        name="rwkv_prep",
    )(x, *params)


def _rwkv_scan_kernel(r_ref, k_ref, v_ref, ld_ref, a_ref, g_ref,
                      kk_ref, ka_ref, rk_ref, gng_ref, gnb_ref, o_ref, s_ref):
    @pl.when(pl.program_id(1) == 0)
    def _():
        s_ref[...] = jnp.zeros_like(s_ref)

    seqs = [(b, slice(p * LANES, (p + 1) * LANES)) for b in range(r_ref.shape[0])
            for p in range(r_ref.shape[2] // LANES)]
    ins = [tuple(z[b, :, cols].astype(F32) for z in (r_ref, k_ref, v_ref, ld_ref, a_ref, g_ref))
           for b, cols in seqs]
    pars = [tuple(z[:, cols] for z in (kk_ref, ka_ref, rk_ref, gng_ref, gnb_ref)) for b, cols in seqs]
    outs, states = _rwkv_blocks(ins, pars, [s_ref[i] for i in range(len(seqs))])
    for i, (b, cols) in enumerate(seqs):
        o_ref[b, :, cols] = outs[i].astype(o_ref.dtype)
        s_ref[i] = states[i]


def _rwkv_blocks(ins, pars, states):
    rows, chunk = RWKV_BLOCK, RWKV_CHUNK
    n_chunks = rows // chunk
    nb = len(ins)
    inst = [(b, h) for b in range(nb) for h in range(HEAD_PAIR)]

    lane = lax.broadcasted_iota(jnp.int32, (1, LANES), 1)
    m0 = lane < RWKV_HEAD_DIM
    rin = lax.broadcasted_iota(jnp.int32, (rows, 1), 0) & (chunk - 1)

    pre = []
    for (r, k, v, ld, a, g), (k_k, k_a, _, _, _) in zip(ins, pars):
        kk = k * k_k
        kk = kk * lax.rsqrt(jnp.maximum(_pair_sum(kk * kk, m0), 1e-12))
        k2 = k * (1.0 + (a - 1.0) * k_a)
        bv = kk * a
        cl = ld
        sh = 1
        while sh < chunk:
            cl = cl + jnp.where(rin >= sh, pltpu.roll(cl, sh, axis=0), 0.0)
            sh *= 2
        w_inc = jnp.exp(cl)
        w_inv = jnp.exp(-cl)
        at = -kk * jnp.exp(cl - ld)
        rt = r * w_inc
        pre.append(dict(
            k2=k2, w_inc=w_inc, rt=rt, bt=_bf(bv * w_inv), kt=_bf(k2 * w_inv), v_bf=_bf(v),
            at_h=(_bf(jnp.where(m0, at, 0.0)), _bf(jnp.where(m0, 0.0, at))),
            rt_h=(_bf(jnp.where(m0, rt, 0.0)), _bf(jnp.where(m0, 0.0, rt)))))

    grams = []
    for d in pre:
        lhs = jnp.concatenate([d["at_h"][0], d["at_h"][1], d["rt_h"][0], d["rt_h"][1]], axis=0)
        rhs = jnp.concatenate([d["bt"], d["kt"]], axis=0)
        grams.append(_dot_nt(lhs, rhs))

    ri = lax.broadcasted_iota(jnp.int32, (rows, rows), 0)
    ci = lax.broadcasted_iota(jnp.int32, (rows, rows), 1)
    same = (ri // chunk) == (ci // chunk)
    strict = same & (ri > ci)
    incl = same & (ri >= ci)
    eye = jnp.where(ri == ci, 1.0, 0.0)

    xs = [jnp.where(strict, grams[b][h * rows:(h + 1) * rows, :rows], 0.0) for b, h in inst]
    t_inv = [eye + x for x in xs]
    n = 2
    while n < chunk:
        xb = [_bf(x) for x in xs]
        xs = [_dot(x, x) for x in xb]
        t_inv = [t + _dot(_bf(t), _bf(x)) for t, x in zip(t_inv, xs)]
        n *= 2
    t_bf = [_bf(t) for t in t_inv]

    akv = [_dot(_bf(jnp.where(strict, grams[b][h * rows:(h + 1) * rows, rows:], 0.0)), pre[b]["v_bf"])
           for b, h in inst]
    up_h = [_dot(t, jnp.concatenate([_bf(z), pre[b]["at_h"][h]], axis=1))
            for t, z, (b, h) in zip(t_bf, akv, inst)]
    u0_h = [z[:, :LANES] for z in up_h]
    p_h = [z[:, LANES:] for z in up_h]
    a_rb = [_bf(jnp.where(incl, grams[b][(2 + h) * rows:(3 + h) * rows, :rows], 0.0)) for b, h in inst]
    a_rk = [_bf(jnp.where(incl, grams[b][(2 + h) * rows:(3 + h) * rows, rows:], 0.0)) for b, h in inst]
    rb_up = [_dot(a_rb[i], _bf(up_h[i])) for i in range(len(inst))]
    y0_h = [rb_up[i][:, :LANES] + _dot(a_rk[i], pre[b]["v_bf"]) for i, (b, h) in enumerate(inst)]
    qp_h = [z[:, LANES:] for z in rb_up]

    bi = lax.broadcasted_iota(jnp.int32, (LANES, LANES), 0) // RWKV_HEAD_DIM
    bj = lax.broadcasted_iota(jnp.int32, (LANES, LANES), 1) // RWKV_HEAD_DIM
    blockdiag = bi == bj

    p_all, q_all, u0_all, y0_all = [], [], [], []
    for b in range(nb):
        i0, i1 = HEAD_PAIR * b, HEAD_PAIR * b + 1
        p_all.append(_bf(p_h[i0] + p_h[i1]))
        q_all.append(_bf(pre[b]["rt"] + qp_h[i0] + qp_h[i1]))
        u0_all.append(_bf(jnp.where(m0, u0_h[i0], u0_h[i1])))
        y0_all.append(jnp.where(m0, y0_h[i0], y0_h[i1]))

    m_c, c_c = {}, {}
    for c in range(n_chunks):
        lo, hi = c * chunk, (c + 1) * chunk
        for b in range(nb):
            bt_c, kt_c = pre[b]["bt"][lo:hi], pre[b]["kt"][lo:hi]
            m_c[b, c] = _bf(jnp.where(blockdiag, _dot_tn(p_all[b][lo:hi], bt_c), 0.0))
            c_c[b, c] = jnp.where(blockdiag, _dot_tn(
                jnp.concatenate([u0_all[b][lo:hi], pre[b]["v_bf"][lo:hi]], axis=0),
                jnp.concatenate([bt_c, kt_c], axis=0)), 0.0)

    ys = [[] for _ in range(nb)]
    states = list(states)
    for c in range(n_chunks):
        lo, hi = c * chunk, (c + 1) * chunk
        for b in range(nb):
            s_bf = _bf(states[b])
            ys[b].append(_dot_nt(q_all[b][lo:hi], s_bf) + y0_all[b][lo:hi])
            states[b] = (states[b] + _dot(s_bf, m_c[b, c]) + c_c[b, c]) * pre[b]["w_inc"][hi - 1:hi, :]

    outs = []
    inv_n = 1.0 / RWKV_HEAD_DIM
    for b, (r, k, v, ld, a, g) in enumerate(ins):
        _, _, r_k, gn_g, gn_b = pars[b]
        y = jnp.concatenate(ys[b], axis=0)
        mean = _pair_sum(y, m0) * inv_n
        d = y - mean
        var = _pair_sum(d * d, m0) * inv_n
        yn = d * lax.rsqrt(var + RWKV_GN_EPS) * gn_g + gn_b
        bonus = _pair_sum(r * pre[b]["k2"] * r_k, m0) * v
        outs.append((yn + bonus) * g)
    return outs, states


def _rwkv_scan(r, k, v, ld, a, g, layer, k_k, k_a, r_k, gn_g, gn_b):
    bsz, t, _ = r.shape
    width = RWKV_SCAN_PAIRS * LANES
    tok = pl.BlockSpec((bsz, RWKV_BLOCK, width), lambda h, c: (0, c, h))
    par = pl.BlockSpec((None, 1, width), lambda h, c: (layer, 0, h))
    return pl.pallas_call(
        _rwkv_scan_kernel,
        grid=(N_PAIRS // RWKV_SCAN_PAIRS, t // RWKV_BLOCK),
        in_specs=[tok] * 6 + [par] * 5,
        out_specs=tok,
        out_shape=jax.ShapeDtypeStruct((bsz, t, RWKV_DIM), BF16),
        scratch_shapes=[pltpu.VMEM((bsz * RWKV_SCAN_PAIRS, LANES, LANES), F32)],
        compiler_params=_params("parallel", "arbitrary"),
        name="rwkv_scan",
    )(r, k, v, ld, a, g, k_k, k_a, r_k, gn_g, gn_b)


def _rope_rotated(w):
    lo, mid, hi = QK_NOPE_DIM, QK_NOPE_DIM + QK_ROPE_DIM // 2, QK_NOPE_DIM + QK_ROPE_DIM
    return jnp.zeros_like(w).at[..., lo:mid].set(-w[..., mid:hi]).at[..., mid:hi].set(w[..., lo:mid])


def _mla_prep_kernel(x_ref, w_ref, qn_ref, qup_ref, kvn_ref, kvup_ref, cos_ref, sin_ref,
                     q_ref, k_ref, v_ref):
    nq, nkv, wq = Q_LORA_RANK, KV_LORA_RANK, MLA_HEADS * LANES
    cos_t, sin_t = cos_ref[...], sin_ref[...]
    reps = MLA_HEADS
    cos_q, sin_q = jnp.tile(cos_t, (1, reps)), jnp.tile(sin_t, (1, reps))
    batch = range(x_ref.shape[0])
    ps = [_dot(x_ref[b], w_ref[...]) for b in batch]
    c_q = [p[:, :nq] for p in ps]
    c_kv = [p[:, nq:nq + nkv] for p in ps]
    c_q = [z * lax.rsqrt(jnp.mean(z * z, -1, keepdims=True) + RMS_EPS) * qn_ref[...] for z in c_q]
    c_kv = [z * lax.rsqrt(jnp.mean(z * z, -1, keepdims=True) + RMS_EPS) * kvn_ref[...] for z in c_kv]
    qs = [_dot(_bf(z), qup_ref[...]) for z in c_q]
    kvs = [_dot(_bf(z), kvup_ref[...]) for z in c_kv]
    for b in batch:
        q = qs[b][:, :wq] * cos_q + qs[b][:, wq:] * sin_q
        q_ref[b] = (q * (ATTN_SCALE * LOG2_E)).astype(q_ref.dtype)
        kr = ps[b][:, nq + nkv:nq + nkv + LANES] * cos_t + ps[b][:, nq + nkv + LANES:] * sin_t
        k_ref[b] = (kvs[b][:, :wq] + jnp.tile(kr, (1, reps))).astype(k_ref.dtype)
        v_ref[b] = kvs[b][:, wq:].astype(v_ref.dtype)


def _mla_prep(x_bf, layer, params, cos_t, sin_t, tm):
    bsz, t, _ = x_bf.shape
    tok = lambda i: (0, i, 0)
    tab = pl.BlockSpec((tm, LANES), lambda i: (i, 0))
    wq = MLA_HEADS * LANES
    return pl.pallas_call(
        _mla_prep_kernel,
        grid=(t // tm,),
        in_specs=[pl.BlockSpec((bsz, tm, D_MODEL), tok)] + [_layer_block(a, layer, single=True) for a in params]
        + [tab, tab],
        out_specs=[pl.BlockSpec((bsz, tm, wq), tok), pl.BlockSpec((bsz, tm, wq), tok),
                   pl.BlockSpec((bsz, tm, MLA_DIM), tok)],
        out_shape=[jax.ShapeDtypeStruct((bsz, t, wq), BF16), jax.ShapeDtypeStruct((bsz, t, wq), BF16),
                   jax.ShapeDtypeStruct((bsz, t, MLA_DIM), BF16)],
        compiler_params=_params("parallel"),
        name="mla_prep",
    )(x_bf, *params, cos_t, sin_t)


def _attn_kernel(q_ref, k_ref, v_ref, o_ref, m_ref, l_ref, acc_ref, *, tq, tk):
    i = pl.program_id(2)
    lane = lax.broadcasted_iota(jnp.int32, (1, LANES), 1)
    m0 = lane < V_HEAD_DIM
    m_ref[...] = jnp.full_like(m_ref, NEG_BIG)
    l_ref[...] = jnp.zeros_like(l_ref)
    acc_ref[...] = jnp.zeros_like(acc_ref)

    def step(start, nk, row_lo, masked):
        rows = slice(row_lo, tq)
        nrow = tq - row_lo
        heads = range(q_ref.shape[1] // LANES)
        s_h = [_dot_nt(q_ref[rows, h * LANES:(h + 1) * LANES], k_ref[pl.ds(start, nk), h * LANES:(h + 1) * LANES])
               for h in heads]
        alphas, ps = [], []
        for h in heads:
            s = s_h[h]
            if masked:
                qi = lax.broadcasted_iota(jnp.int32, (nrow, nk), 0) + (i * tq + row_lo)
                ki = lax.broadcasted_iota(jnp.int32, (nrow, nk), 1) + start
                s = jnp.where(ki <= qi, s, NEG_BIG)
            m_old = m_ref[h, rows]
            m_new = jnp.maximum(m_old, jnp.max(s, -1, keepdims=True))
            alpha = jnp.exp2(m_old - m_new)
            p = jnp.exp2(s - jnp.tile(m_new, (1, nk // LANES)))
            p_sum = p[:, :LANES]
            for c in range(1, nk // LANES):
                p_sum = p_sum + p[:, c * LANES:(c + 1) * LANES]
            l_ref[h, rows] = alpha * l_ref[h, rows] + p_sum
            m_ref[h, rows] = m_new
            alphas.append(alpha)
            ps.append(_bf(p))
        for h in heads:
            pair = slice((h // HEAD_PAIR) * LANES, (h // HEAD_PAIR + 1) * LANES)
            acc_ref[h, rows] = alphas[h] * acc_ref[h, rows] + _dot(ps[h], v_ref[pl.ds(start, nk), pair])

    def body(j, carry):
        step(pl.multiple_of(j * tk, tk), tk, 0, False)
        return carry

    lax.fori_loop(0, i * (tq // tk), body, 0)
    half = tq // 2
    base = pl.multiple_of(i * tq, tq)
    step(base, half, 0, True)
    step(base + half, half, half, True)
    for p in range(o_ref.shape[1] // LANES):
        out = [acc_ref[h] / jnp.sum(l_ref[h], -1, keepdims=True) for h in (HEAD_PAIR * p, HEAD_PAIR * p + 1)]
        o_ref[:, p * LANES:(p + 1) * LANES] = jnp.where(m0, out[0], out[1]).astype(o_ref.dtype)


def _attention(q, k, v, tq, tk):
    bsz, t, _ = q.shape
    npair = ATTN_PAIRS
    heads = npair * HEAD_PAIR
    resident = lambda width: pl.BlockSpec((None, t, width), lambda b, h, i: (b, 0, h))
    return pl.pallas_call(
        functools.partial(_attn_kernel, tq=tq, tk=tk),
        grid=(bsz, N_PAIRS // npair, t // tq),
        in_specs=[
            pl.BlockSpec((None, tq, heads * LANES), lambda b, h, i: (b, i, h)),
            resident(heads * LANES),
            resident(npair * LANES),
        ],
        out_specs=pl.BlockSpec((None, tq, npair * LANES), lambda b, h, i: (b, i, h)),
        out_shape=jax.ShapeDtypeStruct((bsz, t, MLA_DIM), BF16),
        scratch_shapes=[pltpu.VMEM((heads, tq, LANES), F32)] * 3,
        compiler_params=_params("parallel", "parallel", "arbitrary"),
        name="mla_attention",
    )(q, k, v)


def _s5_disc_kernel(lre_ref, lim_ref, ls_ref, br_ref, bi_ref, lbr_ref, lbi_ref, bbr_ref, bbi_ref):
    lam_re = jnp.minimum(lre_ref[...], -1e-4)
    lam_im = lim_ref[...]
    step = jnp.exp(ls_ref[...])
    mag = jnp.exp(lam_re * step)
    ang = lam_im * step
    lb_re, lb_im = mag * jnp.cos(ang), mag * jnp.sin(ang)
    den = lam_re * lam_re + lam_im * lam_im
    n_re = lb_re - 1.0
    f_re = (n_re * lam_re + lb_im * lam_im) / den
    f_im = (lb_im * lam_re - n_re * lam_im) / den
    br, bi = br_ref[...], bi_ref[...]
    lbr_ref[...] = lb_re
    lbi_ref[...] = lb_im
    bbr_ref[...] = f_re * br - f_im * bi
    bbi_ref[...] = f_re * bi + f_im * br


def _s5_discretize(lambda_re, lambda_im, log_step, b_re, b_im):
    depth = lambda_re.shape[0]
    rep = lambda z: jnp.repeat(z, S5_GROUP, axis=1)
    ls = rep(jnp.broadcast_to(log_step[:, :, None], (depth, S5_GROUPS, S5_STATE)))
    tr = lambda z: jnp.transpose(z, (0, 1, 3, 2)).reshape(depth, S5_DIM, S5_STATE)
    spec = pl.BlockSpec((None, S5_DIM, S5_STATE), lambda l: (l, 0, 0))
    out = jax.ShapeDtypeStruct((depth, S5_DIM, S5_STATE), F32)
    return pl.pallas_call(
        _s5_disc_kernel, grid=(depth,), in_specs=[spec] * 5, out_specs=[spec] * 4, out_shape=[out] * 4,
        compiler_params=_params("parallel"), name="s5_discretize",
    )(rep(lambda_re), rep(lambda_im), ls, tr(b_re), tr(b_im))


def _split_bf(z):
    hi = _bf(z)
    return hi, _bf(z - hi.astype(F32))


def _s5_prep_kernel(b_ref, c_ref, ct_ref, lam_ref, wg_ref, kc_ref, kst_ref, lam8_ref):
    n = S5_SLAB_STATE
    b, c, ct, lam = b_ref[0], c_ref[0], ct_ref[0], lam_ref[0]
    ar, ai = lam[:, :n], lam[:, n:]
    br, bi = b[:, :n], b[:, n:]
    ctr, cti = ct[:, :n], ct[:, n:]
    c_hi, c_lo = _split_bf(c)
    pr, pi = jnp.ones_like(ar), jnp.zeros_like(ai)
    for j in range(S5_CHUNK):
        w = jnp.concatenate([br * pr - bi * pi, br * pi + bi * pr], axis=1)
        w_hi, w_lo = _split_bf(w)
        rows = slice(j * LANES, (j + 1) * LANES)
        wg_ref[0, rows, :] = w_hi
        kc_ref[0, rows, :] = _bf(_dot(w_hi, c_hi) + _dot(w_lo, c_hi) + _dot(w_hi, c_lo))
        pr, pi = pr * ar - pi * ai, pr * ai + pi * ar
        kst_ref[0, rows, :] = _bf(jnp.concatenate([ctr * pr + cti * pi, cti * pr - ctr * pi], axis=1))
    lam8_ref[0] = jnp.concatenate([pr, pi], axis=1)


def _s5_prep(bmat, cmat, lam):
    depth = bmat.shape[0]
    n2 = 2 * S5_SLAB_STATE
    kq = S5_CHUNK * LANES
    blk = lambda r, c: pl.BlockSpec((None, 1, r, c), lambda l, s: (l, s, 0, 0))
    shape = lambda r, c, dt: jax.ShapeDtypeStruct((depth, S5_SLABS, r, c), dt)
    return pl.pallas_call(
        _s5_prep_kernel,
        grid=(depth, S5_SLABS),
        in_specs=[blk(LANES, n2), blk(n2, LANES), blk(LANES, n2), blk(1, n2)],
        out_specs=[blk(kq, n2), blk(kq, LANES), blk(kq, n2), blk(1, n2)],
        out_shape=[shape(kq, n2, BF16), shape(kq, LANES, BF16), shape(kq, n2, BF16), shape(1, n2, F32)],
        compiler_params=_params("parallel", "parallel"),
        name="s5_prep",
    )(bmat, cmat, jnp.swapaxes(cmat, 2, 3), lam)


def _s5_kernel(x_ref, w_ref, wg_ref, kc_ref, kst_ref, lam8_ref, d_ref, glu_ref, o_ref,
               carry_ref, u_ref, ys_ref, *, ts):
    @pl.when(pl.program_id(1) == 0)
    def _():
        carry_ref[...] = jnp.zeros_like(carry_ref)

    n = S5_SLAB_STATE
    nc = ts // S5_CHUNK
    nb = x_ref.shape[0]
    inst = [(b, sl) for b in range(nb) for sl in range(S5_SLABS)]
    ids = range(len(inst))
    u = [_dot(x_ref[b], w_ref[...]) for b in range(nb)]
    for b, sl in inst:
        u_ref[b, sl] = u[b][:, sl * LANES:(sl + 1) * LANES]
    rin = lax.broadcasted_iota(jnp.int32, (ts, 1), 0) & (S5_CHUNK - 1)
    crow = lax.broadcasted_iota(jnp.int32, (nc, 1), 0)
    first = crow == 0
    x_end = [jnp.concatenate(
        [_bf(u_ref[b, sl, pl.ds(S5_CHUNK - 1 - j, nc, stride=S5_CHUNK), :]) for j in range(S5_CHUNK)], axis=1)
        for b, sl in inst]
    gain = [_dot(x_end[i], wg_ref[sl]) for i, (b, sl) in enumerate(inst)]
    lam8 = [lam8_ref[sl] for b, sl in inst]
    ar = [z[:, :n] for z in lam8]
    ai = [z[:, n:] for z in lam8]
    cr = [carry_ref[b, sl, 0:1, :n] for b, sl in inst]
    cim = [carry_ref[b, sl, 0:1, n:] for b, sl in inst]
    er = [gain[i][:, :n] + jnp.where(first, ar[i] * cr[i] - ai[i] * cim[i], 0.0) for i in ids]
    ei = [gain[i][:, n:] + jnp.where(first, ar[i] * cim[i] + ai[i] * cr[i], 0.0) for i in ids]
    sh = 1
    while sh < nc:
        keep = crow >= sh
        for i in ids:
            sr = jnp.where(keep, pltpu.roll(er[i], sh, axis=0), 0.0)
            si = jnp.where(keep, pltpu.roll(ei[i], sh, axis=0), 0.0)
            er[i], ei[i] = er[i] + ar[i] * sr - ai[i] * si, ei[i] + ar[i] * si + ai[i] * sr
            ar[i], ai[i] = ar[i] * ar[i] - ai[i] * ai[i], 2.0 * ar[i] * ai[i]
        sh *= 2
    for i, (b, sl) in enumerate(inst):
        carry_ref[b, sl, 0:1, :n] = er[i][nc - 1:nc, :]
        carry_ref[b, sl, 0:1, n:] = ei[i][nc - 1:nc, :]
    h0 = [_bf(jnp.concatenate([jnp.where(first, cr[i], pltpu.roll(er[i], 1, axis=0)),
                               jnp.where(first, cim[i], pltpu.roll(ei[i], 1, axis=0))], axis=1)) for i in ids]
    z = [_dot_nt(h0[i], kst_ref[sl]) for i, (b, sl) in enumerate(inst)]
    for i, (b, sl) in enumerate(inst):
        for k in range(S5_CHUNK):
            ys_ref[b, sl, pl.ds(k, nc, stride=S5_CHUNK), :] = z[i][:, k * LANES:(k + 1) * LANES]
    ys = []
    for b, sl in inst:
        us = u[b][:, sl * LANES:(sl + 1) * LANES]
        lagged = [_bf(us)] + [_bf(jnp.where(rin >= j, pltpu.roll(us, j, axis=0), 0.0))
                              for j in range(1, S5_CHUNK)]
        ys.append(_dot(jnp.concatenate(lagged, axis=1), kc_ref[sl]) + ys_ref[b, sl])
    for b in range(nb):
        y = jnp.concatenate(ys[b * S5_SLABS:(b + 1) * S5_SLABS], axis=1) + d_ref[...] * u[b]
        y = 0.5 * y * (1.0 + jnp.tanh(math.sqrt(2.0 / math.pi) * (y + 0.044715 * (y * y * y))))
        h = _dot(_bf(y), glu_ref[...])
        o_ref[b] = (h[:, :D_MODEL] * _sigmoid(h[:, D_MODEL:])).astype(o_ref.dtype)


def _s5(x_bf, layer, params, ts):
    bsz, t, _ = x_bf.shape
    nb = 1
    tok = lambda b, i: (b, i, 0)
    return pl.pallas_call(
        functools.partial(_s5_kernel, ts=ts),
        grid=(bsz // nb, t // ts),
        in_specs=[pl.BlockSpec((nb, ts, D_MODEL), tok)] + [_layer_block(a, layer, single=True) for a in params],
        out_specs=pl.BlockSpec((nb, ts, D_MODEL), tok),
        out_shape=jax.ShapeDtypeStruct((bsz, t, D_MODEL), BF16),
        scratch_shapes=[pltpu.VMEM((nb, S5_SLABS, 8, 2 * S5_SLAB_STATE), F32),
                        pltpu.VMEM((nb, S5_SLABS, ts, LANES), F32), pltpu.VMEM((nb, S5_SLABS, ts, LANES), F32)],
        compiler_params=_params("parallel", "arbitrary"),
        name="s5_scan",
    )(x_bf, *params)


def _s5_matrices(lb_re, lb_im, bb_re, bb_im, c_re, c_im):
    eye = jnp.eye(S5_SLAB_GROUPS, dtype=F32)
    depth = lb_re.shape[0]
    s, g, c, p = S5_SLABS, S5_SLAB_GROUPS, S5_GROUP, S5_STATE

    def b_blocks(bb):
        return jnp.einsum("lsgcp,gh->lsgchp", bb.reshape(depth, s, g, c, p), eye).reshape(depth, s, g * c, g * p)

    def c_blocks(cc):
        return jnp.einsum("lsgcp,gh->lsgphc", cc.reshape(depth, s, g, c, p), eye).reshape(depth, s, g * p, g * c)

    bmat = jnp.concatenate([b_blocks(bb_re), b_blocks(bb_im)], axis=3)
    cmat = jnp.concatenate([c_blocks(c_re), -c_blocks(c_im)], axis=2)
    row = lambda z: z[:, ::S5_GROUP].reshape(depth, s, 1, g * p)
    lam = jnp.concatenate([row(lb_re), row(lb_im)], axis=3)
    return bmat, cmat, lam


def _merge_kernel(x_ref, ya_ref, ob_ref, yc_ref, wg_ref, gb_ref, wra_ref, wmo_ref, wo_ref,
                  lg_ref, lb_ref, o_ref, *, alpha):
    tm = x_ref.shape[0]
    step = tm // MERGE_ROW_PARTS
    parts = [slice(k * step, (k + 1) * step) for k in range(MERGE_ROW_PARTS)]
    xs = [x_ref[r, :] for r in parts]
    xbs = [_bf(x) for x in xs]
    branches = (lambda r: _dot(ya_ref[r, :], wra_ref[...]), lambda r: _dot(ob_ref[r, :], wmo_ref[...]),
                lambda r: yc_ref[r, :].astype(F32))
    merged = [None] * len(parts)
    for br, branch in enumerate(branches):
        cols = slice(br * D_MODEL, (br + 1) * D_MODEL)
        for i, r in enumerate(parts):
            term = _sigmoid(_dot(xbs[i], wg_ref[:, cols]) + gb_ref[:, cols]) * branch(r)
            merged[i] = term if merged[i] is None else merged[i] + term
    ys = [alpha * xs[i] + _dot(_bf(merged[i]), wo_ref[...]) for i in range(len(parts))]
    for i, r in enumerate(parts):
        o_ref[r, :] = _layer_norm(ys[i], lg_ref[...], lb_ref[...])


def _merge(x, ya, ob, yc, layer, params, alpha, tm):
    n = x.shape[0]
    tok = lambda i: (i, 0)
    return pl.pallas_call(
        functools.partial(_merge_kernel, alpha=alpha),
        grid=(n // tm,),
        in_specs=[
            pl.BlockSpec((tm, D_MODEL), tok),
            pl.BlockSpec((tm, RWKV_DIM), tok), pl.BlockSpec((tm, MLA_DIM), tok),
            pl.BlockSpec((tm, D_MODEL), tok),
        ] + [_layer_block(a, layer, single=True) for a in params],
        out_specs=pl.BlockSpec((tm, D_MODEL), tok),
        out_shape=jax.ShapeDtypeStruct((n, D_MODEL), F32),
        compiler_params=_params("parallel"),
        name="merge_ln",
    )(x, ya, ob, yc, *params)


def _ffn_kernel(x_ref, w1_ref, w3_ref, w2_ref, lg_ref, lb_ref, o_ref, *, alpha, tf):
    x = x_ref[...]
    xb = _bf(x)
    acc = alpha * x
    for j in range(w1_ref.shape[1] // tf):
        cols = slice(j * tf, (j + 1) * tf)
        h1 = _dot(xb, w1_ref[:, cols])
        h3 = _dot(xb, w3_ref[:, cols])
        acc = acc + _dot(_bf(h1 * _sigmoid(h1) * h3), w2_ref[cols, :])
    o_ref[...] = _layer_norm(acc, lg_ref[...], lb_ref[...])


def _ffn(x, layer, params, alpha, tm, tf):
    n = x.shape[0]
    tok = lambda i: (i, 0)
    return pl.pallas_call(
        functools.partial(_ffn_kernel, alpha=alpha, tf=tf),
        grid=(n // tm,),
        in_specs=[pl.BlockSpec((tm, D_MODEL), tok)] + [_layer_block(a, layer, single=True) for a in params],
        out_specs=pl.BlockSpec((tm, D_MODEL), tok),
        out_shape=jax.ShapeDtypeStruct((n, D_MODEL), F32),
        compiler_params=_params("parallel"),
        name="ffn_ln",
    )(x, *params)


def _rope_tables(t):
    pos = jnp.arange(t, dtype=F32)
    inv_freq = ROPE_THETA ** (-jnp.arange(0, QK_ROPE_DIM, 2, dtype=F32) / QK_ROPE_DIM)
    ang = pos[:, None] * inv_freq[None, :]
    cos, sin = jnp.cos(ang), jnp.sin(ang)
    ones = jnp.ones((t, QK_NOPE_DIM), F32)
    zeros = jnp.zeros((t, QK_NOPE_DIM), F32)
    tail1 = jnp.ones((t, LANES - QK_NOPE_DIM - QK_ROPE_DIM), F32)
    tail0 = jnp.zeros((t, LANES - QK_NOPE_DIM - QK_ROPE_DIM), F32)
    cos_t = jnp.concatenate([ones, cos, cos, tail1], axis=1)
    sin_t = jnp.concatenate([zeros, sin, sin, tail0], axis=1)
    return cos_t, sin_t


def _branch_weights(w_in, w2, a2, q_up, kv_up):
    depth = w_in.shape[0]
    o_mla = RWKV_COLS
    o_s5 = o_mla + MLA_COLS
    o_gate = o_s5 + S5_COLS
    nq = Q_LORA_RANK + KV_LORA_RANK
    zl = jnp.zeros((depth, DECAY_LORA, RWKV_DIM), F32)
    w_lora = jnp.concatenate([jnp.concatenate([w2, zl], axis=2), jnp.concatenate([zl, a2], axis=2)], axis=1)
    w_kr = jnp.zeros((depth, D_MODEL, LANES), F32).at[:, :, QK_NOPE_DIM:QK_NOPE_DIM + QK_ROPE_DIM].set(
        w_in[:, :, o_mla + nq:o_s5])
    w_mla = jnp.concatenate([w_in[:, :, o_mla:o_mla + nq], w_kr, _rope_rotated(w_kr)], axis=2)
    q_up = q_up.reshape(depth, Q_LORA_RANK, MLA_HEADS, QK_NOPE_DIM + QK_ROPE_DIM)
    q_up = jnp.pad(q_up, ((0, 0), (0, 0), (0, 0), (0, LANES - QK_NOPE_DIM - QK_ROPE_DIM)))
    q_up = jnp.concatenate([q_up.reshape(depth, Q_LORA_RANK, MLA_HEADS * LANES),
                            _rope_rotated(q_up).reshape(depth, Q_LORA_RANK, MLA_HEADS * LANES)], axis=2)
    kv_up = kv_up.reshape(depth, KV_LORA_RANK, MLA_HEADS, QK_NOPE_DIM + V_HEAD_DIM)
    k_up = jnp.pad(kv_up[..., :QK_NOPE_DIM], ((0, 0), (0, 0), (0, 0), (0, LANES - QK_NOPE_DIM)))
    kv_up = jnp.concatenate([k_up.reshape(depth, KV_LORA_RANK, MLA_HEADS * LANES),
                             kv_up[..., QK_NOPE_DIM:].reshape(depth, KV_LORA_RANK, MLA_DIM)], axis=2)
    return dict(rwkv=_bf(w_in[:, :, :o_mla]), lora=_bf(w_lora), mla=_bf(w_mla), q_up=_bf(q_up),
                kv_up=_bf(kv_up), s5=_bf(w_in[:, :, o_s5:o_gate]), gate=_bf(w_in[:, :, o_gate:]))


def _s5_params(lambda_re, lambda_im, log_step, b_re, b_im, c_re, c_im):
    lb_re, lb_im, bb_re, bb_im = _s5_discretize(lambda_re, lambda_im, log_step, b_re, b_im)
    rows = lambda z: z.reshape(-1, S5_DIM, S5_STATE)
    return _s5_prep(*_s5_matrices(lb_re, lb_im, bb_re, bb_im, rows(c_re), rows(c_im)))


def _rows(z):
    return z.reshape(z.shape[0], 1, -1).astype(F32)


ROW_TILE = dict(rwkv_prep=512, mla_prep=512, attention=1024, s5=1024, merge=1024, ffn=1024)


def _tile(t, name):
    return min(t, ROW_TILE[name])


def kernel(x, w_in, rwkv_mu, rwkv_w0, rwkv_w2, rwkv_a0, rwkv_a2, rwkv_g2, rwkv_k_k, rwkv_k_a, rwkv_r_k,
           rwkv_gn_g, rwkv_gn_b, rwkv_out, mla_q_norm, mla_q_up, mla_kv_norm, mla_kv_up, mla_out,
           s5_lambda_re, s5_lambda_im, s5_log_step, s5_b_re, s5_b_im, s5_c_re, s5_c_im, s5_d, s5_glu,
           gate_b, w_out, ln1_g, ln1_b, ffn_w1, ffn_w3, ffn_w2, ln2_g, ln2_b):
    bsz, t, _ = x.shape
    depth = w_in.shape[0]
    alpha = (2.0 * depth) ** 0.25
    n = bsz * t
    cos_t, sin_t = _rope_tables(t)

    ffn_w1, ffn_w3, ffn_w2, s5_glu, w_out, rwkv_out, mla_out = map(
        _to_bf16, (ffn_w1, ffn_w3, ffn_w2, s5_glu, w_out, rwkv_out, mla_out))

    w = _branch_weights(w_in, rwkv_w2, rwkv_a2, mla_q_up, mla_kv_up)
    wg, kc, kst, lam8 = _s5_params(s5_lambda_re, s5_lambda_im, s5_log_step, s5_b_re, s5_b_im, s5_c_re, s5_c_im)

    rwkv_par = (w["rwkv"], _rows(rwkv_mu), _rows(rwkv_w0), _rows(rwkv_a0), w["lora"], _bf(rwkv_g2))
    scan_par = tuple(_rows(z) for z in (rwkv_k_k, rwkv_k_a, rwkv_r_k, rwkv_gn_g, rwkv_gn_b))
    mla_par = (w["mla"], _rows(mla_q_norm), w["q_up"], _rows(mla_kv_norm), w["kv_up"])
    s5_par = (w["s5"], wg, kc, kst, lam8, _rows(s5_d), s5_glu)
    merge_par = (w["gate"], _rows(gate_b), rwkv_out, mla_out, w_out, _rows(ln1_g), _rows(ln1_b))
    ffn_par = (ffn_w1, ffn_w3, ffn_w2, _rows(ln2_g), _rows(ln2_b))

    for l in range(depth):
        x_bf, r, k, v, ld, a, g = _rwkv_prep(x, l, rwkv_par, _tile(t, "rwkv_prep"))
        ya = _rwkv_scan(r, k, v, ld, a, g, l, *scan_par)
        q, kx, vx = _mla_prep(x_bf, l, mla_par, cos_t, sin_t, _tile(t, "mla_prep"))
        ob = _attention(q, kx, vx, _tile(t, "attention"), min(t, ATTN_KEY_BLOCK))
        yc = _s5(x_bf, l, s5_par, _tile(t, "s5"))
        x1 = _merge(x.reshape(n, D_MODEL), ya.reshape(n, RWKV_DIM), ob.reshape(n, MLA_DIM),
                    yc.reshape(n, D_MODEL), l, merge_par, alpha, _tile(n, "merge"))
        x = _ffn(x1, l, ffn_par, alpha, _tile(n, "ffn"), FFN_SLICE).reshape(bsz, t, D_MODEL)
    return x
```

```python
import functools
import math

import jax
import jax.numpy as jnp
from jax import lax
from jax.experimental import pallas as pl
from jax.experimental.pallas import tpu as pltpu

F32 = jnp.float32
BF16 = jnp.bfloat16

D_MODEL = 1024
RWKV_HEADS = 8
RWKV_HEAD_DIM = 64
RWKV_DIM = RWKV_HEADS * RWKV_HEAD_DIM
DECAY_LORA = 64
AAA_LORA = 64
GATE_LORA = 128
RWKV_GN_EPS = 64e-5
MLA_HEADS = 8
QK_NOPE_DIM = 64
QK_ROPE_DIM = 32
V_HEAD_DIM = 64
Q_LORA_RANK = 256
KV_LORA_RANK = 128
MLA_DIM = MLA_HEADS * V_HEAD_DIM
ROPE_THETA = 10000.0
ATTN_SCALE = 1.0 / math.sqrt(QK_NOPE_DIM + QK_ROPE_DIM)
LOG2_E = math.log2(math.e)
S5_DIM = 512
S5_GROUP = 16
S5_GROUPS = S5_DIM // S5_GROUP
S5_STATE = 64
N_BRANCHES = 3
LN_EPS = 1e-5
RMS_EPS = 1e-6
RWKV_COLS = 3 * RWKV_DIM + DECAY_LORA + AAA_LORA + GATE_LORA
MLA_COLS = Q_LORA_RANK + KV_LORA_RANK + QK_ROPE_DIM
S5_COLS = S5_DIM

LANES = 128
HEAD_PAIR = LANES // RWKV_HEAD_DIM
N_PAIRS = RWKV_HEADS // HEAD_PAIR
S5_SLAB_GROUPS = LANES // S5_GROUP
S5_SLABS = S5_GROUPS // S5_SLAB_GROUPS
S5_SLAB_STATE = S5_SLAB_GROUPS * S5_STATE
S5_CHUNK = 8
VMEM_LIMIT = 48 * 1024 * 1024

RWKV_CHUNK = 64
RWKV_BLOCK = 256
ATTN_PAIRS = 1
ATTN_KEY_BLOCK = 1024
MERGE_ROW_PARTS = 4
FFN_SLICE = 256
RWKV_SCAN_PAIRS = 4
NEG_BIG = -1e30
CAST_BLOCK_BYTES = 6 * 1024 * 1024


def _bf(x):
    return x.astype(BF16)


def _dot(a, b):
    return jnp.dot(a, b, preferred_element_type=F32)


def _dot_nt(a, b):
    return lax.dot_general(a, b, (((1,), (1,)), ((), ())), preferred_element_type=F32)


def _dot_tn(a, b):
    return lax.dot_general(a, b, (((0,), (0,)), ((), ())), preferred_element_type=F32)


def _sigmoid(x):
    return 1.0 / (1.0 + jnp.exp(-x))


def _params(*sem):
    return pltpu.CompilerParams(dimension_semantics=sem, vmem_limit_bytes=VMEM_LIMIT)


def _layer_block(a, layer, single=False):
    idx = (layer,) + (0,) * (a.ndim - 1)
    mode = dict(pipeline_mode=pl.Buffered(1)) if single else {}
    return pl.BlockSpec((None,) + a.shape[1:], lambda *_: idx, **mode)


def _cast_kernel(w_ref, o_ref):
    o_ref[...] = w_ref[...].astype(o_ref.dtype)


def _to_bf16(w):
    depth, rows, cols = w.shape
    cap = CAST_BLOCK_BYTES // (4 * cols)
    tr = max(d for d in range(8, rows + 1, 8) if rows % d == 0 and d <= max(cap, 8))
    spec = pl.BlockSpec((None, tr, cols), lambda l, i: (l, i, 0))
    return pl.pallas_call(
        _cast_kernel, grid=(depth, rows // tr), in_specs=[spec], out_specs=spec,
        out_shape=jax.ShapeDtypeStruct(w.shape, BF16),
        compiler_params=_params("parallel", "parallel"), name="cast_bf16",
    )(w)


def _layer_norm(y, g, b):
    mu = jnp.mean(y, -1, keepdims=True)
    d = y - mu
    var = jnp.mean(d * d, -1, keepdims=True)
    return d * lax.rsqrt(var + LN_EPS) * g + b


def _pair_sum(x, m0):
    s0 = jnp.sum(jnp.where(m0, x, 0.0), -1, keepdims=True)
    s1 = jnp.sum(jnp.where(m0, 0.0, x), -1, keepdims=True)
    return jnp.where(m0, s0, s1)


def _rwkv_prep_kernel(x_ref, w_ref, mu_ref, w0_ref, a0_ref, wl_ref, g2_ref,
                      xb_ref, r_ref, k_ref, v_ref, ld_ref, a_ref, g_ref, prev_ref):
    @pl.when(pl.program_id(0) == 0)
    def _():
        prev_ref[...] = jnp.zeros_like(prev_ref)

    nb = x_ref.shape[0]
    xb = [_bf(x_ref[b]) for b in range(nb)]
    ps = [_dot(z, w_ref[...]) for z in xb]
    row = lax.broadcasted_iota(jnp.int32, (ps[0].shape[0], 1), 0)
    c0, c1, c2 = RWKV_DIM, 2 * RWKV_DIM, 3 * RWKV_DIM
    lane = lax.broadcasted_iota(jnp.int32, (1, DECAY_LORA + AAA_LORA), 1)
    for b in range(nb):
        xb_ref[b] = xb[b]
        p = ps[b]
        rolled = pltpu.roll(p, 1, axis=0)
        shifted = jnp.where(row == 0, prev_ref[b, 0:1, :], rolled)
        prev_ref[b, 0:1, :] = rolled[0:1, :]
        p = p + (shifted - p) * mu_ref[...]
        r_ref[b] = p[:, :c0].astype(r_ref.dtype)
        k_ref[b] = p[:, c0:c1].astype(k_ref.dtype)
        v_ref[b] = p[:, c1:c2].astype(v_ref.dtype)
        lora = p[:, c2:c2 + DECAY_LORA + AAA_LORA]
        lora = jnp.where(lane < DECAY_LORA, jnp.tanh(lora), lora)
        wa = _dot(_bf(lora), wl_ref[...])
        ld_ref[b] = (-math.exp(-0.5)) * _sigmoid(w0_ref[...] + wa[:, :RWKV_DIM])
        a_ref[b] = _sigmoid(a0_ref[...] + wa[:, RWKV_DIM:]).astype(a_ref.dtype)
        dg = p[:, c2 + DECAY_LORA + AAA_LORA:]
        g_ref[b] = _dot(_bf(_sigmoid(dg)), g2_ref[...]).astype(g_ref.dtype)


def _rwkv_prep(x, layer, params, tm):
    bsz, t, _ = x.shape
    outs = [jax.ShapeDtypeStruct((bsz, t, D_MODEL), BF16)] + [
        jax.ShapeDtypeStruct((bsz, t, RWKV_DIM), dt) for dt in (BF16, BF16, BF16, F32, BF16, BF16)]
    tok = lambda i: (0, i, 0)
    o_spec = pl.BlockSpec((bsz, tm, RWKV_DIM), tok)
    return pl.pallas_call(
        _rwkv_prep_kernel,
        grid=(t // tm,),
        in_specs=[pl.BlockSpec((bsz, tm, D_MODEL), tok)] + [_layer_block(a, layer, single=True) for a in params],
        out_specs=[pl.BlockSpec((bsz, tm, D_MODEL), tok)] + [o_spec] * 6,
        out_shape=outs,
        scratch_shapes=[pltpu.VMEM((bsz, 8, RWKV_COLS), F32)],
        compiler_params=_params("arbitrary"),
        name="rwkv_prep",
    )(x, *params)


def _rwkv_scan_kernel(r_ref, k_ref, v_ref, ld_ref, a_ref, g_ref,
                      kk_ref, ka_ref, rk_ref, gng_ref, gnb_ref, o_ref, s_ref):
    @pl.when(pl.program_id(1) == 0)
    def _():
        s_ref[...] = jnp.zeros_like(s_ref)

    seqs = [(b, slice(p * LANES, (p + 1) * LANES)) for b in range(r_ref.shape[0])
            for p in range(r_ref.shape[2] // LANES)]
    ins = [tuple(z[b, :, cols].astype(F32) for z in (r_ref, k_ref, v_ref, ld_ref, a_ref, g_ref))
           for b, cols in seqs]
    pars = [tuple(z[:, cols] for z in (kk_ref, ka_ref, rk_ref, gng_ref, gnb_ref)) for b, cols in seqs]
    outs, states = _rwkv_blocks(ins, pars, [s_ref[i] for i in range(len(seqs))])
    for i, (b, cols) in enumerate(seqs):
        o_ref[b, :, cols] = outs[i].astype(o_ref.dtype)
        s_ref[i] = states[i]


def _rwkv_blocks(ins, pars, states):
    rows, chunk = RWKV_BLOCK, RWKV_CHUNK
    n_chunks = rows // chunk
    nb = len(ins)
    inst = [(b, h) for b in range(nb) for h in range(HEAD_PAIR)]

    lane = lax.broadcasted_iota(jnp.int32, (1, LANES), 1)
    m0 = lane < RWKV_HEAD_DIM
    rin = lax.broadcasted_iota(jnp.int32, (rows, 1), 0) & (chunk - 1)

    pre = []
    for (r, k, v, ld, a, g), (k_k, k_a, _, _, _) in zip(ins, pars):
        kk = k * k_k
        kk = kk * lax.rsqrt(jnp.maximum(_pair_sum(kk * kk, m0), 1e-12))
        k2 = k * (1.0 + (a - 1.0) * k_a)
        bv = kk * a
        cl = ld
        sh = 1
        while sh < chunk:
            cl = cl + jnp.where(rin >= sh, pltpu.roll(cl, sh, axis=0), 0.0)
            sh *= 2
        w_inc = jnp.exp(cl)
        w_inv = jnp.exp(-cl)
        at = -kk * jnp.exp(cl - ld)
        rt = r * w_inc
        pre.append(dict(
            k2=k2, w_inc=w_inc, rt=rt, bt=_bf(bv * w_inv), kt=_bf(k2 * w_inv), v_bf=_bf(v),
            at_h=(_bf(jnp.where(m0, at, 0.0)), _bf(jnp.where(m0, 0.0, at))),
            rt_h=(_bf(jnp.where(m0, rt, 0.0)), _bf(jnp.where(m0, 0.0, rt)))))

    grams = []
    for d in pre:
        lhs = jnp.concatenate([d["at_h"][0], d["at_h"][1], d["rt_h"][0], d["rt_h"][1]], axis=0)
        rhs = jnp.concatenate([d["bt"], d["kt"]], axis=0)
        grams.append(_dot_nt(lhs, rhs))

    ri = lax.broadcasted_iota(jnp.int32, (rows, rows), 0)
    ci = lax.broadcasted_iota(jnp.int32, (rows, rows), 1)
    same = (ri // chunk) == (ci // chunk)
    strict = same & (ri > ci)
    incl = same & (ri >= ci)
    eye = jnp.where(ri == ci, 1.0, 0.0)

    xs = [jnp.where(strict, grams[b][h * rows:(h + 1) * rows, :rows], 0.0) for b, h in inst]
    t_inv = [eye + x for x in xs]
    n = 2
    while n < chunk:
        xb = [_bf(x) for x in xs]
        xs = [_dot(x, x) for x in xb]
        t_inv = [t + _dot(_bf(t), _bf(x)) for t, x in zip(t_inv, xs)]
        n *= 2
    t_bf = [_bf(t) for t in t_inv]

    akv = [_dot(_bf(jnp.where(strict, grams[b][h * rows:(h + 1) * rows, rows:], 0.0)), pre[b]["v_bf"])
           for b, h in inst]
    up_h = [_dot(t, jnp.concatenate([_bf(z), pre[b]["at_h"][h]], axis=1))
            for t, z, (b, h) in zip(t_bf, akv, inst)]
    u0_h = [z[:, :LANES] for z in up_h]
    p_h = [z[:, LANES:] for z in up_h]
    a_rb = [_bf(jnp.where(incl, grams[b][(2 + h) * rows:(3 + h) * rows, :rows], 0.0)) for b, h in inst]
    a_rk = [_bf(jnp.where(incl, grams[b][(2 + h) * rows:(3 + h) * rows, rows:], 0.0)) for b, h in inst]
    rb_up = [_dot(a_rb[i], _bf(up_h[i])) for i in range(len(inst))]
    y0_h = [rb_up[i][:, :LANES] + _dot(a_rk[i], pre[b]["v_bf"]) for i, (b, h) in enumerate(inst)]
    qp_h = [z[:, LANES:] for z in rb_up]

    bi = lax.broadcasted_iota(jnp.int32, (LANES, LANES), 0) // RWKV_HEAD_DIM
    bj = lax.broadcasted_iota(jnp.int32, (LANES, LANES), 1) // RWKV_HEAD_DIM
    blockdiag = bi == bj

    p_all, q_all, u0_all, y0_all = [], [], [], []
    for b in range(nb):
        i0, i1 = HEAD_PAIR * b, HEAD_PAIR * b + 1
        p_all.append(_bf(p_h[i0] + p_h[i1]))
        q_all.append(_bf(pre[b]["rt"] + qp_h[i0] + qp_h[i1]))
        u0_all.append(_bf(jnp.where(m0, u0_h[i0], u0_h[i1])))
        y0_all.append(jnp.where(m0, y0_h[i0], y0_h[i1]))

    m_c, c_c = {}, {}
    for c in range(n_chunks):
        lo, hi = c * chunk, (c + 1) * chunk
        for b in range(nb):
            bt_c, kt_c = pre[b]["bt"][lo:hi], pre[b]["kt"][lo:hi]
            m_c[b, c] = _bf(jnp.where(blockdiag, _dot_tn(p_all[b][lo:hi], bt_c), 0.0))
            c_c[b, c] = jnp.where(blockdiag, _dot_tn(
                jnp.concatenate([u0_all[b][lo:hi], pre[b]["v_bf"][lo:hi]], axis=0),
                jnp.concatenate([bt_c, kt_c], axis=0)), 0.0)

    ys = [[] for _ in range(nb)]
    states = list(states)
    for c in range(n_chunks):
        lo, hi = c * chunk, (c + 1) * chunk
        for b in range(nb):
            s_bf = _bf(states[b])
            ys[b].append(_dot_nt(q_all[b][lo:hi], s_bf) + y0_all[b][lo:hi])
            states[b] = (states[b] + _dot(s_bf, m_c[b, c]) + c_c[b, c]) * pre[b]["w_inc"][hi - 1:hi, :]

    outs = []
    inv_n = 1.0 / RWKV_HEAD_DIM
    for b, (r, k, v, ld, a, g) in enumerate(ins):
        _, _, r_k, gn_g, gn_b = pars[b]
        y = jnp.concatenate(ys[b], axis=0)
        mean = _pair_sum(y, m0) * inv_n
        d = y - mean
        var = _pair_sum(d * d, m0) * inv_n
        yn = d * lax.rsqrt(var + RWKV_GN_EPS) * gn_g + gn_b
        bonus = _pair_sum(r * pre[b]["k2"] * r_k, m0) * v
        outs.append((yn + bonus) * g)
    return outs, states


def _rwkv_scan(r, k, v, ld, a, g, layer, k_k, k_a, r_k, gn_g, gn_b):
    bsz, t, _ = r.shape
    width = RWKV_SCAN_PAIRS * LANES
    tok = pl.BlockSpec((bsz, RWKV_BLOCK, width), lambda h, c: (0, c, h))
    par = pl.BlockSpec((None, 1, width), lambda h, c: (layer, 0, h))
    return pl.pallas_call(
        _rwkv_scan_kernel,
        grid=(N_PAIRS // RWKV_SCAN_PAIRS, t // RWKV_BLOCK),
        in_specs=[tok] * 6 + [par] * 5,
        out_specs=tok,
        out_shape=jax.ShapeDtypeStruct((bsz, t, RWKV_DIM), BF16),
        scratch_shapes=[pltpu.VMEM((bsz * RWKV_SCAN_PAIRS, LANES, LANES), F32)],
        compiler_params=_params("parallel", "arbitrary"),
        name="rwkv_scan",
    )(r, k, v, ld, a, g, k_k, k_a, r_k, gn_g, gn_b)


def _rope_rotated(w):
    lo, mid, hi = QK_NOPE_DIM, QK_NOPE_DIM + QK_ROPE_DIM // 2, QK_NOPE_DIM + QK_ROPE_DIM
    return jnp.zeros_like(w).at[..., lo:mid].set(-w[..., mid:hi]).at[..., mid:hi].set(w[..., lo:mid])


def _mla_prep_kernel(x_ref, w_ref, qn_ref, qup_ref, kvn_ref, kvup_ref, cos_ref, sin_ref,
                     q_ref, k_ref, v_ref):
    nq, nkv, wq = Q_LORA_RANK, KV_LORA_RANK, MLA_HEADS * LANES
    cos_t, sin_t = cos_ref[...], sin_ref[...]
    reps = MLA_HEADS
    cos_q, sin_q = jnp.tile(cos_t, (1, reps)), jnp.tile(sin_t, (1, reps))
    batch = range(x_ref.shape[0])
    ps = [_dot(x_ref[b], w_ref[...]) for b in batch]
    c_q = [p[:, :nq] for p in ps]
    c_kv = [p[:, nq:nq + nkv] for p in ps]
    c_q = [z * lax.rsqrt(jnp.mean(z * z, -1, keepdims=True) + RMS_EPS) * qn_ref[...] for z in c_q]
    c_kv = [z * lax.rsqrt(jnp.mean(z * z, -1, keepdims=True) + RMS_EPS) * kvn_ref[...] for z in c_kv]
    qs = [_dot(_bf(z), qup_ref[...]) for z in c_q]
    kvs = [_dot(_bf(z), kvup_ref[...]) for z in c_kv]
    for b in batch:
        q = qs[b][:, :wq] * cos_q + qs[b][:, wq:] * sin_q
        q_ref[b] = (q * (ATTN_SCALE * LOG2_E)).astype(q_ref.dtype)
        kr = ps[b][:, nq + nkv:nq + nkv + LANES] * cos_t + ps[b][:, nq + nkv + LANES:] * sin_t
        k_ref[b] = (kvs[b][:, :wq] + jnp.tile(kr, (1, reps))).astype(k_ref.dtype)
        v_ref[b] = kvs[b][:, wq:].astype(v_ref.dtype)


def _mla_prep(x_bf, layer, params, cos_t, sin_t, tm):
    bsz, t, _ = x_bf.shape
    tok = lambda i: (0, i, 0)
    tab = pl.BlockSpec((tm, LANES), lambda i: (i, 0))
    wq = MLA_HEADS * LANES
    return pl.pallas_call(
        _mla_prep_kernel,
        grid=(t // tm,),
        in_specs=[pl.BlockSpec((bsz, tm, D_MODEL), tok)] + [_layer_block(a, layer, single=True) for a in params]
        + [tab, tab],
        out_specs=[pl.BlockSpec((bsz, tm, wq), tok), pl.BlockSpec((bsz, tm, wq), tok),
                   pl.BlockSpec((bsz, tm, MLA_DIM), tok)],
        out_shape=[jax.ShapeDtypeStruct((bsz, t, wq), BF16), jax.ShapeDtypeStruct((bsz, t, wq), BF16),
                   jax.ShapeDtypeStruct((bsz, t, MLA_DIM), BF16)],
        compiler_params=_params("parallel"),
        name="mla_prep",
    )(x_bf, *params, cos_t, sin_t)


def _attn_kernel(q_ref, k_ref, v_ref, o_ref, m_ref, l_ref, acc_ref, *, tq, tk):
    i = pl.program_id(2)
    m_ref[...] = jnp.full_like(m_ref, NEG_BIG)
    l_ref[...] = jnp.zeros_like(l_ref)
    acc_ref[...] = jnp.zeros_like(acc_ref)

    def step(start, nk, row_lo, masked):
        cols = slice(row_lo, tq)
        nrow = tq - row_lo
        heads = range(q_ref.shape[1] // LANES)
        s_h = [_dot_nt(k_ref[pl.ds(start, nk), h * LANES:(h + 1) * LANES], q_ref[cols, h * LANES:(h + 1) * LANES])
               for h in heads]
        for h in heads:
            s = s_h[h]
            if masked:
                ki = lax.broadcasted_iota(jnp.int32, (nk, nrow), 0) + start
                qi = lax.broadcasted_iota(jnp.int32, (nk, nrow), 1) + (i * tq + row_lo)
                s = jnp.where(ki <= qi, s, NEG_BIG)
            m_old = m_ref[h, :, cols]
            m_new = jnp.maximum(m_old, jnp.max(s, 0, keepdims=True))
            alpha = jnp.exp2(m_old - m_new)
            p = jnp.exp2(s - m_new)
            l_ref[h, :, cols] = alpha * l_ref[h, :, cols] + jnp.sum(p, 0, keepdims=True)
            m_ref[h, :, cols] = m_new
            pair = slice((h // HEAD_PAIR) * LANES, (h // HEAD_PAIR + 1) * LANES)
            acc_ref[h, :, cols] = alpha * acc_ref[h, :, cols] + _dot_tn(v_ref[pl.ds(start, nk), pair], _bf(p))

    def body(j, carry):
        step(pl.multiple_of(j * tk, tk), tk, 0, False)
        return carry

    lax.fori_loop(0, i * (tq // tk), body, 0)
    half = tq // 2
    base = pl.multiple_of(i * tq, tq)
    step(base, half, 0, True)
    step(base + half, half, half, True)
    first_head = lax.broadcasted_iota(jnp.int32, (LANES, 1), 0) < V_HEAD_DIM
    for p in range(o_ref.shape[1] // LANES):
        out = [acc_ref[h] / l_ref[h] for h in (HEAD_PAIR * p, HEAD_PAIR * p + 1)]
        o_ref[:, p * LANES:(p + 1) * LANES] = jnp.where(first_head, out[0], out[1]).T.astype(o_ref.dtype)


def _attention(q, k, v, tq, tk):
    bsz, t, _ = q.shape
    npair = ATTN_PAIRS
    heads = npair * HEAD_PAIR
    resident = lambda width: pl.BlockSpec((None, t, width), lambda b, h, i: (b, 0, h))
    return pl.pallas_call(
        functools.partial(_attn_kernel, tq=tq, tk=tk),
        grid=(bsz, N_PAIRS // npair, t // tq),
        in_specs=[
            pl.BlockSpec((None, tq, heads * LANES), lambda b, h, i: (b, i, h)),
            resident(heads * LANES),
            resident(npair * LANES),
        ],
        out_specs=pl.BlockSpec((None, tq, npair * LANES), lambda b, h, i: (b, i, h)),
        out_shape=jax.ShapeDtypeStruct((bsz, t, MLA_DIM), BF16),
        scratch_shapes=[pltpu.VMEM((heads, 1, tq), F32), pltpu.VMEM((heads, 1, tq), F32),
                        pltpu.VMEM((heads, LANES, tq), F32)],
        compiler_params=_params("parallel", "parallel", "arbitrary"),
        name="mla_attention",
    )(q, k, v)


def _s5_disc_kernel(lre_ref, lim_ref, ls_ref, br_ref, bi_ref, lbr_ref, lbi_ref, bbr_ref, bbi_ref):
    lam_re = jnp.minimum(lre_ref[...], -1e-4)
    lam_im = lim_ref[...]
    step = jnp.exp(ls_ref[...])
    mag = jnp.exp(lam_re * step)
    ang = lam_im * step
    lb_re, lb_im = mag * jnp.cos(ang), mag * jnp.sin(ang)
    den = lam_re * lam_re + lam_im * lam_im
    n_re = lb_re - 1.0
    f_re = (n_re * lam_re + lb_im * lam_im) / den
    f_im = (lb_im * lam_re - n_re * lam_im) / den
    br, bi = br_ref[...], bi_ref[...]
    lbr_ref[...] = lb_re
    lbi_ref[...] = lb_im
    bbr_ref[...] = f_re * br - f_im * bi
    bbi_ref[...] = f_re * bi + f_im * br


def _s5_discretize(lambda_re, lambda_im, log_step, b_re, b_im):
    depth = lambda_re.shape[0]
    rep = lambda z: jnp.repeat(z, S5_GROUP, axis=1)
    ls = rep(jnp.broadcast_to(log_step[:, :, None], (depth, S5_GROUPS, S5_STATE)))
    tr = lambda z: jnp.transpose(z, (0, 1, 3, 2)).reshape(depth, S5_DIM, S5_STATE)
    spec = pl.BlockSpec((None, S5_DIM, S5_STATE), lambda l: (l, 0, 0))
    out = jax.ShapeDtypeStruct((depth, S5_DIM, S5_STATE), F32)
    return pl.pallas_call(
        _s5_disc_kernel, grid=(depth,), in_specs=[spec] * 5, out_specs=[spec] * 4, out_shape=[out] * 4,
        compiler_params=_params("parallel"), name="s5_discretize",
    )(rep(lambda_re), rep(lambda_im), ls, tr(b_re), tr(b_im))


def _split_bf(z):
    hi = _bf(z)
    return hi, _bf(z - hi.astype(F32))


def _s5_prep_kernel(b_ref, c_ref, ct_ref, lam_ref, wg_ref, kc_ref, kst_ref, lam8_ref):
    n = S5_SLAB_STATE
    b, c, ct, lam = b_ref[0], c_ref[0], ct_ref[0], lam_ref[0]
    ar, ai = lam[:, :n], lam[:, n:]
    br, bi = b[:, :n], b[:, n:]
    ctr, cti = ct[:, :n], ct[:, n:]
    c_hi, c_lo = _split_bf(c)
    pr, pi = jnp.ones_like(ar), jnp.zeros_like(ai)
    for j in range(S5_CHUNK):
        w = jnp.concatenate([br * pr - bi * pi, br * pi + bi * pr], axis=1)
        w_hi, w_lo = _split_bf(w)
        rows = slice(j * LANES, (j + 1) * LANES)
        wg_ref[0, rows, :] = w_hi
        kc_ref[0, rows, :] = _bf(_dot(w_hi, c_hi) + _dot(w_lo, c_hi) + _dot(w_hi, c_lo))
        pr, pi = pr * ar - pi * ai, pr * ai + pi * ar
        kst_ref[0, rows, :] = _bf(jnp.concatenate([ctr * pr + cti * pi, cti * pr - ctr * pi], axis=1))
    lam8_ref[0] = jnp.concatenate([pr, pi], axis=1)


def _s5_prep(bmat, cmat, lam):
    depth = bmat.shape[0]
    n2 = 2 * S5_SLAB_STATE
    kq = S5_CHUNK * LANES
    blk = lambda r, c: pl.BlockSpec((None, 1, r, c), lambda l, s: (l, s, 0, 0))
    shape = lambda r, c, dt: jax.ShapeDtypeStruct((depth, S5_SLABS, r, c), dt)
    return pl.pallas_call(
        _s5_prep_kernel,
        grid=(depth, S5_SLABS),
        in_specs=[blk(LANES, n2), blk(n2, LANES), blk(LANES, n2), blk(1, n2)],
        out_specs=[blk(kq, n2), blk(kq, LANES), blk(kq, n2), blk(1, n2)],
        out_shape=[shape(kq, n2, BF16), shape(kq, LANES, BF16), shape(kq, n2, BF16), shape(1, n2, F32)],
        compiler_params=_params("parallel", "parallel"),
        name="s5_prep",
    )(bmat, cmat, jnp.swapaxes(cmat, 2, 3), lam)


def _s5_kernel(x_ref, w_ref, wg_ref, kc_ref, kst_ref, lam8_ref, d_ref, glu_ref, o_ref,
               carry_ref, u_ref, ys_ref, *, ts):
    @pl.when(pl.program_id(1) == 0)
    def _():
        carry_ref[...] = jnp.zeros_like(carry_ref)

    n = S5_SLAB_STATE
    nc = ts // S5_CHUNK
    nb = x_ref.shape[0]
    inst = [(b, sl) for b in range(nb) for sl in range(S5_SLABS)]
    ids = range(len(inst))
    u = [_dot(x_ref[b], w_ref[...]) for b in range(nb)]
    for b, sl in inst:
        u_ref[b, sl] = u[b][:, sl * LANES:(sl + 1) * LANES]
    rin = lax.broadcasted_iota(jnp.int32, (ts, 1), 0) & (S5_CHUNK - 1)
    crow = lax.broadcasted_iota(jnp.int32, (nc, 1), 0)
    first = crow == 0
    x_end = [jnp.concatenate(
        [_bf(u_ref[b, sl, pl.ds(S5_CHUNK - 1 - j, nc, stride=S5_CHUNK), :]) for j in range(S5_CHUNK)], axis=1)
        for b, sl in inst]
    gain = [_dot(x_end[i], wg_ref[sl]) for i, (b, sl) in enumerate(inst)]
    lam8 = [lam8_ref[sl] for b, sl in inst]
    ar = [z[:, :n] for z in lam8]
    ai = [z[:, n:] for z in lam8]
    cr = [carry_ref[b, sl, 0:1, :n] for b, sl in inst]
    cim = [carry_ref[b, sl, 0:1, n:] for b, sl in inst]
    er = [gain[i][:, :n] + jnp.where(first, ar[i] * cr[i] - ai[i] * cim[i], 0.0) for i in ids]
    ei = [gain[i][:, n:] + jnp.where(first, ar[i] * cim[i] + ai[i] * cr[i], 0.0) for i in ids]
    sh = 1
    while sh < nc:
        keep = crow >= sh
        for i in ids:
            sr = jnp.where(keep, pltpu.roll(er[i], sh, axis=0), 0.0)
            si = jnp.where(keep, pltpu.roll(ei[i], sh, axis=0), 0.0)
            er[i], ei[i] = er[i] + ar[i] * sr - ai[i] * si, ei[i] + ar[i] * si + ai[i] * sr
            ar[i], ai[i] = ar[i] * ar[i] - ai[i] * ai[i], 2.0 * ar[i] * ai[i]
        sh *= 2
    for i, (b, sl) in enumerate(inst):
        carry_ref[b, sl, 0:1, :n] = er[i][nc - 1:nc, :]
        carry_ref[b, sl, 0:1, n:] = ei[i][nc - 1:nc, :]
    h0 = [_bf(jnp.concatenate([jnp.where(first, cr[i], pltpu.roll(er[i], 1, axis=0)),
                               jnp.where(first, cim[i], pltpu.roll(ei[i], 1, axis=0))], axis=1)) for i in ids]
    z = [_dot_nt(h0[i], kst_ref[sl]) for i, (b, sl) in enumerate(inst)]
    for i, (b, sl) in enumerate(inst):
        for k in range(S5_CHUNK):
            ys_ref[b, sl, pl.ds(k, nc, stride=S5_CHUNK), :] = z[i][:, k * LANES:(k + 1) * LANES]
    ys = []
    for b, sl in inst:
        us = u[b][:, sl * LANES:(sl + 1) * LANES]
        lagged = [_bf(us)] + [_bf(jnp.where(rin >= j, pltpu.roll(us, j, axis=0), 0.0))
                              for j in range(1, S5_CHUNK)]
        ys.append(_dot(jnp.concatenate(lagged, axis=1), kc_ref[sl]) + ys_ref[b, sl])
    for b in range(nb):
        y = jnp.concatenate(ys[b * S5_SLABS:(b + 1) * S5_SLABS], axis=1) + d_ref[...] * u[b]
        y = 0.5 * y * (1.0 + jnp.tanh(math.sqrt(2.0 / math.pi) * (y + 0.044715 * (y * y * y))))
        h = _dot(_bf(y), glu_ref[...])
        o_ref[b] = (h[:, :D_MODEL] * _sigmoid(h[:, D_MODEL:])).astype(o_ref.dtype)


def _s5(x_bf, layer, params, ts):
    bsz, t, _ = x_bf.shape
    nb = 1
    tok = lambda b, i: (b, i, 0)
    return pl.pallas_call(
        functools.partial(_s5_kernel, ts=ts),
        grid=(bsz // nb, t // ts),
        in_specs=[pl.BlockSpec((nb, ts, D_MODEL), tok)] + [_layer_block(a, layer, single=True) for a in params],
        out_specs=pl.BlockSpec((nb, ts, D_MODEL), tok),
        out_shape=jax.ShapeDtypeStruct((bsz, t, D_MODEL), BF16),
        scratch_shapes=[pltpu.VMEM((nb, S5_SLABS, 8, 2 * S5_SLAB_STATE), F32),
                        pltpu.VMEM((nb, S5_SLABS, ts, LANES), F32), pltpu.VMEM((nb, S5_SLABS, ts, LANES), F32)],
        compiler_params=_params("parallel", "arbitrary"),
        name="s5_scan",
    )(x_bf, *params)


def _s5_matrices(lb_re, lb_im, bb_re, bb_im, c_re, c_im):
    eye = jnp.eye(S5_SLAB_GROUPS, dtype=F32)
    depth = lb_re.shape[0]
    s, g, c, p = S5_SLABS, S5_SLAB_GROUPS, S5_GROUP, S5_STATE

    def b_blocks(bb):
        return jnp.einsum("lsgcp,gh->lsgchp", bb.reshape(depth, s, g, c, p), eye).reshape(depth, s, g * c, g * p)

    def c_blocks(cc):
        return jnp.einsum("lsgcp,gh->lsgphc", cc.reshape(depth, s, g, c, p), eye).reshape(depth, s, g * p, g * c)

    bmat = jnp.concatenate([b_blocks(bb_re), b_blocks(bb_im)], axis=3)
    cmat = jnp.concatenate([c_blocks(c_re), -c_blocks(c_im)], axis=2)
    row = lambda z: z[:, ::S5_GROUP].reshape(depth, s, 1, g * p)
    lam = jnp.concatenate([row(lb_re), row(lb_im)], axis=3)
    return bmat, cmat, lam


def _merge_kernel(x_ref, ya_ref, ob_ref, yc_ref, wg_ref, gb_ref, wra_ref, wmo_ref, wo_ref,
                  lg_ref, lb_ref, o_ref, *, alpha):
    tm = x_ref.shape[0]
    step = tm // MERGE_ROW_PARTS
    parts = [slice(k * step, (k + 1) * step) for k in range(MERGE_ROW_PARTS)]
    xs = [x_ref[r, :] for r in parts]
    xbs = [_bf(x) for x in xs]
    branches = (lambda r: _dot(ya_ref[r, :], wra_ref[...]), lambda r: _dot(ob_ref[r, :], wmo_ref[...]),
                lambda r: yc_ref[r, :].astype(F32))
    merged = [None] * len(parts)
    for br, branch in enumerate(branches):
        cols = slice(br * D_MODEL, (br + 1) * D_MODEL)
        for i, r in enumerate(parts):
            term = _sigmoid(_dot(xbs[i], wg_ref[:, cols]) + gb_ref[:, cols]) * branch(r)
            merged[i] = term if merged[i] is None else merged[i] + term
    ys = [alpha * xs[i] + _dot(_bf(merged[i]), wo_ref[...]) for i in range(len(parts))]
    for i, r in enumerate(parts):
        o_ref[r, :] = _layer_norm(ys[i], lg_ref[...], lb_ref[...])


def _merge(x, ya, ob, yc, layer, params, alpha, tm):
    n = x.shape[0]
    tok = lambda i: (i, 0)
    return pl.pallas_call(
        functools.partial(_merge_kernel, alpha=alpha),
        grid=(n // tm,),
        in_specs=[
            pl.BlockSpec((tm, D_MODEL), tok),
            pl.BlockSpec((tm, RWKV_DIM), tok), pl.BlockSpec((tm, MLA_DIM), tok),
            pl.BlockSpec((tm, D_MODEL), tok),
        ] + [_layer_block(a, layer, single=True) for a in params],
        out_specs=pl.BlockSpec((tm, D_MODEL), tok),
        out_shape=jax.ShapeDtypeStruct((n, D_MODEL), F32),
        compiler_params=_params("parallel"),
        name="merge_ln",
    )(x, ya, ob, yc, *params)


def _ffn_kernel(x_ref, w1_ref, w3_ref, w2_ref, lg_ref, lb_ref, o_ref, *, alpha, tf):
    x = x_ref[...]
    xb = _bf(x)
    acc = alpha * x
    for j in range(w1_ref.shape[1] // tf):
        cols = slice(j * tf, (j + 1) * tf)
        h1 = _dot(xb, w1_ref[:, cols])
        h3 = _dot(xb, w3_ref[:, cols])
        acc = acc + _dot(_bf(h1 * _sigmoid(h1) * h3), w2_ref[cols, :])
    o_ref[...] = _layer_norm(acc, lg_ref[...], lb_ref[...])


def _ffn(x, layer, params, alpha, tm, tf):
    n = x.shape[0]
    tok = lambda i: (i, 0)
    return pl.pallas_call(
        functools.partial(_ffn_kernel, alpha=alpha, tf=tf),
        grid=(n // tm,),
        in_specs=[pl.BlockSpec((tm, D_MODEL), tok)] + [_layer_block(a, layer, single=True) for a in params],
        out_specs=pl.BlockSpec((tm, D_MODEL), tok),
        out_shape=jax.ShapeDtypeStruct((n, D_MODEL), F32),
        compiler_params=_params("parallel"),
        name="ffn_ln",
    )(x, *params)


def _rope_tables(t):
    pos = jnp.arange(t, dtype=F32)
    inv_freq = ROPE_THETA ** (-jnp.arange(0, QK_ROPE_DIM, 2, dtype=F32) / QK_ROPE_DIM)
    ang = pos[:, None] * inv_freq[None, :]
    cos, sin = jnp.cos(ang), jnp.sin(ang)
    ones = jnp.ones((t, QK_NOPE_DIM), F32)
    zeros = jnp.zeros((t, QK_NOPE_DIM), F32)
    tail1 = jnp.ones((t, LANES - QK_NOPE_DIM - QK_ROPE_DIM), F32)
    tail0 = jnp.zeros((t, LANES - QK_NOPE_DIM - QK_ROPE_DIM), F32)
    cos_t = jnp.concatenate([ones, cos, cos, tail1], axis=1)
    sin_t = jnp.concatenate([zeros, sin, sin, tail0], axis=1)
    return cos_t, sin_t


def _branch_weights(w_in, w2, a2, q_up, kv_up):
    depth = w_in.shape[0]
    o_mla = RWKV_COLS
    o_s5 = o_mla + MLA_COLS
    o_gate = o_s5 + S5_COLS
    nq = Q_LORA_RANK + KV_LORA_RANK
    zl = jnp.zeros((depth, DECAY_LORA, RWKV_DIM), F32)
    w_lora = jnp.concatenate([jnp.concatenate([w2, zl], axis=2), jnp.concatenate([zl, a2], axis=2)], axis=1)
    w_kr = jnp.zeros((depth, D_MODEL, LANES), F32).at[:, :, QK_NOPE_DIM:QK_NOPE_DIM + QK_ROPE_DIM].set(
        w_in[:, :, o_mla + nq:o_s5])
    w_mla = jnp.concatenate([w_in[:, :, o_mla:o_mla + nq], w_kr, _rope_rotated(w_kr)], axis=2)
    q_up = q_up.reshape(depth, Q_LORA_RANK, MLA_HEADS, QK_NOPE_DIM + QK_ROPE_DIM)
    q_up = jnp.pad(q_up, ((0, 0), (0, 0), (0, 0), (0, LANES - QK_NOPE_DIM - QK_ROPE_DIM)))
    q_up = jnp.concatenate([q_up.reshape(depth, Q_LORA_RANK, MLA_HEADS * LANES),
                            _rope_rotated(q_up).reshape(depth, Q_LORA_RANK, MLA_HEADS * LANES)], axis=2)
    kv_up = kv_up.reshape(depth, KV_LORA_RANK, MLA_HEADS, QK_NOPE_DIM + V_HEAD_DIM)
    k_up = jnp.pad(kv_up[..., :QK_NOPE_DIM], ((0, 0), (0, 0), (0, 0), (0, LANES - QK_NOPE_DIM)))
    kv_up = jnp.concatenate([k_up.reshape(depth, KV_LORA_RANK, MLA_HEADS * LANES),
                             kv_up[..., QK_NOPE_DIM:].reshape(depth, KV_LORA_RANK, MLA_DIM)], axis=2)
    return dict(rwkv=_bf(w_in[:, :, :o_mla]), lora=_bf(w_lora), mla=_bf(w_mla), q_up=_bf(q_up),
                kv_up=_bf(kv_up), s5=_bf(w_in[:, :, o_s5:o_gate]), gate=_bf(w_in[:, :, o_gate:]))


def _s5_params(lambda_re, lambda_im, log_step, b_re, b_im, c_re, c_im):
    lb_re, lb_im, bb_re, bb_im = _s5_discretize(lambda_re, lambda_im, log_step, b_re, b_im)
    rows = lambda z: z.reshape(-1, S5_DIM, S5_STATE)
    return _s5_prep(*_s5_matrices(lb_re, lb_im, bb_re, bb_im, rows(c_re), rows(c_im)))


def _rows(z):
    return z.reshape(z.shape[0], 1, -1).astype(F32)


ROW_TILE = dict(rwkv_prep=512, mla_prep=512, attention=1024, s5=1024, merge=1024, ffn=1024)


def _tile(t, name):
    return min(t, ROW_TILE[name])


def kernel(x, w_in, rwkv_mu, rwkv_w0, rwkv_w2, rwkv_a0, rwkv_a2, rwkv_g2, rwkv_k_k, rwkv_k_a, rwkv_r_k,
           rwkv_gn_g, rwkv_gn_b, rwkv_out, mla_q_norm, mla_q_up, mla_kv_norm, mla_kv_up, mla_out,
           s5_lambda_re, s5_lambda_im, s5_log_step, s5_b_re, s5_b_im, s5_c_re, s5_c_im, s5_d, s5_glu,
           gate_b, w_out, ln1_g, ln1_b, ffn_w1, ffn_w3, ffn_w2, ln2_g, ln2_b):
    bsz, t, _ = x.shape
    depth = w_in.shape[0]
    alpha = (2.0 * depth) ** 0.25
    n = bsz * t
    cos_t, sin_t = _rope_tables(t)

    ffn_w1, ffn_w3, ffn_w2, s5_glu, w_out, rwkv_out, mla_out = map(
        _to_bf16, (ffn_w1, ffn_w3, ffn_w2, s5_glu, w_out, rwkv_out, mla_out))

    w = _branch_weights(w_in, rwkv_w2, rwkv_a2, mla_q_up, mla_kv_up)
    wg, kc, kst, lam8 = _s5_params(s5_lambda_re, s5_lambda_im, s5_log_step, s5_b_re, s5_b_im, s5_c_re, s5_c_im)

    rwkv_par = (w["rwkv"], _rows(rwkv_mu), _rows(rwkv_w0), _rows(rwkv_a0), w["lora"], _bf(rwkv_g2))
    scan_par = tuple(_rows(z) for z in (rwkv_k_k, rwkv_k_a, rwkv_r_k, rwkv_gn_g, rwkv_gn_b))
    mla_par = (w["mla"], _rows(mla_q_norm), w["q_up"], _rows(mla_kv_norm), w["kv_up"])
    s5_par = (w["s5"], wg, kc, kst, lam8, _rows(s5_d), s5_glu)
    merge_par = (w["gate"], _rows(gate_b), rwkv_out, mla_out, w_out, _rows(ln1_g), _rows(ln1_b))
    ffn_par = (ffn_w1, ffn_w3, ffn_w2, _rows(ln2_g), _rows(ln2_b))

    for l in range(depth):
        x_bf, r, k, v, ld, a, g = _rwkv_prep(x, l, rwkv_par, _tile(t, "rwkv_prep"))
        ya = _rwkv_scan(r, k, v, ld, a, g, l, *scan_par)
        q, kx, vx = _mla_prep(x_bf, l, mla_par, cos_t, sin_t, _tile(t, "mla_prep"))
        ob = _attention(q, kx, vx, _tile(t, "attention"), min(t, ATTN_KEY_BLOCK))
        yc = _s5(x_bf, l, s5_par, _tile(t, "s5"))
        x1 = _merge(x.reshape(n, D_MODEL), ya.reshape(n, RWKV_DIM), ob.reshape(n, MLA_DIM),
                    yc.reshape(n, D_MODEL), l, merge_par, alpha, _tile(n, "merge"))
        x = _ffn(x1, l, ffn_par, alpha, _tile(n, "ffn"), FFN_SLICE).reshape(bsz, t, D_MODEL)
    return x
```
